```python
import math
import jax, jax.numpy as jnp
from jax import lax
import numpy as np

D_MODEL = 1024
BATCH = 8
SEQ = 8192
DEPTH = 2

GRID_W = 64
CTX_LEN = 256
EPS = 1e-6

DA_HEADS = 4
DA_HEAD_DIM = 64
DA_V_DIM = 2 * DA_HEAD_DIM
DA_QK_W = DA_HEADS * 2 * DA_HEAD_DIM
DA_V_W = DA_HEADS * DA_V_DIM
Q_BLOCK = 128
ROPE_BASE = 10000.0

SC_W = 512
SC_K = 3

FN_W = 512
FN_GROUPS = 4

POOL_WINDOWS = (2, 4, 8, 16)
POOL_W = 512
POOL_G = len(POOL_WINDOWS)
POOL_GW = POOL_W // POOL_G

N_BRANCH = 4
BRANCH_W = 512

COL_WIDTHS = (DA_QK_W, DA_QK_W, DA_V_W, SC_W, SC_W, SC_W, FN_W, POOL_W)
IN_W = sum(COL_WIDTHS)
COL_SPLITS = tuple(sum(COL_WIDTHS[:i + 1]) for i in range(len(COL_WIDTHS) - 1))
KV_LO = DA_QK_W
KV_HI = 2 * DA_QK_W + DA_V_W

N_EXPERTS = 32
TOP_K = 4
D_EXPERT = D_MODEL
SWIGLU_LIMIT = 7.0
SWIGLU_ALPHA = 1.702

kernel_name = "hybrid_parallel_diffusion_block"


def rmsnorm(x, g):
    xf = x.astype(jnp.float32)
    y = xf * lax.rsqrt(jnp.mean(xf * xf, axis=-1, keepdims=True) + EPS)
    return (y * g.astype(jnp.float32)).astype(x.dtype)


def adaln(cv, w, b):
    m = jax.nn.silu(cv) @ w + b
    return m.reshape(m.shape[:-1] + (6, w.shape[0]))


def modulate(h, shift, scale):
    return h * (1 + scale) + shift


def axial_rope(n):
    rows = n // GRID_W
    row = jnp.repeat(jnp.arange(rows), GRID_W).astype(jnp.float32)
    col = jnp.tile(jnp.arange(GRID_W), rows).astype(jnp.float32)
    half = DA_HEAD_DIM // 2
    inv = ROPE_BASE ** (-jnp.arange(0, half, 2, dtype=jnp.float32) / half)
    ar = row[:, None] * inv
    ac = col[:, None] * inv
    ang = jnp.concatenate([ar, ar, ac, ac], axis=-1)
    return jnp.cos(ang), jnp.sin(ang)


def apply_rope(x, cos, sin):
    x0, x1, x2, x3 = jnp.split(x, 4, axis=-1)
    rot = jnp.concatenate([-x1, x0, -x3, x2], axis=-1)
    return x * cos + rot * sin


def split_qk(t):
    b, n, _ = t.shape
    return t.reshape(b, n, DA_HEADS, 2, DA_HEAD_DIM).transpose(0, 2, 3, 1, 4)


def split_v(t):
    b, n, _ = t.shape
    return t.reshape(b, n, DA_HEADS, DA_V_DIM).transpose(0, 2, 1, 3)


def diff_lambda(lam, lam_init):
    lf = lam.astype(jnp.float32)
    return jnp.exp(jnp.sum(lf[0] * lf[1])) - jnp.exp(jnp.sum(lf[2] * lf[3])) + lam_init


def diff_attend(q, k, v, lam):
    s = jnp.einsum('bhiqd,bhikd->bhiqk', q, k).astype(jnp.float32) * (DA_HEAD_DIM ** -0.5)
    p = jax.nn.softmax(s, axis=-1)
    a = p[:, :, 0] - lam * p[:, :, 1]
    return jnp.einsum('bhqk,bhkv->bhqv', a.astype(v.dtype), v)


def latent_diff_attention(q, k_all, v_all, lam):
    b, h, _, s, d = q.shape
    nb = s // Q_BLOCK
    qb = q.reshape(b, h, 2, nb, Q_BLOCK, d).transpose(3, 0, 1, 2, 4, 5)
    ob = lax.map(lambda qq: diff_attend(qq, k_all, v_all, lam), qb)
    return ob.transpose(1, 2, 0, 3, 4).reshape(b, h, s, DA_V_DIM)


def diff_head_out(o, g_sub, lam_init):
    o = rmsnorm(o, g_sub) * (1 - lam_init)
    b, h, n, dv = o.shape
    return o.transpose(0, 2, 1, 3).reshape(b, n, h * dv)


def short_conv(xin, bgate, cgate, w):
    u = cgate * xin
    up = jnp.pad(u, ((0, 0), (1, 1), (0, 0)))
    y = w[0] * up[:, :-2] + w[1] * up[:, 1:-1] + w[2] * up[:, 2:]
    return bgate * y


def fourier_mix(xf):
    b, n, w = xf.shape
    g = xf.astype(jnp.float32).reshape(b, n, FN_GROUPS, w // FN_GROUPS)
    y = jnp.fft.fftn(g, axes=(1, 3), norm='ortho').real
    return y.reshape(b, n, w).astype(xf.dtype)


def multiscale_pool(xp, w_grp, scale):
    b, n, w = xp.shape
    xf = xp.reshape(b, n, POOL_G, POOL_GW).astype(jnp.float32)
    cs = jnp.concatenate([jnp.zeros_like(xf[:, :1]), jnp.cumsum(xf, axis=1)], axis=1)
    t = jnp.arange(n)[:, None]
    half = jnp.array([wd // 2 for wd in POOL_WINDOWS])[None, :]
    lo = jnp.clip(t - half, 0, n)
    hi = jnp.clip(t + half, 0, n)
    gi = jnp.arange(POOL_G)[None, :]
    win_sum = cs[:, hi, gi] - cs[:, lo, gi]
    cnt = (hi - lo).astype(jnp.float32)[None, :, :, None]
    pooled = (win_sum / cnt - xf).astype(xp.dtype)
    y = jnp.einsum('bngc,gcd->bngd', pooled, w_grp)
    return y.reshape(b, n, w) * scale


def merge_branches(ys, h, w_branch, w_gate, b_gate, w_out):
    p = jnp.einsum('bnkc,kcd->bnkd', ys, w_branch)
    g = jax.nn.sigmoid(h @ w_gate + b_gate).reshape(p.shape)
    return jnp.sum(g * p, axis=2) @ w_out


def moe_ffn(h, w_router, b_router, w_g, b_g, w_u, b_u, w_d, b_d):
    shp = h.shape
    t = h.reshape(-1, shp[-1])
    logits = (t @ w_router + b_router).astype(jnp.float32)
    top_v, top_i = lax.top_k(logits, TOP_K)
    top_w = jax.nn.softmax(top_v, axis=-1)
    comb = jnp.sum(jax.nn.one_hot(top_i, N_EXPERTS, dtype=jnp.float32) * top_w[..., None], axis=1)
    comb = comb.astype(h.dtype)

    def expert_step(acc, ex):
        wg, bg, wu, bu, wd, bd, ce = ex
        a = jnp.minimum(t @ wg + bg, SWIGLU_LIMIT)
        u = jnp.clip(t @ wu + bu, -SWIGLU_LIMIT, SWIGLU_LIMIT)
        y = (a * jax.nn.sigmoid(SWIGLU_ALPHA * a) * (u + 1)) @ wd + bd
        return acc + ce[:, None] * y, None

    acc, _ = lax.scan(expert_step, jnp.zeros_like(t), (w_g, b_g, w_u, b_u, w_d, b_d, comb.T))
    return acc.reshape(shp)


def setup_inputs(seed: int = 0) -> dict:
    key = jax.random.key(seed)
    ks = jax.random.split(key, 32)
    f32 = jnp.float32

    def nrm(k, shape, s):
        return jax.random.normal(k, shape, f32) * s

    L, D, E, F = DEPTH, D_MODEL, N_EXPERTS, D_EXPERT
    return {
        "x": nrm(ks[0], (BATCH, SEQ, D), 1.0),
        "c": nrm(ks[1], (BATCH, D), 1.0),
        "ctx": nrm(ks[2], (BATCH, CTX_LEN, D), 1.0),
        "c_ctx": nrm(ks[3], (D,), 1.0),
        "w_ada": nrm(ks[4], (L, D, 6 * D), 0.5 * D ** -0.5),
        "b_ada": nrm(ks[5], (L, 6 * D), 0.02),
        "g_norm1": 1.0 + nrm(ks[6], (L, D), 0.05),
        "w_in": nrm(ks[7], (L, D, IN_W), D ** -0.5),
        "da_lambda": nrm(ks[8], (L, 4, DA_HEAD_DIM), 0.1),
        "g_subln": 1.0 + nrm(ks[9], (L, DA_V_DIM), 0.05),
        "w_conv": nrm(ks[10], (L, SC_K, SC_W), SC_K ** -0.5),
        "w_pool": nrm(ks[11], (L, POOL_G, POOL_GW, POOL_GW), POOL_GW ** -0.5),
        "pool_scale": 1.0 + nrm(ks[12], (L, POOL_W), 0.1),
        "w_branch": nrm(ks[13], (L, N_BRANCH, BRANCH_W, D), BRANCH_W ** -0.5),
        "w_mgate": nrm(ks[14], (L, D, N_BRANCH * D), D ** -0.5),
        "b_mgate": nrm(ks[15], (L, N_BRANCH * D), 0.02),
        "w_out": nrm(ks[16], (L, D, D), D ** -0.5),
        "g_norm2": 1.0 + nrm(ks[17], (L, D), 0.05),
        "w_router": nrm(ks[18], (L, D, E), D ** -0.5),
        "b_router": nrm(ks[19], (L, E), 0.01),
        "w_e_gate": nrm(ks[20], (L, E, D, F), D ** -0.5),
        "b_e_gate": nrm(ks[21], (L, E, F), 0.01),
        "w_e_up": nrm(ks[22], (L, E, D, F), D ** -0.5),
        "b_e_up": nrm(ks[23], (L, E, F), 0.01),
        "w_e_down": nrm(ks[24], (L, E, F, D), F ** -0.5),
        "b_e_down": nrm(ks[25], (L, E, D), 0.01),
        "g_final": 1.0 + nrm(ks[26], (D,), 0.05),
    }


def reference(x, c, ctx, c_ctx, w_ada, b_ada, g_norm1, w_in, da_lambda, g_subln, w_conv, w_pool,
              pool_scale, w_branch, w_mgate, b_mgate, w_out, g_norm2, w_router, b_router,
              w_e_gate, b_e_gate, w_e_up, b_e_up, w_e_down, b_e_down, g_final):
    dt = x.dtype
    n_lat = x.shape[1]
    cos, sin = axial_rope(n_lat)
    cos = cos.astype(dt)
    sin = sin.astype(dt)

    for l in range(DEPTH):
        last = l == DEPTH - 1
        lam_init = 0.8 - 0.6 * math.exp(-0.3 * l)
        lam = diff_lambda(da_lambda[l], lam_init)
        m_lat = adaln(c, w_ada[l], b_ada[l])[:, None]
        m_ctx = adaln(c_ctx, w_ada[l], b_ada[l])

        h_lat = modulate(rmsnorm(x, g_norm1[l]), m_lat[..., 0, :], m_lat[..., 1, :])
        h_ctx = modulate(rmsnorm(ctx, g_norm1[l]), m_ctx[0], m_ctx[1])

        q, k, v, sx, sb, sc, fx, px = jnp.split(h_lat @ w_in[l], COL_SPLITS, axis=-1)
        if last:
            ck, cv = jnp.split(h_ctx @ w_in[l][:, KV_LO:KV_HI], [DA_QK_W], axis=-1)
        else:
            cq, ck, cv, csx, csb, csc, cfx, cpx = jnp.split(h_ctx @ w_in[l], COL_SPLITS, axis=-1)

        ck_h = split_qk(ck)
        cv_h = split_v(cv)
        q_lat = apply_rope(split_qk(q), cos, sin)
        k_all = jnp.concatenate([ck_h, apply_rope(split_qk(k), cos, sin)], axis=3)
        v_all = jnp.concatenate([cv_h, split_v(v)], axis=2)
        a_lat = diff_head_out(latent_diff_attention(q_lat, k_all, v_all, lam), g_subln[l], lam_init)

        y_lat = jnp.stack([
            a_lat,
            short_conv(sx, sb, sc, w_conv[l]),
            fourier_mix(fx),
            multiscale_pool(px, w_pool[l], pool_scale[l]),
        ], axis=2)
        x = x + m_lat[..., 2, :] * merge_branches(y_lat, h_lat, w_branch[l], w_mgate[l], b_mgate[l], w_out[l])

        if not last:
            a_ctx = diff_head_out(diff_attend(split_qk(cq), ck_h, cv_h, lam), g_subln[l], lam_init)
            y_ctx = jnp.stack([
                a_ctx,
                short_conv(csx, csb, csc, w_conv[l]),
                fourier_mix(cfx),
                multiscale_pool(cpx, w_pool[l], pool_scale[l]),
            ], axis=2)
            ctx = ctx + m_ctx[2] * merge_branches(y_ctx, h_ctx, w_branch[l], w_mgate[l], b_mgate[l], w_out[l])

        h2 = modulate(rmsnorm(x, g_norm2[l]), m_lat[..., 3, :], m_lat[..., 4, :])
        x = x + m_lat[..., 5, :] * moe_ffn(h2, w_router[l], b_router[l], w_e_gate[l], b_e_gate[l],
                                           w_e_up[l], b_e_up[l], w_e_down[l], b_e_down[l])
        if not last:
            h2c = modulate(rmsnorm(ctx, g_norm2[l]), m_ctx[3], m_ctx[4])
            ctx = ctx + m_ctx[5] * moe_ffn(h2c, w_router[l], b_router[l], w_e_gate[l], b_e_gate[l],
                                           w_e_up[l], b_e_up[l], w_e_down[l], b_e_down[l])

    return rmsnorm(x, g_final)
```

```python
import functools
import math

import numpy as np
import jax
import jax.numpy as jnp
from jax import lax
from jax.experimental import pallas as pl
from jax.experimental.pallas import tpu as pltpu

F32 = jnp.float32
BF = jnp.bfloat16

D = 1024
HEADS = 4
HEAD_DIM = 64
HEAD_W = 128
BW = 512
GRID_W = 64
EPS = 1e-6
ROPE_BASE = 10000.0
POOL_WINDOWS = (2, 4, 8, 16)
TOP_K = 4
SWIGLU_LIMIT = 7.0
SWIGLU_ALPHA = 1.702
FFT_N2 = 128

TM = 256
TME = 256
TQ = 256
TK = 512
VT_CHUNK = 256
HALO = 16
LANES = 128
QSCALE = (HEAD_DIM ** -0.5) * math.log2(math.e)
VMEM_LIMIT = 56 * 2 ** 20


def _cparams(sem, vmem=None):
    return pltpu.CompilerParams(dimension_semantics=sem, vmem_limit_bytes=vmem)


def _modnorm(x, g, shift, scale):
    y = x * lax.rsqrt(jnp.mean(x * x, axis=-1, keepdims=True) + EPS) * g
    return y * (1.0 + scale) + shift


def _sigmoid(x):
    return 1.0 / (1.0 + jnp.exp(-x))


def _adaln_body(c_ref, w_ref, b_ref, o_ref):
    c = c_ref[...]
    s = c * _sigmoid(c)
    o_ref[...] = jnp.dot(s.astype(BF), w_ref[...].astype(BF), preferred_element_type=F32) + b_ref[...]


def _adaln(c16, w_ada, b_ada):
    nl = w_ada.shape[0]
    tn = 1536
    return pl.pallas_call(
        _adaln_body,
        grid=(nl, 6 * D // tn),
        in_specs=[pl.BlockSpec((16, D), lambda l, j: (0, 0)),
                  pl.BlockSpec((None, D, tn), lambda l, j: (l, 0, j)),
                  pl.BlockSpec((None, 1, tn), lambda l, j: (l, 0, j))],
        out_specs=pl.BlockSpec((None, 16, tn), lambda l, j: (l, 0, j)),
        out_shape=jax.ShapeDtypeStruct((nl, 16, 6 * D), F32),
        compiler_params=_cparams(("arbitrary", "arbitrary"), VMEM_LIMIT),
    )(c16, w_ada, b_ada.reshape(nl, 1, 6 * D))


def _inproj_body(x_ref, mod_ref, g_ref, w_ref, cos_ref, sin_ref, cs_ref,
                 q_ref, k_ref, v_ref, u_ref, sb_ref, p_ref, qq_ref, px_ref):
    h = _modnorm(x_ref[...], g_ref[...], mod_ref[0:1, :], mod_ref[1:2, :]).astype(BF)

    def proj(c0):
        return jnp.dot(h, w_ref[:, c0:c0 + BW], preferred_element_type=F32)

    cos = jnp.concatenate([cos_ref[...]] * 4, axis=1)
    sin = jnp.concatenate([sin_ref[...]] * 4, axis=1)
    q_ref[...] = ((proj(0) * cos + proj(8 * BW) * sin) * QSCALE).astype(BF)
    k_ref[...] = (proj(BW) * cos + proj(9 * BW) * sin).astype(BF)
    v_ref[...] = proj(2 * BW).astype(BF)
    u_ref[...] = (proj(5 * BW) * proj(3 * BW)).astype(BF)
    sb_ref[...] = proj(4 * BW).astype(BF)
    px_ref[...] = proj(7 * BW).astype(BF)
    fx = proj(6 * BW).astype(BF)
    for g in range(4):
        pq = jnp.dot(fx[:, g * LANES:(g + 1) * LANES], cs_ref[...], preferred_element_type=F32)
        p_ref[:, g * LANES:(g + 1) * LANES] = pq[:, :LANES].astype(BF)
        qq_ref[:, g * LANES:(g + 1) * LANES] = pq[:, LANES:].astype(BF)


def _inproj(x_all, mods, g1, w_ext, cos_t, sin_t, cs_tab, dims):
    t_all, nlat, tps, tpc, nb = dims["t_all"], dims["nlat"], dims["tps"], dims["tpc"], dims["nb"]

    def tab_idx(i):
        return (jnp.where(i < nlat, i % tps, tps + (i - nlat) % tpc), 0)

    out = jax.ShapeDtypeStruct((t_all, BW), BF)
    row = pl.BlockSpec((TM, BW), lambda i: (i, 0))
    return pl.pallas_call(
        _inproj_body,
        grid=(t_all // TM,),
        in_specs=[pl.BlockSpec((TM, D), lambda i: (i, 0)),
                  pl.BlockSpec((None, 8, D), lambda i: (jnp.minimum(i // tps, nb), 0, 0)),
                  pl.BlockSpec((1, D), lambda i: (0, 0)),
                  pl.BlockSpec((D, 10 * BW), lambda i: (0, 0)),
                  pl.BlockSpec((TM, LANES), tab_idx),
                  pl.BlockSpec((TM, LANES), tab_idx),
                  pl.BlockSpec((LANES, 2 * LANES), lambda i: (0, 0))],
        out_specs=[row] * 8,
        out_shape=[out] * 8,
        compiler_params=_cparams(("arbitrary",), VMEM_LIMIT),
    )(x_all, mods, g1, w_ext, cos_t, sin_t, cs_tab)


def _attn_body(*refs, n_ctx_chunks, n_lat_chunks, lam_init):
    if n_lat_chunks:
        qt_ref, kc_ref, vtc_ref, kl_ref, vtl_ref, g_ref, dl_ref, o_ref = refs
    else:
        qt_ref, kc_ref, vtc_ref, g_ref, dl_ref, o_ref = refs
    qt = qt_ref[...]
    tq = qt.shape[1]
    row = lax.broadcasted_iota(jnp.int32, qt.shape, 0)
    zero = jnp.zeros_like(qt)
    q2 = jnp.concatenate([jnp.where(row < HEAD_DIM, qt, zero), jnp.where(row >= HEAD_DIM, qt, zero)], axis=1)

    def step(kc, vt, carry):
        m, l, acc = carry
        s = jnp.dot(kc, q2, preferred_element_type=F32)
        m_new = jnp.maximum(m, jnp.max(s, axis=0, keepdims=True))
        p = jnp.exp2(s - m_new)
        alpha = jnp.exp2(m - m_new)
        l = alpha * l + jnp.sum(p, axis=0, keepdims=True)
        acc = alpha * acc + jnp.dot(vt, p.astype(BF), preferred_element_type=F32)
        return m_new, l, acc

    def vt_chunk(ref, c0, n):
        parts = [ref[c0 + r] for r in range(n)]
        return parts[0] if n == 1 else jnp.concatenate(parts, axis=1)

    carry = (jnp.full((1, 2 * tq), -1e30, F32), jnp.zeros((1, 2 * tq), F32), jnp.zeros((HEAD_W, 2 * tq), F32))
    for c in range(n_ctx_chunks):
        carry = step(kc_ref[c * VT_CHUNK:(c + 1) * VT_CHUNK, :], vtc_ref[c], carry)
    if n_lat_chunks:
        per = TK // VT_CHUNK

        def body(j, carry):
            kc = kl_ref[pl.ds(pl.multiple_of(j * TK, TK), TK), :]
            return step(kc, vt_chunk(vtl_ref, j * per, per), carry)

        carry = lax.fori_loop(0, n_lat_chunks, body, carry)
    _, l, acc = carry
    o = acc / l
    dl = dl_ref[...]
    lam = (jnp.exp(jnp.sum(dl[0:1] * dl[1:2], axis=1, keepdims=True))
           - jnp.exp(jnp.sum(dl[2:3] * dl[3:4], axis=1, keepdims=True)) + lam_init)
    d = o[:, :tq] - lam * o[:, tq:]
    y = d * lax.rsqrt(jnp.mean(d * d, axis=0, keepdims=True) + EPS) * g_ref[...] * (1.0 - lam_init)
    o_ref[...] = y.astype(BF)


def _attention(qt, k, vt, g_sub_b, da_lambda, lam_init, dims, latent):
    nb, s, ctx, t_lat = dims["nb"], dims["s"], dims["ctx"], dims["t_lat"]
    cpb = ctx // VT_CHUNK
    common_tail = [pl.BlockSpec((HEAD_W, TQ), lambda b, h, qi: (0, 0)),
                   pl.BlockSpec((4, HEAD_DIM), lambda b, h, qi: (0, 0))]
    kc_spec = pl.BlockSpec((ctx, HEAD_W), lambda b, h, qi: (t_lat // ctx + b, h))
    vtc_spec = pl.BlockSpec((cpb, HEAD_W, VT_CHUNK), lambda b, h, qi: (t_lat // ctx + b, h, 0))
    if latent:
        nq = s // TQ
        in_specs = [pl.BlockSpec((HEAD_W, TQ), lambda b, h, qi: (h, b * nq + qi)), kc_spec, vtc_spec,
                    pl.BlockSpec((s, HEAD_W), lambda b, h, qi: (b, h)),
                    pl.BlockSpec((s // VT_CHUNK, HEAD_W, VT_CHUNK), lambda b, h, qi: (b, h, 0))] + common_tail
        args = (qt, k, vt, k, vt, g_sub_b, da_lambda)
        out_cols, col0, n_lat_chunks = t_lat, 0, s // TK
    else:
        nq = ctx // TQ
        in_specs = [pl.BlockSpec((HEAD_W, TQ), lambda b, h, qi: (h, t_lat // TQ + b * nq + qi)),
                    kc_spec, vtc_spec] + common_tail
        args = (qt, k, vt, g_sub_b, da_lambda)
        out_cols, col0, n_lat_chunks = nb * ctx, 0, 0
    del col0
    return pl.pallas_call(
        functools.partial(_attn_body, n_ctx_chunks=cpb, n_lat_chunks=n_lat_chunks, lam_init=lam_init),
        grid=(nb, HEADS, nq),
        in_specs=in_specs,
        out_specs=pl.BlockSpec((HEAD_W, TQ), lambda b, h, qi: (h, b * nq + qi)),
        out_shape=jax.ShapeDtypeStruct((HEADS * HEAD_W, out_cols), BF),
        compiler_params=_cparams(("arbitrary", "arbitrary", "arbitrary"), VMEM_LIMIT),
    )(*args)


def _fft_a_body(fa_ref, p_ref, q_ref, zr_ref, zi_ref):
    x = jnp.concatenate([p_ref[...], q_ref[...]], axis=0)
    z = jnp.dot(fa_ref[...], x, preferred_element_type=F32)
    zr_ref[...] = z[:FFT_N2].astype(BF)
    zi_ref[...] = z[FFT_N2:].astype(BF)


def _fft_b_body(fb_ref, tc_ref, ts_ref, zr_ref, zi_ref, o_ref):
    nk = zr_ref.shape[0]
    parts_r, parts_i = [], []
    for j in range(nk):
        zr = zr_ref[j].astype(F32)
        zi = zi_ref[j].astype(F32)
        tc = jnp.concatenate([tc_ref[j]] * 4, axis=1)
        ts = jnp.concatenate([ts_ref[j]] * 4, axis=1)
        parts_r.append((zr * tc - zi * ts).astype(BF))
        parts_i.append((zr * ts + zi * tc).astype(BF))
    rhs = jnp.concatenate([jnp.concatenate(parts_r, axis=1), jnp.concatenate(parts_i, axis=1)], axis=0)
    o_ref[...] = jnp.dot(fb_ref[...], rhs, preferred_element_type=F32).astype(BF)


def _fft_ctx_body(f_ref, p_ref, q_ref, o_ref):
    x = jnp.concatenate([p_ref[...], q_ref[...]], axis=0)
    o_ref[...] = jnp.dot(f_ref[...], x, preferred_element_type=F32).astype(BF)


def _fft_tables(s, ctx):
    n1, n2 = s // FFT_N2, FFT_N2
    norm = 1.0 / math.sqrt(s * LANES)
    a = 2 * np.pi * np.outer(np.arange(n2), np.arange(n2)) / n2
    c, sn = np.cos(a), np.sin(a)
    fa = np.block([[c, -sn], [sn, c]]) * norm
    tw = 2 * np.pi * np.outer(np.arange(n2), np.arange(n1)) / s
    tc = np.repeat(np.cos(tw)[:, :, None], LANES, axis=2)
    ts = np.repeat(np.sin(tw)[:, :, None], LANES, axis=2)
    b = 2 * np.pi * np.outer(np.arange(n1), np.arange(n1)) / n1
    fb = np.concatenate([np.cos(b), -np.sin(b)], axis=1)
    ch = 2 * np.pi * np.outer(np.arange(LANES), np.arange(LANES)) / LANES
    cs = np.concatenate([np.cos(ch), np.sin(ch)], axis=1)
    cx = 2 * np.pi * np.outer(np.arange(ctx), np.arange(ctx)) / ctx
    fctx = np.concatenate([np.cos(cx), -np.sin(cx)], axis=1) / math.sqrt(ctx * LANES)
    return dict(fa=jnp.asarray(fa, BF), tc=jnp.asarray(tc, F32), ts=jnp.asarray(ts, F32),
                fb=jnp.asarray(fb, BF), cs=jnp.asarray(cs, BF), fctx=jnp.asarray(fctx, BF))


def _fourier_latent(p, q, tabs, dims):
    nb, s = dims["nb"], dims["s"]
    n1 = s // FFT_N2
    wide = n1 * BW
    wc = min(4096, wide)
    p2 = p.reshape(-1, wide)
    q2 = q.reshape(-1, wide)
    zshape = jax.ShapeDtypeStruct((nb * FFT_N2, wide), BF)
    zr, zi = pl.pallas_call(
        _fft_a_body,
        grid=(nb, wide // wc),
        in_specs=[pl.BlockSpec((2 * FFT_N2, 2 * FFT_N2), lambda b, j: (0, 0)),
                  pl.BlockSpec((FFT_N2, wc), lambda b, j: (b, j)),
                  pl.BlockSpec((FFT_N2, wc), lambda b, j: (b, j))],
        out_specs=[pl.BlockSpec((FFT_N2, wc), lambda b, j: (b, j))] * 2,
        out_shape=[zshape, zshape],
        compiler_params=_cparams(("arbitrary", "arbitrary"), VMEM_LIMIT),
    )(tabs["fa"], p2, q2)
    kb = 8
    zr4 = zr.reshape(nb, FFT_N2, n1, BW)
    zi4 = zi.reshape(nb, FFT_N2, n1, BW)
    y = pl.pallas_call(
        _fft_b_body,
        grid=(FFT_N2 // kb, nb),
        in_specs=[pl.BlockSpec((n1, 2 * n1), lambda kk, b: (0, 0)),
                  pl.BlockSpec((kb, n1, LANES), lambda kk, b: (kk, 0, 0)),
                  pl.BlockSpec((kb, n1, LANES), lambda kk, b: (kk, 0, 0)),
                  pl.BlockSpec((None, kb, n1, BW), lambda kk, b: (b, kk, 0, 0)),
                  pl.BlockSpec((None, kb, n1, BW), lambda kk, b: (b, kk, 0, 0))],
        out_specs=pl.BlockSpec((None, n1, kb * BW), lambda kk, b: (b, 0, kk)),
        out_shape=jax.ShapeDtypeStruct((nb, n1, FFT_N2 * BW), BF),
        compiler_params=_cparams(("arbitrary", "arbitrary"), VMEM_LIMIT),
    )(tabs["fb"], tabs["tc"], tabs["ts"], zr4, zi4)
    return y.reshape(nb * s, BW)


def _fourier_ctx(p, q, tabs, dims):
    nb, ctx, t_lat = dims["nb"], dims["ctx"], dims["t_lat"]
    return pl.pallas_call(
        _fft_ctx_body,
        grid=(nb,),
        in_specs=[pl.BlockSpec((ctx, 2 * ctx), lambda b: (0, 0)),
                  pl.BlockSpec((ctx, BW), lambda b: (t_lat // ctx + b, 0)),
                  pl.BlockSpec((ctx, BW), lambda b: (t_lat // ctx + b, 0))],
        out_specs=pl.BlockSpec((ctx, BW), lambda b: (b, 0)),
        out_shape=jax.ShapeDtypeStruct((nb * ctx, BW), BF),
        compiler_params=_cparams(("arbitrary",), VMEM_LIMIT),
    )(tabs["fctx"], p, q)


def _merge_body(x_ref, mod_ref, g1_ref, a_ref, u_ref, up_ref, un_ref, sb_ref, yf_ref, px_ref, pp_ref, pn_ref,
                wconv_ref, wpool_ref, pscale_ref, wb_ref, wg_ref, bg_ref, wo_ref, g2_ref, wr_ref, br_ref,
                xo_ref, h2_ref, route_ref, cnt_ref, carry_ref, *, nlat, tps, tpc, s, ctx):
    i = pl.program_id(0)
    is_lat = i < nlat
    j = jnp.where(is_lat, i % tps, (i - nlat) % tpc)
    first = j == 0
    last = j == jnp.where(is_lat, tps, tpc) - 1
    seq_len = jnp.where(is_lat, s, ctx)

    @pl.when(i == 0)
    def _():
        carry_ref[...] = jnp.zeros_like(carry_ref)

    x = x_ref[...]
    h = _modnorm(x, g1_ref[...], mod_ref[0:1, :], mod_ref[1:2, :]).astype(BF)
    rowi = lax.broadcasted_iota(jnp.int32, (TM, 1), 0)

    u = u_ref[...].astype(F32)
    u_prev = jnp.where(first, 0.0, up_ref[...].astype(F32)[HALO - 1:HALO, :])
    u_next = jnp.where(last, 0.0, un_ref[...].astype(F32)[0:1, :])
    u_dn = jnp.where(rowi == 0, u_prev, pltpu.roll(u, 1, axis=0))
    u_up = jnp.where(rowi == TM - 1, u_next, pltpu.roll(u, TM - 1, axis=0))
    wc = wconv_ref[...]
    y_conv = sb_ref[...].astype(F32) * (wc[0:1] * u_dn + wc[1:2] * u + wc[2:3] * u_up)

    px = px_ref[...].astype(F32)
    ext = jnp.concatenate([jnp.where(first, 0.0, pp_ref[...].astype(F32)), px,
                           jnp.where(last, 0.0, pn_ref[...].astype(F32))], axis=0)
    ext_len = TM + 2 * HALO
    pos = j * TM + rowi
    pooled_proj = []
    for g, wd in enumerate(POOL_WINDOWS):
        e = ext[:, g * LANES:(g + 1) * LANES]
        win = e + pltpu.roll(e, 1, axis=0)
        half = 1
        while 2 * half < wd:
            win = pltpu.roll(win, half, axis=0) + pltpu.roll(win, ext_len - half, axis=0)
            half *= 2
        win = win[HALO:HALO + TM]
        cnt = (jnp.minimum(pos + wd // 2, seq_len) - jnp.maximum(pos - wd // 2, 0)).astype(F32)
        pooled = (win / cnt - px[:, g * LANES:(g + 1) * LANES]).astype(BF)
        pooled_proj.append(jnp.dot(pooled, wpool_ref[g], preferred_element_type=F32))
    y_pool = jnp.concatenate(pooled_proj, axis=1) * pscale_ref[...]

    ys = (a_ref[...], y_conv.astype(BF), yf_ref[...], y_pool.astype(BF))
    acc = jnp.zeros((TM, D), F32)
    for kk in range(4):
        pk = jnp.dot(ys[kk], wb_ref[kk], preferred_element_type=F32)
        gk = _sigmoid(jnp.dot(h, wg_ref[:, kk * D:(kk + 1) * D], preferred_element_type=F32)
                      + bg_ref[:, kk * D:(kk + 1) * D])
        acc = acc + gk * pk
    xn = x + mod_ref[2:3, :] * jnp.dot(acc.astype(BF), wo_ref[...], preferred_element_type=F32)
    xo_ref[...] = xn

    h2 = _modnorm(xn, g2_ref[...], mod_ref[3:4, :], mod_ref[4:5, :])
    h2_ref[...] = h2
    logits = jnp.dot(h2.astype(BF), wr_ref[...], preferred_element_type=F32) + br_ref[...]
    lane = lax.broadcasted_iota(jnp.int32, (TM, LANES), 1).astype(F32)
    cur = logits
    vals, idxs = [], []
    for _ in range(TOP_K):
        mx = jnp.max(cur, axis=1, keepdims=True)
        idx = jnp.min(jnp.where(cur == mx, lane, float(LANES)), axis=1, keepdims=True)
        vals.append(mx)
        idxs.append(idx)
        cur = jnp.where(lane == idx, -jnp.inf, cur)
    ex = [jnp.exp(v - vals[0]) for v in vals]
    den = ex[0] + ex[1] + ex[2] + ex[3]
    member = jnp.zeros((TM, LANES), F32)
    for idx in idxs:
        member = member + jnp.where(lane == idx, 1.0, 0.0)
    tri = (lax.broadcasted_iota(jnp.int32, (TM, TM), 0) > lax.broadcasted_iota(jnp.int32, (TM, TM), 1))
    before = jnp.dot(jnp.where(tri, 1.0, 0.0).astype(BF), member.astype(BF), preferred_element_type=F32)
    rank_all = carry_ref[0:1, :] + before
    route = jnp.zeros((TM, LANES), F32)
    for r in range(TOP_K):
        rank = jnp.sum(jnp.where(lane == idxs[r], rank_all, 0.0), axis=1, keepdims=True)
        route = (route + jnp.where(lane == float(r), idxs[r], 0.0)
                 + jnp.where(lane == float(TOP_K + r), ex[r] / den, 0.0)
                 + jnp.where(lane == float(2 * TOP_K + r), rank, 0.0))
    route_ref[...] = route
    carry_ref[...] = carry_ref[...] + jnp.sum(member, axis=0, keepdims=True)
    cnt_ref[...] = carry_ref[...]


def _merge(x_all, mods, g1, a, u, sb, yf, px, wconv, wpool, pscale, wb, wg, bg, wo, g2, wr, br, dims, rows):
    nlat, tps, tpc, nb = dims["nlat"], dims["tps"], dims["tpc"], dims["nb"]
    nhalo = TM // HALO
    last_halo = u.shape[0] // HALO - 1
    tile = lambda w: pl.BlockSpec((TM, w), lambda i: (i, 0))
    prev = pl.BlockSpec((HALO, BW), lambda i: (jnp.maximum(i * nhalo - 1, 0), 0))
    nxt = pl.BlockSpec((HALO, BW), lambda i: (jnp.minimum((i + 1) * nhalo, last_halo), 0))
    const = lambda shape: pl.BlockSpec(shape, lambda i: (0,) * len(shape))
    return pl.pallas_call(
        functools.partial(_merge_body, nlat=nlat, tps=tps, tpc=tpc, s=dims["s"], ctx=dims["ctx"]),
        grid=(rows // TM,),
        in_specs=[tile(D),
                  pl.BlockSpec((None, 8, D), lambda i: (jnp.minimum(i // tps, nb), 0, 0)),
                  const((1, D)), tile(BW), tile(BW), prev, nxt, tile(BW), tile(BW), tile(BW), prev, nxt,
                  const((8, BW)), const((4, LANES, LANES)), const((1, BW)), const((4, BW, D)),
                  const((D, 4 * D)), const((1, 4 * D)), const((D, D)), const((1, D)),
                  const((D, LANES)), const((1, LANES))],
        out_specs=[tile(D), tile(D), tile(LANES), pl.BlockSpec((8, LANES), lambda i: (0, 0))],
        out_shape=[jax.ShapeDtypeStruct((rows, D), F32), jax.ShapeDtypeStruct((rows, D), F32),
                   jax.ShapeDtypeStruct((rows, LANES), F32), jax.ShapeDtypeStruct((8, LANES), F32)],
        scratch_shapes=[pltpu.VMEM((8, LANES), F32)],
        compiler_params=_cparams(("arbitrary",), VMEM_LIMIT),
    )(x_all, mods, g1, a, u, u, u, sb, yf, px, px, px, wconv, wpool, pscale, wb, wg, bg, wo, g2, wr, br)


def _row_copy_wait(src_ref, dst_ref, sem, n):
    def body(_, c):
        pltpu.make_async_copy(src_ref.at[pl.ds(0, 1)], dst_ref.at[pl.ds(0, 1)], sem).wait()
        return c
    lax.fori_loop(0, n, body, 0)


def _dispatch_body(dest_ref, h_ref, xs_ref, sem):
    def body(t, c):
        for r in range(TOP_K):
            pltpu.make_async_copy(h_ref.at[pl.ds(t, 1)], xs_ref.at[pl.ds(dest_ref[0, t * TOP_K + r], 1)], sem).start()
        return c
    lax.fori_loop(0, TM, body, 0)
    _row_copy_wait(h_ref, xs_ref, sem, TM * TOP_K)


def _dispatch(dest3, h2, n_slots):
    rows = h2.shape[0]
    return pl.pallas_call(
        _dispatch_body,
        grid=(rows // TM,),
        in_specs=[pl.BlockSpec((None, 1, TM * TOP_K), lambda i: (i, 0, 0), memory_space=pltpu.SMEM),
                  pl.BlockSpec((TM, D), lambda i: (i, 0))],
        out_specs=pl.BlockSpec(memory_space=pl.ANY),
        out_shape=jax.ShapeDtypeStruct((n_slots, D), F32),
        scratch_shapes=[pltpu.SemaphoreType.DMA(())],
        compiler_params=_cparams(("arbitrary",), VMEM_LIMIT),
    )(dest3, h2)


def _expert_body(te_ref, na_ref, xs_ref, wg_ref, bg_ref, wu_ref, bu_ref, wd_ref, bd_ref, y_ref):
    @pl.when(pl.program_id(0) < na_ref[0])
    def _():
        x = xs_ref[...].astype(BF)
        a = jnp.minimum(jnp.dot(x, wg_ref[...], preferred_element_type=F32) + bg_ref[...], SWIGLU_LIMIT)
        u = jnp.clip(jnp.dot(x, wu_ref[...], preferred_element_type=F32) + bu_ref[...], -SWIGLU_LIMIT, SWIGLU_LIMIT)
        act = a * _sigmoid(SWIGLU_ALPHA * a) * (u + 1.0)
        y_ref[...] = jnp.dot(act.astype(BF), wd_ref[...], preferred_element_type=F32) + bd_ref[...]


def _experts(tile_expert, n_active, xs, wg, bg, wu, bu, wd, bd):
    n_slots = xs.shape[0]
    f = wg.shape[2]
    slot = lambda j, te, na: (jnp.minimum(j, na[0] - 1), 0)
    wsel = lambda j, te, na: (te[j], 0, 0)
    return pl.pallas_call(
        _expert_body,
        grid_spec=pltpu.PrefetchScalarGridSpec(
            num_scalar_prefetch=2,
            grid=(n_slots // TME,),
            in_specs=[pl.BlockSpec((TME, D), slot),
                      pl.BlockSpec((None, D, f), wsel), pl.BlockSpec((None, 1, f), wsel),
                      pl.BlockSpec((None, D, f), wsel), pl.BlockSpec((None, 1, f), wsel),
                      pl.BlockSpec((None, f, D), wsel), pl.BlockSpec((None, 1, D), wsel)],
            out_specs=pl.BlockSpec((TME, D), slot)),
        out_shape=jax.ShapeDtypeStruct((n_slots, D), F32),
        compiler_params=_cparams(("arbitrary",), VMEM_LIMIT),
    )(tile_expert, n_active, xs, wg, bg, wu, bu, wd, bd)


def _combine_body(dest_ref, x_ref, mod_ref, route_ref, gf_ref, y_ref, o_ref, buf, sem, *, final):
    def body(t, c):
        for r in range(TOP_K):
            pltpu.make_async_copy(y_ref.at[pl.ds(dest_ref[0, t * TOP_K + r], 1)], buf.at[r, pl.ds(t, 1)], sem).start()
        return c
    lax.fori_loop(0, TM, body, 0)
    _row_copy_wait(y_ref, buf.at[0], sem, TM * TOP_K)
    route = route_ref[...]
    acc = jnp.zeros((TM, D), F32)
    for r in range(TOP_K):
        acc = acc + route[:, TOP_K + r:TOP_K + r + 1] * buf[r]
    xn = x_ref[...] + mod_ref[5:6, :] * acc
    if final:
        xn = xn * lax.rsqrt(jnp.mean(xn * xn, axis=-1, keepdims=True) + EPS) * gf_ref[...]
    o_ref[...] = xn


def _combine(dest3, x_mid, mods, route, g_final, y, dims, final):
    rows = x_mid.shape[0]
    tps, nb = dims["tps"], dims["nb"]
    return pl.pallas_call(
        functools.partial(_combine_body, final=final),
        grid=(rows // TM,),
        in_specs=[pl.BlockSpec((None, 1, TM * TOP_K), lambda i: (i, 0, 0), memory_space=pltpu.SMEM),
                  pl.BlockSpec((TM, D), lambda i: (i, 0)),
                  pl.BlockSpec((None, 8, D), lambda i: (jnp.minimum(i // tps, nb), 0, 0)),
                  pl.BlockSpec((TM, LANES), lambda i: (i, 0)),
                  pl.BlockSpec((1, D), lambda i: (0, 0)),
                  pl.BlockSpec(memory_space=pl.ANY)],
        out_specs=pl.BlockSpec((TM, D), lambda i: (i, 0)),
        out_shape=jax.ShapeDtypeStruct((rows, D), F32),
        scratch_shapes=[pltpu.VMEM((TOP_K, TM, D), F32), pltpu.SemaphoreType.DMA(())],
        compiler_params=_cparams(("arbitrary",), VMEM_LIMIT),
    )(dest3, x_mid, mods, route, g_final, y)


def _moe(x_mid, h2, route, counts, mods, g_final, wg, bg, wu, bu, wd, bd, dims, final):
    rows = x_mid.shape[0]
    n_exp = wg.shape[0]
    n_slots = rows * TOP_K + n_exp * TME
    e = route[:, 0:TOP_K].astype(jnp.int32)
    rank = route[:, 2 * TOP_K:3 * TOP_K].astype(jnp.int32)
    cnt = counts[0, :n_exp].astype(jnp.int32)
    cnt_pad = ((cnt + TME - 1) // TME) * TME
    offs_end = jnp.cumsum(cnt_pad)
    offs = offs_end - cnt_pad
    onehot = e[:, :, None] == jnp.arange(n_exp, dtype=jnp.int32)[None, None, :]
    dest = jnp.sum(jnp.where(onehot, offs[None, None, :], 0), axis=-1) + rank
    dest3 = dest.reshape(rows // TM, 1, TM * TOP_K)
    tile_start = jnp.arange(n_slots // TME, dtype=jnp.int32) * TME
    tile_expert = jnp.minimum(jnp.sum(tile_start[:, None] >= offs_end[None, :], axis=1), n_exp - 1).astype(jnp.int32)
    n_active = (offs_end[-1:] // TME).astype(jnp.int32)
    xs = _dispatch(dest3, h2, n_slots)
    y = _experts(tile_expert, n_active, xs, wg, bg, wu, bu, wd, bd)
    return _combine(dest3, x_mid, mods, route, g_final, y, dims, final)


def _rope_tables(s, ctx):
    rows = s // GRID_W
    row = np.repeat(np.arange(rows), GRID_W).astype(np.float32)
    col = np.tile(np.arange(GRID_W), rows).astype(np.float32)
    half = HEAD_DIM // 2
    inv = (np.float32(ROPE_BASE) ** (-np.arange(0, half, 2, dtype=np.float32) / half)).astype(np.float32)
    ar = row[:, None] * inv
    ac = col[:, None] * inv
    ang = np.concatenate([ar, ar, ac, ac], axis=-1)
    cos = np.concatenate([np.cos(ang), np.ones((ctx, HEAD_DIM), np.float32)], axis=0)
    sin = np.concatenate([np.sin(ang), np.zeros((ctx, HEAD_DIM), np.float32)], axis=0)
    return (jnp.asarray(np.tile(cos, (1, 2)), F32), jnp.asarray(np.tile(sin, (1, 2)), F32))


def _rot_columns(w):
    d, n = w.shape
    w4 = w.reshape(d, n // HEAD_DIM, 4, HEAD_DIM // 4)
    return jnp.stack([-w4[:, :, 1], w4[:, :, 0], -w4[:, :, 3], w4[:, :, 2]], axis=2).reshape(d, n)


def kernel(x, c, ctx, c_ctx, w_ada, b_ada, g_norm1, w_in, da_lambda, g_subln, w_conv, w_pool, pool_scale,
           w_branch, w_mgate, b_mgate, w_out, g_norm2, w_router, b_router, w_e_gate, b_e_gate, w_e_up,
           b_e_up, w_e_down, b_e_down, g_final):
    nb, s, _ = x.shape
    nctx = ctx.shape[1]
    depth = w_in.shape[0]
    n_exp = w_router.shape[2]
    t_lat, t_ctx = nb * s, nb * nctx
    dims = dict(nb=nb, s=s, ctx=nctx, t_lat=t_lat, t_all=t_lat + t_ctx, nlat=t_lat // TM, tps=s // TM, tpc=nctx // TM)
    assert s % TK == 0 and s % (FFT_N2 * 8) == 0 and nctx % TM == 0 and nctx % VT_CHUNK == 0 and nb < 16

    cos_t, sin_t = _rope_tables(s, nctx)
    tabs = _fft_tables(s, nctx)
    c16 = jnp.concatenate([c, c_ctx[None, :], jnp.zeros((15 - nb, D), F32)], axis=0)
    mods_all = _adaln(c16, w_ada, b_ada).reshape(depth, 16, 6, D)
    mods_all = jnp.concatenate([mods_all, jnp.zeros((depth, 16, 2, D), F32)], axis=2)

    x_all = jnp.concatenate([x.reshape(t_lat, D), ctx.reshape(t_ctx, D)], axis=0)
    for l in range(depth):
        last = l == depth - 1
        lam_init = 0.8 - 0.6 * math.exp(-0.3 * l)
        mods = mods_all[l]
        w_l = w_in[l]
        w_ext = jnp.concatenate([w_l, _rot_columns(w_l[:, :BW]), _rot_columns(w_l[:, BW:2 * BW])], axis=1).astype(BF)
        q, k, v, u, sb, p, qq, px = _inproj(x_all, mods, g_norm1[l][None, :], w_ext, cos_t, sin_t, tabs["cs"], dims)

        qt = q.T
        vt = v.reshape(-1, VT_CHUNK, HEADS * HEAD_W).transpose(0, 2, 1)
        g_sub_b = jnp.broadcast_to(g_subln[l][:, None], (HEAD_W, TQ))
        a_t = _attention(qt, k, vt, g_sub_b, da_lambda[l], lam_init, dims, latent=True)
        yf = _fourier_latent(p, qq, tabs, dims)
        if last:
            a, rows = a_t.T, t_lat
        else:
            a_ctx_t = _attention(qt, k, vt, g_sub_b, da_lambda[l], lam_init, dims, latent=False)
            a = jnp.concatenate([a_t, a_ctx_t], axis=1).T
            yf = jnp.concatenate([yf, _fourier_ctx(p, qq, tabs, dims)], axis=0)
            rows = t_lat + t_ctx

        wconv = jnp.concatenate([w_conv[l], jnp.zeros((5, BW), F32)], axis=0)
        wr = jnp.concatenate([w_router[l], jnp.zeros((D, LANES - n_exp), F32)], axis=1).astype(BF)
        br = jnp.concatenate([b_router[l], jnp.full((LANES - n_exp,), -1e30, F32)])[None, :]
        x_mid, h2, route, counts = _merge(
            x_all, mods, g_norm1[l][None, :], a, u, sb, yf, px, wconv, w_pool[l].astype(BF), pool_scale[l][None, :],
            w_branch[l].astype(BF), w_mgate[l].astype(BF), b_mgate[l][None, :], w_out[l].astype(BF),
            g_norm2[l][None, :], wr, br, dims, rows)
        x_new = _moe(x_mid, h2, route, counts, mods, g_final[None, :],
                     w_e_gate[l].astype(BF), b_e_gate[l][:, None, :], w_e_up[l].astype(BF), b_e_up[l][:, None, :],
                     w_e_down[l].astype(BF), b_e_down[l][:, None, :], dims, final=last)
        if last:
            return x_new.reshape(nb, s, D)
        x_all = x_new
```

```python
import functools
import math

import numpy as np
import jax
import jax.numpy as jnp
from jax import lax
from jax.experimental import pallas as pl
from jax.experimental.pallas import tpu as pltpu

F32 = jnp.float32
BF = jnp.bfloat16

D = 1024
HEADS = 4
HEAD_DIM = 64
HEAD_W = 128
BW = 512
GRID_W = 64
EPS = 1e-6
ROPE_BASE = 10000.0
POOL_WINDOWS = (2, 4, 8, 16)
TOP_K = 4
SWIGLU_LIMIT = 7.0
SWIGLU_ALPHA = 1.702
FFT_N2 = 128

TM = 256
TME = 256
TQ = 256
TK = 512
VT_CHUNK = 256
HALO = 16
LANES = 128
QSCALE = (HEAD_DIM ** -0.5) * math.log2(math.e)
VMEM_LIMIT = 56 * 2 ** 20


def _cparams(sem, vmem=None):
    return pltpu.CompilerParams(dimension_semantics=sem, vmem_limit_bytes=vmem)


def _modnorm(x, g, shift, scale):
    y = x * lax.rsqrt(jnp.mean(x * x, axis=-1, keepdims=True) + EPS) * g
    return y * (1.0 + scale) + shift


def _sigmoid(x):
    return 1.0 / (1.0 + jnp.exp(-x))


def _adaln_body(c_ref, w_ref, b_ref, o_ref):
    c = c_ref[...]
    s = c * _sigmoid(c)
    o_ref[...] = jnp.dot(s.astype(BF), w_ref[...].astype(BF), preferred_element_type=F32) + b_ref[...]


def _adaln(c16, w_ada, b_ada):
    nl = w_ada.shape[0]
    tn = 1536
    return pl.pallas_call(
        _adaln_body,
        grid=(nl, 6 * D // tn),
        in_specs=[pl.BlockSpec((16, D), lambda l, j: (0, 0)),
                  pl.BlockSpec((None, D, tn), lambda l, j: (l, 0, j)),
                  pl.BlockSpec((None, 1, tn), lambda l, j: (l, 0, j))],
        out_specs=pl.BlockSpec((None, 16, tn), lambda l, j: (l, 0, j)),
        out_shape=jax.ShapeDtypeStruct((nl, 16, 6 * D), F32),
        compiler_params=_cparams(("arbitrary", "arbitrary"), VMEM_LIMIT),
    )(c16, w_ada, b_ada.reshape(nl, 1, 6 * D))


def _inproj_body(x_ref, mod_ref, g_ref, w_ref, cos_ref, sin_ref, cs_ref,
                 q_ref, k_ref, v_ref, u_ref, sb_ref, p_ref, qq_ref, px_ref):
    h = _modnorm(x_ref[...], g_ref[...], mod_ref[0:1, :], mod_ref[1:2, :]).astype(BF)

    def proj(c0):
        return jnp.dot(h, w_ref[:, c0:c0 + BW], preferred_element_type=F32)

    cos = jnp.concatenate([cos_ref[...]] * 4, axis=1)
    sin = jnp.concatenate([sin_ref[...]] * 4, axis=1)
    q_ref[...] = ((proj(0) * cos + proj(8 * BW) * sin) * QSCALE).astype(BF)
    k_ref[...] = (proj(BW) * cos + proj(9 * BW) * sin).astype(BF)
    v_ref[...] = proj(2 * BW).astype(BF)
    u_ref[...] = (proj(5 * BW) * proj(3 * BW)).astype(BF)
    sb_ref[...] = proj(4 * BW).astype(BF)
    px_ref[...] = proj(7 * BW).astype(BF)
    fx = proj(6 * BW).astype(BF)
    for g in range(4):
        pq = jnp.dot(fx[:, g * LANES:(g + 1) * LANES], cs_ref[...], preferred_element_type=F32)
        p_ref[:, g * LANES:(g + 1) * LANES] = pq[:, :LANES].astype(BF)
        qq_ref[:, g * LANES:(g + 1) * LANES] = pq[:, LANES:].astype(BF)


def _inproj(x_all, mods, g1, w_ext, cos_t, sin_t, cs_tab, dims):
    t_all, nlat, tps, tpc, nb = dims["t_all"], dims["nlat"], dims["tps"], dims["tpc"], dims["nb"]

    def tab_idx(i):
        return (jnp.where(i < nlat, i % tps, tps + (i - nlat) % tpc), 0)

    out = jax.ShapeDtypeStruct((t_all, BW), BF)
    row = pl.BlockSpec((TM, BW), lambda i: (i, 0))
    return pl.pallas_call(
        _inproj_body,
        grid=(t_all // TM,),
        in_specs=[pl.BlockSpec((TM, D), lambda i: (i, 0)),
                  pl.BlockSpec((None, 8, D), lambda i: (jnp.minimum(i // tps, nb), 0, 0)),
                  pl.BlockSpec((1, D), lambda i: (0, 0)),
                  pl.BlockSpec((D, 10 * BW), lambda i: (0, 0)),
                  pl.BlockSpec((TM, LANES), tab_idx),
                  pl.BlockSpec((TM, LANES), tab_idx),
                  pl.BlockSpec((LANES, 2 * LANES), lambda i: (0, 0))],
        out_specs=[row] * 8,
        out_shape=[out] * 8,
        compiler_params=_cparams(("arbitrary",), VMEM_LIMIT),
    )(x_all, mods, g1, w_ext, cos_t, sin_t, cs_tab)


def _attn_body(*refs, n_ctx_chunks, n_lat_chunks, lam_init):
    if n_lat_chunks:
        qt_ref, kc_ref, vtc_ref, kl_ref, vtl_ref, g_ref, dl_ref, o_ref, acc_ref, s_buf, p_buf = refs
    else:
        qt_ref, kc_ref, vtc_ref, g_ref, dl_ref, o_ref, acc_ref = refs
    qt = qt_ref[...]
    tq = qt.shape[1]
    row = lax.broadcasted_iota(jnp.int32, qt.shape, 0)
    zero = jnp.zeros_like(qt)
    q2 = jnp.concatenate([jnp.where(row < HEAD_DIM, qt, zero), jnp.where(row >= HEAD_DIM, qt, zero)], axis=1)

    def scores(kc):
        return jnp.dot(kc, q2, preferred_element_type=F32)

    def softmax(s, m, l):
        m_new = jnp.maximum(m, jnp.max(s, axis=0, keepdims=True))
        p = jnp.exp2(s - m_new)
        alpha = jnp.exp2(m - m_new)
        return p.astype(BF), m_new, alpha * l + jnp.sum(p, axis=0, keepdims=True), alpha

    def accumulate(alpha, vt, p):
        acc_ref[...] = alpha * acc_ref[...] + jnp.dot(vt, p, preferred_element_type=F32)

    m = jnp.full((1, 2 * tq), -1e30, F32)
    l = jnp.zeros((1, 2 * tq), F32)
    acc_ref[...] = jnp.zeros_like(acc_ref)
    for c in range(n_ctx_chunks):
        p, m, l, alpha = softmax(scores(kc_ref[c * VT_CHUNK:(c + 1) * VT_CHUNK, :]), m, l)
        accumulate(alpha, vtc_ref[c], p)
    if n_lat_chunks:
        per = TK // VT_CHUNK
        n = n_lat_chunks

        def keys(j):
            return kl_ref[pl.ds(pl.multiple_of(j * TK, TK), TK), :]

        def values(j):
            parts = [vtl_ref[j * per + r] for r in range(per)]
            return parts[0] if per == 1 else jnp.concatenate(parts, axis=1)

        def half_step(j, prev, cur, carry):
            m, l, alpha_prev = carry
            accumulate(alpha_prev, values(j - 1), p_buf[prev])
            s_buf[prev] = scores(keys(jnp.minimum(j + 1, n - 1)))
            p, m, l, alpha = softmax(s_buf[cur], m, l)
            p_buf[cur] = p
            return m, l, alpha

        s_buf[0] = scores(keys(0))
        p, m, l, alpha = softmax(s_buf[0], m, l)
        p_buf[0] = p
        s_buf[1] = scores(keys(1))

        def body(i, carry):
            carry = half_step(2 * i + 1, 0, 1, carry)
            return half_step(2 * i + 2, 1, 0, carry)

        carry = lax.fori_loop(0, (n - 1) // 2, body, (m, l, alpha))
        if (n - 1) % 2:
            carry = half_step(n - 1, 0, 1, carry)
        m, l, alpha = carry
        accumulate(alpha, values(n - 1), p_buf[(n - 1) % 2])
    o = acc_ref[...] / l
    dl = dl_ref[...]
    lam = (jnp.exp(jnp.sum(dl[0:1] * dl[1:2], axis=1, keepdims=True))
           - jnp.exp(jnp.sum(dl[2:3] * dl[3:4], axis=1, keepdims=True)) + lam_init)
    d = o[:, :tq] - lam * o[:, tq:]
    y = d * lax.rsqrt(jnp.mean(d * d, axis=0, keepdims=True) + EPS) * g_ref[...] * (1.0 - lam_init)
    o_ref[...] = y.astype(BF)


def _attention(qt, k, vt, g_sub, da_lambda, lam_init, dims, latent):
    nb, s, ctx, t_lat = dims["nb"], dims["s"], dims["ctx"], dims["t_lat"]
    cpb = ctx // VT_CHUNK
    tq = TQ if latent else min(TQ, ctx)
    g_sub_b = jnp.broadcast_to(g_sub[:, None], (HEAD_W, tq))
    common_tail = [pl.BlockSpec((HEAD_W, tq), lambda b, h, qi: (0, 0)),
                   pl.BlockSpec((4, HEAD_DIM), lambda b, h, qi: (0, 0))]
    kc_spec = pl.BlockSpec((ctx, HEAD_W), lambda b, h, qi: (t_lat // ctx + b, h))
    vtc_spec = pl.BlockSpec((cpb, HEAD_W, VT_CHUNK), lambda b, h, qi: (t_lat // ctx + b, h, 0))
    scratch = [pltpu.VMEM((HEAD_W, 2 * tq), F32)]
    if latent:
        nq = s // tq
        in_specs = [pl.BlockSpec((HEAD_W, tq), lambda b, h, qi: (h, b * nq + qi)), kc_spec, vtc_spec,
                    pl.BlockSpec((s, HEAD_W), lambda b, h, qi: (b, h)),
                    pl.BlockSpec((s // VT_CHUNK, HEAD_W, VT_CHUNK), lambda b, h, qi: (b, h, 0))] + common_tail
        args = (qt, k, vt, k, vt, g_sub_b, da_lambda)
        out_cols, n_lat_chunks = t_lat, s // TK
        scratch += [pltpu.VMEM((2, TK, 2 * tq), F32), pltpu.VMEM((2, TK, 2 * tq), BF)]
    else:
        nq = ctx // tq
        in_specs = [pl.BlockSpec((HEAD_W, tq), lambda b, h, qi: (h, t_lat // tq + b * nq + qi)),
                    kc_spec, vtc_spec] + common_tail
        args = (qt, k, vt, g_sub_b, da_lambda)
        out_cols, n_lat_chunks = nb * ctx, 0
    return pl.pallas_call(
        functools.partial(_attn_body, n_ctx_chunks=cpb, n_lat_chunks=n_lat_chunks, lam_init=lam_init),
        grid=(nb, HEADS, nq),
        in_specs=in_specs,
        out_specs=pl.BlockSpec((HEAD_W, tq), lambda b, h, qi: (h, b * nq + qi)),
        out_shape=jax.ShapeDtypeStruct((HEADS * HEAD_W, out_cols), BF),
        scratch_shapes=scratch,
        compiler_params=_cparams(("arbitrary", "arbitrary", "arbitrary"), VMEM_LIMIT),
    )(*args)


def _fft_a_body(fa_ref, p_ref, q_ref, zr_ref, zi_ref):
    x = jnp.concatenate([p_ref[...], q_ref[...]], axis=0)
    z = jnp.dot(fa_ref[...], x, preferred_element_type=F32)
    zr_ref[...] = z[:FFT_N2].astype(BF)
    zi_ref[...] = z[FFT_N2:].astype(BF)


def _fft_b_body(fb_ref, tc_ref, ts_ref, zr_ref, zi_ref, o_ref):
    nk = zr_ref.shape[0]
    parts_r, parts_i = [], []
    for j in range(nk):
        zr = zr_ref[j].astype(F32)
        zi = zi_ref[j].astype(F32)
        tc = jnp.concatenate([tc_ref[j]] * 4, axis=1)
        ts = jnp.concatenate([ts_ref[j]] * 4, axis=1)
        parts_r.append((zr * tc - zi * ts).astype(BF))
        parts_i.append((zr * ts + zi * tc).astype(BF))
    rhs = jnp.concatenate([jnp.concatenate(parts_r, axis=1), jnp.concatenate(parts_i, axis=1)], axis=0)
    o_ref[...] = jnp.dot(fb_ref[...], rhs, preferred_element_type=F32).astype(BF)


def _fft_ctx_body(f_ref, p_ref, q_ref, o_ref):
    x = jnp.concatenate([p_ref[...], q_ref[...]], axis=0)
    o_ref[...] = jnp.dot(f_ref[...], x, preferred_element_type=F32).astype(BF)


def _fft_tables(s, ctx):
    n1, n2 = s // FFT_N2, FFT_N2
    norm = 1.0 / math.sqrt(s * LANES)
    a = 2 * np.pi * np.outer(np.arange(n2), np.arange(n2)) / n2
    c, sn = np.cos(a), np.sin(a)
    fa = np.block([[c, -sn], [sn, c]]) * norm
    tw = 2 * np.pi * np.outer(np.arange(n2), np.arange(n1)) / s
    tc = np.repeat(np.cos(tw)[:, :, None], LANES, axis=2)
    ts = np.repeat(np.sin(tw)[:, :, None], LANES, axis=2)
    b = 2 * np.pi * np.outer(np.arange(n1), np.arange(n1)) / n1
    fb = np.concatenate([np.cos(b), -np.sin(b)], axis=1)
    ch = 2 * np.pi * np.outer(np.arange(LANES), np.arange(LANES)) / LANES
    cs = np.concatenate([np.cos(ch), np.sin(ch)], axis=1)
    cx = 2 * np.pi * np.outer(np.arange(ctx), np.arange(ctx)) / ctx
    fctx = np.concatenate([np.cos(cx), -np.sin(cx)], axis=1) / math.sqrt(ctx * LANES)
    return dict(fa=jnp.asarray(fa, BF), tc=jnp.asarray(tc, F32), ts=jnp.asarray(ts, F32),
                fb=jnp.asarray(fb, BF), cs=jnp.asarray(cs, BF), fctx=jnp.asarray(fctx, BF))


def _fourier_latent(p, q, tabs, dims):
    nb, s = dims["nb"], dims["s"]
    n1 = s // FFT_N2
    wide = n1 * BW
    wc = min(4096, wide)
    p2 = p.reshape(-1, wide)
    q2 = q.reshape(-1, wide)
    zshape = jax.ShapeDtypeStruct((nb * FFT_N2, wide), BF)
    zr, zi = pl.pallas_call(
        _fft_a_body,
        grid=(nb, wide // wc),
        in_specs=[pl.BlockSpec((2 * FFT_N2, 2 * FFT_N2), lambda b, j: (0, 0)),
                  pl.BlockSpec((FFT_N2, wc), lambda b, j: (b, j)),
                  pl.BlockSpec((FFT_N2, wc), lambda b, j: (b, j))],
        out_specs=[pl.BlockSpec((FFT_N2, wc), lambda b, j: (b, j))] * 2,
        out_shape=[zshape, zshape],
        compiler_params=_cparams(("arbitrary", "arbitrary"), VMEM_LIMIT),
    )(tabs["fa"], p2, q2)
    kb = 8
    zr4 = zr.reshape(nb, FFT_N2, n1, BW)
    zi4 = zi.reshape(nb, FFT_N2, n1, BW)
    y = pl.pallas_call(
        _fft_b_body,
        grid=(FFT_N2 // kb, nb),
        in_specs=[pl.BlockSpec((n1, 2 * n1), lambda kk, b: (0, 0)),
                  pl.BlockSpec((kb, n1, LANES), lambda kk, b: (kk, 0, 0)),
                  pl.BlockSpec((kb, n1, LANES), lambda kk, b: (kk, 0, 0)),
                  pl.BlockSpec((None, kb, n1, BW), lambda kk, b: (b, kk, 0, 0)),
                  pl.BlockSpec((None, kb, n1, BW), lambda kk, b: (b, kk, 0, 0))],
        out_specs=pl.BlockSpec((None, n1, kb * BW), lambda kk, b: (b, 0, kk)),
        out_shape=jax.ShapeDtypeStruct((nb, n1, FFT_N2 * BW), BF),
        compiler_params=_cparams(("arbitrary", "arbitrary"), VMEM_LIMIT),
    )(tabs["fb"], tabs["tc"], tabs["ts"], zr4, zi4)
    return y.reshape(nb * s, BW)


def _fourier_ctx(p, q, tabs, dims):
    nb, ctx, t_lat = dims["nb"], dims["ctx"], dims["t_lat"]
    return pl.pallas_call(
        _fft_ctx_body,
        grid=(nb,),
        in_specs=[pl.BlockSpec((ctx, 2 * ctx), lambda b: (0, 0)),
                  pl.BlockSpec((ctx, BW), lambda b: (t_lat // ctx + b, 0)),
                  pl.BlockSpec((ctx, BW), lambda b: (t_lat // ctx + b, 0))],
        out_specs=pl.BlockSpec((ctx, BW), lambda b: (b, 0)),
        out_shape=jax.ShapeDtypeStruct((nb * ctx, BW), BF),
        compiler_params=_cparams(("arbitrary",), VMEM_LIMIT),
    )(tabs["fctx"], p, q)


def _merge_body(x_ref, mod_ref, g1_ref, a_ref, u_ref, up_ref, un_ref, sb_ref, yf_ref, px_ref, pp_ref, pn_ref,
                wconv_ref, wpool_ref, pscale_ref, wb_ref, wg_ref, bg_ref, wo_ref, g2_ref, wr_ref, br_ref,
                xo_ref, h2_ref, route_ref, cnt_ref, carry_ref, *, nlat, tps, tpc, s, ctx):
    i = pl.program_id(0)
    is_lat = i < nlat
    j = jnp.where(is_lat, i % tps, (i - nlat) % tpc)
    first = j == 0
    last = j == jnp.where(is_lat, tps, tpc) - 1
    seq_len = jnp.where(is_lat, s, ctx)

    @pl.when(i == 0)
    def _():
        carry_ref[...] = jnp.zeros_like(carry_ref)

    x = x_ref[...]
    h = _modnorm(x, g1_ref[...], mod_ref[0:1, :], mod_ref[1:2, :]).astype(BF)
    rowi = lax.broadcasted_iota(jnp.int32, (TM, 1), 0)

    u = u_ref[...].astype(F32)
    u_prev = jnp.where(first, 0.0, up_ref[...].astype(F32)[HALO - 1:HALO, :])
    u_next = jnp.where(last, 0.0, un_ref[...].astype(F32)[0:1, :])
    u_dn = jnp.where(rowi == 0, u_prev, pltpu.roll(u, 1, axis=0))
    u_up = jnp.where(rowi == TM - 1, u_next, pltpu.roll(u, TM - 1, axis=0))
    wc = wconv_ref[...]
    y_conv = sb_ref[...].astype(F32) * (wc[0:1] * u_dn + wc[1:2] * u + wc[2:3] * u_up)

    px = px_ref[...].astype(F32)
    ext = jnp.concatenate([jnp.where(first, 0.0, pp_ref[...].astype(F32)), px,
                           jnp.where(last, 0.0, pn_ref[...].astype(F32))], axis=0)
    ext_len = TM + 2 * HALO
    pos = j * TM + rowi
    pooled_proj = []
    for g, wd in enumerate(POOL_WINDOWS):
        e = ext[:, g * LANES:(g + 1) * LANES]
        win = e + pltpu.roll(e, 1, axis=0)
        half = 1
        while 2 * half < wd:
            win = pltpu.roll(win, half, axis=0) + pltpu.roll(win, ext_len - half, axis=0)
            half *= 2
        win = win[HALO:HALO + TM]
        cnt = (jnp.minimum(pos + wd // 2, seq_len) - jnp.maximum(pos - wd // 2, 0)).astype(F32)
        pooled = (win / cnt - px[:, g * LANES:(g + 1) * LANES]).astype(BF)
        pooled_proj.append(jnp.dot(pooled, wpool_ref[g], preferred_element_type=F32))
    y_pool = jnp.concatenate(pooled_proj, axis=1) * pscale_ref[...]

    ys = (a_ref[...], y_conv.astype(BF), yf_ref[...], y_pool.astype(BF))
    acc = jnp.zeros((TM, D), F32)
    for kk in range(4):
        pk = jnp.dot(ys[kk], wb_ref[kk], preferred_element_type=F32)
        gk = _sigmoid(jnp.dot(h, wg_ref[:, kk * D:(kk + 1) * D], preferred_element_type=F32)
                      + bg_ref[:, kk * D:(kk + 1) * D])
        acc = acc + gk * pk
    xn = x + mod_ref[2:3, :] * jnp.dot(acc.astype(BF), wo_ref[...], preferred_element_type=F32)
    xo_ref[...] = xn

    h2 = _modnorm(xn, g2_ref[...], mod_ref[3:4, :], mod_ref[4:5, :])
    h2_ref[...] = h2
    logits = jnp.dot(h2.astype(BF), wr_ref[...], preferred_element_type=F32) + br_ref[...]
    lane = lax.broadcasted_iota(jnp.int32, (TM, LANES), 1).astype(F32)
    cur = logits
    vals, idxs = [], []
    for _ in range(TOP_K):
        mx = jnp.max(cur, axis=1, keepdims=True)
        idx = jnp.min(jnp.where(cur == mx, lane, float(LANES)), axis=1, keepdims=True)
        vals.append(mx)
        idxs.append(idx)
        cur = jnp.where(lane == idx, -jnp.inf, cur)
    ex = [jnp.exp(v - vals[0]) for v in vals]
    den = ex[0] + ex[1] + ex[2] + ex[3]
    member = jnp.zeros((TM, LANES), F32)
    for idx in idxs:
        member = member + jnp.where(lane == idx, 1.0, 0.0)
    tri = (lax.broadcasted_iota(jnp.int32, (TM, TM), 0) > lax.broadcasted_iota(jnp.int32, (TM, TM), 1))
    before = jnp.dot(jnp.where(tri, 1.0, 0.0).astype(BF), member.astype(BF), preferred_element_type=F32)
    rank_all = carry_ref[0:1, :] + before
    route = jnp.zeros((TM, LANES), F32)
    for r in range(TOP_K):
        rank = jnp.sum(jnp.where(lane == idxs[r], rank_all, 0.0), axis=1, keepdims=True)
        route = (route + jnp.where(lane == float(r), idxs[r], 0.0)
                 + jnp.where(lane == float(TOP_K + r), ex[r] / den, 0.0)
                 + jnp.where(lane == float(2 * TOP_K + r), rank, 0.0))
    route_ref[...] = route
    carry_ref[...] = carry_ref[...] + jnp.sum(member, axis=0, keepdims=True)
    cnt_ref[...] = carry_ref[...]


def _merge(x_all, mods, g1, a, u, sb, yf, px, wconv, wpool, pscale, wb, wg, bg, wo, g2, wr, br, dims, rows):
    nlat, tps, tpc, nb = dims["nlat"], dims["tps"], dims["tpc"], dims["nb"]
    nhalo = TM // HALO
    last_halo = u.shape[0] // HALO - 1
    tile = lambda w: pl.BlockSpec((TM, w), lambda i: (i, 0))
    prev = pl.BlockSpec((HALO, BW), lambda i: (jnp.maximum(i * nhalo - 1, 0), 0))
    nxt = pl.BlockSpec((HALO, BW), lambda i: (jnp.minimum((i + 1) * nhalo, last_halo), 0))
    const = lambda shape: pl.BlockSpec(shape, lambda i: (0,) * len(shape))
    return pl.pallas_call(
        functools.partial(_merge_body, nlat=nlat, tps=tps, tpc=tpc, s=dims["s"], ctx=dims["ctx"]),
        grid=(rows // TM,),
        in_specs=[tile(D),
                  pl.BlockSpec((None, 8, D), lambda i: (jnp.minimum(i // tps, nb), 0, 0)),
                  const((1, D)), tile(BW), tile(BW), prev, nxt, tile(BW), tile(BW), tile(BW), prev, nxt,
                  const((8, BW)), const((4, LANES, LANES)), const((1, BW)), const((4, BW, D)),
                  const((D, 4 * D)), const((1, 4 * D)), const((D, D)), const((1, D)),
                  const((D, LANES)), const((1, LANES))],
        out_specs=[tile(D), tile(D), tile(LANES), pl.BlockSpec((8, LANES), lambda i: (0, 0))],
        out_shape=[jax.ShapeDtypeStruct((rows, D), F32), jax.ShapeDtypeStruct((rows, D), F32),
                   jax.ShapeDtypeStruct((rows, LANES), F32), jax.ShapeDtypeStruct((8, LANES), F32)],
        scratch_shapes=[pltpu.VMEM((8, LANES), F32)],
        compiler_params=_cparams(("arbitrary",), VMEM_LIMIT),
    )(x_all, mods, g1, a, u, u, u, sb, yf, px, px, px, wconv, wpool, pscale, wb, wg, bg, wo, g2, wr, br)


def _wait_row_copies(src_ref, dst_ref, sem, n):
    for _ in range(n):
        pltpu.make_async_copy(src_ref.at[pl.ds(0, 1)], dst_ref.at[pl.ds(0, 1)], sem).wait()


def _dispatch_body(dest_ref, h_ref, xs_ref, sem):
    def body(t, c):
        for r in range(TOP_K):
            pltpu.make_async_copy(h_ref.at[pl.ds(t, 1)], xs_ref.at[pl.ds(dest_ref[0, t * TOP_K + r], 1)], sem).start()
        return c
    lax.fori_loop(0, TM, body, 0, unroll=8)
    _wait_row_copies(h_ref, xs_ref, sem, TM * TOP_K)


def _dispatch(dest3, h2, n_slots):
    rows = h2.shape[0]
    return pl.pallas_call(
        _dispatch_body,
        grid=(rows // TM,),
        in_specs=[pl.BlockSpec((None, 1, TM * TOP_K), lambda i: (i, 0, 0), memory_space=pltpu.SMEM),
                  pl.BlockSpec((TM, D), lambda i: (i, 0))],
        out_specs=pl.BlockSpec(memory_space=pl.ANY),
        out_shape=jax.ShapeDtypeStruct((n_slots, D), F32),
        scratch_shapes=[pltpu.SemaphoreType.DMA(())],
        compiler_params=_cparams(("arbitrary",), VMEM_LIMIT),
    )(dest3, h2)


def _expert_body(te_ref, na_ref, xs_ref, wg_ref, bg_ref, wu_ref, bu_ref, wd_ref, bd_ref, y_ref):
    @pl.when(pl.program_id(0) < na_ref[0])
    def _():
        x = xs_ref[...].astype(BF)
        a = jnp.minimum(jnp.dot(x, wg_ref[...], preferred_element_type=F32) + bg_ref[...], SWIGLU_LIMIT)
        u = jnp.clip(jnp.dot(x, wu_ref[...], preferred_element_type=F32) + bu_ref[...], -SWIGLU_LIMIT, SWIGLU_LIMIT)
        act = a * _sigmoid(SWIGLU_ALPHA * a) * (u + 1.0)
        y_ref[...] = jnp.dot(act.astype(BF), wd_ref[...], preferred_element_type=F32) + bd_ref[...]


def _experts(tile_expert, n_active, xs, wg, bg, wu, bu, wd, bd):
    n_slots = xs.shape[0]
    f = wg.shape[2]
    slot = lambda j, te, na: (jnp.minimum(j, na[0] - 1), 0)
    wsel = lambda j, te, na: (te[j], 0, 0)
    return pl.pallas_call(
        _expert_body,
        grid_spec=pltpu.PrefetchScalarGridSpec(
            num_scalar_prefetch=2,
            grid=(n_slots // TME,),
            in_specs=[pl.BlockSpec((TME, D), slot),
                      pl.BlockSpec((None, D, f), wsel), pl.BlockSpec((None, 1, f), wsel),
                      pl.BlockSpec((None, D, f), wsel), pl.BlockSpec((None, 1, f), wsel),
                      pl.BlockSpec((None, f, D), wsel), pl.BlockSpec((None, 1, D), wsel)],
            out_specs=pl.BlockSpec((TME, D), slot)),
        out_shape=jax.ShapeDtypeStruct((n_slots, D), F32),
        compiler_params=_cparams(("arbitrary",), VMEM_LIMIT),
    )(tile_expert, n_active, xs, wg, bg, wu, bu, wd, bd)


def _combine_body(dest_ref, x_ref, mod_ref, route_ref, gf_ref, y_ref, o_ref, buf, sem, *, final):
    def body(t, c):
        for r in range(TOP_K):
            pltpu.make_async_copy(y_ref.at[pl.ds(dest_ref[0, t * TOP_K + r], 1)], buf.at[r, pl.ds(t, 1)], sem).start()
        return c
    lax.fori_loop(0, TM, body, 0, unroll=8)
    _wait_row_copies(y_ref, buf.at[0], sem, TM * TOP_K)
    route = route_ref[...]
    acc = jnp.zeros((TM, D), F32)
    for r in range(TOP_K):
        acc = acc + route[:, TOP_K + r:TOP_K + r + 1] * buf[r]
    xn = x_ref[...] + mod_ref[5:6, :] * acc
    if final:
        xn = xn * lax.rsqrt(jnp.mean(xn * xn, axis=-1, keepdims=True) + EPS) * gf_ref[...]
    o_ref[...] = xn


def _combine(dest3, x_mid, mods, route, g_final, y, dims, final):
    rows = x_mid.shape[0]
    tps, nb = dims["tps"], dims["nb"]
    return pl.pallas_call(
        functools.partial(_combine_body, final=final),
        grid=(rows // TM,),
        in_specs=[pl.BlockSpec((None, 1, TM * TOP_K), lambda i: (i, 0, 0), memory_space=pltpu.SMEM),
                  pl.BlockSpec((TM, D), lambda i: (i, 0)),
                  pl.BlockSpec((None, 8, D), lambda i: (jnp.minimum(i // tps, nb), 0, 0)),
                  pl.BlockSpec((TM, LANES), lambda i: (i, 0)),
                  pl.BlockSpec((1, D), lambda i: (0, 0)),
                  pl.BlockSpec(memory_space=pl.ANY)],
        out_specs=pl.BlockSpec((TM, D), lambda i: (i, 0)),
        out_shape=jax.ShapeDtypeStruct((rows, D), F32),
        scratch_shapes=[pltpu.VMEM((TOP_K, TM, D), F32), pltpu.SemaphoreType.DMA(())],
        compiler_params=_cparams(("arbitrary",), VMEM_LIMIT),
    )(dest3, x_mid, mods, route, g_final, y)


def _moe(x_mid, h2, route, counts, mods, g_final, wg, bg, wu, bu, wd, bd, dims, final):
    rows = x_mid.shape[0]
    n_exp = wg.shape[0]
    n_slots = rows * TOP_K + n_exp * TME
    e = route[:, 0:TOP_K].astype(jnp.int32)
    rank = route[:, 2 * TOP_K:3 * TOP_K].astype(jnp.int32)
    cnt = counts[0, :n_exp].astype(jnp.int32)
    cnt_pad = ((cnt + TME - 1) // TME) * TME
    offs_end = jnp.cumsum(cnt_pad)
    offs = offs_end - cnt_pad
    onehot = e[:, :, None] == jnp.arange(n_exp, dtype=jnp.int32)[None, None, :]
    dest = jnp.sum(jnp.where(onehot, offs[None, None, :], 0), axis=-1) + rank
    dest3 = dest.reshape(rows // TM, 1, TM * TOP_K)
    tile_start = jnp.arange(n_slots // TME, dtype=jnp.int32) * TME
    tile_expert = jnp.minimum(jnp.sum(tile_start[:, None] >= offs_end[None, :], axis=1), n_exp - 1).astype(jnp.int32)
    n_active = (offs_end[-1:] // TME).astype(jnp.int32)
    xs = _dispatch(dest3, h2, n_slots)
    y = _experts(tile_expert, n_active, xs, wg, bg, wu, bu, wd, bd)
    return _combine(dest3, x_mid, mods, route, g_final, y, dims, final)


def _rope_tables(s, ctx):
    rows = s // GRID_W
    row = np.repeat(np.arange(rows), GRID_W).astype(np.float32)
    col = np.tile(np.arange(GRID_W), rows).astype(np.float32)
    half = HEAD_DIM // 2
    inv = (np.float32(ROPE_BASE) ** (-np.arange(0, half, 2, dtype=np.float32) / half)).astype(np.float32)
    ar = row[:, None] * inv
    ac = col[:, None] * inv
    ang = np.concatenate([ar, ar, ac, ac], axis=-1)
    cos = np.concatenate([np.cos(ang), np.ones((ctx, HEAD_DIM), np.float32)], axis=0)
    sin = np.concatenate([np.sin(ang), np.zeros((ctx, HEAD_DIM), np.float32)], axis=0)
    return (jnp.asarray(np.tile(cos, (1, 2)), F32), jnp.asarray(np.tile(sin, (1, 2)), F32))


def _rot_columns(w):
    d, n = w.shape
    w4 = w.reshape(d, n // HEAD_DIM, 4, HEAD_DIM // 4)
    return jnp.stack([-w4[:, :, 1], w4[:, :, 0], -w4[:, :, 3], w4[:, :, 2]], axis=2).reshape(d, n)


def kernel(x, c, ctx, c_ctx, w_ada, b_ada, g_norm1, w_in, da_lambda, g_subln, w_conv, w_pool, pool_scale,
           w_branch, w_mgate, b_mgate, w_out, g_norm2, w_router, b_router, w_e_gate, b_e_gate, w_e_up,
           b_e_up, w_e_down, b_e_down, g_final):
    nb, s, _ = x.shape
    nctx = ctx.shape[1]
    depth = w_in.shape[0]
    n_exp = w_router.shape[2]
    t_lat, t_ctx = nb * s, nb * nctx
    dims = dict(nb=nb, s=s, ctx=nctx, t_lat=t_lat, t_all=t_lat + t_ctx, nlat=t_lat // TM, tps=s // TM, tpc=nctx // TM)
    assert s % TK == 0 and s % (FFT_N2 * 8) == 0 and nctx % TM == 0 and nctx % VT_CHUNK == 0 and nb < 16

    cos_t, sin_t = _rope_tables(s, nctx)
    tabs = _fft_tables(s, nctx)
    c16 = jnp.concatenate([c, c_ctx[None, :], jnp.zeros((15 - nb, D), F32)], axis=0)
    mods_all = _adaln(c16, w_ada, b_ada).reshape(depth, 16, 6, D)
    mods_all = jnp.concatenate([mods_all, jnp.zeros((depth, 16, 2, D), F32)], axis=2)

    x_all = jnp.concatenate([x.reshape(t_lat, D), ctx.reshape(t_ctx, D)], axis=0)
    for l in range(depth):
        last = l == depth - 1
        lam_init = 0.8 - 0.6 * math.exp(-0.3 * l)
        mods = mods_all[l]
        w_l = w_in[l]
        w_ext = jnp.concatenate([w_l, _rot_columns(w_l[:, :BW]), _rot_columns(w_l[:, BW:2 * BW])], axis=1).astype(BF)
        q, k, v, u, sb, p, qq, px = _inproj(x_all, mods, g_norm1[l][None, :], w_ext, cos_t, sin_t, tabs["cs"], dims)

        qt = q.T
        vt = v.reshape(-1, VT_CHUNK, HEADS * HEAD_W).transpose(0, 2, 1)
        a_t = _attention(qt, k, vt, g_subln[l], da_lambda[l], lam_init, dims, latent=True)
        yf = _fourier_latent(p, qq, tabs, dims)
        if last:
            a, rows = a_t.T, t_lat
        else:
            a_ctx_t = _attention(qt, k, vt, g_subln[l], da_lambda[l], lam_init, dims, latent=False)
            a = jnp.concatenate([a_t, a_ctx_t], axis=1).T
            yf = jnp.concatenate([yf, _fourier_ctx(p, qq, tabs, dims)], axis=0)
            rows = t_lat + t_ctx

        wconv = jnp.concatenate([w_conv[l], jnp.zeros((5, BW), F32)], axis=0)
        wr = jnp.concatenate([w_router[l], jnp.zeros((D, LANES - n_exp), F32)], axis=1).astype(BF)
        br = jnp.concatenate([b_router[l], jnp.full((LANES - n_exp,), -1e30, F32)])[None, :]
        x_mid, h2, route, counts = _merge(
            x_all, mods, g_norm1[l][None, :], a, u, sb, yf, px, wconv, w_pool[l].astype(BF), pool_scale[l][None, :],
            w_branch[l].astype(BF), w_mgate[l].astype(BF), b_mgate[l][None, :], w_out[l].astype(BF),
            g_norm2[l][None, :], wr, br, dims, rows)
        x_new = _moe(x_mid, h2, route, counts, mods, g_final[None, :],
                     w_e_gate[l].astype(BF), b_e_gate[l][:, None, :], w_e_up[l].astype(BF), b_e_up[l][:, None, :],
                     w_e_down[l].astype(BF), b_e_down[l][:, None, :], dims, final=last)
        if last:
            return x_new.reshape(nb, s, D)
        x_all = x_new
```

```python
import functools
import math

import numpy as np
import jax
import jax.numpy as jnp
from jax import lax
from jax.experimental import pallas as pl
from jax.experimental.pallas import tpu as pltpu

F32 = jnp.float32
BF = jnp.bfloat16

D = 1024
HEADS = 4
HEAD_DIM = 64
HEAD_W = 128
BW = 512
GRID_W = 64
EPS = 1e-6
ROPE_BASE = 10000.0
POOL_WINDOWS = (2, 4, 8, 16)
TOP_K = 4
SWIGLU_LIMIT = 7.0
SWIGLU_ALPHA = 1.702
FFT_N2 = 128

TM = 256
TME = 256
TQ = 256
TK = 1024
ATT_UNROLL = 16
VT_CHUNK = 256
HALO = 16
LANES = 128
QSCALE = (HEAD_DIM ** -0.5) * math.log2(math.e)
VMEM_LIMIT = 56 * 2 ** 20


def _cparams(sem, vmem=None):
    return pltpu.CompilerParams(dimension_semantics=sem, vmem_limit_bytes=vmem)


def _modnorm(x, g, shift, scale):
    y = x * lax.rsqrt(jnp.mean(x * x, axis=-1, keepdims=True) + EPS) * g
    return y * (1.0 + scale) + shift


def _sigmoid(x):
    return 1.0 / (1.0 + jnp.exp(-x))


def _adaln_body(c_ref, w_ref, b_ref, o_ref):
    c = c_ref[...]
    s = c * _sigmoid(c)
    o_ref[...] = jnp.dot(s.astype(BF), w_ref[...].astype(BF), preferred_element_type=F32) + b_ref[...]


def _adaln(c16, w_ada, b_ada):
    nl = w_ada.shape[0]
    tn = 1536
    return pl.pallas_call(
        _adaln_body,
        grid=(nl, 6 * D // tn),
        in_specs=[pl.BlockSpec((16, D), lambda l, j: (0, 0)),
                  pl.BlockSpec((None, D, tn), lambda l, j: (l, 0, j)),
                  pl.BlockSpec((None, 1, tn), lambda l, j: (l, 0, j))],
        out_specs=pl.BlockSpec((None, 16, tn), lambda l, j: (l, 0, j)),
        out_shape=jax.ShapeDtypeStruct((nl, 16, 6 * D), F32),
        compiler_params=_cparams(("arbitrary", "arbitrary"), VMEM_LIMIT),
    )(c16, w_ada, b_ada.reshape(nl, 1, 6 * D))


def _inproj_body(x_ref, mod_ref, g_ref, w_ref, cos_ref, sin_ref, cs_ref,
                 q_ref, k_ref, v_ref, u_ref, sb_ref, p_ref, qq_ref, px_ref):
    h = _modnorm(x_ref[...], g_ref[...], mod_ref[0:1, :], mod_ref[1:2, :]).astype(BF)

    def proj(c0):
        return jnp.dot(h, w_ref[:, c0:c0 + BW], preferred_element_type=F32)

    cos = jnp.concatenate([cos_ref[...]] * 4, axis=1)
    sin = jnp.concatenate([sin_ref[...]] * 4, axis=1)
    q_ref[...] = ((proj(0) * cos + proj(8 * BW) * sin) * QSCALE).T.astype(BF)
    k_ref[...] = (proj(BW) * cos + proj(9 * BW) * sin).astype(BF)
    v_ref[...] = proj(2 * BW).T.astype(BF)
    u_ref[...] = (proj(5 * BW) * proj(3 * BW)).astype(BF)
    sb_ref[...] = proj(4 * BW).astype(BF)
    px_ref[...] = proj(7 * BW).astype(BF)
    fx = proj(6 * BW).astype(BF)
    for g in range(4):
        pq = jnp.dot(fx[:, g * LANES:(g + 1) * LANES], cs_ref[...], preferred_element_type=F32)
        p_ref[:, g * LANES:(g + 1) * LANES] = pq[:, :LANES].astype(BF)
        qq_ref[:, g * LANES:(g + 1) * LANES] = pq[:, LANES:].astype(BF)


def _inproj(x_all, mods, g1, w_ext, cos_t, sin_t, cs_tab, dims):
    t_all, nlat, tps, tpc, nb = dims["t_all"], dims["nlat"], dims["tps"], dims["tpc"], dims["nb"]

    def tab_idx(i):
        return (jnp.where(i < nlat, i % tps, tps + (i - nlat) % tpc), 0)

    out = jax.ShapeDtypeStruct((t_all, BW), BF)
    row = pl.BlockSpec((TM, BW), lambda i: (i, 0))
    return pl.pallas_call(
        _inproj_body,
        grid=(t_all // TM,),
        in_specs=[pl.BlockSpec((TM, D), lambda i: (i, 0)),
                  pl.BlockSpec((None, 8, D), lambda i: (jnp.minimum(i // tps, nb), 0, 0)),
                  pl.BlockSpec((1, D), lambda i: (0, 0)),
                  pl.BlockSpec((D, 10 * BW), lambda i: (0, 0)),
                  pl.BlockSpec((TM, LANES), tab_idx),
                  pl.BlockSpec((TM, LANES), tab_idx),
                  pl.BlockSpec((LANES, 2 * LANES), lambda i: (0, 0))],
        out_specs=[pl.BlockSpec((BW, TM), lambda i: (0, i)), row,
                   pl.BlockSpec((None, BW, VT_CHUNK), lambda i: (i, 0, 0))] + [row] * 5,
        out_shape=[jax.ShapeDtypeStruct((BW, t_all), BF), out,
                   jax.ShapeDtypeStruct((t_all // VT_CHUNK, BW, VT_CHUNK), BF)] + [out] * 5,
        compiler_params=_cparams(("arbitrary",), VMEM_LIMIT),
    )(x_all, mods, g1, w_ext, cos_t, sin_t, cs_tab)


def _attn_body(*refs, n_ctx_chunks, n_lat_chunks, lam_init):
    if n_lat_chunks:
        qt_ref, kc_ref, vtc_ref, kl_ref, vtl_ref, g_ref, dl_ref, o_ref, acc_ref, s_buf, p_buf = refs
    else:
        qt_ref, kc_ref, vtc_ref, g_ref, dl_ref, _, o_ref, acc_ref = refs
    qt = qt_ref[...]
    tq = qt.shape[1]
    row = lax.broadcasted_iota(jnp.int32, qt.shape, 0)
    zero = jnp.zeros_like(qt)
    q2 = jnp.concatenate([jnp.where(row < HEAD_DIM, qt, zero), jnp.where(row >= HEAD_DIM, qt, zero)], axis=1)

    def scores(kc):
        s = jnp.dot(kc, q2, preferred_element_type=F32)
        return s, jnp.max(s, axis=0, keepdims=True)

    def softmax(s, smax, m, l):
        m_new = jnp.maximum(m, smax)
        p = jnp.exp2(s - m_new)
        alpha = jnp.exp2(m - m_new)
        return p.astype(BF), m_new, alpha * l + jnp.sum(p, axis=0, keepdims=True), alpha

    def accumulate(alpha, vt, p):
        acc_ref[...] = alpha * acc_ref[...] + jnp.dot(vt, p, preferred_element_type=F32)

    m = jnp.full((1, 2 * tq), -1e30, F32)
    l = jnp.zeros((1, 2 * tq), F32)
    acc_ref[...] = jnp.zeros_like(acc_ref)
    for c in range(n_ctx_chunks):
        p, m, l, alpha = softmax(*scores(kc_ref[c * VT_CHUNK:(c + 1) * VT_CHUNK, :]), m, l)
        accumulate(alpha, vtc_ref[c], p)
    if n_lat_chunks:
        per = TK // VT_CHUNK
        n = n_lat_chunks

        def keys(j):
            return kl_ref[pl.ds(pl.multiple_of(j * TK, TK), TK), :]

        def values(j):
            parts = [vtl_ref[j * per + r] for r in range(per)]
            return parts[0] if per == 1 else jnp.concatenate(parts, axis=1)

        def half_step(j, prev, cur, carry):
            m, l, alpha_prev, smax = carry
            accumulate(alpha_prev, values(j - 1), p_buf[prev])
            if isinstance(j, int) and j + 1 >= n:
                smax_next = smax
            else:
                s_buf[prev], smax_next = scores(keys(jnp.minimum(j + 1, n - 1)))
            p, m, l, alpha = softmax(s_buf[cur], smax, m, l)
            p_buf[cur] = p
            return m, l, alpha, smax_next

        s0, smax0 = scores(keys(0))
        p, m, l, alpha = softmax(s0, smax0, m, l)
        p_buf[0] = p
        s_buf[1], smax = scores(keys(1))

        def body(i, carry):
            for r in range(ATT_UNROLL):
                carry = half_step(ATT_UNROLL * i + r + 1, r % 2, 1 - r % 2, carry)
            return carry

        n_body = (n - 1) // ATT_UNROLL
        carry = lax.fori_loop(0, n_body, body, (m, l, alpha, smax))
        for j in range(n_body * ATT_UNROLL + 1, n):
            carry = half_step(j, (j - 1) % 2, j % 2, carry)
        m, l, alpha, _ = carry
        accumulate(alpha, values(n - 1), p_buf[(n - 1) % 2])
    o = acc_ref[...] / l
    dl = dl_ref[...]
    lam = (jnp.exp(jnp.sum(dl[0:1] * dl[1:2], axis=1, keepdims=True))
           - jnp.exp(jnp.sum(dl[2:3] * dl[3:4], axis=1, keepdims=True)) + lam_init)
    d = o[:, :tq] - lam * o[:, tq:]
    y = d * lax.rsqrt(jnp.mean(d * d, axis=0, keepdims=True) + EPS) * g_ref[...] * (1.0 - lam_init)
    o_ref[...] = y.T.astype(BF)


def _attention(qt, k, vt, g_sub, da_lambda, lam_init, dims, latent, rows=None, fill=None):
    nb, s, ctx, t_lat = dims["nb"], dims["s"], dims["ctx"], dims["t_lat"]
    cpb = ctx // VT_CHUNK
    tq = TQ if latent else min(TQ, ctx)
    g_sub_b = jnp.broadcast_to(g_sub[:, None], (HEAD_W, tq))
    common_tail = [pl.BlockSpec((HEAD_W, tq), lambda b, h, qi: (0, 0)),
                   pl.BlockSpec((4, HEAD_DIM), lambda b, h, qi: (0, 0))]
    kc_spec = pl.BlockSpec((ctx, HEAD_W), lambda b, h, qi: (t_lat // ctx + b, h))
    vtc_spec = pl.BlockSpec((cpb, HEAD_W, VT_CHUNK), lambda b, h, qi: (t_lat // ctx + b, h, 0))
    scratch = [pltpu.VMEM((HEAD_W, 2 * tq), F32)]
    if latent:
        nq = s // tq
        in_specs = [pl.BlockSpec((HEAD_W, tq), lambda b, h, qi: (h, b * nq + qi)), kc_spec, vtc_spec,
                    pl.BlockSpec((s, HEAD_W), lambda b, h, qi: (b, h)),
                    pl.BlockSpec((s // VT_CHUNK, HEAD_W, VT_CHUNK), lambda b, h, qi: (b, h, 0))] + common_tail
        args = (qt, k, vt, k, vt, g_sub_b, da_lambda)
        row0, n_lat_chunks, aliases = 0, s // TK, {}
        scratch += [pltpu.VMEM((2, TK, 2 * tq), F32), pltpu.VMEM((2, TK, 2 * tq), BF)]
    else:
        nq = ctx // tq
        in_specs = [pl.BlockSpec((HEAD_W, tq), lambda b, h, qi: (h, t_lat // tq + b * nq + qi)),
                    kc_spec, vtc_spec] + common_tail
        in_specs.append(pl.BlockSpec(memory_space=pl.ANY))
        args = (qt, k, vt, g_sub_b, da_lambda, fill)
        row0, n_lat_chunks, rows, aliases = t_lat // tq, 0, fill.shape[0], {5: 0}
    return pl.pallas_call(
        functools.partial(_attn_body, n_ctx_chunks=cpb, n_lat_chunks=n_lat_chunks, lam_init=lam_init),
        grid=(nb, HEADS, nq),
        in_specs=in_specs,
        out_specs=pl.BlockSpec((tq, HEAD_W), lambda b, h, qi: (row0 + b * nq + qi, h)),
        out_shape=jax.ShapeDtypeStruct((rows, HEADS * HEAD_W), BF),
        input_output_aliases=aliases,
        scratch_shapes=scratch,
        compiler_params=_cparams(("arbitrary", "arbitrary", "arbitrary"), VMEM_LIMIT),
    )(*args)


def _fft_a_body(fa_ref, p_ref, q_ref, zr_ref, zi_ref):
    x = jnp.concatenate([p_ref[...], q_ref[...]], axis=0)
    z = jnp.dot(fa_ref[...], x, preferred_element_type=F32)
    zr_ref[...] = z[:FFT_N2].astype(BF)
    zi_ref[...] = z[FFT_N2:].astype(BF)


def _fft_b_body(fb_ref, tc_ref, ts_ref, zr_ref, zi_ref, o_ref):
    nk = zr_ref.shape[0]
    parts_r, parts_i = [], []
    for j in range(nk):
        zr = zr_ref[j].astype(F32)
        zi = zi_ref[j].astype(F32)
        tc = jnp.concatenate([tc_ref[j]] * 4, axis=1)
        ts = jnp.concatenate([ts_ref[j]] * 4, axis=1)
        parts_r.append((zr * tc - zi * ts).astype(BF))
        parts_i.append((zr * ts + zi * tc).astype(BF))
    rhs = jnp.concatenate([jnp.concatenate(parts_r, axis=1), jnp.concatenate(parts_i, axis=1)], axis=0)
    o_ref[...] = jnp.dot(fb_ref[...], rhs, preferred_element_type=F32).astype(BF)


def _fft_ctx_body(f_ref, p_ref, q_ref, _, o_ref):
    x = jnp.concatenate([p_ref[...], q_ref[...]], axis=0)
    o_ref[...] = jnp.dot(f_ref[...], x, preferred_element_type=F32).astype(BF)


def _fft_tables(s, ctx):
    n1, n2 = s // FFT_N2, FFT_N2
    norm = 1.0 / math.sqrt(s * LANES)
    a = 2 * np.pi * np.outer(np.arange(n2), np.arange(n2)) / n2
    c, sn = np.cos(a), np.sin(a)
    fa = np.block([[c, -sn], [sn, c]]) * norm
    tw = 2 * np.pi * np.outer(np.arange(n2), np.arange(n1)) / s
    tc = np.repeat(np.cos(tw)[:, :, None], LANES, axis=2)
    ts = np.repeat(np.sin(tw)[:, :, None], LANES, axis=2)
    b = 2 * np.pi * np.outer(np.arange(n1), np.arange(n1)) / n1
    fb = np.concatenate([np.cos(b), -np.sin(b)], axis=1)
    ch = 2 * np.pi * np.outer(np.arange(LANES), np.arange(LANES)) / LANES
    cs = np.concatenate([np.cos(ch), np.sin(ch)], axis=1)
    cx = 2 * np.pi * np.outer(np.arange(ctx), np.arange(ctx)) / ctx
    fctx = np.concatenate([np.cos(cx), -np.sin(cx)], axis=1) / math.sqrt(ctx * LANES)
    return dict(fa=jnp.asarray(fa, BF), tc=jnp.asarray(tc, F32), ts=jnp.asarray(ts, F32),
                fb=jnp.asarray(fb, BF), cs=jnp.asarray(cs, BF), fctx=jnp.asarray(fctx, BF))


def _fourier_latent(p, q, tabs, dims, rows):
    nb, s = dims["nb"], dims["s"]
    n1 = s // FFT_N2
    wide = n1 * BW
    wc = min(4096, wide)
    p2 = p.reshape(-1, wide)
    q2 = q.reshape(-1, wide)
    zshape = jax.ShapeDtypeStruct((nb * FFT_N2, wide), BF)
    zr, zi = pl.pallas_call(
        _fft_a_body,
        grid=(nb, wide // wc),
        in_specs=[pl.BlockSpec((2 * FFT_N2, 2 * FFT_N2), lambda b, j: (0, 0)),
                  pl.BlockSpec((FFT_N2, wc), lambda b, j: (b, j)),
                  pl.BlockSpec((FFT_N2, wc), lambda b, j: (b, j))],
        out_specs=[pl.BlockSpec((FFT_N2, wc), lambda b, j: (b, j))] * 2,
        out_shape=[zshape, zshape],
        compiler_params=_cparams(("arbitrary", "arbitrary"), VMEM_LIMIT),
    )(tabs["fa"], p2, q2)
    kb = 8
    zr4 = zr.reshape(nb, FFT_N2, n1, BW)
    zi4 = zi.reshape(nb, FFT_N2, n1, BW)
    y = pl.pallas_call(
        _fft_b_body,
        grid=(FFT_N2 // kb, nb),
        in_specs=[pl.BlockSpec((n1, 2 * n1), lambda kk, b: (0, 0)),
                  pl.BlockSpec((kb, n1, LANES), lambda kk, b: (kk, 0, 0)),
                  pl.BlockSpec((kb, n1, LANES), lambda kk, b: (kk, 0, 0)),
                  pl.BlockSpec((None, kb, n1, BW), lambda kk, b: (b, kk, 0, 0)),
                  pl.BlockSpec((None, kb, n1, BW), lambda kk, b: (b, kk, 0, 0))],
        out_specs=pl.BlockSpec((n1, kb * BW), lambda kk, b: (b, kk)),
        out_shape=jax.ShapeDtypeStruct((rows // FFT_N2, FFT_N2 * BW), BF),
        compiler_params=_cparams(("arbitrary", "arbitrary"), VMEM_LIMIT),
    )(tabs["fb"], tabs["tc"], tabs["ts"], zr4, zi4)
    return y.reshape(rows, BW)


def _fourier_ctx(p, q, tabs, dims, fill):
    nb, ctx, t_lat = dims["nb"], dims["ctx"], dims["t_lat"]
    return pl.pallas_call(
        _fft_ctx_body,
        grid=(nb,),
        in_specs=[pl.BlockSpec((ctx, 2 * ctx), lambda b: (0, 0)),
                  pl.BlockSpec((ctx, BW), lambda b: (t_lat // ctx + b, 0)),
                  pl.BlockSpec((ctx, BW), lambda b: (t_lat // ctx + b, 0)),
                  pl.BlockSpec(memory_space=pl.ANY)],
        out_specs=pl.BlockSpec((ctx, BW), lambda b: (t_lat // ctx + b, 0)),
        out_shape=jax.ShapeDtypeStruct(fill.shape, BF),
        input_output_aliases={3: 0},
        compiler_params=_cparams(("arbitrary",), VMEM_LIMIT),
    )(tabs["fctx"], p, q, fill)


def _merge_body(x_ref, mod_ref, g1_ref, a_ref, u_ref, up_ref, un_ref, sb_ref, yf_ref, px_ref, pp_ref, pn_ref,
                wconv_ref, wpool_ref, pscale_ref, wb_ref, wg_ref, bg_ref, wo_ref, g2_ref, wr_ref, br_ref,
                xo_ref, h2_ref, route_ref, cnt_ref, carry_ref, *, nlat, tps, tpc, s, ctx):
    i = pl.program_id(0)
    is_lat = i < nlat
    j = jnp.where(is_lat, i % tps, (i - nlat) % tpc)
    first = j == 0
    last = j == jnp.where(is_lat, tps, tpc) - 1
    seq_len = jnp.where(is_lat, s, ctx)

    @pl.when(i == 0)
    def _():
        carry_ref[...] = jnp.zeros_like(carry_ref)

    x = x_ref[...]
    h = _modnorm(x, g1_ref[...], mod_ref[0:1, :], mod_ref[1:2, :]).astype(BF)
    rowi = lax.broadcasted_iota(jnp.int32, (TM, 1), 0)

    u = u_ref[...].astype(F32)
    u_prev = jnp.where(first, 0.0, up_ref[...].astype(F32)[HALO - 1:HALO, :])
    u_next = jnp.where(last, 0.0, un_ref[...].astype(F32)[0:1, :])
    u_dn = jnp.where(rowi == 0, u_prev, pltpu.roll(u, 1, axis=0))
    u_up = jnp.where(rowi == TM - 1, u_next, pltpu.roll(u, TM - 1, axis=0))
    wc = wconv_ref[...]
    y_conv = sb_ref[...].astype(F32) * (wc[0:1] * u_dn + wc[1:2] * u + wc[2:3] * u_up)

    px = px_ref[...].astype(F32)
    ext = jnp.concatenate([jnp.where(first, 0.0, pp_ref[...].astype(F32)), px,
                           jnp.where(last, 0.0, pn_ref[...].astype(F32))], axis=0)
    ext_len = TM + 2 * HALO
    pos = j * TM + rowi
    pooled_proj = []
    for g, wd in enumerate(POOL_WINDOWS):
        e = ext[:, g * LANES:(g + 1) * LANES]
        win = e + pltpu.roll(e, 1, axis=0)
        half = 1
        while 2 * half < wd:
            win = pltpu.roll(win, half, axis=0) + pltpu.roll(win, ext_len - half, axis=0)
            half *= 2
        win = win[HALO:HALO + TM]
        cnt = (jnp.minimum(pos + wd // 2, seq_len) - jnp.maximum(pos - wd // 2, 0)).astype(F32)
        pooled = (win / cnt - px[:, g * LANES:(g + 1) * LANES]).astype(BF)
        pooled_proj.append(jnp.dot(pooled, wpool_ref[g], preferred_element_type=F32))
    y_pool = jnp.concatenate(pooled_proj, axis=1) * pscale_ref[...]

    ys = (a_ref[...], y_conv.astype(BF), yf_ref[...], y_pool.astype(BF))
    acc = jnp.zeros((TM, D), F32)
    for kk in range(4):
        pk = jnp.dot(ys[kk], wb_ref[kk], preferred_element_type=F32)
        gk = _sigmoid(jnp.dot(h, wg_ref[:, kk * D:(kk + 1) * D], preferred_element_type=F32)
                      + bg_ref[:, kk * D:(kk + 1) * D])
        acc = acc + gk * pk
    xn = x + mod_ref[2:3, :] * jnp.dot(acc.astype(BF), wo_ref[...], preferred_element_type=F32)
    xo_ref[...] = xn

    h2 = _modnorm(xn, g2_ref[...], mod_ref[3:4, :], mod_ref[4:5, :])
    h2_ref[...] = h2
    logits = jnp.dot(h2.astype(BF), wr_ref[...], preferred_element_type=F32) + br_ref[...]
    lane = lax.broadcasted_iota(jnp.int32, (TM, LANES), 1).astype(F32)
    cur = logits
    vals, idxs = [], []
    for _ in range(TOP_K):
        mx = jnp.max(cur, axis=1, keepdims=True)
        idx = jnp.min(jnp.where(cur == mx, lane, float(LANES)), axis=1, keepdims=True)
        vals.append(mx)
        idxs.append(idx)
        cur = jnp.where(lane == idx, -jnp.inf, cur)
    ex = [jnp.exp(v - vals[0]) for v in vals]
    den = ex[0] + ex[1] + ex[2] + ex[3]
    member = jnp.zeros((TM, LANES), F32)
    for idx in idxs:
        member = member + jnp.where(lane == idx, 1.0, 0.0)
    tri = (lax.broadcasted_iota(jnp.int32, (TM, TM), 0) > lax.broadcasted_iota(jnp.int32, (TM, TM), 1))
    before = jnp.dot(jnp.where(tri, 1.0, 0.0).astype(BF), member.astype(BF), preferred_element_type=F32)
    rank_all = carry_ref[0:1, :] + before
    route = jnp.zeros((TM, LANES), F32)
    for r in range(TOP_K):
        rank = jnp.sum(jnp.where(lane == idxs[r], rank_all, 0.0), axis=1, keepdims=True)
        route = (route + jnp.where(lane == float(r), idxs[r], 0.0)
                 + jnp.where(lane == float(TOP_K + r), ex[r] / den, 0.0)
                 + jnp.where(lane == float(2 * TOP_K + r), rank, 0.0))
    route_ref[...] = route
    carry_ref[...] = carry_ref[...] + jnp.sum(member, axis=0, keepdims=True)
    cnt_ref[...] = carry_ref[...]


def _merge(x_all, mods, g1, a, u, sb, yf, px, wconv, wpool, pscale, wb, wg, bg, wo, g2, wr, br, dims, rows):
    nlat, tps, tpc, nb = dims["nlat"], dims["tps"], dims["tpc"], dims["nb"]
    nhalo = TM // HALO
    last_halo = u.shape[0] // HALO - 1
    tile = lambda w: pl.BlockSpec((TM, w), lambda i: (i, 0))
    prev = pl.BlockSpec((HALO, BW), lambda i: (jnp.maximum(i * nhalo - 1, 0), 0))
    nxt = pl.BlockSpec((HALO, BW), lambda i: (jnp.minimum((i + 1) * nhalo, last_halo), 0))
    const = lambda shape: pl.BlockSpec(shape, lambda i: (0,) * len(shape))
    return pl.pallas_call(
        functools.partial(_merge_body, nlat=nlat, tps=tps, tpc=tpc, s=dims["s"], ctx=dims["ctx"]),
        grid=(rows // TM,),
        in_specs=[tile(D),
                  pl.BlockSpec((None, 8, D), lambda i: (jnp.minimum(i // tps, nb), 0, 0)),
                  const((1, D)), tile(BW), tile(BW), prev, nxt, tile(BW), tile(BW), tile(BW), prev, nxt,
                  const((8, BW)), const((4, LANES, LANES)), const((1, BW)), const((4, BW, D)),
                  const((D, 4 * D)), const((1, 4 * D)), const((D, D)), const((1, D)),
                  const((D, LANES)), const((1, LANES))],
        out_specs=[tile(D), tile(D), tile(LANES), pl.BlockSpec((8, LANES), lambda i: (0, 0))],
        out_shape=[jax.ShapeDtypeStruct((rows, D), F32), jax.ShapeDtypeStruct((rows, D), F32),
                   jax.ShapeDtypeStruct((rows, LANES), F32), jax.ShapeDtypeStruct((8, LANES), F32)],
        scratch_shapes=[pltpu.VMEM((8, LANES), F32)],
        compiler_params=_cparams(("arbitrary",), VMEM_LIMIT),
    )(x_all, mods, g1, a, u, u, u, sb, yf, px, px, px, wconv, wpool, pscale, wb, wg, bg, wo, g2, wr, br)


def _wait_row_copies(src_ref, dst_ref, sem, n):
    for _ in range(n):
        pltpu.make_async_copy(src_ref.at[pl.ds(0, 1)], dst_ref.at[pl.ds(0, 1)], sem).wait()


def _dispatch_body(dest_ref, h_ref, xs_ref, sem):
    def body(t, c):
        for r in range(TOP_K):
            pltpu.make_async_copy(h_ref.at[pl.ds(t, 1)], xs_ref.at[pl.ds(dest_ref[0, t * TOP_K + r], 1)], sem).start()
        return c
    lax.fori_loop(0, TM, body, 0, unroll=8)
    _wait_row_copies(h_ref, xs_ref, sem, TM * TOP_K)


def _dispatch(dest3, h2, n_slots):
    rows = h2.shape[0]
    return pl.pallas_call(
        _dispatch_body,
        grid=(rows // TM,),
        in_specs=[pl.BlockSpec((None, 1, TM * TOP_K), lambda i: (i, 0, 0), memory_space=pltpu.SMEM),
                  pl.BlockSpec((TM, D), lambda i: (i, 0))],
        out_specs=pl.BlockSpec(memory_space=pl.ANY),
        out_shape=jax.ShapeDtypeStruct((n_slots, D), F32),
        scratch_shapes=[pltpu.SemaphoreType.DMA(())],
        compiler_params=_cparams(("arbitrary",), VMEM_LIMIT),
    )(dest3, h2)


def _expert_body(te_ref, na_ref, xs_ref, wg_ref, bg_ref, wu_ref, bu_ref, wd_ref, bd_ref, y_ref, wgb, wub, wdb):
    j = pl.program_id(0)

    @pl.when(j < na_ref[0])
    def _():
        @pl.when((j == 0) | (te_ref[j] != te_ref[jnp.maximum(j - 1, 0)]))
        def _():
            wgb[...] = wg_ref[...].astype(BF)
            wub[...] = wu_ref[...].astype(BF)
            wdb[...] = wd_ref[...].astype(BF)

        x = xs_ref[...].astype(BF)
        a = jnp.minimum(jnp.dot(x, wgb[...], preferred_element_type=F32) + bg_ref[...], SWIGLU_LIMIT)
        u = jnp.clip(jnp.dot(x, wub[...], preferred_element_type=F32) + bu_ref[...], -SWIGLU_LIMIT, SWIGLU_LIMIT)
        act = a * _sigmoid(SWIGLU_ALPHA * a) * (u + 1.0)
        y_ref[...] = jnp.dot(act.astype(BF), wdb[...], preferred_element_type=F32) + bd_ref[...]


def _experts(tile_expert, n_active, xs, wg, bg, wu, bu, wd, bd):
    n_slots = xs.shape[0]
    f = wg.shape[2]
    slot = lambda j, te, na: (jnp.minimum(j, na[0] - 1), 0)
    wsel = lambda j, te, na: (te[j], 0, 0)
    return pl.pallas_call(
        _expert_body,
        grid_spec=pltpu.PrefetchScalarGridSpec(
            num_scalar_prefetch=2,
            grid=(n_slots // TME,),
            in_specs=[pl.BlockSpec((TME, D), slot),
                      pl.BlockSpec((None, D, f), wsel), pl.BlockSpec((None, 1, f), wsel),
                      pl.BlockSpec((None, D, f), wsel), pl.BlockSpec((None, 1, f), wsel),
                      pl.BlockSpec((None, f, D), wsel), pl.BlockSpec((None, 1, D), wsel)],
            out_specs=pl.BlockSpec((TME, D), slot),
            scratch_shapes=[pltpu.VMEM((D, f), BF), pltpu.VMEM((D, f), BF), pltpu.VMEM((f, D), BF)]),
        out_shape=jax.ShapeDtypeStruct((n_slots, D), F32),
        compiler_params=_cparams(("arbitrary",), VMEM_LIMIT),
    )(tile_expert, n_active, xs, wg, bg, wu, bu, wd, bd)


def _combine_body(dest_ref, x_ref, mod_ref, route_ref, gf_ref, y_ref, o_ref, buf, sem, *, final):
    def body(t, c):
        for r in range(TOP_K):
            pltpu.make_async_copy(y_ref.at[pl.ds(dest_ref[0, t * TOP_K + r], 1)], buf.at[r, pl.ds(t, 1)], sem).start()
        return c
    lax.fori_loop(0, TM, body, 0, unroll=8)
    _wait_row_copies(y_ref, buf.at[0], sem, TM * TOP_K)
    route = route_ref[...]
    acc = jnp.zeros((TM, D), F32)
    for r in range(TOP_K):
        acc = acc + route[:, TOP_K + r:TOP_K + r + 1] * buf[r]
    xn = x_ref[...] + mod_ref[5:6, :] * acc
    if final:
        xn = xn * lax.rsqrt(jnp.mean(xn * xn, axis=-1, keepdims=True) + EPS) * gf_ref[...]
    o_ref[...] = xn


def _combine(dest3, x_mid, mods, route, g_final, y, dims, final):
    rows = x_mid.shape[0]
    tps, nb = dims["tps"], dims["nb"]
    return pl.pallas_call(
        functools.partial(_combine_body, final=final),
        grid=(rows // TM,),
        in_specs=[pl.BlockSpec((None, 1, TM * TOP_K), lambda i: (i, 0, 0), memory_space=pltpu.SMEM),
                  pl.BlockSpec((TM, D), lambda i: (i, 0)),
                  pl.BlockSpec((None, 8, D), lambda i: (jnp.minimum(i // tps, nb), 0, 0)),
                  pl.BlockSpec((TM, LANES), lambda i: (i, 0)),
                  pl.BlockSpec((1, D), lambda i: (0, 0)),
                  pl.BlockSpec(memory_space=pl.ANY)],
        out_specs=pl.BlockSpec((TM, D), lambda i: (i, 0)),
        out_shape=jax.ShapeDtypeStruct((rows, D), F32),
        scratch_shapes=[pltpu.VMEM((TOP_K, TM, D), F32), pltpu.SemaphoreType.DMA(())],
        compiler_params=_cparams(("arbitrary",), VMEM_LIMIT),
    )(dest3, x_mid, mods, route, g_final, y)


def _moe(x_mid, h2, route, counts, mods, g_final, wg, bg, wu, bu, wd, bd, dims, final):
    rows = x_mid.shape[0]
    n_exp = wg.shape[0]
    n_slots = rows * TOP_K + n_exp * TME
    e = route[:, 0:TOP_K].astype(jnp.int32)
    rank = route[:, 2 * TOP_K:3 * TOP_K].astype(jnp.int32)
    cnt = counts[0, :n_exp].astype(jnp.int32)
    cnt_pad = ((cnt + TME - 1) // TME) * TME
    offs_end = jnp.cumsum(cnt_pad)
    offs = offs_end - cnt_pad
    onehot = e[:, :, None] == jnp.arange(n_exp, dtype=jnp.int32)[None, None, :]
    dest = jnp.sum(jnp.where(onehot, offs[None, None, :], 0), axis=-1) + rank
    dest3 = dest.reshape(rows // TM, 1, TM * TOP_K)
    tile_start = jnp.arange(n_slots // TME, dtype=jnp.int32) * TME
    tile_expert = jnp.minimum(jnp.sum(tile_start[:, None] >= offs_end[None, :], axis=1), n_exp - 1).astype(jnp.int32)
    n_active = (offs_end[-1:] // TME).astype(jnp.int32)
    xs = _dispatch(dest3, h2, n_slots)
    y = _experts(tile_expert, n_active, xs, wg, bg, wu, bu, wd, bd)
    return _combine(dest3, x_mid, mods, route, g_final, y, dims, final)


def _rope_tables(s, ctx):
    rows = s // GRID_W
    row = np.repeat(np.arange(rows), GRID_W).astype(np.float32)
    col = np.tile(np.arange(GRID_W), rows).astype(np.float32)
    half = HEAD_DIM // 2
    inv = (np.float32(ROPE_BASE) ** (-np.arange(0, half, 2, dtype=np.float32) / half)).astype(np.float32)
    ar = row[:, None] * inv
    ac = col[:, None] * inv
    ang = np.concatenate([ar, ar, ac, ac], axis=-1)
    cos = np.concatenate([np.cos(ang), np.ones((ctx, HEAD_DIM), np.float32)], axis=0)
    sin = np.concatenate([np.sin(ang), np.zeros((ctx, HEAD_DIM), np.float32)], axis=0)
    return (jnp.asarray(np.tile(cos, (1, 2)), F32), jnp.asarray(np.tile(sin, (1, 2)), F32))


def _rot_columns(w):
    d, n = w.shape
    w4 = w.reshape(d, n // HEAD_DIM, 4, HEAD_DIM // 4)
    return jnp.stack([-w4[:, :, 1], w4[:, :, 0], -w4[:, :, 3], w4[:, :, 2]], axis=2).reshape(d, n)


def kernel(x, c, ctx, c_ctx, w_ada, b_ada, g_norm1, w_in, da_lambda, g_subln, w_conv, w_pool, pool_scale,
           w_branch, w_mgate, b_mgate, w_out, g_norm2, w_router, b_router, w_e_gate, b_e_gate, w_e_up,
           b_e_up, w_e_down, b_e_down, g_final):
    nb, s, _ = x.shape
    nctx = ctx.shape[1]
    depth = w_in.shape[0]
    n_exp = w_router.shape[2]
    t_lat, t_ctx = nb * s, nb * nctx
    dims = dict(nb=nb, s=s, ctx=nctx, t_lat=t_lat, t_all=t_lat + t_ctx, nlat=t_lat // TM, tps=s // TM, tpc=nctx // TM)
    assert s % TK == 0 and s // TK >= 2 and s % (FFT_N2 * 8) == 0 and nctx % TM == 0 and nb < 16
    assert TM == VT_CHUNK and nctx % VT_CHUNK == 0 and ATT_UNROLL % 2 == 0

    cos_t, sin_t = _rope_tables(s, nctx)
    tabs = _fft_tables(s, nctx)
    c16 = jnp.concatenate([c, c_ctx[None, :], jnp.zeros((15 - nb, D), F32)], axis=0)
    mods_all = _adaln(c16, w_ada, b_ada).reshape(depth, 16, 6, D)
    mods_all = jnp.concatenate([mods_all, jnp.zeros((depth, 16, 2, D), F32)], axis=2)

    x_all = jnp.concatenate([x.reshape(t_lat, D), ctx.reshape(t_ctx, D)], axis=0)
    for l in range(depth):
        last = l == depth - 1
        lam_init = 0.8 - 0.6 * math.exp(-0.3 * l)
        mods = mods_all[l]
        w_l = w_in[l]
        w_ext = jnp.concatenate([w_l, _rot_columns(w_l[:, :BW]), _rot_columns(w_l[:, BW:2 * BW])], axis=1).astype(BF)
        qt, k, vt, u, sb, p, qq, px = _inproj(x_all, mods, g_norm1[l][None, :], w_ext, cos_t, sin_t, tabs["cs"], dims)

        rows = t_lat if last else t_lat + t_ctx
        a = _attention(qt, k, vt, g_subln[l], da_lambda[l], lam_init, dims, latent=True, rows=rows)
        yf = _fourier_latent(p, qq, tabs, dims, rows)
        if not last:
            a = _attention(qt, k, vt, g_subln[l], da_lambda[l], lam_init, dims, latent=False, fill=a)
            yf = _fourier_ctx(p, qq, tabs, dims, yf)

        wconv = jnp.concatenate([w_conv[l], jnp.zeros((5, BW), F32)], axis=0)
        wr = jnp.concatenate([w_router[l], jnp.zeros((D, LANES - n_exp), F32)], axis=1).astype(BF)
        br = jnp.concatenate([b_router[l], jnp.full((LANES - n_exp,), -1e30, F32)])[None, :]
        x_mid, h2, route, counts = _merge(
            x_all, mods, g_norm1[l][None, :], a, u, sb, yf, px, wconv, w_pool[l].astype(BF), pool_scale[l][None, :],
            w_branch[l].astype(BF), w_mgate[l].astype(BF), b_mgate[l][None, :], w_out[l].astype(BF),
            g_norm2[l][None, :], wr, br, dims, rows)
        x_new = _moe(x_mid, h2, route, counts, mods, g_final[None, :],
                     w_e_gate[l], b_e_gate[l][:, None, :], w_e_up[l], b_e_up[l][:, None, :],
                     w_e_down[l], b_e_down[l][:, None, :], dims, final=last)
        if last:
            return x_new.reshape(nb, s, D)
        x_all = x_new
```

```python
import functools
import math

import numpy as np
import jax
import jax.numpy as jnp
from jax import lax
from jax.experimental import pallas as pl
from jax.experimental.pallas import tpu as pltpu

F32 = jnp.float32
BF = jnp.bfloat16

D = 1024
HEADS = 4
HEAD_DIM = 64
HEAD_W = 128
BW = 512
GRID_W = 64
EPS = 1e-6
ROPE_BASE = 10000.0
POOL_WINDOWS = (2, 4, 8, 16)
TOP_K = 4
SWIGLU_LIMIT = 7.0
SWIGLU_ALPHA = 1.702
FFT_N2 = 128

TM = 256
TME = 512
TQ = 256
TK = 1024
ATT_UNROLL = 16
VT_CHUNK = 256
HALO = 16
LANES = 128
QSCALE = (HEAD_DIM ** -0.5) * math.log2(math.e)
VMEM_LIMIT = 56 * 2 ** 20


def _cparams(sem, vmem=None):
    return pltpu.CompilerParams(dimension_semantics=sem, vmem_limit_bytes=vmem)


def _modnorm(x, g, shift, scale):
    y = x * lax.rsqrt(jnp.mean(x * x, axis=-1, keepdims=True) + EPS) * g
    return y * (1.0 + scale) + shift


def _sigmoid(x):
    return 1.0 / (1.0 + jnp.exp(-x))


def _adaln_body(c_ref, w_ref, b_ref, o_ref):
    c = c_ref[...]
    s = c * _sigmoid(c)
    o_ref[...] = jnp.dot(s.astype(BF), w_ref[...].astype(BF), preferred_element_type=F32) + b_ref[...]


def _adaln(c16, w_ada, b_ada):
    nl = w_ada.shape[0]
    tn = 1536
    return pl.pallas_call(
        _adaln_body,
        grid=(nl, 6 * D // tn),
        in_specs=[pl.BlockSpec((16, D), lambda l, j: (0, 0)),
                  pl.BlockSpec((None, D, tn), lambda l, j: (l, 0, j)),
                  pl.BlockSpec((None, 1, tn), lambda l, j: (l, 0, j))],
        out_specs=pl.BlockSpec((None, 16, tn), lambda l, j: (l, 0, j)),
        out_shape=jax.ShapeDtypeStruct((nl, 16, 6 * D), F32),
        compiler_params=_cparams(("arbitrary", "arbitrary"), VMEM_LIMIT),
    )(c16, w_ada, b_ada.reshape(nl, 1, 6 * D))


def _token_tile(xa_ref, xb_ref, n_a):
    return jnp.where(pl.program_id(0) < n_a, xa_ref[...], xb_ref[...])


def _token_specs(xa, xb):
    n_a, n_b = xa.shape[0] // TM, xb.shape[0] // TM
    return [pl.BlockSpec((TM, D), lambda i: (jnp.minimum(i, n_a - 1), 0)),
            pl.BlockSpec((TM, D), lambda i: (jnp.clip(i - n_a, 0, n_b - 1), 0))]


def _inproj_body(xa_ref, xb_ref, mod_ref, g_ref, w_ref, cos_ref, sin_ref, cs_ref,
                 q_ref, k_ref, v_ref, u_ref, sb_ref, p_ref, qq_ref, px_ref, *, n_a):
    h = _modnorm(_token_tile(xa_ref, xb_ref, n_a), g_ref[...], mod_ref[0:1, :], mod_ref[1:2, :]).astype(BF)

    def proj(c0):
        return jnp.dot(h, w_ref[:, c0:c0 + BW], preferred_element_type=F32)

    cos = jnp.concatenate([cos_ref[...]] * 4, axis=1)
    sin = jnp.concatenate([sin_ref[...]] * 4, axis=1)
    q_ref[...] = ((proj(0) * cos + proj(8 * BW) * sin) * QSCALE).T.astype(BF)
    k_ref[...] = (proj(BW) * cos + proj(9 * BW) * sin).astype(BF)
    v_ref[...] = proj(2 * BW).T.astype(BF)
    u_ref[...] = (proj(5 * BW) * proj(3 * BW)).astype(BF)
    sb_ref[...] = proj(4 * BW).astype(BF)
    px_ref[...] = proj(7 * BW).astype(BF)
    fx = proj(6 * BW).astype(BF)
    for g in range(4):
        pq = jnp.dot(fx[:, g * LANES:(g + 1) * LANES], cs_ref[...], preferred_element_type=F32)
        p_ref[:, g * LANES:(g + 1) * LANES] = pq[:, :LANES].astype(BF)
        qq_ref[:, g * LANES:(g + 1) * LANES] = pq[:, LANES:].astype(BF)


def _inproj(xa, xb, mods, g1, w_ext, cos_t, sin_t, cs_tab, dims):
    t_all, nlat, tps, tpc, nb = dims["t_all"], dims["nlat"], dims["tps"], dims["tpc"], dims["nb"]

    def tab_idx(i):
        return (jnp.where(i < nlat, i % tps, tps + (i - nlat) % tpc), 0)

    out = jax.ShapeDtypeStruct((t_all, BW), BF)
    row = pl.BlockSpec((TM, BW), lambda i: (i, 0))
    return pl.pallas_call(
        functools.partial(_inproj_body, n_a=xa.shape[0] // TM),
        grid=(t_all // TM,),
        in_specs=_token_specs(xa, xb) + [
                  pl.BlockSpec((None, 8, D), lambda i: (jnp.minimum(i // tps, nb), 0, 0)),
                  pl.BlockSpec((1, D), lambda i: (0, 0)),
                  pl.BlockSpec((D, 10 * BW), lambda i: (0, 0)),
                  pl.BlockSpec((TM, LANES), tab_idx),
                  pl.BlockSpec((TM, LANES), tab_idx),
                  pl.BlockSpec((LANES, 2 * LANES), lambda i: (0, 0))],
        out_specs=[pl.BlockSpec((BW, TM), lambda i: (0, i)), row,
                   pl.BlockSpec((None, BW, VT_CHUNK), lambda i: (i, 0, 0))] + [row] * 5,
        out_shape=[jax.ShapeDtypeStruct((BW, t_all), BF), out,
                   jax.ShapeDtypeStruct((t_all // VT_CHUNK, BW, VT_CHUNK), BF)] + [out] * 5,
        compiler_params=_cparams(("arbitrary",), VMEM_LIMIT),
    )(xa, xb, mods, g1, w_ext, cos_t, sin_t, cs_tab)


def _attn_body(*refs, n_ctx_chunks, n_lat_chunks, lam_init):
    if n_lat_chunks:
        qt_ref, kc_ref, vtc_ref, kl_ref, vtl_ref, g_ref, dl_ref, o_ref, acc_ref, s_buf, p_buf = refs
    else:
        qt_ref, kc_ref, vtc_ref, g_ref, dl_ref, _, o_ref, acc_ref = refs
    qt = qt_ref[...]
    tq = qt.shape[1]
    row = lax.broadcasted_iota(jnp.int32, qt.shape, 0)
    zero = jnp.zeros_like(qt)
    q2 = jnp.concatenate([jnp.where(row < HEAD_DIM, qt, zero), jnp.where(row >= HEAD_DIM, qt, zero)], axis=1)

    def scores(kc):
        s = jnp.dot(kc, q2, preferred_element_type=F32)
        return s, jnp.max(s, axis=0, keepdims=True)

    def softmax(s, smax, m, l):
        m_new = jnp.maximum(m, smax)
        p = jnp.exp2(s - m_new)
        alpha = jnp.exp2(m - m_new)
        return p.astype(BF), m_new, alpha * l + jnp.sum(p, axis=0, keepdims=True), alpha

    def accumulate(alpha, vt, p):
        acc_ref[...] = alpha * acc_ref[...] + jnp.dot(vt, p, preferred_element_type=F32)

    m = jnp.full((1, 2 * tq), -1e30, F32)
    l = jnp.zeros((1, 2 * tq), F32)
    acc_ref[...] = jnp.zeros_like(acc_ref)
    for c in range(n_ctx_chunks):
        p, m, l, alpha = softmax(*scores(kc_ref[c * VT_CHUNK:(c + 1) * VT_CHUNK, :]), m, l)
        accumulate(alpha, vtc_ref[c], p)
    if n_lat_chunks:
        per = TK // VT_CHUNK
        n = n_lat_chunks

        def keys(j):
            return kl_ref[pl.ds(pl.multiple_of(j * TK, TK), TK), :]

        def values(j):
            parts = [vtl_ref[j * per + r] for r in range(per)]
            return parts[0] if per == 1 else jnp.concatenate(parts, axis=1)

        def half_step(j, prev, cur, carry):
            m, l, alpha_prev, smax = carry
            accumulate(alpha_prev, values(j - 1), p_buf[prev])
            if isinstance(j, int) and j + 1 >= n:
                smax_next = smax
            else:
                s_buf[prev], smax_next = scores(keys(jnp.minimum(j + 1, n - 1)))
            p, m, l, alpha = softmax(s_buf[cur], smax, m, l)
            p_buf[cur] = p
            return m, l, alpha, smax_next

        s0, smax0 = scores(keys(0))
        p, m, l, alpha = softmax(s0, smax0, m, l)
        p_buf[0] = p
        s_buf[1], smax = scores(keys(1))

        def body(i, carry):
            for r in range(ATT_UNROLL):
                carry = half_step(ATT_UNROLL * i + r + 1, r % 2, 1 - r % 2, carry)
            return carry

        n_body = (n - 1) // ATT_UNROLL
        carry = lax.fori_loop(0, n_body, body, (m, l, alpha, smax))
        for j in range(n_body * ATT_UNROLL + 1, n):
            carry = half_step(j, (j - 1) % 2, j % 2, carry)
        m, l, alpha, _ = carry
        accumulate(alpha, values(n - 1), p_buf[(n - 1) % 2])
    o = acc_ref[...] / l
    dl = dl_ref[...]
    lam = (jnp.exp(jnp.sum(dl[0:1] * dl[1:2], axis=1, keepdims=True))
           - jnp.exp(jnp.sum(dl[2:3] * dl[3:4], axis=1, keepdims=True)) + lam_init)
    d = o[:, :tq] - lam * o[:, tq:]
    y = d * lax.rsqrt(jnp.mean(d * d, axis=0, keepdims=True) + EPS) * g_ref[...] * (1.0 - lam_init)
    o_ref[...] = y.T.astype(BF)


def _attention(qt, k, vt, g_sub, da_lambda, lam_init, dims, latent, rows=None, fill=None):
    nb, s, ctx, t_lat = dims["nb"], dims["s"], dims["ctx"], dims["t_lat"]
    cpb = ctx // VT_CHUNK
    tq = TQ if latent else min(TQ, ctx)
    g_sub_b = jnp.broadcast_to(g_sub[:, None], (HEAD_W, tq))
    common_tail = [pl.BlockSpec((HEAD_W, tq), lambda b, h, qi: (0, 0)),
                   pl.BlockSpec((4, HEAD_DIM), lambda b, h, qi: (0, 0))]
    kc_spec = pl.BlockSpec((ctx, HEAD_W), lambda b, h, qi: (t_lat // ctx + b, h))
    vtc_spec = pl.BlockSpec((cpb, HEAD_W, VT_CHUNK), lambda b, h, qi: (t_lat // ctx + b, h, 0))
    scratch = [pltpu.VMEM((HEAD_W, 2 * tq), F32)]
    if latent:
        nq = s // tq
        in_specs = [pl.BlockSpec((HEAD_W, tq), lambda b, h, qi: (h, b * nq + qi)), kc_spec, vtc_spec,
                    pl.BlockSpec((s, HEAD_W), lambda b, h, qi: (b, h)),
                    pl.BlockSpec((s // VT_CHUNK, HEAD_W, VT_CHUNK), lambda b, h, qi: (b, h, 0))] + common_tail
        args = (qt, k, vt, k, vt, g_sub_b, da_lambda)
        row0, n_lat_chunks, aliases = 0, s // TK, {}
        scratch += [pltpu.VMEM((2, TK, 2 * tq), F32), pltpu.VMEM((2, TK, 2 * tq), BF)]
    else:
        nq = ctx // tq
        in_specs = [pl.BlockSpec((HEAD_W, tq), lambda b, h, qi: (h, t_lat // tq + b * nq + qi)),
                    kc_spec, vtc_spec] + common_tail
        in_specs.append(pl.BlockSpec(memory_space=pl.ANY))
        args = (qt, k, vt, g_sub_b, da_lambda, fill)
        row0, n_lat_chunks, rows, aliases = t_lat // tq, 0, fill.shape[0], {5: 0}
    return pl.pallas_call(
        functools.partial(_attn_body, n_ctx_chunks=cpb, n_lat_chunks=n_lat_chunks, lam_init=lam_init),
        grid=(nb, HEADS, nq),
        in_specs=in_specs,
        out_specs=pl.BlockSpec((tq, HEAD_W), lambda b, h, qi: (row0 + b * nq + qi, h)),
        out_shape=jax.ShapeDtypeStruct((rows, HEADS * HEAD_W), BF),
        input_output_aliases=aliases,
        scratch_shapes=scratch,
        compiler_params=_cparams(("arbitrary", "arbitrary", "arbitrary"), VMEM_LIMIT),
    )(*args)


def _fft_a_body(fa_ref, p_ref, q_ref, zr_ref, zi_ref):
    x = jnp.concatenate([p_ref[...], q_ref[...]], axis=0)
    z = jnp.dot(fa_ref[...], x, preferred_element_type=F32)
    zr_ref[...] = z[:FFT_N2].astype(BF)
    zi_ref[...] = z[FFT_N2:].astype(BF)


def _fft_b_body(fb_ref, tc_ref, ts_ref, zr_ref, zi_ref, o_ref):
    nk = zr_ref.shape[0]
    parts_r, parts_i = [], []
    for j in range(nk):
        zr = zr_ref[j].astype(F32)
        zi = zi_ref[j].astype(F32)
        tc = jnp.concatenate([tc_ref[j]] * 4, axis=1)
        ts = jnp.concatenate([ts_ref[j]] * 4, axis=1)
        parts_r.append((zr * tc - zi * ts).astype(BF))
        parts_i.append((zr * ts + zi * tc).astype(BF))
    rhs = jnp.concatenate([jnp.concatenate(parts_r, axis=1), jnp.concatenate(parts_i, axis=1)], axis=0)
    o_ref[...] = jnp.dot(fb_ref[...], rhs, preferred_element_type=F32).astype(BF)


def _fft_ctx_body(f_ref, p_ref, q_ref, _, o_ref):
    x = jnp.concatenate([p_ref[...], q_ref[...]], axis=0)
    o_ref[...] = jnp.dot(f_ref[...], x, preferred_element_type=F32).astype(BF)


def _fft_tables(s, ctx):
    n1, n2 = s // FFT_N2, FFT_N2
    norm = 1.0 / math.sqrt(s * LANES)
    a = 2 * np.pi * np.outer(np.arange(n2), np.arange(n2)) / n2
    c, sn = np.cos(a), np.sin(a)
    fa = np.block([[c, -sn], [sn, c]]) * norm
    tw = 2 * np.pi * np.outer(np.arange(n2), np.arange(n1)) / s
    tc = np.repeat(np.cos(tw)[:, :, None], LANES, axis=2)
    ts = np.repeat(np.sin(tw)[:, :, None], LANES, axis=2)
    b = 2 * np.pi * np.outer(np.arange(n1), np.arange(n1)) / n1
    fb = np.concatenate([np.cos(b), -np.sin(b)], axis=1)
    ch = 2 * np.pi * np.outer(np.arange(LANES), np.arange(LANES)) / LANES
    cs = np.concatenate([np.cos(ch), np.sin(ch)], axis=1)
    cx = 2 * np.pi * np.outer(np.arange(ctx), np.arange(ctx)) / ctx
    fctx = np.concatenate([np.cos(cx), -np.sin(cx)], axis=1) / math.sqrt(ctx * LANES)
    return dict(fa=jnp.asarray(fa, BF), tc=jnp.asarray(tc, F32), ts=jnp.asarray(ts, F32),
                fb=jnp.asarray(fb, BF), cs=jnp.asarray(cs, BF), fctx=jnp.asarray(fctx, BF))


def _fourier_latent(p, q, tabs, dims, rows):
    nb, s = dims["nb"], dims["s"]
    n1 = s // FFT_N2
    wide = n1 * BW
    wc = min(4096, wide)
    p2 = p.reshape(-1, wide)
    q2 = q.reshape(-1, wide)
    zshape = jax.ShapeDtypeStruct((nb * FFT_N2, wide), BF)
    zr, zi = pl.pallas_call(
        _fft_a_body,
        grid=(nb, wide // wc),
        in_specs=[pl.BlockSpec((2 * FFT_N2, 2 * FFT_N2), lambda b, j: (0, 0)),
                  pl.BlockSpec((FFT_N2, wc), lambda b, j: (b, j)),
                  pl.BlockSpec((FFT_N2, wc), lambda b, j: (b, j))],
        out_specs=[pl.BlockSpec((FFT_N2, wc), lambda b, j: (b, j))] * 2,
        out_shape=[zshape, zshape],
        compiler_params=_cparams(("arbitrary", "arbitrary"), VMEM_LIMIT),
    )(tabs["fa"], p2, q2)
    kb = 8
    zr4 = zr.reshape(nb, FFT_N2, n1, BW)
    zi4 = zi.reshape(nb, FFT_N2, n1, BW)
    y = pl.pallas_call(
        _fft_b_body,
        grid=(FFT_N2 // kb, nb),
        in_specs=[pl.BlockSpec((n1, 2 * n1), lambda kk, b: (0, 0)),
                  pl.BlockSpec((kb, n1, LANES), lambda kk, b: (kk, 0, 0)),
                  pl.BlockSpec((kb, n1, LANES), lambda kk, b: (kk, 0, 0)),
                  pl.BlockSpec((None, kb, n1, BW), lambda kk, b: (b, kk, 0, 0)),
                  pl.BlockSpec((None, kb, n1, BW), lambda kk, b: (b, kk, 0, 0))],
        out_specs=pl.BlockSpec((n1, kb * BW), lambda kk, b: (b, kk)),
        out_shape=jax.ShapeDtypeStruct((rows // FFT_N2, FFT_N2 * BW), BF),
        compiler_params=_cparams(("arbitrary", "arbitrary"), VMEM_LIMIT),
    )(tabs["fb"], tabs["tc"], tabs["ts"], zr4, zi4)
    return y.reshape(rows, BW)


def _fourier_ctx(p, q, tabs, dims, fill):
    nb, ctx, t_lat = dims["nb"], dims["ctx"], dims["t_lat"]
    return pl.pallas_call(
        _fft_ctx_body,
        grid=(nb,),
        in_specs=[pl.BlockSpec((ctx, 2 * ctx), lambda b: (0, 0)),
                  pl.BlockSpec((ctx, BW), lambda b: (t_lat // ctx + b, 0)),
                  pl.BlockSpec((ctx, BW), lambda b: (t_lat // ctx + b, 0)),
                  pl.BlockSpec(memory_space=pl.ANY)],
        out_specs=pl.BlockSpec((ctx, BW), lambda b: (t_lat // ctx + b, 0)),
        out_shape=jax.ShapeDtypeStruct(fill.shape, BF),
        input_output_aliases={3: 0},
        compiler_params=_cparams(("arbitrary",), VMEM_LIMIT),
    )(tabs["fctx"], p, q, fill)


def _merge_body(xa_ref, xb_ref, mod_ref, g1_ref, a_ref, u_ref, up_ref, un_ref, sb_ref, yf_ref, px_ref, pp_ref, pn_ref,
                wconv_ref, wpool_ref, pscale_ref, wb_ref, wg_ref, bg_ref, wo_ref, g2_ref, wr_ref, br_ref,
                xo_ref, h2_ref, route_ref, cnt_ref, carry_ref, *, nlat, tps, tpc, s, ctx, n_a):
    i = pl.program_id(0)
    is_lat = i < nlat
    j = jnp.where(is_lat, i % tps, (i - nlat) % tpc)
    first = j == 0
    last = j == jnp.where(is_lat, tps, tpc) - 1
    seq_len = jnp.where(is_lat, s, ctx)

    @pl.when(i == 0)
    def _():
        carry_ref[...] = jnp.zeros_like(carry_ref)

    x = _token_tile(xa_ref, xb_ref, n_a)
    h = _modnorm(x, g1_ref[...], mod_ref[0:1, :], mod_ref[1:2, :]).astype(BF)
    rowi = lax.broadcasted_iota(jnp.int32, (TM, 1), 0)

    u = u_ref[...].astype(F32)
    u_prev = jnp.where(first, 0.0, up_ref[...].astype(F32)[HALO - 1:HALO, :])
    u_next = jnp.where(last, 0.0, un_ref[...].astype(F32)[0:1, :])
    u_dn = jnp.where(rowi == 0, u_prev, pltpu.roll(u, 1, axis=0))
    u_up = jnp.where(rowi == TM - 1, u_next, pltpu.roll(u, TM - 1, axis=0))
    wc = wconv_ref[...]
    y_conv = sb_ref[...].astype(F32) * (wc[0:1] * u_dn + wc[1:2] * u + wc[2:3] * u_up)

    px = px_ref[...].astype(F32)
    ext = jnp.concatenate([jnp.where(first, 0.0, pp_ref[...].astype(F32)), px,
                           jnp.where(last, 0.0, pn_ref[...].astype(F32))], axis=0)
    ext_len = TM + 2 * HALO
    pos = j * TM + rowi
    pooled_proj = []
    for g, wd in enumerate(POOL_WINDOWS):
        e = ext[:, g * LANES:(g + 1) * LANES]
        win = e + pltpu.roll(e, 1, axis=0)
        half = 1
        while 2 * half < wd:
            win = pltpu.roll(win, half, axis=0) + pltpu.roll(win, ext_len - half, axis=0)
            half *= 2
        win = win[HALO:HALO + TM]
        cnt = (jnp.minimum(pos + wd // 2, seq_len) - jnp.maximum(pos - wd // 2, 0)).astype(F32)
        pooled = (win / cnt - px[:, g * LANES:(g + 1) * LANES]).astype(BF)
        pooled_proj.append(jnp.dot(pooled, wpool_ref[g], preferred_element_type=F32))
    y_pool = jnp.concatenate(pooled_proj, axis=1) * pscale_ref[...]

    ys = (a_ref[...], y_conv.astype(BF), yf_ref[...], y_pool.astype(BF))
    acc = jnp.zeros((TM, D), F32)
    for kk in range(4):
        pk = jnp.dot(ys[kk], wb_ref[kk], preferred_element_type=F32)
        gk = _sigmoid(jnp.dot(h, wg_ref[:, kk * D:(kk + 1) * D], preferred_element_type=F32)
                      + bg_ref[:, kk * D:(kk + 1) * D])
        acc = acc + gk * pk
    xn = x + mod_ref[2:3, :] * jnp.dot(acc.astype(BF), wo_ref[...], preferred_element_type=F32)
    xo_ref[...] = xn

    h2 = _modnorm(xn, g2_ref[...], mod_ref[3:4, :], mod_ref[4:5, :])
    h2_ref[...] = h2
    logits = jnp.dot(h2.astype(BF), wr_ref[...], preferred_element_type=F32) + br_ref[...]
    lane = lax.broadcasted_iota(jnp.int32, (TM, LANES), 1).astype(F32)
    cur = logits
    vals, idxs = [], []
    for _ in range(TOP_K):
        mx = jnp.max(cur, axis=1, keepdims=True)
        idx = jnp.min(jnp.where(cur == mx, lane, float(LANES)), axis=1, keepdims=True)
        vals.append(mx)
        idxs.append(idx)
        cur = jnp.where(lane == idx, -jnp.inf, cur)
    ex = [jnp.exp(v - vals[0]) for v in vals]
    den = ex[0] + ex[1] + ex[2] + ex[3]
    member = jnp.zeros((TM, LANES), F32)
    for idx in idxs:
        member = member + jnp.where(lane == idx, 1.0, 0.0)
    tri = (lax.broadcasted_iota(jnp.int32, (TM, TM), 0) > lax.broadcasted_iota(jnp.int32, (TM, TM), 1))
    before = jnp.dot(jnp.where(tri, 1.0, 0.0).astype(BF), member.astype(BF), preferred_element_type=F32)
    rank_all = carry_ref[0:1, :] + before
    route = jnp.zeros((TM, LANES), F32)
    for r in range(TOP_K):
        rank = jnp.sum(jnp.where(lane == idxs[r], rank_all, 0.0), axis=1, keepdims=True)
        route = (route + jnp.where(lane == float(r), idxs[r], 0.0)
                 + jnp.where(lane == float(TOP_K + r), ex[r] / den, 0.0)
                 + jnp.where(lane == float(2 * TOP_K + r), rank, 0.0))
    route_ref[...] = route
    carry_ref[...] = carry_ref[...] + jnp.sum(member, axis=0, keepdims=True)
    cnt_ref[...] = carry_ref[...]


def _merge(xa, xb, mods, g1, a, u, sb, yf, px, wconv, wpool, pscale, wb, wg, bg, wo, g2, wr, br, dims, rows):
    nlat, tps, tpc, nb = dims["nlat"], dims["tps"], dims["tpc"], dims["nb"]
    nhalo = TM // HALO
    last_halo = u.shape[0] // HALO - 1
    tile = lambda w: pl.BlockSpec((TM, w), lambda i: (i, 0))
    prev = pl.BlockSpec((HALO, BW), lambda i: (jnp.maximum(i * nhalo - 1, 0), 0))
    nxt = pl.BlockSpec((HALO, BW), lambda i: (jnp.minimum((i + 1) * nhalo, last_halo), 0))
    const = lambda shape: pl.BlockSpec(shape, lambda i: (0,) * len(shape))
    return pl.pallas_call(
        functools.partial(_merge_body, nlat=nlat, tps=tps, tpc=tpc, s=dims["s"], ctx=dims["ctx"],
                          n_a=xa.shape[0] // TM),
        grid=(rows // TM,),
        in_specs=_token_specs(xa, xb) + [
                  pl.BlockSpec((None, 8, D), lambda i: (jnp.minimum(i // tps, nb), 0, 0)),
                  const((1, D)), tile(BW), tile(BW), prev, nxt, tile(BW), tile(BW), tile(BW), prev, nxt,
                  const((8, BW)), const((4, LANES, LANES)), const((1, BW)), const((4, BW, D)),
                  const((D, 4 * D)), const((1, 4 * D)), const((D, D)), const((1, D)),
                  const((D, LANES)), const((1, LANES))],
        out_specs=[tile(D), tile(D), tile(LANES), pl.BlockSpec((8, LANES), lambda i: (0, 0))],
        out_shape=[jax.ShapeDtypeStruct((rows, D), F32), jax.ShapeDtypeStruct((rows, D), F32),
                   jax.ShapeDtypeStruct((rows, LANES), F32), jax.ShapeDtypeStruct((8, LANES), F32)],
        scratch_shapes=[pltpu.VMEM((8, LANES), F32)],
        compiler_params=_cparams(("arbitrary",), VMEM_LIMIT),
    )(xa, xb, mods, g1, a, u, u, u, sb, yf, px, px, px, wconv, wpool, pscale, wb, wg, bg, wo, g2, wr, br)


def _wait_row_copies(src_ref, dst_ref, sem, n):
    for _ in range(n):
        pltpu.make_async_copy(src_ref.at[pl.ds(0, 1)], dst_ref.at[pl.ds(0, 1)], sem).wait()


def _dispatch_body(dest_ref, h_ref, xs_ref, sem):
    def body(t, c):
        for r in range(TOP_K):
            pltpu.make_async_copy(h_ref.at[pl.ds(t, 1)], xs_ref.at[pl.ds(dest_ref[0, t * TOP_K + r], 1)], sem).start()
        return c
    lax.fori_loop(0, TM, body, 0, unroll=8)
    _wait_row_copies(h_ref, xs_ref, sem, TM * TOP_K)


def _dispatch(dest3, h2, n_slots):
    rows = h2.shape[0]
    return pl.pallas_call(
        _dispatch_body,
        grid=(rows // TM,),
        in_specs=[pl.BlockSpec((None, 1, TM * TOP_K), lambda i: (i, 0, 0), memory_space=pltpu.SMEM),
                  pl.BlockSpec((TM, D), lambda i: (i, 0))],
        out_specs=pl.BlockSpec(memory_space=pl.ANY),
        out_shape=jax.ShapeDtypeStruct((n_slots, D), F32),
        scratch_shapes=[pltpu.SemaphoreType.DMA(())],
        compiler_params=_cparams(("arbitrary",), VMEM_LIMIT),
    )(dest3, h2)


def _expert_body(te_ref, na_ref, xs_ref, wg_ref, bg_ref, wu_ref, bu_ref, wd_ref, bd_ref, y_ref, wgb, wub, wdb):
    j = pl.program_id(0)

    @pl.when(j < na_ref[0])
    def _():
        @pl.when((j == 0) | (te_ref[j] != te_ref[jnp.maximum(j - 1, 0)]))
        def _():
            wgb[...] = wg_ref[...].astype(BF)
            wub[...] = wu_ref[...].astype(BF)
            wdb[...] = wd_ref[...].astype(BF)

        x = xs_ref[...].astype(BF)
        a = jnp.minimum(jnp.dot(x, wgb[...], preferred_element_type=F32) + bg_ref[...], SWIGLU_LIMIT)
        u = jnp.clip(jnp.dot(x, wub[...], preferred_element_type=F32) + bu_ref[...], -SWIGLU_LIMIT, SWIGLU_LIMIT)
        act = a * _sigmoid(SWIGLU_ALPHA * a) * (u + 1.0)
        y_ref[...] = jnp.dot(act.astype(BF), wdb[...], preferred_element_type=F32) + bd_ref[...]


def _experts(tile_expert, n_active, xs, layer, wg, bg, wu, bu, wd, bd):
    n_slots = xs.shape[0]
    f = wg.shape[3]
    slot = lambda j, te, na: (jnp.minimum(j, na[0] - 1), 0)
    wsel = lambda j, te, na: (layer, te[j], 0, 0)
    return pl.pallas_call(
        _expert_body,
        grid_spec=pltpu.PrefetchScalarGridSpec(
            num_scalar_prefetch=2,
            grid=(n_slots // TME,),
            in_specs=[pl.BlockSpec((TME, D), slot),
                      pl.BlockSpec((None, None, D, f), wsel), pl.BlockSpec((None, None, 1, f), wsel),
                      pl.BlockSpec((None, None, D, f), wsel), pl.BlockSpec((None, None, 1, f), wsel),
                      pl.BlockSpec((None, None, f, D), wsel), pl.BlockSpec((None, None, 1, D), wsel)],
            out_specs=pl.BlockSpec((TME, D), slot),
            scratch_shapes=[pltpu.VMEM((D, f), BF), pltpu.VMEM((D, f), BF), pltpu.VMEM((f, D), BF)]),
        out_shape=jax.ShapeDtypeStruct((n_slots, D), F32),
        compiler_params=_cparams(("arbitrary",), VMEM_LIMIT),
    )(tile_expert, n_active, xs, wg, bg, wu, bu, wd, bd)


def _combine_body(dest_ref, x_ref, mod_ref, route_ref, gf_ref, y_ref, o_ref, buf, sem, *, final):
    def body(t, c):
        for r in range(TOP_K):
            pltpu.make_async_copy(y_ref.at[pl.ds(dest_ref[0, t * TOP_K + r], 1)], buf.at[r, pl.ds(t, 1)], sem).start()
        return c
    lax.fori_loop(0, TM, body, 0, unroll=8)
    _wait_row_copies(y_ref, buf.at[0], sem, TM * TOP_K)
    route = route_ref[...]
    acc = jnp.zeros((TM, D), F32)
    for r in range(TOP_K):
        acc = acc + route[:, TOP_K + r:TOP_K + r + 1] * buf[r]
    xn = x_ref[...] + mod_ref[5:6, :] * acc
    if final:
        xn = xn * lax.rsqrt(jnp.mean(xn * xn, axis=-1, keepdims=True) + EPS) * gf_ref[...]
    o_ref[...] = xn


def _combine(dest3, x_mid, mods, route, g_final, y, dims, final):
    rows = x_mid.shape[0]
    tps, nb = dims["tps"], dims["nb"]
    return pl.pallas_call(
        functools.partial(_combine_body, final=final),
        grid=(rows // TM,),
        in_specs=[pl.BlockSpec((None, 1, TM * TOP_K), lambda i: (i, 0, 0), memory_space=pltpu.SMEM),
                  pl.BlockSpec((TM, D), lambda i: (i, 0)),
                  pl.BlockSpec((None, 8, D), lambda i: (jnp.minimum(i // tps, nb), 0, 0)),
                  pl.BlockSpec((TM, LANES), lambda i: (i, 0)),
                  pl.BlockSpec((1, D), lambda i: (0, 0)),
                  pl.BlockSpec(memory_space=pl.ANY)],
        out_specs=pl.BlockSpec((TM, D), lambda i: (i, 0)),
        out_shape=jax.ShapeDtypeStruct((rows, D), F32),
        scratch_shapes=[pltpu.VMEM((TOP_K, TM, D), F32), pltpu.SemaphoreType.DMA(())],
        compiler_params=_cparams(("arbitrary",), VMEM_LIMIT),
    )(dest3, x_mid, mods, route, g_final, y)


def _moe(x_mid, h2, route, counts, mods, g_final, layer, wg, bg, wu, bu, wd, bd, dims, final):
    rows = x_mid.shape[0]
    n_exp = wg.shape[1]
    n_slots = rows * TOP_K + n_exp * TME
    e = route[:, 0:TOP_K].astype(jnp.int32)
    rank = route[:, 2 * TOP_K:3 * TOP_K].astype(jnp.int32)
    cnt = counts[0, :n_exp].astype(jnp.int32)
    cnt_pad = ((cnt + TME - 1) // TME) * TME
    offs_end = jnp.cumsum(cnt_pad)
    offs = offs_end - cnt_pad
    onehot = e[:, :, None] == jnp.arange(n_exp, dtype=jnp.int32)[None, None, :]
    dest = jnp.sum(jnp.where(onehot, offs[None, None, :], 0), axis=-1) + rank
    dest3 = dest.reshape(rows // TM, 1, TM * TOP_K)
    tile_start = jnp.arange(n_slots // TME, dtype=jnp.int32) * TME
    tile_expert = jnp.minimum(jnp.sum(tile_start[:, None] >= offs_end[None, :], axis=1), n_exp - 1).astype(jnp.int32)
    n_active = (offs_end[-1:] // TME).astype(jnp.int32)
    xs = _dispatch(dest3, h2, n_slots)
    y = _experts(tile_expert, n_active, xs, layer, wg, bg, wu, bu, wd, bd)
    return _combine(dest3, x_mid, mods, route, g_final, y, dims, final)


def _rope_tables(s, ctx):
    rows = s // GRID_W
    row = np.repeat(np.arange(rows), GRID_W).astype(np.float32)
    col = np.tile(np.arange(GRID_W), rows).astype(np.float32)
    half = HEAD_DIM // 2
    inv = (np.float32(ROPE_BASE) ** (-np.arange(0, half, 2, dtype=np.float32) / half)).astype(np.float32)
    ar = row[:, None] * inv
    ac = col[:, None] * inv
    ang = np.concatenate([ar, ar, ac, ac], axis=-1)
    cos = np.concatenate([np.cos(ang), np.ones((ctx, HEAD_DIM), np.float32)], axis=0)
    sin = np.concatenate([np.sin(ang), np.zeros((ctx, HEAD_DIM), np.float32)], axis=0)
    return (jnp.asarray(np.tile(cos, (1, 2)), F32), jnp.asarray(np.tile(sin, (1, 2)), F32))


def _rot_columns(w):
    d, n = w.shape
    w4 = w.reshape(d, n // HEAD_DIM, 4, HEAD_DIM // 4)
    return jnp.stack([-w4[:, :, 1], w4[:, :, 0], -w4[:, :, 3], w4[:, :, 2]], axis=2).reshape(d, n)


def kernel(x, c, ctx, c_ctx, w_ada, b_ada, g_norm1, w_in, da_lambda, g_subln, w_conv, w_pool, pool_scale,
           w_branch, w_mgate, b_mgate, w_out, g_norm2, w_router, b_router, w_e_gate, b_e_gate, w_e_up,
           b_e_up, w_e_down, b_e_down, g_final):
    nb, s, _ = x.shape
    nctx = ctx.shape[1]
    depth = w_in.shape[0]
    n_exp = w_router.shape[2]
    t_lat, t_ctx = nb * s, nb * nctx
    dims = dict(nb=nb, s=s, ctx=nctx, t_lat=t_lat, t_all=t_lat + t_ctx, nlat=t_lat // TM, tps=s // TM, tpc=nctx // TM)
    assert s % TK == 0 and s // TK >= 2 and s % (FFT_N2 * 8) == 0 and nctx % TM == 0 and nb < 16
    assert TM == VT_CHUNK and nctx % VT_CHUNK == 0 and ATT_UNROLL % 2 == 0

    cos_t, sin_t = _rope_tables(s, nctx)
    tabs = _fft_tables(s, nctx)
    c16 = jnp.concatenate([c, c_ctx[None, :], jnp.zeros((15 - nb, D), F32)], axis=0)
    mods_all = _adaln(c16, w_ada, b_ada).reshape(depth, 16, 6, D)
    mods_all = jnp.concatenate([mods_all, jnp.zeros((depth, 16, 2, D), F32)], axis=2)

    xa, xb = x.reshape(t_lat, D), ctx.reshape(t_ctx, D)
    for l in range(depth):
        last = l == depth - 1
        lam_init = 0.8 - 0.6 * math.exp(-0.3 * l)
        mods = mods_all[l]
        w_l = w_in[l]
        w_ext = jnp.concatenate([w_l, _rot_columns(w_l[:, :BW]), _rot_columns(w_l[:, BW:2 * BW])], axis=1).astype(BF)
        qt, k, vt, u, sb, p, qq, px = _inproj(xa, xb, mods, g_norm1[l][None, :], w_ext, cos_t, sin_t, tabs["cs"], dims)

        rows = t_lat if last else t_lat + t_ctx
        a = _attention(qt, k, vt, g_subln[l], da_lambda[l], lam_init, dims, latent=True, rows=rows)
        yf = _fourier_latent(p, qq, tabs, dims, rows)
        if not last:
            a = _attention(qt, k, vt, g_subln[l], da_lambda[l], lam_init, dims, latent=False, fill=a)
            yf = _fourier_ctx(p, qq, tabs, dims, yf)

        wconv = jnp.concatenate([w_conv[l], jnp.zeros((5, BW), F32)], axis=0)
        wr = jnp.concatenate([w_router[l], jnp.zeros((D, LANES - n_exp), F32)], axis=1).astype(BF)
        br = jnp.concatenate([b_router[l], jnp.full((LANES - n_exp,), -1e30, F32)])[None, :]
        x_mid, h2, route, counts = _merge(
            xa, xb, mods, g_norm1[l][None, :], a, u, sb, yf, px, wconv, w_pool[l].astype(BF), pool_scale[l][None, :],
            w_branch[l].astype(BF), w_mgate[l].astype(BF), b_mgate[l][None, :], w_out[l].astype(BF),
            g_norm2[l][None, :], wr, br, dims, rows)
        x_new = _moe(x_mid, h2, route, counts, mods, g_final[None, :], l,
                     w_e_gate, b_e_gate[:, :, None, :], w_e_up, b_e_up[:, :, None, :],
                     w_e_down, b_e_down[:, :, None, :], dims, final=last)
        if last:
            return x_new.reshape(nb, s, D)
        xa = xb = x_new
```

```python
import functools
import math

import numpy as np
import jax
import jax.numpy as jnp
from jax import lax
from jax.experimental import pallas as pl
from jax.experimental.pallas import tpu as pltpu

F32 = jnp.float32
BF = jnp.bfloat16

D = 1024
HEADS = 4
HEAD_DIM = 64
HEAD_W = 128
BW = 512
GRID_W = 64
EPS = 1e-6
ROPE_BASE = 10000.0
POOL_WINDOWS = (2, 4, 8, 16)
TOP_K = 4
SWIGLU_LIMIT = 7.0
SWIGLU_ALPHA = 1.702
FFT_N2 = 128

TM = 256
TME = 512
TQ = 512
TK = 1024
ATT_UNROLL = 2
VT_CHUNK = 256
HALO = 16
LANES = 128
QSCALE = (HEAD_DIM ** -0.5) * math.log2(math.e)
EXP_HEADROOM = 64.0
NORM_MARGIN = 1.05
VMEM_LIMIT = 56 * 2 ** 20


def _cparams(sem, vmem=None):
    return pltpu.CompilerParams(dimension_semantics=sem, vmem_limit_bytes=vmem)


def _modnorm(x, g, shift, scale):
    y = x * lax.rsqrt(jnp.mean(x * x, axis=-1, keepdims=True) + EPS) * g
    return y * (1.0 + scale) + shift


def _sigmoid(x):
    return 1.0 / (1.0 + jnp.exp(-x))


def _adaln_body(c_ref, w_ref, b_ref, o_ref):
    c = c_ref[...]
    s = c * _sigmoid(c)
    o_ref[...] = jnp.dot(s.astype(BF), w_ref[...].astype(BF), preferred_element_type=F32) + b_ref[...]


def _adaln(c16, w_ada, b_ada):
    nl = w_ada.shape[0]
    tn = 1536
    return pl.pallas_call(
        _adaln_body,
        grid=(nl, 6 * D // tn),
        in_specs=[pl.BlockSpec((16, D), lambda l, j: (0, 0)),
                  pl.BlockSpec((None, D, tn), lambda l, j: (l, 0, j)),
                  pl.BlockSpec((None, 1, tn), lambda l, j: (l, 0, j))],
        out_specs=pl.BlockSpec((None, 16, tn), lambda l, j: (l, 0, j)),
        out_shape=jax.ShapeDtypeStruct((nl, 16, 6 * D), F32),
        compiler_params=_cparams(("arbitrary", "arbitrary"), VMEM_LIMIT),
    )(c16, w_ada, b_ada.reshape(nl, 1, 6 * D))


def _token_tile(xa_ref, xb_ref, n_a):
    return jnp.where(pl.program_id(0) < n_a, xa_ref[...], xb_ref[...])


def _token_specs(xa, xb):
    n_a, n_b = xa.shape[0] // TM, xb.shape[0] // TM
    return [pl.BlockSpec((TM, D), lambda i: (jnp.minimum(i, n_a - 1), 0)),
            pl.BlockSpec((TM, D), lambda i: (jnp.clip(i - n_a, 0, n_b - 1), 0))]


def _inproj_body(xa_ref, xb_ref, mod_ref, g_ref, w_ref, cos_ref, sin_ref, cs_ref,
                 q_ref, k_ref, v_ref, u_ref, sb_ref, p_ref, qq_ref, px_ref, *, n_a):
    h = _modnorm(_token_tile(xa_ref, xb_ref, n_a), g_ref[...], mod_ref[0:1, :], mod_ref[1:2, :]).astype(BF)

    def proj(c0):
        return jnp.dot(h, w_ref[:, c0:c0 + BW], preferred_element_type=F32)

    cos = jnp.concatenate([cos_ref[...]] * 4, axis=1)
    sin = jnp.concatenate([sin_ref[...]] * 4, axis=1)
    q_ref[...] = ((proj(0) * cos + proj(8 * BW) * sin) * QSCALE).T.astype(BF)
    k_ref[...] = (proj(BW) * cos + proj(9 * BW) * sin).astype(BF)
    v_ref[...] = proj(2 * BW).T.astype(BF)
    u_ref[...] = (proj(5 * BW) * proj(3 * BW)).astype(BF)
    sb_ref[...] = proj(4 * BW).astype(BF)
    px_ref[...] = proj(7 * BW).astype(BF)
    fx = proj(6 * BW).astype(BF)
    for g in range(4):
        pq = jnp.dot(fx[:, g * LANES:(g + 1) * LANES], cs_ref[...], preferred_element_type=F32)
        p_ref[:, g * LANES:(g + 1) * LANES] = pq[:, :LANES].astype(BF)
        qq_ref[:, g * LANES:(g + 1) * LANES] = pq[:, LANES:].astype(BF)


def _inproj(xa, xb, mods, g1, w_ext, cos_t, sin_t, cs_tab, dims):
    t_all, nlat, tps, tpc, nb = dims["t_all"], dims["nlat"], dims["tps"], dims["tpc"], dims["nb"]

    def tab_idx(i):
        return (jnp.where(i < nlat, i % tps, tps + (i - nlat) % tpc), 0)

    out = jax.ShapeDtypeStruct((t_all, BW), BF)
    row = pl.BlockSpec((TM, BW), lambda i: (i, 0))
    return pl.pallas_call(
        functools.partial(_inproj_body, n_a=xa.shape[0] // TM),
        grid=(t_all // TM,),
        in_specs=_token_specs(xa, xb) + [
                  pl.BlockSpec((None, 8, D), lambda i: (jnp.minimum(i // tps, nb), 0, 0)),
                  pl.BlockSpec((1, D), lambda i: (0, 0)),
                  pl.BlockSpec((D, 10 * BW), lambda i: (0, 0)),
                  pl.BlockSpec((TM, LANES), tab_idx),
                  pl.BlockSpec((TM, LANES), tab_idx),
                  pl.BlockSpec((LANES, 2 * LANES), lambda i: (0, 0))],
        out_specs=[pl.BlockSpec((BW, TM), lambda i: (0, i)), row,
                   pl.BlockSpec((None, BW, VT_CHUNK), lambda i: (i, 0, 0))] + [row] * 5,
        out_shape=[jax.ShapeDtypeStruct((BW, t_all), BF), out,
                   jax.ShapeDtypeStruct((t_all // VT_CHUNK, BW, VT_CHUNK), BF)] + [out] * 5,
        compiler_params=_cparams(("arbitrary",), VMEM_LIMIT),
    )(xa, xb, mods, g1, w_ext, cos_t, sin_t, cs_tab)


def _attn_body(*refs, n_ctx_chunks, n_lat_chunks, lam_init):
    if n_lat_chunks:
        qt_ref, kc_ref, vtc_ref, kl_ref, vtl_ref, g_ref, dl_ref, o_ref, acc_ref, s_buf, p_buf, l_ref, knorm_ref = refs
    else:
        qt_ref, kc_ref, vtc_ref, g_ref, dl_ref, _, o_ref, acc_ref = refs
    qt = qt_ref[...]
    tq = qt.shape[1]
    row = lax.broadcasted_iota(jnp.int32, qt.shape, 0)
    zero = jnp.zeros_like(qt)
    q2 = jnp.concatenate([jnp.where(row < HEAD_DIM, qt, zero), jnp.where(row >= HEAD_DIM, qt, zero)], axis=1)

    def scores(kc):
        s = jnp.dot(kc, q2, preferred_element_type=F32)
        return s, jnp.max(s, axis=0, keepdims=True)

    def softmax(s, smax, m, l):
        m_new = jnp.maximum(m, smax)
        p = jnp.exp2(s - m_new)
        alpha = jnp.exp2(m - m_new)
        return p.astype(BF), m_new, alpha * l + jnp.sum(p, axis=0, keepdims=True), alpha

    def accumulate(alpha, vt, p):
        acc_ref[...] = alpha * acc_ref[...] + jnp.dot(vt, p, preferred_element_type=F32)

    m = jnp.full((1, 2 * tq), -1e30, F32)
    l = jnp.zeros((1, 2 * tq), F32)
    acc_ref[...] = jnp.zeros_like(acc_ref)
    ctx_scores = [scores(kc_ref[c * VT_CHUNK:(c + 1) * VT_CHUNK, :]) for c in range(n_ctx_chunks)]
    if n_lat_chunks:
        per = TK // VT_CHUNK
        n = n_lat_chunks

        def keys(j):
            return kl_ref[pl.ds(pl.multiple_of(j * TK, TK), TK), :]

        s_buf[0] = jnp.dot(keys(0), q2, preferred_element_type=F32)
    for c in range(n_ctx_chunks):
        p, m, l, alpha = softmax(*ctx_scores[c], m, l)
        accumulate(alpha, vtc_ref[c], p)
    if n_lat_chunks:

        def values(j):
            parts = [vtl_ref[j * per + r] for r in range(per)]
            return parts[0] if per == 1 else jnp.concatenate(parts, axis=1)

        @pl.when(pl.program_id(2) == 0)
        def _():
            d_i = lax.broadcasted_iota(jnp.int32, (HEAD_W, LANES), 0)
            c_i = lax.broadcasted_iota(jnp.int32, (HEAD_W, LANES), 1)
            sel = jnp.where((d_i >= HEAD_DIM) == (c_i == 1), 1.0, 0.0) * jnp.where(c_i < 2, 1.0, 0.0)
            best = jnp.zeros((1, LANES), F32)
            for j in range(n):
                kf = keys(j).astype(F32)
                n2 = jnp.dot((kf * kf).astype(BF), sel.astype(BF), preferred_element_type=F32)
                best = jnp.maximum(best, jnp.max(n2, axis=0, keepdims=True))
            lane2 = lax.broadcasted_iota(jnp.int32, (1, 2 * tq), 1)
            knorm_ref[...] = jnp.sqrt(jnp.where(lane2 < tq, best[:, 0:1], best[:, 1:2]))

        qf = q2.astype(F32)
        bound = jnp.sqrt(jnp.sum(qf * qf, axis=0, keepdims=True)) * knorm_ref[...] * NORM_MARGIN
        gap = jnp.max(bound - m)
        l_ref[...] = l

        @pl.when(gap <= EXP_HEADROOM)
        def _():
            lsum = l
            s = s_buf[0]
            for j in range(n):
                s_next = jnp.dot(keys(j + 1), q2, preferred_element_type=F32) if j + 1 < n else None
                p = jnp.exp2(s - m)
                lsum = lsum + jnp.sum(p, axis=0, keepdims=True)
                acc_ref[...] += jnp.dot(values(j), p.astype(BF), preferred_element_type=F32)
                s = s_next
            l_ref[...] = lsum

        @pl.when(gap > EXP_HEADROOM)
        def _():
            _attn_exact_latent(m, l, n, keys, values, scores, softmax, accumulate, s_buf, p_buf, l_ref)

        l = l_ref[...]
    o = acc_ref[...] / l
    dl = dl_ref[...]
    lam = (jnp.exp(jnp.sum(dl[0:1] * dl[1:2], axis=1, keepdims=True))
           - jnp.exp(jnp.sum(dl[2:3] * dl[3:4], axis=1, keepdims=True)) + lam_init)
    d = o[:, :tq] - lam * o[:, tq:]
    y = d * lax.rsqrt(jnp.mean(d * d, axis=0, keepdims=True) + EPS) * g_ref[...] * (1.0 - lam_init)
    o_ref[...] = y.T.astype(BF)


def _attn_exact_latent(m, l, n, keys, values, scores, softmax, accumulate, s_buf, p_buf, l_ref):
    def half_step(j, prev, cur, carry):
        m, l, alpha_prev, smax = carry
        accumulate(alpha_prev, values(j - 1), p_buf[prev])
        if isinstance(j, int) and j + 1 >= n:
            smax_next = smax
        else:
            s_buf[prev], smax_next = scores(keys(jnp.minimum(j + 1, n - 1)))
        p, m, l, alpha = softmax(s_buf[cur], smax, m, l)
        p_buf[cur] = p
        return m, l, alpha, smax_next

    s0 = s_buf[0]
    p, m, l, alpha = softmax(s0, jnp.max(s0, axis=0, keepdims=True), m, l)
    p_buf[0] = p
    s_buf[1], smax = scores(keys(1))

    def body(i, carry):
        for r in range(ATT_UNROLL):
            carry = half_step(ATT_UNROLL * i + r + 1, r % 2, 1 - r % 2, carry)
        return carry

    n_body = (n - 1) // ATT_UNROLL
    carry = lax.fori_loop(0, n_body, body, (m, l, alpha, smax))
    for j in range(n_body * ATT_UNROLL + 1, n):
        carry = half_step(j, (j - 1) % 2, j % 2, carry)
    m, l, alpha, _ = carry
    accumulate(alpha, values(n - 1), p_buf[(n - 1) % 2])
    l_ref[...] = l


def _attention(qt, k, vt, g_sub, da_lambda, lam_init, dims, latent, rows=None, fill=None):
    nb, s, ctx, t_lat = dims["nb"], dims["s"], dims["ctx"], dims["t_lat"]
    cpb = ctx // VT_CHUNK
    tq = TQ if latent else min(TQ, ctx)
    g_sub_b = jnp.broadcast_to(g_sub[:, None], (HEAD_W, tq))
    common_tail = [pl.BlockSpec((HEAD_W, tq), lambda b, h, qi: (0, 0)),
                   pl.BlockSpec((4, HEAD_DIM), lambda b, h, qi: (0, 0))]
    kc_spec = pl.BlockSpec((ctx, HEAD_W), lambda b, h, qi: (t_lat // ctx + b, h))
    vtc_spec = pl.BlockSpec((cpb, HEAD_W, VT_CHUNK), lambda b, h, qi: (t_lat // ctx + b, h, 0))
    scratch = [pltpu.VMEM((HEAD_W, 2 * tq), F32)]
    if latent:
        nq = s // tq
        in_specs = [pl.BlockSpec((HEAD_W, tq), lambda b, h, qi: (h, b * nq + qi)), kc_spec, vtc_spec,
                    pl.BlockSpec((s, HEAD_W), lambda b, h, qi: (b, h)),
                    pl.BlockSpec((s // VT_CHUNK, HEAD_W, VT_CHUNK), lambda b, h, qi: (b, h, 0))] + common_tail
        args = (qt, k, vt, k, vt, g_sub_b, da_lambda)
        row0, n_lat_chunks, aliases = 0, s // TK, {}
        scratch += [pltpu.VMEM((2, TK, 2 * tq), F32), pltpu.VMEM((2, TK, 2 * tq), BF),
                    pltpu.VMEM((1, 2 * tq), F32), pltpu.VMEM((1, 2 * tq), F32)]
    else:
        nq = ctx // tq
        in_specs = [pl.BlockSpec((HEAD_W, tq), lambda b, h, qi: (h, t_lat // tq + b * nq + qi)),
                    kc_spec, vtc_spec] + common_tail
        in_specs.append(pl.BlockSpec(memory_space=pl.ANY))
        args = (qt, k, vt, g_sub_b, da_lambda, fill)
        row0, n_lat_chunks, rows, aliases = t_lat // tq, 0, fill.shape[0], {5: 0}
    return pl.pallas_call(
        functools.partial(_attn_body, n_ctx_chunks=cpb, n_lat_chunks=n_lat_chunks, lam_init=lam_init),
        grid=(nb, HEADS, nq),
        in_specs=in_specs,
        out_specs=pl.BlockSpec((tq, HEAD_W), lambda b, h, qi: (row0 + b * nq + qi, h)),
        out_shape=jax.ShapeDtypeStruct((rows, HEADS * HEAD_W), BF),
        input_output_aliases=aliases,
        scratch_shapes=scratch,
        compiler_params=_cparams(("arbitrary", "arbitrary", "arbitrary"), VMEM_LIMIT),
    )(*args)


def _fft_a_body(fa_ref, p_ref, q_ref, zr_ref, zi_ref):
    x = jnp.concatenate([p_ref[...], q_ref[...]], axis=0)
    z = jnp.dot(fa_ref[...], x, preferred_element_type=F32)
    zr_ref[...] = z[:FFT_N2].astype(BF)
    zi_ref[...] = z[FFT_N2:].astype(BF)


def _fft_b_body(fb_ref, tc_ref, ts_ref, zr_ref, zi_ref, o_ref):
    nk = zr_ref.shape[0]
    parts_r, parts_i = [], []
    for j in range(nk):
        zr = zr_ref[j].astype(F32)
        zi = zi_ref[j].astype(F32)
        tc = jnp.concatenate([tc_ref[j]] * 4, axis=1)
        ts = jnp.concatenate([ts_ref[j]] * 4, axis=1)
        parts_r.append((zr * tc - zi * ts).astype(BF))
        parts_i.append((zr * ts + zi * tc).astype(BF))
    rhs = jnp.concatenate([jnp.concatenate(parts_r, axis=1), jnp.concatenate(parts_i, axis=1)], axis=0)
    o_ref[...] = jnp.dot(fb_ref[...], rhs, preferred_element_type=F32).astype(BF)


def _fft_ctx_body(f_ref, p_ref, q_ref, _, o_ref):
    x = jnp.concatenate([p_ref[...], q_ref[...]], axis=0)
    o_ref[...] = jnp.dot(f_ref[...], x, preferred_element_type=F32).astype(BF)


def _fft_tables(s, ctx):
    n1, n2 = s // FFT_N2, FFT_N2
    norm = 1.0 / math.sqrt(s * LANES)
    a = 2 * np.pi * np.outer(np.arange(n2), np.arange(n2)) / n2
    c, sn = np.cos(a), np.sin(a)
    fa = np.block([[c, -sn], [sn, c]]) * norm
    tw = 2 * np.pi * np.outer(np.arange(n2), np.arange(n1)) / s
    tc = np.repeat(np.cos(tw)[:, :, None], LANES, axis=2)
    ts = np.repeat(np.sin(tw)[:, :, None], LANES, axis=2)
    b = 2 * np.pi * np.outer(np.arange(n1), np.arange(n1)) / n1
    fb = np.concatenate([np.cos(b), -np.sin(b)], axis=1)
    ch = 2 * np.pi * np.outer(np.arange(LANES), np.arange(LANES)) / LANES
    cs = np.concatenate([np.cos(ch), np.sin(ch)], axis=1)
    cx = 2 * np.pi * np.outer(np.arange(ctx), np.arange(ctx)) / ctx
    fctx = np.concatenate([np.cos(cx), -np.sin(cx)], axis=1) / math.sqrt(ctx * LANES)
    return dict(fa=jnp.asarray(fa, BF), tc=jnp.asarray(tc, F32), ts=jnp.asarray(ts, F32),
                fb=jnp.asarray(fb, BF), cs=jnp.asarray(cs, BF), fctx=jnp.asarray(fctx, BF))


def _fourier_latent(p, q, tabs, dims, rows):
    nb, s = dims["nb"], dims["s"]
    n1 = s // FFT_N2
    wide = n1 * BW
    wc = min(4096, wide)
    p2 = p.reshape(-1, wide)
    q2 = q.reshape(-1, wide)
    zshape = jax.ShapeDtypeStruct((nb * FFT_N2, wide), BF)
    zr, zi = pl.pallas_call(
        _fft_a_body,
        grid=(nb, wide // wc),
        in_specs=[pl.BlockSpec((2 * FFT_N2, 2 * FFT_N2), lambda b, j: (0, 0)),
                  pl.BlockSpec((FFT_N2, wc), lambda b, j: (b, j)),
                  pl.BlockSpec((FFT_N2, wc), lambda b, j: (b, j))],
        out_specs=[pl.BlockSpec((FFT_N2, wc), lambda b, j: (b, j))] * 2,
        out_shape=[zshape, zshape],
        compiler_params=_cparams(("arbitrary", "arbitrary"), VMEM_LIMIT),
    )(tabs["fa"], p2, q2)
    kb = 8
    zr4 = zr.reshape(nb, FFT_N2, n1, BW)
    zi4 = zi.reshape(nb, FFT_N2, n1, BW)
    y = pl.pallas_call(
        _fft_b_body,
        grid=(FFT_N2 // kb, nb),
        in_specs=[pl.BlockSpec((n1, 2 * n1), lambda kk, b: (0, 0)),
                  pl.BlockSpec((kb, n1, LANES), lambda kk, b: (kk, 0, 0)),
                  pl.BlockSpec((kb, n1, LANES), lambda kk, b: (kk, 0, 0)),
                  pl.BlockSpec((None, kb, n1, BW), lambda kk, b: (b, kk, 0, 0)),
                  pl.BlockSpec((None, kb, n1, BW), lambda kk, b: (b, kk, 0, 0))],
        out_specs=pl.BlockSpec((n1, kb * BW), lambda kk, b: (b, kk)),
        out_shape=jax.ShapeDtypeStruct((rows // FFT_N2, FFT_N2 * BW), BF),
        compiler_params=_cparams(("arbitrary", "arbitrary"), VMEM_LIMIT),
    )(tabs["fb"], tabs["tc"], tabs["ts"], zr4, zi4)
    return y.reshape(rows, BW)


def _fourier_ctx(p, q, tabs, dims, fill):
    nb, ctx, t_lat = dims["nb"], dims["ctx"], dims["t_lat"]
    return pl.pallas_call(
        _fft_ctx_body,
        grid=(nb,),
        in_specs=[pl.BlockSpec((ctx, 2 * ctx), lambda b: (0, 0)),
                  pl.BlockSpec((ctx, BW), lambda b: (t_lat // ctx + b, 0)),
                  pl.BlockSpec((ctx, BW), lambda b: (t_lat // ctx + b, 0)),
                  pl.BlockSpec(memory_space=pl.ANY)],
        out_specs=pl.BlockSpec((ctx, BW), lambda b: (t_lat // ctx + b, 0)),
        out_shape=jax.ShapeDtypeStruct(fill.shape, BF),
        input_output_aliases={3: 0},
        compiler_params=_cparams(("arbitrary",), VMEM_LIMIT),
    )(tabs["fctx"], p, q, fill)


def _merge_body(xa_ref, xb_ref, mod_ref, g1_ref, a_ref, u_ref, up_ref, un_ref, sb_ref, yf_ref, px_ref, pp_ref, pn_ref,
                wconv_ref, wpool_ref, pscale_ref, wb_ref, wg_ref, bg_ref, wo_ref, g2_ref, wr_ref, br_ref,
                xo_ref, h2_ref, route_ref, cnt_ref, carry_ref, *, nlat, tps, tpc, s, ctx, n_a):
    i = pl.program_id(0)
    is_lat = i < nlat
    j = jnp.where(is_lat, i % tps, (i - nlat) % tpc)
    first = j == 0
    last = j == jnp.where(is_lat, tps, tpc) - 1
    seq_len = jnp.where(is_lat, s, ctx)

    @pl.when(i == 0)
    def _():
        carry_ref[...] = jnp.zeros_like(carry_ref)

    x = _token_tile(xa_ref, xb_ref, n_a)
    h = _modnorm(x, g1_ref[...], mod_ref[0:1, :], mod_ref[1:2, :]).astype(BF)
    rowi = lax.broadcasted_iota(jnp.int32, (TM, 1), 0)

    u = u_ref[...].astype(F32)
    u_prev = jnp.where(first, 0.0, up_ref[...].astype(F32)[HALO - 1:HALO, :])
    u_next = jnp.where(last, 0.0, un_ref[...].astype(F32)[0:1, :])
    u_dn = jnp.where(rowi == 0, u_prev, pltpu.roll(u, 1, axis=0))
    u_up = jnp.where(rowi == TM - 1, u_next, pltpu.roll(u, TM - 1, axis=0))
    wc = wconv_ref[...]
    y_conv = sb_ref[...].astype(F32) * (wc[0:1] * u_dn + wc[1:2] * u + wc[2:3] * u_up)

    px = px_ref[...].astype(F32)
    ext = jnp.concatenate([jnp.where(first, 0.0, pp_ref[...].astype(F32)), px,
                           jnp.where(last, 0.0, pn_ref[...].astype(F32))], axis=0)
    ext_len = TM + 2 * HALO
    pos = j * TM + rowi
    pooled_proj = []
    for g, wd in enumerate(POOL_WINDOWS):
        e = ext[:, g * LANES:(g + 1) * LANES]
        win = e + pltpu.roll(e, 1, axis=0)
        half = 1
        while 2 * half < wd:
            win = pltpu.roll(win, half, axis=0) + pltpu.roll(win, ext_len - half, axis=0)
            half *= 2
        win = win[HALO:HALO + TM]
        cnt = (jnp.minimum(pos + wd // 2, seq_len) - jnp.maximum(pos - wd // 2, 0)).astype(F32)
        pooled = (win / cnt - px[:, g * LANES:(g + 1) * LANES]).astype(BF)
        pooled_proj.append(jnp.dot(pooled, wpool_ref[g], preferred_element_type=F32))
    y_pool = jnp.concatenate(pooled_proj, axis=1) * pscale_ref[...]

    ys = (a_ref[...], y_conv.astype(BF), yf_ref[...], y_pool.astype(BF))
    acc = jnp.zeros((TM, D), F32)
    for kk in range(4):
        pk = jnp.dot(ys[kk], wb_ref[kk], preferred_element_type=F32)
        gk = _sigmoid(jnp.dot(h, wg_ref[:, kk * D:(kk + 1) * D], preferred_element_type=F32)
                      + bg_ref[:, kk * D:(kk + 1) * D])
        acc = acc + gk * pk
    xn = x + mod_ref[2:3, :] * jnp.dot(acc.astype(BF), wo_ref[...], preferred_element_type=F32)
    xo_ref[...] = xn

    h2 = _modnorm(xn, g2_ref[...], mod_ref[3:4, :], mod_ref[4:5, :])
    h2_ref[...] = h2
    logits = jnp.dot(h2.astype(BF), wr_ref[...], preferred_element_type=F32) + br_ref[...]
    lane = lax.broadcasted_iota(jnp.int32, (TM, LANES), 1).astype(F32)
    cur = logits
    vals, idxs = [], []
    for _ in range(TOP_K):
        mx = jnp.max(cur, axis=1, keepdims=True)
        idx = jnp.min(jnp.where(cur == mx, lane, float(LANES)), axis=1, keepdims=True)
        vals.append(mx)
        idxs.append(idx)
        cur = jnp.where(lane == idx, -jnp.inf, cur)
    ex = [jnp.exp(v - vals[0]) for v in vals]
    den = ex[0] + ex[1] + ex[2] + ex[3]
    member = jnp.zeros((TM, LANES), F32)
    for idx in idxs:
        member = member + jnp.where(lane == idx, 1.0, 0.0)
    tri = (lax.broadcasted_iota(jnp.int32, (TM, TM), 0) > lax.broadcasted_iota(jnp.int32, (TM, TM), 1))
    before = jnp.dot(jnp.where(tri, 1.0, 0.0).astype(BF), member.astype(BF), preferred_element_type=F32)
    rank_all = carry_ref[0:1, :] + before
    route = jnp.zeros((TM, LANES), F32)
    for r in range(TOP_K):
        rank = jnp.sum(jnp.where(lane == idxs[r], rank_all, 0.0), axis=1, keepdims=True)
        route = (route + jnp.where(lane == float(r), idxs[r], 0.0)
                 + jnp.where(lane == float(TOP_K + r), ex[r] / den, 0.0)
                 + jnp.where(lane == float(2 * TOP_K + r), rank, 0.0))
    route_ref[...] = route
    carry_ref[...] = carry_ref[...] + jnp.sum(member, axis=0, keepdims=True)
    cnt_ref[...] = carry_ref[...]


def _merge(xa, xb, mods, g1, a, u, sb, yf, px, wconv, wpool, pscale, wb, wg, bg, wo, g2, wr, br, dims, rows):
    nlat, tps, tpc, nb = dims["nlat"], dims["tps"], dims["tpc"], dims["nb"]
    nhalo = TM // HALO
    last_halo = u.shape[0] // HALO - 1
    tile = lambda w: pl.BlockSpec((TM, w), lambda i: (i, 0))
    prev = pl.BlockSpec((HALO, BW), lambda i: (jnp.maximum(i * nhalo - 1, 0), 0))
    nxt = pl.BlockSpec((HALO, BW), lambda i: (jnp.minimum((i + 1) * nhalo, last_halo), 0))
    const = lambda shape: pl.BlockSpec(shape, lambda i: (0,) * len(shape))
    return pl.pallas_call(
        functools.partial(_merge_body, nlat=nlat, tps=tps, tpc=tpc, s=dims["s"], ctx=dims["ctx"],
                          n_a=xa.shape[0] // TM),
        grid=(rows // TM,),
        in_specs=_token_specs(xa, xb) + [
                  pl.BlockSpec((None, 8, D), lambda i: (jnp.minimum(i // tps, nb), 0, 0)),
                  const((1, D)), tile(BW), tile(BW), prev, nxt, tile(BW), tile(BW), tile(BW), prev, nxt,
                  const((8, BW)), const((4, LANES, LANES)), const((1, BW)), const((4, BW, D)),
                  const((D, 4 * D)), const((1, 4 * D)), const((D, D)), const((1, D)),
                  const((D, LANES)), const((1, LANES))],
        out_specs=[tile(D), tile(D), tile(LANES), pl.BlockSpec((8, LANES), lambda i: (0, 0))],
        out_shape=[jax.ShapeDtypeStruct((rows, D), F32), jax.ShapeDtypeStruct((rows, D), F32),
                   jax.ShapeDtypeStruct((rows, LANES), F32), jax.ShapeDtypeStruct((8, LANES), F32)],
        scratch_shapes=[pltpu.VMEM((8, LANES), F32)],
        compiler_params=_cparams(("arbitrary",), VMEM_LIMIT),
    )(xa, xb, mods, g1, a, u, u, u, sb, yf, px, px, px, wconv, wpool, pscale, wb, wg, bg, wo, g2, wr, br)


def _wait_row_copies(src_ref, dst_ref, sem, n):
    for _ in range(n):
        pltpu.make_async_copy(src_ref.at[pl.ds(0, 1)], dst_ref.at[pl.ds(0, 1)], sem).wait()


def _dispatch_body(dest_ref, h_ref, xs_ref, sem):
    def body(t, c):
        for r in range(TOP_K):
            pltpu.make_async_copy(h_ref.at[pl.ds(t, 1)], xs_ref.at[pl.ds(dest_ref[0, t * TOP_K + r], 1)], sem).start()
        return c
    lax.fori_loop(0, TM, body, 0, unroll=8)
    _wait_row_copies(h_ref, xs_ref, sem, TM * TOP_K)


def _dispatch(dest3, h2, n_slots):
    rows = h2.shape[0]
    return pl.pallas_call(
        _dispatch_body,
        grid=(rows // TM,),
        in_specs=[pl.BlockSpec((None, 1, TM * TOP_K), lambda i: (i, 0, 0), memory_space=pltpu.SMEM),
                  pl.BlockSpec((TM, D), lambda i: (i, 0))],
        out_specs=pl.BlockSpec(memory_space=pl.ANY),
        out_shape=jax.ShapeDtypeStruct((n_slots, D), F32),
        scratch_shapes=[pltpu.SemaphoreType.DMA(())],
        compiler_params=_cparams(("arbitrary",), VMEM_LIMIT),
    )(dest3, h2)


def _expert_body(te_ref, na_ref, xs_ref, wg_ref, bg_ref, wu_ref, bu_ref, wd_ref, bd_ref, y_ref, wgb, wub, wdb):
    j = pl.program_id(0)

    @pl.when(j < na_ref[0])
    def _():
        @pl.when((j == 0) | (te_ref[j] != te_ref[jnp.maximum(j - 1, 0)]))
        def _():
            wgb[...] = wg_ref[...].astype(BF)
            wub[...] = wu_ref[...].astype(BF)
            wdb[...] = wd_ref[...].astype(BF)

        x = xs_ref[...].astype(BF)
        a = jnp.minimum(jnp.dot(x, wgb[...], preferred_element_type=F32) + bg_ref[...], SWIGLU_LIMIT)
        u = jnp.clip(jnp.dot(x, wub[...], preferred_element_type=F32) + bu_ref[...], -SWIGLU_LIMIT, SWIGLU_LIMIT)
        act = a * _sigmoid(SWIGLU_ALPHA * a) * (u + 1.0)
        y_ref[...] = jnp.dot(act.astype(BF), wdb[...], preferred_element_type=F32) + bd_ref[...]


def _experts(tile_expert, n_active, xs, layer, wg, bg, wu, bu, wd, bd):
    n_slots = xs.shape[0]
    f = wg.shape[3]
    slot = lambda j, te, na: (jnp.minimum(j, na[0] - 1), 0)
    wsel = lambda j, te, na: (layer, te[j], 0, 0)
    return pl.pallas_call(
        _expert_body,
        grid_spec=pltpu.PrefetchScalarGridSpec(
            num_scalar_prefetch=2,
            grid=(n_slots // TME,),
            in_specs=[pl.BlockSpec((TME, D), slot),
                      pl.BlockSpec((None, None, D, f), wsel), pl.BlockSpec((None, None, 1, f), wsel),
                      pl.BlockSpec((None, None, D, f), wsel), pl.BlockSpec((None, None, 1, f), wsel),
                      pl.BlockSpec((None, None, f, D), wsel), pl.BlockSpec((None, None, 1, D), wsel)],
            out_specs=pl.BlockSpec((TME, D), slot),
            scratch_shapes=[pltpu.VMEM((D, f), BF), pltpu.VMEM((D, f), BF), pltpu.VMEM((f, D), BF)]),
        out_shape=jax.ShapeDtypeStruct((n_slots, D), F32),
        compiler_params=_cparams(("arbitrary",), VMEM_LIMIT),
    )(tile_expert, n_active, xs, wg, bg, wu, bu, wd, bd)


def _combine_body(dest_ref, x_ref, mod_ref, route_ref, gf_ref, y_ref, o_ref, buf, sem, *, final):
    def body(t, c):
        for r in range(TOP_K):
            pltpu.make_async_copy(y_ref.at[pl.ds(dest_ref[0, t * TOP_K + r], 1)], buf.at[r, pl.ds(t, 1)], sem).start()
        return c
    lax.fori_loop(0, TM, body, 0, unroll=8)
    _wait_row_copies(y_ref, buf.at[0], sem, TM * TOP_K)
    route = route_ref[...]
    acc = jnp.zeros((TM, D), F32)
    for r in range(TOP_K):
        acc = acc + route[:, TOP_K + r:TOP_K + r + 1] * buf[r]
    xn = x_ref[...] + mod_ref[5:6, :] * acc
    if final:
        xn = xn * lax.rsqrt(jnp.mean(xn * xn, axis=-1, keepdims=True) + EPS) * gf_ref[...]
    o_ref[...] = xn


def _combine(dest3, x_mid, mods, route, g_final, y, dims, final):
    rows = x_mid.shape[0]
    tps, nb = dims["tps"], dims["nb"]
    return pl.pallas_call(
        functools.partial(_combine_body, final=final),
        grid=(rows // TM,),
        in_specs=[pl.BlockSpec((None, 1, TM * TOP_K), lambda i: (i, 0, 0), memory_space=pltpu.SMEM),
                  pl.BlockSpec((TM, D), lambda i: (i, 0)),
                  pl.BlockSpec((None, 8, D), lambda i: (jnp.minimum(i // tps, nb), 0, 0)),
                  pl.BlockSpec((TM, LANES), lambda i: (i, 0)),
                  pl.BlockSpec((1, D), lambda i: (0, 0)),
                  pl.BlockSpec(memory_space=pl.ANY)],
        out_specs=pl.BlockSpec((TM, D), lambda i: (i, 0)),
        out_shape=jax.ShapeDtypeStruct((rows, D), F32),
        scratch_shapes=[pltpu.VMEM((TOP_K, TM, D), F32), pltpu.SemaphoreType.DMA(())],
        compiler_params=_cparams(("arbitrary",), VMEM_LIMIT),
    )(dest3, x_mid, mods, route, g_final, y)


def _moe(x_mid, h2, route, counts, mods, g_final, layer, wg, bg, wu, bu, wd, bd, dims, final):
    rows = x_mid.shape[0]
    n_exp = wg.shape[1]
    n_slots = rows * TOP_K + n_exp * TME
    e = route[:, 0:TOP_K].astype(jnp.int32)
    rank = route[:, 2 * TOP_K:3 * TOP_K].astype(jnp.int32)
    cnt = counts[0, :n_exp].astype(jnp.int32)
    cnt_pad = ((cnt + TME - 1) // TME) * TME
    offs_end = jnp.cumsum(cnt_pad)
    offs = offs_end - cnt_pad
    onehot = e[:, :, None] == jnp.arange(n_exp, dtype=jnp.int32)[None, None, :]
    dest = jnp.sum(jnp.where(onehot, offs[None, None, :], 0), axis=-1) + rank
    dest3 = dest.reshape(rows // TM, 1, TM * TOP_K)
    tile_start = jnp.arange(n_slots // TME, dtype=jnp.int32) * TME
    tile_expert = jnp.minimum(jnp.sum(tile_start[:, None] >= offs_end[None, :], axis=1), n_exp - 1).astype(jnp.int32)
    n_active = (offs_end[-1:] // TME).astype(jnp.int32)
    xs = _dispatch(dest3, h2, n_slots)
    y = _experts(tile_expert, n_active, xs, layer, wg, bg, wu, bu, wd, bd)
    return _combine(dest3, x_mid, mods, route, g_final, y, dims, final)


def _rope_tables(s, ctx):
    rows = s // GRID_W
    row = np.repeat(np.arange(rows), GRID_W).astype(np.float32)
    col = np.tile(np.arange(GRID_W), rows).astype(np.float32)
    half = HEAD_DIM // 2
    inv = (np.float32(ROPE_BASE) ** (-np.arange(0, half, 2, dtype=np.float32) / half)).astype(np.float32)
    ar = row[:, None] * inv
    ac = col[:, None] * inv
    ang = np.concatenate([ar, ar, ac, ac], axis=-1)
    cos = np.concatenate([np.cos(ang), np.ones((ctx, HEAD_DIM), np.float32)], axis=0)
    sin = np.concatenate([np.sin(ang), np.zeros((ctx, HEAD_DIM), np.float32)], axis=0)
    return (jnp.asarray(np.tile(cos, (1, 2)), F32), jnp.asarray(np.tile(sin, (1, 2)), F32))


def _rot_columns(w):
    d, n = w.shape
    w4 = w.reshape(d, n // HEAD_DIM, 4, HEAD_DIM // 4)
    return jnp.stack([-w4[:, :, 1], w4[:, :, 0], -w4[:, :, 3], w4[:, :, 2]], axis=2).reshape(d, n)


def kernel(x, c, ctx, c_ctx, w_ada, b_ada, g_norm1, w_in, da_lambda, g_subln, w_conv, w_pool, pool_scale,
           w_branch, w_mgate, b_mgate, w_out, g_norm2, w_router, b_router, w_e_gate, b_e_gate, w_e_up,
           b_e_up, w_e_down, b_e_down, g_final):
    nb, s, _ = x.shape
    nctx = ctx.shape[1]
    depth = w_in.shape[0]
    n_exp = w_router.shape[2]
    t_lat, t_ctx = nb * s, nb * nctx
    dims = dict(nb=nb, s=s, ctx=nctx, t_lat=t_lat, t_all=t_lat + t_ctx, nlat=t_lat // TM, tps=s // TM, tpc=nctx // TM)
    assert s % TK == 0 and s // TK >= 2 and s % (FFT_N2 * 8) == 0 and nctx % TM == 0 and nb < 16
    assert TM == VT_CHUNK and nctx % VT_CHUNK == 0 and ATT_UNROLL % 2 == 0

    cos_t, sin_t = _rope_tables(s, nctx)
    tabs = _fft_tables(s, nctx)
    c16 = jnp.concatenate([c, c_ctx[None, :], jnp.zeros((15 - nb, D), F32)], axis=0)
    mods_all = _adaln(c16, w_ada, b_ada).reshape(depth, 16, 6, D)
    mods_all = jnp.concatenate([mods_all, jnp.zeros((depth, 16, 2, D), F32)], axis=2)

    xa, xb = x.reshape(t_lat, D), ctx.reshape(t_ctx, D)
    for l in range(depth):
        last = l == depth - 1
        lam_init = 0.8 - 0.6 * math.exp(-0.3 * l)
        mods = mods_all[l]
        w_l = w_in[l]
        w_ext = jnp.concatenate([w_l, _rot_columns(w_l[:, :BW]), _rot_columns(w_l[:, BW:2 * BW])], axis=1).astype(BF)
        qt, k, vt, u, sb, p, qq, px = _inproj(xa, xb, mods, g_norm1[l][None, :], w_ext, cos_t, sin_t, tabs["cs"], dims)

        rows = t_lat if last else t_lat + t_ctx
        a = _attention(qt, k, vt, g_subln[l], da_lambda[l], lam_init, dims, latent=True, rows=rows)
        yf = _fourier_latent(p, qq, tabs, dims, rows)
        if not last:
            a = _attention(qt, k, vt, g_subln[l], da_lambda[l], lam_init, dims, latent=False, fill=a)
            yf = _fourier_ctx(p, qq, tabs, dims, yf)

        wconv = jnp.concatenate([w_conv[l], jnp.zeros((5, BW), F32)], axis=0)
        wr = jnp.concatenate([w_router[l], jnp.zeros((D, LANES - n_exp), F32)], axis=1).astype(BF)
        br = jnp.concatenate([b_router[l], jnp.full((LANES - n_exp,), -1e30, F32)])[None, :]
        x_mid, h2, route, counts = _merge(
            xa, xb, mods, g_norm1[l][None, :], a, u, sb, yf, px, wconv, w_pool[l].astype(BF), pool_scale[l][None, :],
            w_branch[l].astype(BF), w_mgate[l].astype(BF), b_mgate[l][None, :], w_out[l].astype(BF),
            g_norm2[l][None, :], wr, br, dims, rows)
        x_new = _moe(x_mid, h2, route, counts, mods, g_final[None, :], l,
                     w_e_gate, b_e_gate[:, :, None, :], w_e_up, b_e_up[:, :, None, :],
                     w_e_down, b_e_down[:, :, None, :], dims, final=last)
        if last:
            return x_new.reshape(nb, s, D)
        xa = xb = x_new
```

```python
import functools
import math

import numpy as np
import jax
import jax.numpy as jnp
from jax import lax
from jax.experimental import pallas as pl
from jax.experimental.pallas import tpu as pltpu

F32 = jnp.float32
BF = jnp.bfloat16

D = 1024
HEADS = 4
HEAD_DIM = 64
HEAD_W = 128
BW = 512
GRID_W = 64
EPS = 1e-6
ROPE_BASE = 10000.0
POOL_WINDOWS = (2, 4, 8, 16)
TOP_K = 4
SWIGLU_LIMIT = 7.0
SWIGLU_ALPHA = 1.702
FFT_N2 = 128

TM = 256
TME = 512
TM_MOVE = 1024
TQ = 512
TK = 1024
ATT_UNROLL = 2
VT_CHUNK = 256
HALO = 16
LANES = 128
QSCALE = (HEAD_DIM ** -0.5) * math.log2(math.e)
EXP_HEADROOM = 64.0
NORM_MARGIN = 1.05
VMEM_LIMIT = 56 * 2 ** 20


def _cparams(sem, vmem=None):
    return pltpu.CompilerParams(dimension_semantics=sem, vmem_limit_bytes=vmem)


def _modnorm(x, g, shift, scale):
    y = x * lax.rsqrt(jnp.mean(x * x, axis=-1, keepdims=True) + EPS) * g
    return y * (1.0 + scale) + shift


def _sigmoid(x):
    return 1.0 / (1.0 + jnp.exp(-x))


def _adaln_body(c_ref, w_ref, b_ref, o_ref):
    c = c_ref[...]
    s = c * _sigmoid(c)
    o_ref[...] = jnp.dot(s.astype(BF), w_ref[...].astype(BF), preferred_element_type=F32) + b_ref[...]


def _adaln(c16, w_ada, b_ada):
    nl = w_ada.shape[0]
    tn = 1536
    return pl.pallas_call(
        _adaln_body,
        grid=(nl, 6 * D // tn),
        in_specs=[pl.BlockSpec((16, D), lambda l, j: (0, 0)),
                  pl.BlockSpec((None, D, tn), lambda l, j: (l, 0, j)),
                  pl.BlockSpec((None, 1, tn), lambda l, j: (l, 0, j))],
        out_specs=pl.BlockSpec((None, 16, tn), lambda l, j: (l, 0, j)),
        out_shape=jax.ShapeDtypeStruct((nl, 16, 6 * D), F32),
        compiler_params=_cparams(("arbitrary", "arbitrary"), VMEM_LIMIT),
    )(c16, w_ada, b_ada.reshape(nl, 1, 6 * D))


def _token_tile(xa_ref, xb_ref, n_a):
    return jnp.where(pl.program_id(0) < n_a, xa_ref[...], xb_ref[...])


def _token_specs(xa, xb):
    n_a, n_b = xa.shape[0] // TM, xb.shape[0] // TM
    return [pl.BlockSpec((TM, D), lambda i: (jnp.minimum(i, n_a - 1), 0)),
            pl.BlockSpec((TM, D), lambda i: (jnp.clip(i - n_a, 0, n_b - 1), 0))]


def _inproj_body(xa_ref, xb_ref, mod_ref, g_ref, w_ref, cos_ref, sin_ref, cs_ref,
                 q_ref, k_ref, v_ref, u_ref, sb_ref, p_ref, qq_ref, px_ref, *, n_a):
    h = _modnorm(_token_tile(xa_ref, xb_ref, n_a), g_ref[...], mod_ref[0:1, :], mod_ref[1:2, :]).astype(BF)

    def proj(c0):
        return jnp.dot(h, w_ref[:, c0:c0 + BW], preferred_element_type=F32)

    cos = jnp.concatenate([cos_ref[...]] * 4, axis=1)
    sin = jnp.concatenate([sin_ref[...]] * 4, axis=1)
    q_ref[...] = ((proj(0) * cos + proj(8 * BW) * sin) * QSCALE).T.astype(BF)
    k_ref[...] = (proj(BW) * cos + proj(9 * BW) * sin).astype(BF)
    v_ref[...] = proj(2 * BW).T.astype(BF)
    u_ref[...] = (proj(5 * BW) * proj(3 * BW)).astype(BF)
    sb_ref[...] = proj(4 * BW).astype(BF)
    px_ref[...] = proj(7 * BW).astype(BF)
    fx = proj(6 * BW).astype(BF)
    for g in range(4):
        pq = jnp.dot(fx[:, g * LANES:(g + 1) * LANES], cs_ref[...], preferred_element_type=F32)
        p_ref[:, g * LANES:(g + 1) * LANES] = pq[:, :LANES].astype(BF)
        qq_ref[:, g * LANES:(g + 1) * LANES] = pq[:, LANES:].astype(BF)


def _inproj(xa, xb, mods, g1, w_ext, cos_t, sin_t, cs_tab, dims):
    t_all, nlat, tps, tpc, nb = dims["t_all"], dims["nlat"], dims["tps"], dims["tpc"], dims["nb"]

    def tab_idx(i):
        return (jnp.where(i < nlat, i % tps, tps + (i - nlat) % tpc), 0)

    out = jax.ShapeDtypeStruct((t_all, BW), BF)
    row = pl.BlockSpec((TM, BW), lambda i: (i, 0))
    return pl.pallas_call(
        functools.partial(_inproj_body, n_a=xa.shape[0] // TM),
        grid=(t_all // TM,),
        in_specs=_token_specs(xa, xb) + [
                  pl.BlockSpec((None, 8, D), lambda i: (jnp.minimum(i // tps, nb), 0, 0)),
                  pl.BlockSpec((1, D), lambda i: (0, 0)),
                  pl.BlockSpec((D, 10 * BW), lambda i: (0, 0)),
                  pl.BlockSpec((TM, LANES), tab_idx),
                  pl.BlockSpec((TM, LANES), tab_idx),
                  pl.BlockSpec((LANES, 2 * LANES), lambda i: (0, 0))],
        out_specs=[pl.BlockSpec((BW, TM), lambda i: (0, i)), row,
                   pl.BlockSpec((None, BW, VT_CHUNK), lambda i: (i, 0, 0))] + [row] * 5,
        out_shape=[jax.ShapeDtypeStruct((BW, t_all), BF), out,
                   jax.ShapeDtypeStruct((t_all // VT_CHUNK, BW, VT_CHUNK), BF)] + [out] * 5,
        compiler_params=_cparams(("arbitrary",), VMEM_LIMIT),
    )(xa, xb, mods, g1, w_ext, cos_t, sin_t, cs_tab)


def _attn_body(*refs, n_ctx_chunks, n_lat_chunks, lam_init):
    if n_lat_chunks:
        qt_ref, kc_ref, vtc_ref, kl_ref, vtl_ref, g_ref, dl_ref, o_ref, acc_ref, s_buf, p_buf, l_ref, knorm_ref = refs
    else:
        qt_ref, kc_ref, vtc_ref, g_ref, dl_ref, _, o_ref, acc_ref = refs
    qt = qt_ref[...]
    tq = qt.shape[1]
    row = lax.broadcasted_iota(jnp.int32, qt.shape, 0)
    zero = jnp.zeros_like(qt)
    q2 = jnp.concatenate([jnp.where(row < HEAD_DIM, qt, zero), jnp.where(row >= HEAD_DIM, qt, zero)], axis=1)

    def scores(kc):
        s = jnp.dot(kc, q2, preferred_element_type=F32)
        return s, jnp.max(s, axis=0, keepdims=True)

    def softmax(s, smax, m, l):
        m_new = jnp.maximum(m, smax)
        p = jnp.exp2(s - m_new)
        alpha = jnp.exp2(m - m_new)
        return p.astype(BF), m_new, alpha * l + jnp.sum(p, axis=0, keepdims=True), alpha

    def accumulate(alpha, vt, p):
        acc_ref[...] = alpha * acc_ref[...] + jnp.dot(vt, p, preferred_element_type=F32)

    m = jnp.full((1, 2 * tq), -1e30, F32)
    l = jnp.zeros((1, 2 * tq), F32)
    acc_ref[...] = jnp.zeros_like(acc_ref)
    ctx_scores = [scores(kc_ref[c * VT_CHUNK:(c + 1) * VT_CHUNK, :]) for c in range(n_ctx_chunks)]
    if n_lat_chunks:
        per = TK // VT_CHUNK
        n = n_lat_chunks

        def keys(j):
            return kl_ref[pl.ds(pl.multiple_of(j * TK, TK), TK), :]

        s_buf[0] = jnp.dot(keys(0), q2, preferred_element_type=F32)
    for c in range(n_ctx_chunks):
        p, m, l, alpha = softmax(*ctx_scores[c], m, l)
        accumulate(alpha, vtc_ref[c], p)
    if n_lat_chunks:

        def values(j):
            parts = [vtl_ref[j * per + r] for r in range(per)]
            return parts[0] if per == 1 else jnp.concatenate(parts, axis=1)

        @pl.when(pl.program_id(2) == 0)
        def _():
            d_i = lax.broadcasted_iota(jnp.int32, (HEAD_W, LANES), 0)
            c_i = lax.broadcasted_iota(jnp.int32, (HEAD_W, LANES), 1)
            sel = jnp.where((d_i >= HEAD_DIM) == (c_i == 1), 1.0, 0.0) * jnp.where(c_i < 2, 1.0, 0.0)
            best = jnp.zeros((1, LANES), F32)
            for j in range(n):
                kf = keys(j).astype(F32)
                n2 = jnp.dot((kf * kf).astype(BF), sel.astype(BF), preferred_element_type=F32)
                best = jnp.maximum(best, jnp.max(n2, axis=0, keepdims=True))
            lane2 = lax.broadcasted_iota(jnp.int32, (1, 2 * tq), 1)
            knorm_ref[...] = jnp.sqrt(jnp.where(lane2 < tq, best[:, 0:1], best[:, 1:2]))

        qf = q2.astype(F32)
        bound = jnp.sqrt(jnp.sum(qf * qf, axis=0, keepdims=True)) * knorm_ref[...] * NORM_MARGIN
        gap = jnp.max(bound - m)
        l_ref[...] = l

        @pl.when(gap <= EXP_HEADROOM)
        def _():
            lsum = l
            s = s_buf[0]
            for j in range(n):
                s_next = jnp.dot(keys(j + 1), q2, preferred_element_type=F32) if j + 1 < n else None
                p = jnp.exp2(s - m)
                lsum = lsum + jnp.sum(p, axis=0, keepdims=True)
                acc_ref[...] += jnp.dot(values(j), p.astype(BF), preferred_element_type=F32)
                s = s_next
            l_ref[...] = lsum

        @pl.when(gap > EXP_HEADROOM)
        def _():
            _attn_exact_latent(m, l, n, keys, values, scores, softmax, accumulate, s_buf, p_buf, l_ref)

        l = l_ref[...]
    o = acc_ref[...] / l
    dl = dl_ref[...]
    lam = (jnp.exp(jnp.sum(dl[0:1] * dl[1:2], axis=1, keepdims=True))
           - jnp.exp(jnp.sum(dl[2:3] * dl[3:4], axis=1, keepdims=True)) + lam_init)
    d = o[:, :tq] - lam * o[:, tq:]
    y = d * lax.rsqrt(jnp.mean(d * d, axis=0, keepdims=True) + EPS) * g_ref[...] * (1.0 - lam_init)
    o_ref[...] = y.T.astype(BF)


def _attn_exact_latent(m, l, n, keys, values, scores, softmax, accumulate, s_buf, p_buf, l_ref):
    def half_step(j, prev, cur, carry):
        m, l, alpha_prev, smax = carry
        accumulate(alpha_prev, values(j - 1), p_buf[prev])
        if isinstance(j, int) and j + 1 >= n:
            smax_next = smax
        else:
            s_buf[prev], smax_next = scores(keys(jnp.minimum(j + 1, n - 1)))
        p, m, l, alpha = softmax(s_buf[cur], smax, m, l)
        p_buf[cur] = p
        return m, l, alpha, smax_next

    s0 = s_buf[0]
    p, m, l, alpha = softmax(s0, jnp.max(s0, axis=0, keepdims=True), m, l)
    p_buf[0] = p
    s_buf[1], smax = scores(keys(1))

    def body(i, carry):
        for r in range(ATT_UNROLL):
            carry = half_step(ATT_UNROLL * i + r + 1, r % 2, 1 - r % 2, carry)
        return carry

    n_body = (n - 1) // ATT_UNROLL
    carry = lax.fori_loop(0, n_body, body, (m, l, alpha, smax))
    for j in range(n_body * ATT_UNROLL + 1, n):
        carry = half_step(j, (j - 1) % 2, j % 2, carry)
    m, l, alpha, _ = carry
    accumulate(alpha, values(n - 1), p_buf[(n - 1) % 2])
    l_ref[...] = l


def _attention(qt, k, vt, g_sub, da_lambda, lam_init, dims, latent, rows=None, fill=None):
    nb, s, ctx, t_lat = dims["nb"], dims["s"], dims["ctx"], dims["t_lat"]
    cpb = ctx // VT_CHUNK
    tq = TQ if latent else min(TQ, ctx)
    g_sub_b = jnp.broadcast_to(g_sub[:, None], (HEAD_W, tq))
    common_tail = [pl.BlockSpec((HEAD_W, tq), lambda b, h, qi: (0, 0)),
                   pl.BlockSpec((4, HEAD_DIM), lambda b, h, qi: (0, 0))]
    kc_spec = pl.BlockSpec((ctx, HEAD_W), lambda b, h, qi: (t_lat // ctx + b, h))
    vtc_spec = pl.BlockSpec((cpb, HEAD_W, VT_CHUNK), lambda b, h, qi: (t_lat // ctx + b, h, 0))
    scratch = [pltpu.VMEM((HEAD_W, 2 * tq), F32)]
    if latent:
        nq = s // tq
        in_specs = [pl.BlockSpec((HEAD_W, tq), lambda b, h, qi: (h, b * nq + qi)), kc_spec, vtc_spec,
                    pl.BlockSpec((s, HEAD_W), lambda b, h, qi: (b, h)),
                    pl.BlockSpec((s // VT_CHUNK, HEAD_W, VT_CHUNK), lambda b, h, qi: (b, h, 0))] + common_tail
        args = (qt, k, vt, k, vt, g_sub_b, da_lambda)
        row0, n_lat_chunks, aliases = 0, s // TK, {}
        scratch += [pltpu.VMEM((2, TK, 2 * tq), F32), pltpu.VMEM((2, TK, 2 * tq), BF),
                    pltpu.VMEM((1, 2 * tq), F32), pltpu.VMEM((1, 2 * tq), F32)]
    else:
        nq = ctx // tq
        in_specs = [pl.BlockSpec((HEAD_W, tq), lambda b, h, qi: (h, t_lat // tq + b * nq + qi)),
                    kc_spec, vtc_spec] + common_tail
        in_specs.append(pl.BlockSpec(memory_space=pl.ANY))
        args = (qt, k, vt, g_sub_b, da_lambda, fill)
        row0, n_lat_chunks, rows, aliases = t_lat // tq, 0, fill.shape[0], {5: 0}
    return pl.pallas_call(
        functools.partial(_attn_body, n_ctx_chunks=cpb, n_lat_chunks=n_lat_chunks, lam_init=lam_init),
        grid=(nb, HEADS, nq),
        in_specs=in_specs,
        out_specs=pl.BlockSpec((tq, HEAD_W), lambda b, h, qi: (row0 + b * nq + qi, h)),
        out_shape=jax.ShapeDtypeStruct((rows, HEADS * HEAD_W), BF),
        input_output_aliases=aliases,
        scratch_shapes=scratch,
        compiler_params=_cparams(("arbitrary", "arbitrary", "arbitrary"), VMEM_LIMIT),
    )(*args)


def _fft_a_body(fa_ref, p_ref, q_ref, zr_ref, zi_ref):
    x = jnp.concatenate([p_ref[...], q_ref[...]], axis=0)
    z = jnp.dot(fa_ref[...], x, preferred_element_type=F32)
    zr_ref[...] = z[:FFT_N2].astype(BF)
    zi_ref[...] = z[FFT_N2:].astype(BF)


def _fft_b_body(fb_ref, tc_ref, ts_ref, zr_ref, zi_ref, o_ref):
    nk = zr_ref.shape[0]
    parts_r, parts_i = [], []
    for j in range(nk):
        zr = zr_ref[j].astype(F32)
        zi = zi_ref[j].astype(F32)
        tc = jnp.concatenate([tc_ref[j]] * 4, axis=1)
        ts = jnp.concatenate([ts_ref[j]] * 4, axis=1)
        parts_r.append((zr * tc - zi * ts).astype(BF))
        parts_i.append((zr * ts + zi * tc).astype(BF))
    rhs = jnp.concatenate([jnp.concatenate(parts_r, axis=1), jnp.concatenate(parts_i, axis=1)], axis=0)
    o_ref[...] = jnp.dot(fb_ref[...], rhs, preferred_element_type=F32).astype(BF)


def _fft_ctx_body(f_ref, p_ref, q_ref, _, o_ref):
    x = jnp.concatenate([p_ref[...], q_ref[...]], axis=0)
    o_ref[...] = jnp.dot(f_ref[...], x, preferred_element_type=F32).astype(BF)


def _fft_tables(s, ctx):
    n1, n2 = s // FFT_N2, FFT_N2
    norm = 1.0 / math.sqrt(s * LANES)
    a = 2 * np.pi * np.outer(np.arange(n2), np.arange(n2)) / n2
    c, sn = np.cos(a), np.sin(a)
    fa = np.block([[c, -sn], [sn, c]]) * norm
    tw = 2 * np.pi * np.outer(np.arange(n2), np.arange(n1)) / s
    tc = np.repeat(np.cos(tw)[:, :, None], LANES, axis=2)
    ts = np.repeat(np.sin(tw)[:, :, None], LANES, axis=2)
    b = 2 * np.pi * np.outer(np.arange(n1), np.arange(n1)) / n1
    fb = np.concatenate([np.cos(b), -np.sin(b)], axis=1)
    ch = 2 * np.pi * np.outer(np.arange(LANES), np.arange(LANES)) / LANES
    cs = np.concatenate([np.cos(ch), np.sin(ch)], axis=1)
    cx = 2 * np.pi * np.outer(np.arange(ctx), np.arange(ctx)) / ctx
    fctx = np.concatenate([np.cos(cx), -np.sin(cx)], axis=1) / math.sqrt(ctx * LANES)
    return dict(fa=jnp.asarray(fa, BF), tc=jnp.asarray(tc, F32), ts=jnp.asarray(ts, F32),
                fb=jnp.asarray(fb, BF), cs=jnp.asarray(cs, BF), fctx=jnp.asarray(fctx, BF))


def _fourier_latent(p, q, tabs, dims, rows):
    nb, s = dims["nb"], dims["s"]
    n1 = s // FFT_N2
    wide = n1 * BW
    wc = min(4096, wide)
    p2 = p.reshape(-1, wide)
    q2 = q.reshape(-1, wide)
    zshape = jax.ShapeDtypeStruct((nb * FFT_N2, wide), BF)
    zr, zi = pl.pallas_call(
        _fft_a_body,
        grid=(nb, wide // wc),
        in_specs=[pl.BlockSpec((2 * FFT_N2, 2 * FFT_N2), lambda b, j: (0, 0)),
                  pl.BlockSpec((FFT_N2, wc), lambda b, j: (b, j)),
                  pl.BlockSpec((FFT_N2, wc), lambda b, j: (b, j))],
        out_specs=[pl.BlockSpec((FFT_N2, wc), lambda b, j: (b, j))] * 2,
        out_shape=[zshape, zshape],
        compiler_params=_cparams(("arbitrary", "arbitrary"), VMEM_LIMIT),
    )(tabs["fa"], p2, q2)
    kb = 8
    zr4 = zr.reshape(nb, FFT_N2, n1, BW)
    zi4 = zi.reshape(nb, FFT_N2, n1, BW)
    y = pl.pallas_call(
        _fft_b_body,
        grid=(FFT_N2 // kb, nb),
        in_specs=[pl.BlockSpec((n1, 2 * n1), lambda kk, b: (0, 0)),
                  pl.BlockSpec((kb, n1, LANES), lambda kk, b: (kk, 0, 0)),
                  pl.BlockSpec((kb, n1, LANES), lambda kk, b: (kk, 0, 0)),
                  pl.BlockSpec((None, kb, n1, BW), lambda kk, b: (b, kk, 0, 0)),
                  pl.BlockSpec((None, kb, n1, BW), lambda kk, b: (b, kk, 0, 0))],
        out_specs=pl.BlockSpec((n1, kb * BW), lambda kk, b: (b, kk)),
        out_shape=jax.ShapeDtypeStruct((rows // FFT_N2, FFT_N2 * BW), BF),
        compiler_params=_cparams(("arbitrary", "arbitrary"), VMEM_LIMIT),
    )(tabs["fb"], tabs["tc"], tabs["ts"], zr4, zi4)
    return y.reshape(rows, BW)


def _fourier_ctx(p, q, tabs, dims, fill):
    nb, ctx, t_lat = dims["nb"], dims["ctx"], dims["t_lat"]
    return pl.pallas_call(
        _fft_ctx_body,
        grid=(nb,),
        in_specs=[pl.BlockSpec((ctx, 2 * ctx), lambda b: (0, 0)),
                  pl.BlockSpec((ctx, BW), lambda b: (t_lat // ctx + b, 0)),
                  pl.BlockSpec((ctx, BW), lambda b: (t_lat // ctx + b, 0)),
                  pl.BlockSpec(memory_space=pl.ANY)],
        out_specs=pl.BlockSpec((ctx, BW), lambda b: (t_lat // ctx + b, 0)),
        out_shape=jax.ShapeDtypeStruct(fill.shape, BF),
        input_output_aliases={3: 0},
        compiler_params=_cparams(("arbitrary",), VMEM_LIMIT),
    )(tabs["fctx"], p, q, fill)


def _route_topk(h2b, wr_ref, br_ref):
    logits = jnp.dot(h2b, wr_ref[...], preferred_element_type=F32) + br_ref[...]
    lane = lax.broadcasted_iota(jnp.int32, (TM, LANES), 1).astype(F32)
    cur = logits
    vals, idxs = [], []
    for _ in range(TOP_K):
        mx = jnp.max(cur, axis=1, keepdims=True)
        idx = jnp.min(jnp.where(cur == mx, lane, float(LANES)), axis=1, keepdims=True)
        vals.append(mx)
        idxs.append(idx)
        cur = jnp.where(lane == idx, -jnp.inf, cur)
    ex = [jnp.exp(v - vals[0]) for v in vals]
    den = ex[0] + ex[1] + ex[2] + ex[3]
    member = jnp.zeros((TM, LANES), F32)
    for idx in idxs:
        member = member + jnp.where(lane == idx, 1.0, 0.0)
    return idxs, [e / den for e in ex], member


def _route_ranks(idxs, weights, member, carry_ref, live):
    lane = lax.broadcasted_iota(jnp.int32, (TM, LANES), 1).astype(F32)
    tri = (lax.broadcasted_iota(jnp.int32, (TM, TM), 0) > lax.broadcasted_iota(jnp.int32, (TM, TM), 1))
    before = jnp.dot(jnp.where(tri, 1.0, 0.0).astype(BF), member.astype(BF), preferred_element_type=F32)
    rank_all = carry_ref[0:1, :] + before
    route = jnp.zeros((TM, LANES), F32)
    for r in range(TOP_K):
        rank = jnp.sum(jnp.where(lane == idxs[r], rank_all, 0.0), axis=1, keepdims=True)
        route = (route + jnp.where(lane == float(r), idxs[r], 0.0)
                 + jnp.where(lane == float(TOP_K + r), weights[r], 0.0)
                 + jnp.where(lane == float(2 * TOP_K + r), rank, 0.0))
    carry_ref[...] = carry_ref[...] + live * jnp.sum(member, axis=0, keepdims=True)
    return route


def _merge_body(xa_ref, xb_ref, mod_ref, g1_ref, a_ref, u_ref, up_ref, un_ref, sb_ref, yf_ref, px_ref, pp_ref, pn_ref,
                wconv_ref, wpool_ref, pscale_ref, wb_ref, wg_ref, bg_ref, wo_ref, g2_ref, wr_ref, br_ref,
                xo_ref, h2_ref, route_ref, cnt_ref, carry_ref, h2p_ref, *, n_tiles, nlat, tps, tpc, s, ctx, n_a):
    step = pl.program_id(0)
    i = jnp.minimum(step, n_tiles - 1)
    is_lat = i < nlat
    j = jnp.where(is_lat, i % tps, (i - nlat) % tpc)
    first = j == 0
    last = j == jnp.where(is_lat, tps, tpc) - 1
    seq_len = jnp.where(is_lat, s, ctx)

    @pl.when(step == 0)
    def _():
        carry_ref[...] = jnp.zeros_like(carry_ref)
        h2p_ref[...] = jnp.zeros_like(h2p_ref)

    topk_prev = _route_topk(h2p_ref[...], wr_ref, br_ref)

    x = jnp.where(i < n_a, xa_ref[...], xb_ref[...])
    h = _modnorm(x, g1_ref[...], mod_ref[0:1, :], mod_ref[1:2, :]).astype(BF)
    rowi = lax.broadcasted_iota(jnp.int32, (TM, 1), 0)

    u = u_ref[...].astype(F32)
    u_prev = jnp.where(first, 0.0, up_ref[...].astype(F32)[HALO - 1:HALO, :])
    u_next = jnp.where(last, 0.0, un_ref[...].astype(F32)[0:1, :])
    u_dn = jnp.where(rowi == 0, u_prev, pltpu.roll(u, 1, axis=0))
    u_up = jnp.where(rowi == TM - 1, u_next, pltpu.roll(u, TM - 1, axis=0))
    wc = wconv_ref[...]
    y_conv = sb_ref[...].astype(F32) * (wc[0:1] * u_dn + wc[1:2] * u + wc[2:3] * u_up)

    px = px_ref[...].astype(F32)
    ext = jnp.concatenate([jnp.where(first, 0.0, pp_ref[...].astype(F32)), px,
                           jnp.where(last, 0.0, pn_ref[...].astype(F32))], axis=0)
    ext_len = TM + 2 * HALO
    pos = j * TM + rowi
    pooled = []
    for g, wd in enumerate(POOL_WINDOWS):
        e = ext[:, g * LANES:(g + 1) * LANES]
        win = e + pltpu.roll(e, 1, axis=0)
        half = 1
        while 2 * half < wd:
            win = pltpu.roll(win, half, axis=0) + pltpu.roll(win, ext_len - half, axis=0)
            half *= 2
        win = win[HALO:HALO + TM]
        cnt = (jnp.minimum(pos + wd // 2, seq_len) - jnp.maximum(pos - wd // 2, 0)).astype(F32)
        pooled.append((win / cnt - px[:, g * LANES:(g + 1) * LANES]).astype(BF))

    gates = [_sigmoid(jnp.dot(h, wg_ref[:, kk * D:(kk + 1) * D], preferred_element_type=F32)
                      + bg_ref[:, kk * D:(kk + 1) * D]) for kk in range(4)]

    def branch(kk, y):
        return gates[kk] * jnp.dot(y, wb_ref[kk], preferred_element_type=F32)

    acc = branch(0, a_ref[...]) + branch(2, yf_ref[...]) + branch(1, y_conv.astype(BF))
    y_pool = jnp.concatenate([jnp.dot(pooled[g], wpool_ref[g], preferred_element_type=F32)
                              for g in range(len(POOL_WINDOWS))], axis=1) * pscale_ref[...]
    acc = acc + branch(3, y_pool.astype(BF))

    xn = x + mod_ref[2:3, :] * jnp.dot(acc.astype(BF), wo_ref[...], preferred_element_type=F32)
    xo_ref[...] = xn
    h2 = _modnorm(xn, g2_ref[...], mod_ref[3:4, :], mod_ref[4:5, :])
    h2_ref[...] = h2
    h2p_ref[...] = h2.astype(BF)
    route_ref[...] = _route_ranks(*topk_prev, carry_ref, jnp.where(step > 0, 1.0, 0.0))
    cnt_ref[...] = carry_ref[...]


def _merge(xa, xb, mods, g1, a, u, sb, yf, px, wconv, wpool, pscale, wb, wg, bg, wo, g2, wr, br, dims, rows):
    nlat, tps, tpc, nb = dims["nlat"], dims["tps"], dims["tpc"], dims["nb"]
    n_tiles = rows // TM
    n_a, n_b = xa.shape[0] // TM, xb.shape[0] // TM
    nhalo = TM // HALO
    last_halo = u.shape[0] // HALO - 1
    cur = lambda i: jnp.minimum(i, n_tiles - 1)
    tile = lambda w: pl.BlockSpec((TM, w), lambda i: (cur(i), 0))
    prev = pl.BlockSpec((HALO, BW), lambda i: (jnp.maximum(cur(i) * nhalo - 1, 0), 0))
    nxt = pl.BlockSpec((HALO, BW), lambda i: (jnp.minimum((cur(i) + 1) * nhalo, last_halo), 0))
    const = lambda shape: pl.BlockSpec(shape, lambda i: (0,) * len(shape))
    return pl.pallas_call(
        functools.partial(_merge_body, n_tiles=n_tiles, nlat=nlat, tps=tps, tpc=tpc, s=dims["s"], ctx=dims["ctx"],
                          n_a=n_a),
        grid=(n_tiles + 1,),
        in_specs=[pl.BlockSpec((TM, D), lambda i: (jnp.minimum(cur(i), n_a - 1), 0)),
                  pl.BlockSpec((TM, D), lambda i: (jnp.clip(cur(i) - n_a, 0, n_b - 1), 0)),
                  pl.BlockSpec((None, 8, D), lambda i: (jnp.minimum(cur(i) // tps, nb), 0, 0)),
                  const((1, D)), tile(BW), tile(BW), prev, nxt, tile(BW), tile(BW), tile(BW), prev, nxt,
                  const((8, BW)), const((4, LANES, LANES)), const((1, BW)), const((4, BW, D)),
                  const((D, 4 * D)), const((1, 4 * D)), const((D, D)), const((1, D)),
                  const((D, LANES)), const((1, LANES))],
        out_specs=[tile(D), tile(D), pl.BlockSpec((TM, LANES), lambda i: (jnp.maximum(i - 1, 0), 0)),
                   pl.BlockSpec((8, LANES), lambda i: (0, 0))],
        out_shape=[jax.ShapeDtypeStruct((rows, D), F32), jax.ShapeDtypeStruct((rows, D), F32),
                   jax.ShapeDtypeStruct((rows, LANES), F32), jax.ShapeDtypeStruct((8, LANES), F32)],
        scratch_shapes=[pltpu.VMEM((8, LANES), F32), pltpu.VMEM((TM, D), BF)],
        compiler_params=_cparams(("arbitrary",), VMEM_LIMIT),
    )(xa, xb, mods, g1, a, u, u, u, sb, yf, px, px, px, wconv, wpool, pscale, wb, wg, bg, wo, g2, wr, br)


def _wait_row_copies(src_ref, dst_ref, sem, n):
    for _ in range(n):
        pltpu.make_async_copy(src_ref.at[pl.ds(0, 1)], dst_ref.at[pl.ds(0, 1)], sem).wait()


def _dispatch_body(dest_ref, h_ref, xs_ref, sem):
    def body(t, c):
        for r in range(TOP_K):
            pltpu.make_async_copy(h_ref.at[pl.ds(t, 1)], xs_ref.at[pl.ds(dest_ref[0, t * TOP_K + r], 1)], sem).start()
        return c
    lax.fori_loop(0, h_ref.shape[0], body, 0, unroll=8)
    _wait_row_copies(h_ref, xs_ref, sem, h_ref.shape[0] * TOP_K)


def _dispatch(dest3, h2, n_slots):
    rows = h2.shape[0]
    tmv = dest3.shape[2] // TOP_K
    return pl.pallas_call(
        _dispatch_body,
        grid=(rows // tmv,),
        in_specs=[pl.BlockSpec((None, 1, tmv * TOP_K), lambda i: (i, 0, 0), memory_space=pltpu.SMEM),
                  pl.BlockSpec((tmv, D), lambda i: (i, 0))],
        out_specs=pl.BlockSpec(memory_space=pl.ANY),
        out_shape=jax.ShapeDtypeStruct((n_slots, D), F32),
        scratch_shapes=[pltpu.SemaphoreType.DMA(())],
        compiler_params=_cparams(("arbitrary",), VMEM_LIMIT),
    )(dest3, h2)


def _expert_body(te_ref, na_ref, xs_ref, wg_ref, bg_ref, wu_ref, bu_ref, wd_ref, bd_ref, y_ref, wgb, wub, wdb):
    j = pl.program_id(0)

    @pl.when(j < na_ref[0])
    def _():
        @pl.when((j == 0) | (te_ref[j] != te_ref[jnp.maximum(j - 1, 0)]))
        def _():
            wgb[...] = wg_ref[...].astype(BF)
            wub[...] = wu_ref[...].astype(BF)
            wdb[...] = wd_ref[...].astype(BF)

        x = xs_ref[...].astype(BF)
        a = jnp.minimum(jnp.dot(x, wgb[...], preferred_element_type=F32) + bg_ref[...], SWIGLU_LIMIT)
        u = jnp.clip(jnp.dot(x, wub[...], preferred_element_type=F32) + bu_ref[...], -SWIGLU_LIMIT, SWIGLU_LIMIT)
        act = a * _sigmoid(SWIGLU_ALPHA * a) * (u + 1.0)
        y_ref[...] = jnp.dot(act.astype(BF), wdb[...], preferred_element_type=F32) + bd_ref[...]


def _experts(tile_expert, n_active, xs, layer, wg, bg, wu, bu, wd, bd):
    n_slots = xs.shape[0]
    f = wg.shape[3]
    slot = lambda j, te, na: (jnp.minimum(j, na[0] - 1), 0)
    wsel = lambda j, te, na: (layer, te[j], 0, 0)
    return pl.pallas_call(
        _expert_body,
        grid_spec=pltpu.PrefetchScalarGridSpec(
            num_scalar_prefetch=2,
            grid=(n_slots // TME,),
            in_specs=[pl.BlockSpec((TME, D), slot),
                      pl.BlockSpec((None, None, D, f), wsel), pl.BlockSpec((None, None, 1, f), wsel),
                      pl.BlockSpec((None, None, D, f), wsel), pl.BlockSpec((None, None, 1, f), wsel),
                      pl.BlockSpec((None, None, f, D), wsel), pl.BlockSpec((None, None, 1, D), wsel)],
            out_specs=pl.BlockSpec((TME, D), slot),
            scratch_shapes=[pltpu.VMEM((D, f), BF), pltpu.VMEM((D, f), BF), pltpu.VMEM((f, D), BF)]),
        out_shape=jax.ShapeDtypeStruct((n_slots, D), F32),
        compiler_params=_cparams(("arbitrary",), VMEM_LIMIT),
    )(tile_expert, n_active, xs, wg, bg, wu, bu, wd, bd)


def _combine_body(dest_ref, x_ref, mod_ref, route_ref, gf_ref, y_ref, o_ref, buf, sem, *, final):
    def body(t, c):
        for r in range(TOP_K):
            pltpu.make_async_copy(y_ref.at[pl.ds(dest_ref[0, t * TOP_K + r], 1)], buf.at[r, pl.ds(t, 1)], sem).start()
        return c
    tmv = x_ref.shape[0]
    lax.fori_loop(0, tmv, body, 0, unroll=8)
    _wait_row_copies(y_ref, buf.at[0], sem, tmv * TOP_K)
    route = route_ref[...]
    acc = jnp.zeros((tmv, D), F32)
    for r in range(TOP_K):
        acc = acc + route[:, TOP_K + r:TOP_K + r + 1] * buf[r]
    xn = x_ref[...] + mod_ref[5:6, :] * acc
    if final:
        xn = xn * lax.rsqrt(jnp.mean(xn * xn, axis=-1, keepdims=True) + EPS) * gf_ref[...]
    o_ref[...] = xn


def _combine(dest3, x_mid, mods, route, g_final, y, dims, final):
    rows = x_mid.shape[0]
    tmv = dest3.shape[2] // TOP_K
    tps, nb = dims["s"] // tmv, dims["nb"]
    return pl.pallas_call(
        functools.partial(_combine_body, final=final),
        grid=(rows // tmv,),
        in_specs=[pl.BlockSpec((None, 1, tmv * TOP_K), lambda i: (i, 0, 0), memory_space=pltpu.SMEM),
                  pl.BlockSpec((tmv, D), lambda i: (i, 0)),
                  pl.BlockSpec((None, 8, D), lambda i: (jnp.minimum(i // tps, nb), 0, 0)),
                  pl.BlockSpec((tmv, LANES), lambda i: (i, 0)),
                  pl.BlockSpec((1, D), lambda i: (0, 0)),
                  pl.BlockSpec(memory_space=pl.ANY)],
        out_specs=pl.BlockSpec((tmv, D), lambda i: (i, 0)),
        out_shape=jax.ShapeDtypeStruct((rows, D), F32),
        scratch_shapes=[pltpu.VMEM((TOP_K, tmv, D), F32), pltpu.SemaphoreType.DMA(())],
        compiler_params=_cparams(("arbitrary",), VMEM_LIMIT),
    )(dest3, x_mid, mods, route, g_final, y)


def _moe(x_mid, h2, route, counts, mods, g_final, layer, wg, bg, wu, bu, wd, bd, dims, final):
    rows = x_mid.shape[0]
    n_exp = wg.shape[1]
    n_slots = rows * TOP_K + n_exp * TME
    e = route[:, 0:TOP_K].astype(jnp.int32)
    rank = route[:, 2 * TOP_K:3 * TOP_K].astype(jnp.int32)
    cnt = counts[0, :n_exp].astype(jnp.int32)
    cnt_pad = ((cnt + TME - 1) // TME) * TME
    offs_end = jnp.cumsum(cnt_pad)
    offs = offs_end - cnt_pad
    onehot = e[:, :, None] == jnp.arange(n_exp, dtype=jnp.int32)[None, None, :]
    dest = jnp.sum(jnp.where(onehot, offs[None, None, :], 0), axis=-1) + rank
    tmv = max(t for t in (TM_MOVE, TM) if dims["s"] % t == 0 and dims["t_lat"] % t == 0 and rows % t == 0)
    dest3 = dest.reshape(rows // tmv, 1, tmv * TOP_K)
    tile_start = jnp.arange(n_slots // TME, dtype=jnp.int32) * TME
    tile_expert = jnp.minimum(jnp.sum(tile_start[:, None] >= offs_end[None, :], axis=1), n_exp - 1).astype(jnp.int32)
    n_active = (offs_end[-1:] // TME).astype(jnp.int32)
    xs = _dispatch(dest3, h2, n_slots)
    y = _experts(tile_expert, n_active, xs, layer, wg, bg, wu, bu, wd, bd)
    return _combine(dest3, x_mid, mods, route, g_final, y, dims, final)


def _rope_tables(s, ctx):
    rows = s // GRID_W
    row = np.repeat(np.arange(rows), GRID_W).astype(np.float32)
    col = np.tile(np.arange(GRID_W), rows).astype(np.float32)
    half = HEAD_DIM // 2
    inv = (np.float32(ROPE_BASE) ** (-np.arange(0, half, 2, dtype=np.float32) / half)).astype(np.float32)
    ar = row[:, None] * inv
    ac = col[:, None] * inv
    ang = np.concatenate([ar, ar, ac, ac], axis=-1)
    cos = np.concatenate([np.cos(ang), np.ones((ctx, HEAD_DIM), np.float32)], axis=0)
    sin = np.concatenate([np.sin(ang), np.zeros((ctx, HEAD_DIM), np.float32)], axis=0)
    return (jnp.asarray(np.tile(cos, (1, 2)), F32), jnp.asarray(np.tile(sin, (1, 2)), F32))


def _rot_columns(w):
    d, n = w.shape
    w4 = w.reshape(d, n // HEAD_DIM, 4, HEAD_DIM // 4)
    return jnp.stack([-w4[:, :, 1], w4[:, :, 0], -w4[:, :, 3], w4[:, :, 2]], axis=2).reshape(d, n)


def kernel(x, c, ctx, c_ctx, w_ada, b_ada, g_norm1, w_in, da_lambda, g_subln, w_conv, w_pool, pool_scale,
           w_branch, w_mgate, b_mgate, w_out, g_norm2, w_router, b_router, w_e_gate, b_e_gate, w_e_up,
           b_e_up, w_e_down, b_e_down, g_final):
    nb, s, _ = x.shape
    nctx = ctx.shape[1]
    depth = w_in.shape[0]
    n_exp = w_router.shape[2]
    t_lat, t_ctx = nb * s, nb * nctx
    dims = dict(nb=nb, s=s, ctx=nctx, t_lat=t_lat, t_all=t_lat + t_ctx, nlat=t_lat // TM, tps=s // TM, tpc=nctx // TM)
    assert s % TK == 0 and s // TK >= 2 and s % (FFT_N2 * 8) == 0 and nctx % TM == 0 and nb < 16
    assert TM == VT_CHUNK and nctx % VT_CHUNK == 0 and ATT_UNROLL % 2 == 0

    cos_t, sin_t = _rope_tables(s, nctx)
    tabs = _fft_tables(s, nctx)
    c16 = jnp.concatenate([c, c_ctx[None, :], jnp.zeros((15 - nb, D), F32)], axis=0)
    mods_all = _adaln(c16, w_ada, b_ada).reshape(depth, 16, 6, D)
    mods_all = jnp.concatenate([mods_all, jnp.zeros((depth, 16, 2, D), F32)], axis=2)

    xa, xb = x.reshape(t_lat, D), ctx.reshape(t_ctx, D)
    for l in range(depth):
        last = l == depth - 1
        lam_init = 0.8 - 0.6 * math.exp(-0.3 * l)
        mods = mods_all[l]
        w_l = w_in[l]
        w_ext = jnp.concatenate([w_l, _rot_columns(w_l[:, :BW]), _rot_columns(w_l[:, BW:2 * BW])], axis=1).astype(BF)
        qt, k, vt, u, sb, p, qq, px = _inproj(xa, xb, mods, g_norm1[l][None, :], w_ext, cos_t, sin_t, tabs["cs"], dims)

        rows = t_lat if last else t_lat + t_ctx
        a = _attention(qt, k, vt, g_subln[l], da_lambda[l], lam_init, dims, latent=True, rows=rows)
        yf = _fourier_latent(p, qq, tabs, dims, rows)
        if not last:
            a = _attention(qt, k, vt, g_subln[l], da_lambda[l], lam_init, dims, latent=False, fill=a)
            yf = _fourier_ctx(p, qq, tabs, dims, yf)

        wconv = jnp.concatenate([w_conv[l], jnp.zeros((5, BW), F32)], axis=0)
        wr = jnp.concatenate([w_router[l], jnp.zeros((D, LANES - n_exp), F32)], axis=1).astype(BF)
        br = jnp.concatenate([b_router[l], jnp.full((LANES - n_exp,), -1e30, F32)])[None, :]
        x_mid, h2, route, counts = _merge(
            xa, xb, mods, g_norm1[l][None, :], a, u, sb, yf, px, wconv, w_pool[l].astype(BF), pool_scale[l][None, :],
            w_branch[l].astype(BF), w_mgate[l].astype(BF), b_mgate[l][None, :], w_out[l].astype(BF),
            g_norm2[l][None, :], wr, br, dims, rows)
        x_new = _moe(x_mid, h2, route, counts, mods, g_final[None, :], l,
                     w_e_gate, b_e_gate[:, :, None, :], w_e_up, b_e_up[:, :, None, :],
                     w_e_down, b_e_down[:, :, None, :], dims, final=last)
        if last:
            return x_new.reshape(nb, s, D)
        xa = xb = x_new
```

```python
import functools
import math

import numpy as np
import jax
import jax.numpy as jnp
from jax import lax
from jax.experimental import pallas as pl
from jax.experimental.pallas import tpu as pltpu

F32 = jnp.float32
BF = jnp.bfloat16

D = 1024
HEADS = 4
HEAD_DIM = 64
HEAD_W = 128
BW = 512
GRID_W = 64
EPS = 1e-6
ROPE_BASE = 10000.0
POOL_WINDOWS = (2, 4, 8, 16)
TOP_K = 4
SWIGLU_LIMIT = 7.0
SWIGLU_ALPHA = 1.702
FFT_N2 = 128

TM = 256
TME = 512
TM_MOVE = 1024
TQ = 512
TK = 512
ATT_UNROLL = 2
VT_CHUNK = 256
HALO = 16
LANES = 128
ROW_BLK = 8
QSCALE = (HEAD_DIM ** -0.5) * math.log2(math.e)
EXP_HEADROOM = 64.0
NORM_MARGIN = 1.05
VMEM_LIMIT = 56 * 2 ** 20


def _cparams(sem, vmem=None):
    return pltpu.CompilerParams(dimension_semantics=sem, vmem_limit_bytes=vmem)


def _modnorm(x, g, shift, scale):
    y = x * lax.rsqrt(jnp.mean(x * x, axis=-1, keepdims=True) + EPS) * g
    return y * (1.0 + scale) + shift


def _sigmoid(x):
    return 1.0 / (1.0 + jnp.exp(-x))


def _adaln_body(c_ref, w_ref, b_ref, o_ref):
    c = c_ref[...]
    s = c * _sigmoid(c)
    o_ref[...] = jnp.dot(s.astype(BF), w_ref[...].astype(BF), preferred_element_type=F32) + b_ref[...]


def _adaln(c16, w_ada, b_ada):
    nl = w_ada.shape[0]
    tn = 1536
    return pl.pallas_call(
        _adaln_body,
        grid=(nl, 6 * D // tn),
        in_specs=[pl.BlockSpec((16, D), lambda l, j: (0, 0)),
                  pl.BlockSpec((None, D, tn), lambda l, j: (l, 0, j)),
                  pl.BlockSpec((None, 1, tn), lambda l, j: (l, 0, j))],
        out_specs=pl.BlockSpec((None, 16, tn), lambda l, j: (l, 0, j)),
        out_shape=jax.ShapeDtypeStruct((nl, 16, 6 * D), F32),
        compiler_params=_cparams(("arbitrary", "arbitrary"), VMEM_LIMIT),
    )(c16, w_ada, b_ada.reshape(nl, 1, 6 * D))


def _token_tile(xa_ref, xb_ref, n_a):
    return jnp.where(pl.program_id(0) < n_a, xa_ref[...], xb_ref[...])


def _token_specs(xa, xb):
    n_a, n_b = xa.shape[0] // TM, xb.shape[0] // TM
    return [pl.BlockSpec((TM, D), lambda i: (jnp.minimum(i, n_a - 1), 0)),
            pl.BlockSpec((TM, D), lambda i: (jnp.clip(i - n_a, 0, n_b - 1), 0))]


def _inproj_body(xa_ref, xb_ref, mod_ref, g_ref, w_ref, cos_ref, sin_ref, cs_ref,
                 q_ref, k_ref, v_ref, u_ref, sb_ref, p_ref, qq_ref, px_ref, *, n_a):
    h = _modnorm(_token_tile(xa_ref, xb_ref, n_a), g_ref[...], mod_ref[0:1, :], mod_ref[1:2, :]).astype(BF)

    def proj(c0):
        return jnp.dot(h, w_ref[:, c0:c0 + BW], preferred_element_type=F32)

    cos = jnp.concatenate([cos_ref[...]] * 4, axis=1)
    sin = jnp.concatenate([sin_ref[...]] * 4, axis=1)
    q_ref[...] = ((proj(0) * cos + proj(8 * BW) * sin) * QSCALE).T.astype(BF)
    k_ref[...] = (proj(BW) * cos + proj(9 * BW) * sin).astype(BF)
    v_ref[...] = proj(2 * BW).T.astype(BF)
    u_ref[...] = (proj(5 * BW) * proj(3 * BW)).astype(BF)
    sb_ref[...] = proj(4 * BW).astype(BF)
    px_ref[...] = proj(7 * BW).astype(BF)
    fx = proj(6 * BW).astype(BF)
    for g in range(4):
        pq = jnp.dot(fx[:, g * LANES:(g + 1) * LANES], cs_ref[...], preferred_element_type=F32)
        p_ref[:, g * LANES:(g + 1) * LANES] = pq[:, :LANES].astype(BF)
        qq_ref[:, g * LANES:(g + 1) * LANES] = pq[:, LANES:].astype(BF)


def _inproj(xa, xb, mods, g1, w_ext, cos_t, sin_t, cs_tab, dims):
    t_all, nlat, tps, tpc, nb = dims["t_all"], dims["nlat"], dims["tps"], dims["tpc"], dims["nb"]

    def tab_idx(i):
        return (jnp.where(i < nlat, i % tps, tps + (i - nlat) % tpc), 0)

    out = jax.ShapeDtypeStruct((t_all, BW), BF)
    row = pl.BlockSpec((TM, BW), lambda i: (i, 0))
    return pl.pallas_call(
        functools.partial(_inproj_body, n_a=xa.shape[0] // TM),
        grid=(t_all // TM,),
        in_specs=_token_specs(xa, xb) + [
                  pl.BlockSpec((None, 8, D), lambda i: (jnp.minimum(i // tps, nb), 0, 0)),
                  pl.BlockSpec((1, D), lambda i: (0, 0)),
                  pl.BlockSpec((D, 10 * BW), lambda i: (0, 0)),
                  pl.BlockSpec((TM, LANES), tab_idx),
                  pl.BlockSpec((TM, LANES), tab_idx),
                  pl.BlockSpec((LANES, 2 * LANES), lambda i: (0, 0))],
        out_specs=[pl.BlockSpec((BW, TM), lambda i: (0, i)), row,
                   pl.BlockSpec((None, BW, VT_CHUNK), lambda i: (i, 0, 0))] + [row] * 5,
        out_shape=[jax.ShapeDtypeStruct((BW, t_all), BF), out,
                   jax.ShapeDtypeStruct((t_all // VT_CHUNK, BW, VT_CHUNK), BF)] + [out] * 5,
        compiler_params=_cparams(("arbitrary",), VMEM_LIMIT),
    )(xa, xb, mods, g1, w_ext, cos_t, sin_t, cs_tab)


def _attn_body(*refs, n_ctx_chunks, n_lat_chunks, lam_init):
    if n_lat_chunks:
        qt_ref, kc_ref, vtc_ref, kl_ref, vtl_ref, g_ref, dl_ref, o_ref, acc_ref, s_buf, p_buf, l_ref, knorm_ref = refs
    else:
        qt_ref, kc_ref, vtc_ref, g_ref, dl_ref, _, o_ref, acc_ref = refs
    qt = qt_ref[...]
    tq = qt.shape[1]
    row = lax.broadcasted_iota(jnp.int32, qt.shape, 0)
    zero = jnp.zeros_like(qt)
    q2 = jnp.concatenate([jnp.where(row < HEAD_DIM, qt, zero), jnp.where(row >= HEAD_DIM, qt, zero)], axis=1)

    def scores(kc):
        s = jnp.dot(kc, q2, preferred_element_type=F32)
        return s, jnp.max(s, axis=0, keepdims=True)

    def softmax(s, smax, m, l):
        m_new = jnp.maximum(m, smax)
        p = jnp.exp2(s - m_new)
        alpha = jnp.exp2(m - m_new)
        return p.astype(BF), m_new, alpha * l + jnp.sum(p, axis=0, keepdims=True), alpha

    def accumulate(alpha, vt, p):
        acc_ref[...] = alpha * acc_ref[...] + jnp.dot(vt, p, preferred_element_type=F32)

    m = jnp.full((1, 2 * tq), -1e30, F32)
    l = jnp.zeros((1, 2 * tq), F32)
    acc_ref[...] = jnp.zeros_like(acc_ref)
    ctx_scores = [scores(kc_ref[c * VT_CHUNK:(c + 1) * VT_CHUNK, :]) for c in range(n_ctx_chunks)]
    if n_lat_chunks:
        per = TK // VT_CHUNK
        n = n_lat_chunks

        def keys(j):
            return kl_ref[pl.ds(pl.multiple_of(j * TK, TK), TK), :]

        s_buf[0] = jnp.dot(keys(0), q2, preferred_element_type=F32)
    for c in range(n_ctx_chunks):
        p, m, l, alpha = softmax(*ctx_scores[c], m, l)
        accumulate(alpha, vtc_ref[c], p)
    if n_lat_chunks:

        def values(j):
            parts = [vtl_ref[j * per + r] for r in range(per)]
            return parts[0] if per == 1 else jnp.concatenate(parts, axis=1)

        @pl.when(pl.program_id(2) == 0)
        def _():
            d_i = lax.broadcasted_iota(jnp.int32, (HEAD_W, LANES), 0)
            c_i = lax.broadcasted_iota(jnp.int32, (HEAD_W, LANES), 1)
            sel = jnp.where((d_i >= HEAD_DIM) == (c_i == 1), 1.0, 0.0) * jnp.where(c_i < 2, 1.0, 0.0)
            best = jnp.zeros((1, LANES), F32)
            for j in range(n):
                kf = keys(j).astype(F32)
                n2 = jnp.dot((kf * kf).astype(BF), sel.astype(BF), preferred_element_type=F32)
                best = jnp.maximum(best, jnp.max(n2, axis=0, keepdims=True))
            lane2 = lax.broadcasted_iota(jnp.int32, (1, 2 * tq), 1)
            knorm_ref[...] = jnp.sqrt(jnp.where(lane2 < tq, best[:, 0:1], best[:, 1:2]))

        qf = q2.astype(F32)
        bound = jnp.sqrt(jnp.sum(qf * qf, axis=0, keepdims=True)) * knorm_ref[...] * NORM_MARGIN
        gap = jnp.max(bound - m)
        l_ref[...] = l

        @pl.when(gap <= EXP_HEADROOM)
        def _():
            lsum = l
            s = s_buf[0]
            for j in range(n):
                s_next = jnp.dot(keys(j + 1), q2, preferred_element_type=F32) if j + 1 < n else None
                p = jnp.exp2(s - m)
                lsum = lsum + jnp.sum(p, axis=0, keepdims=True)
                acc_ref[...] += jnp.dot(values(j), p.astype(BF), preferred_element_type=F32)
                s = s_next
            l_ref[...] = lsum

        @pl.when(gap > EXP_HEADROOM)
        def _():
            _attn_exact_latent(m, l, n, keys, values, scores, softmax, accumulate, s_buf, p_buf, l_ref)

        l = l_ref[...]
    o = acc_ref[...] / l
    dl = dl_ref[...]
    lam = (jnp.exp(jnp.sum(dl[0:1] * dl[1:2], axis=1, keepdims=True))
           - jnp.exp(jnp.sum(dl[2:3] * dl[3:4], axis=1, keepdims=True)) + lam_init)
    d = o[:, :tq] - lam * o[:, tq:]
    y = d * lax.rsqrt(jnp.mean(d * d, axis=0, keepdims=True) + EPS) * g_ref[...] * (1.0 - lam_init)
    o_ref[...] = y.T.astype(BF)


def _attn_exact_latent(m, l, n, keys, values, scores, softmax, accumulate, s_buf, p_buf, l_ref):
    def half_step(j, prev, cur, carry):
        m, l, alpha_prev, smax = carry
        accumulate(alpha_prev, values(j - 1), p_buf[prev])
        if isinstance(j, int) and j + 1 >= n:
            smax_next = smax
        else:
            s_buf[prev], smax_next = scores(keys(jnp.minimum(j + 1, n - 1)))
        p, m, l, alpha = softmax(s_buf[cur], smax, m, l)
        p_buf[cur] = p
        return m, l, alpha, smax_next

    s0 = s_buf[0]
    p, m, l, alpha = softmax(s0, jnp.max(s0, axis=0, keepdims=True), m, l)
    p_buf[0] = p
    s_buf[1], smax = scores(keys(1))

    def body(i, carry):
        for r in range(ATT_UNROLL):
            carry = half_step(ATT_UNROLL * i + r + 1, r % 2, 1 - r % 2, carry)
        return carry

    n_body = (n - 1) // ATT_UNROLL
    carry = lax.fori_loop(0, n_body, body, (m, l, alpha, smax))
    for j in range(n_body * ATT_UNROLL + 1, n):
        carry = half_step(j, (j - 1) % 2, j % 2, carry)
    m, l, alpha, _ = carry
    accumulate(alpha, values(n - 1), p_buf[(n - 1) % 2])
    l_ref[...] = l


def _attention(qt, k, vt, g_sub, da_lambda, lam_init, dims, latent, rows=None, fill=None):
    nb, s, ctx, t_lat = dims["nb"], dims["s"], dims["ctx"], dims["t_lat"]
    cpb = ctx // VT_CHUNK
    tq = TQ if latent else min(TQ, ctx)
    g_sub_b = jnp.broadcast_to(g_sub[:, None], (HEAD_W, tq))
    common_tail = [pl.BlockSpec((HEAD_W, tq), lambda b, h, qi: (0, 0)),
                   pl.BlockSpec((4, HEAD_DIM), lambda b, h, qi: (0, 0))]
    kc_spec = pl.BlockSpec((ctx, HEAD_W), lambda b, h, qi: (t_lat // ctx + b, h))
    vtc_spec = pl.BlockSpec((cpb, HEAD_W, VT_CHUNK), lambda b, h, qi: (t_lat // ctx + b, h, 0))
    scratch = [pltpu.VMEM((HEAD_W, 2 * tq), F32)]
    if latent:
        nq = s // tq
        in_specs = [pl.BlockSpec((HEAD_W, tq), lambda b, h, qi: (h, b * nq + qi)), kc_spec, vtc_spec,
                    pl.BlockSpec((s, HEAD_W), lambda b, h, qi: (b, h)),
                    pl.BlockSpec((s // VT_CHUNK, HEAD_W, VT_CHUNK), lambda b, h, qi: (b, h, 0))] + common_tail
        args = (qt, k, vt, k, vt, g_sub_b, da_lambda)
        row0, n_lat_chunks, aliases = 0, s // TK, {}
        scratch += [pltpu.VMEM((2, TK, 2 * tq), F32), pltpu.VMEM((2, TK, 2 * tq), BF),
                    pltpu.VMEM((1, 2 * tq), F32), pltpu.VMEM((1, 2 * tq), F32)]
    else:
        nq = ctx // tq
        in_specs = [pl.BlockSpec((HEAD_W, tq), lambda b, h, qi: (h, t_lat // tq + b * nq + qi)),
                    kc_spec, vtc_spec] + common_tail
        in_specs.append(pl.BlockSpec(memory_space=pl.ANY))
        args = (qt, k, vt, g_sub_b, da_lambda, fill)
        row0, n_lat_chunks, rows, aliases = t_lat // tq, 0, fill.shape[0], {5: 0}
    return pl.pallas_call(
        functools.partial(_attn_body, n_ctx_chunks=cpb, n_lat_chunks=n_lat_chunks, lam_init=lam_init),
        grid=(nb, HEADS, nq),
        in_specs=in_specs,
        out_specs=pl.BlockSpec((tq, HEAD_W), lambda b, h, qi: (row0 + b * nq + qi, h)),
        out_shape=jax.ShapeDtypeStruct((rows, HEADS * HEAD_W), BF),
        input_output_aliases=aliases,
        scratch_shapes=scratch,
        compiler_params=_cparams(("arbitrary", "arbitrary", "arbitrary"), VMEM_LIMIT),
    )(*args)


def _fft_a_body(fa_ref, p_ref, q_ref, zr_ref, zi_ref):
    x = jnp.concatenate([p_ref[...], q_ref[...]], axis=0)
    z = jnp.dot(fa_ref[...], x, preferred_element_type=F32)
    zr_ref[...] = z[:FFT_N2].astype(BF)
    zi_ref[...] = z[FFT_N2:].astype(BF)


def _fft_b_body(fb_ref, tc_ref, ts_ref, zr_ref, zi_ref, o_ref):
    nk = zr_ref.shape[0]
    parts_r, parts_i = [], []
    for j in range(nk):
        zr = zr_ref[j].astype(F32)
        zi = zi_ref[j].astype(F32)
        tc = jnp.concatenate([tc_ref[j]] * 4, axis=1)
        ts = jnp.concatenate([ts_ref[j]] * 4, axis=1)
        parts_r.append((zr * tc - zi * ts).astype(BF))
        parts_i.append((zr * ts + zi * tc).astype(BF))
    rhs = jnp.concatenate([jnp.concatenate(parts_r, axis=1), jnp.concatenate(parts_i, axis=1)], axis=0)
    o_ref[...] = jnp.dot(fb_ref[...], rhs, preferred_element_type=F32).astype(BF)


def _fft_ctx_body(f_ref, p_ref, q_ref, _, o_ref):
    x = jnp.concatenate([p_ref[...], q_ref[...]], axis=0)
    o_ref[...] = jnp.dot(f_ref[...], x, preferred_element_type=F32).astype(BF)


def _fft_tables(s, ctx):
    n1, n2 = s // FFT_N2, FFT_N2
    norm = 1.0 / math.sqrt(s * LANES)
    a = 2 * np.pi * np.outer(np.arange(n2), np.arange(n2)) / n2
    c, sn = np.cos(a), np.sin(a)
    fa = np.block([[c, -sn], [sn, c]]) * norm
    tw = 2 * np.pi * np.outer(np.arange(n2), np.arange(n1)) / s
    tc = np.repeat(np.cos(tw)[:, :, None], LANES, axis=2)
    ts = np.repeat(np.sin(tw)[:, :, None], LANES, axis=2)
    b = 2 * np.pi * np.outer(np.arange(n1), np.arange(n1)) / n1
    fb = np.concatenate([np.cos(b), -np.sin(b)], axis=1)
    ch = 2 * np.pi * np.outer(np.arange(LANES), np.arange(LANES)) / LANES
    cs = np.concatenate([np.cos(ch), np.sin(ch)], axis=1)
    cx = 2 * np.pi * np.outer(np.arange(ctx), np.arange(ctx)) / ctx
    fctx = np.concatenate([np.cos(cx), -np.sin(cx)], axis=1) / math.sqrt(ctx * LANES)
    return dict(fa=jnp.asarray(fa, BF), tc=jnp.asarray(tc, F32), ts=jnp.asarray(ts, F32),
                fb=jnp.asarray(fb, BF), cs=jnp.asarray(cs, BF), fctx=jnp.asarray(fctx, BF))


def _fourier_latent(p, q, tabs, dims, rows):
    nb, s = dims["nb"], dims["s"]
    n1 = s // FFT_N2
    wide = n1 * BW
    wc = min(4096, wide)
    p2 = p.reshape(-1, wide)
    q2 = q.reshape(-1, wide)
    zshape = jax.ShapeDtypeStruct((nb * FFT_N2, wide), BF)
    zr, zi = pl.pallas_call(
        _fft_a_body,
        grid=(nb, wide // wc),
        in_specs=[pl.BlockSpec((2 * FFT_N2, 2 * FFT_N2), lambda b, j: (0, 0)),
                  pl.BlockSpec((FFT_N2, wc), lambda b, j: (b, j)),
                  pl.BlockSpec((FFT_N2, wc), lambda b, j: (b, j))],
        out_specs=[pl.BlockSpec((FFT_N2, wc), lambda b, j: (b, j))] * 2,
        out_shape=[zshape, zshape],
        compiler_params=_cparams(("arbitrary", "arbitrary"), VMEM_LIMIT),
    )(tabs["fa"], p2, q2)
    kb = 8
    zr4 = zr.reshape(nb, FFT_N2, n1, BW)
    zi4 = zi.reshape(nb, FFT_N2, n1, BW)
    y = pl.pallas_call(
        _fft_b_body,
        grid=(FFT_N2 // kb, nb),
        in_specs=[pl.BlockSpec((n1, 2 * n1), lambda kk, b: (0, 0)),
                  pl.BlockSpec((kb, n1, LANES), lambda kk, b: (kk, 0, 0)),
                  pl.BlockSpec((kb, n1, LANES), lambda kk, b: (kk, 0, 0)),
                  pl.BlockSpec((None, kb, n1, BW), lambda kk, b: (b, kk, 0, 0)),
                  pl.BlockSpec((None, kb, n1, BW), lambda kk, b: (b, kk, 0, 0))],
        out_specs=pl.BlockSpec((n1, kb * BW), lambda kk, b: (b, kk)),
        out_shape=jax.ShapeDtypeStruct((rows // FFT_N2, FFT_N2 * BW), BF),
        compiler_params=_cparams(("arbitrary", "arbitrary"), VMEM_LIMIT),
    )(tabs["fb"], tabs["tc"], tabs["ts"], zr4, zi4)
    return y.reshape(rows, BW)


def _fourier_ctx(p, q, tabs, dims, fill):
    nb, ctx, t_lat = dims["nb"], dims["ctx"], dims["t_lat"]
    return pl.pallas_call(
        _fft_ctx_body,
        grid=(nb,),
        in_specs=[pl.BlockSpec((ctx, 2 * ctx), lambda b: (0, 0)),
                  pl.BlockSpec((ctx, BW), lambda b: (t_lat // ctx + b, 0)),
                  pl.BlockSpec((ctx, BW), lambda b: (t_lat // ctx + b, 0)),
                  pl.BlockSpec(memory_space=pl.ANY)],
        out_specs=pl.BlockSpec((ctx, BW), lambda b: (t_lat // ctx + b, 0)),
        out_shape=jax.ShapeDtypeStruct(fill.shape, BF),
        input_output_aliases={3: 0},
        compiler_params=_cparams(("arbitrary",), VMEM_LIMIT),
    )(tabs["fctx"], p, q, fill)


def _route_topk(h2b, wr_ref, br_ref):
    logits = jnp.dot(h2b, wr_ref[...], preferred_element_type=F32) + br_ref[...]
    lane = lax.broadcasted_iota(jnp.int32, (TM, LANES), 1).astype(F32)
    cur = logits
    vals, idxs = [], []
    for _ in range(TOP_K):
        mx = jnp.max(cur, axis=1, keepdims=True)
        idx = jnp.min(jnp.where(cur == mx, lane, float(LANES)), axis=1, keepdims=True)
        vals.append(mx)
        idxs.append(idx)
        cur = jnp.where(lane == idx, -jnp.inf, cur)
    ex = [jnp.exp(v - vals[0]) for v in vals]
    den = ex[0] + ex[1] + ex[2] + ex[3]
    member = jnp.zeros((TM, LANES), F32)
    for idx in idxs:
        member = member + jnp.where(lane == idx, 1.0, 0.0)
    return idxs, [e / den for e in ex], member


def _route_ranks(idxs, weights, member, carry_ref, live):
    lane = lax.broadcasted_iota(jnp.int32, (TM, LANES), 1).astype(F32)
    tri = (lax.broadcasted_iota(jnp.int32, (TM, TM), 0) > lax.broadcasted_iota(jnp.int32, (TM, TM), 1))
    before = jnp.dot(jnp.where(tri, 1.0, 0.0).astype(BF), member.astype(BF), preferred_element_type=F32)
    rank_all = carry_ref[0:1, :] + before
    route = jnp.zeros((TM, LANES), F32)
    for r in range(TOP_K):
        rank = jnp.sum(jnp.where(lane == idxs[r], rank_all, 0.0), axis=1, keepdims=True)
        route = (route + jnp.where(lane == float(r), idxs[r], 0.0)
                 + jnp.where(lane == float(TOP_K + r), weights[r], 0.0)
                 + jnp.where(lane == float(2 * TOP_K + r), rank, 0.0))
    carry_ref[...] = carry_ref[...] + live * jnp.sum(member, axis=0, keepdims=True)
    return route


def _merge_body(xa_ref, xb_ref, mod_ref, g1_ref, a_ref, u_ref, up_ref, un_ref, sb_ref, yf_ref, px_ref, pp_ref, pn_ref,
                wconv_ref, wpool_ref, pscale_ref, wb_ref, wg_ref, bg_ref, wo_ref, g2_ref, wr_ref, br_ref,
                xo_ref, h2_ref, route_ref, cnt_ref, carry_ref, h2p_ref, *, n_tiles, nlat, tps, tpc, s, ctx, n_a):
    step = pl.program_id(0)
    i = jnp.minimum(step, n_tiles - 1)
    is_lat = i < nlat
    j = jnp.where(is_lat, i % tps, (i - nlat) % tpc)
    first = j == 0
    last = j == jnp.where(is_lat, tps, tpc) - 1
    seq_len = jnp.where(is_lat, s, ctx)

    @pl.when(step == 0)
    def _():
        carry_ref[...] = jnp.zeros_like(carry_ref)
        h2p_ref[...] = jnp.zeros_like(h2p_ref)

    topk_prev = _route_topk(h2p_ref[...], wr_ref, br_ref)

    x = jnp.where(i < n_a, xa_ref[...], xb_ref[...])
    h = _modnorm(x, g1_ref[...], mod_ref[0:1, :], mod_ref[1:2, :]).astype(BF)
    rowi = lax.broadcasted_iota(jnp.int32, (TM, 1), 0)

    u = u_ref[...].astype(F32)
    u_prev = jnp.where(first, 0.0, up_ref[...].astype(F32)[HALO - 1:HALO, :])
    u_next = jnp.where(last, 0.0, un_ref[...].astype(F32)[0:1, :])
    u_dn = jnp.where(rowi == 0, u_prev, pltpu.roll(u, 1, axis=0))
    u_up = jnp.where(rowi == TM - 1, u_next, pltpu.roll(u, TM - 1, axis=0))
    wc = wconv_ref[...]
    y_conv = sb_ref[...].astype(F32) * (wc[0:1] * u_dn + wc[1:2] * u + wc[2:3] * u_up)

    px = px_ref[...].astype(F32)
    ext = jnp.concatenate([jnp.where(first, 0.0, pp_ref[...].astype(F32)), px,
                           jnp.where(last, 0.0, pn_ref[...].astype(F32))], axis=0)
    ext_len = TM + 2 * HALO
    pos = j * TM + rowi
    pooled = []
    for g, wd in enumerate(POOL_WINDOWS):
        e = ext[:, g * LANES:(g + 1) * LANES]
        win = e + pltpu.roll(e, 1, axis=0)
        half = 1
        while 2 * half < wd:
            win = pltpu.roll(win, half, axis=0) + pltpu.roll(win, ext_len - half, axis=0)
            half *= 2
        win = win[HALO:HALO + TM]
        cnt = (jnp.minimum(pos + wd // 2, seq_len) - jnp.maximum(pos - wd // 2, 0)).astype(F32)
        pooled.append((win / cnt - px[:, g * LANES:(g + 1) * LANES]).astype(BF))

    gates = [_sigmoid(jnp.dot(h, wg_ref[:, kk * D:(kk + 1) * D], preferred_element_type=F32)
                      + bg_ref[:, kk * D:(kk + 1) * D]) for kk in range(4)]

    def branch(kk, y):
        return gates[kk] * jnp.dot(y, wb_ref[kk], preferred_element_type=F32)

    acc = branch(0, a_ref[...]) + branch(2, yf_ref[...]) + branch(1, y_conv.astype(BF))
    y_pool = jnp.concatenate([jnp.dot(pooled[g], wpool_ref[g], preferred_element_type=F32)
                              for g in range(len(POOL_WINDOWS))], axis=1) * pscale_ref[...]
    acc = acc + branch(3, y_pool.astype(BF))

    xn = x + mod_ref[2:3, :] * jnp.dot(acc.astype(BF), wo_ref[...], preferred_element_type=F32)
    xo_ref[...] = xn
    h2 = _modnorm(xn, g2_ref[...], mod_ref[3:4, :], mod_ref[4:5, :])
    h2_ref[...] = h2
    h2p_ref[...] = h2.astype(BF)
    route_ref[...] = _route_ranks(*topk_prev, carry_ref, jnp.where(step > 0, 1.0, 0.0))
    cnt_ref[...] = carry_ref[...]


def _merge(xa, xb, mods, g1, a, u, sb, yf, px, wconv, wpool, pscale, wb, wg, bg, wo, g2, wr, br, dims, rows):
    nlat, tps, tpc, nb = dims["nlat"], dims["tps"], dims["tpc"], dims["nb"]
    n_tiles = rows // TM
    n_a, n_b = xa.shape[0] // TM, xb.shape[0] // TM
    nhalo = TM // HALO
    last_halo = u.shape[0] // HALO - 1
    cur = lambda i: jnp.minimum(i, n_tiles - 1)
    tile = lambda w: pl.BlockSpec((TM, w), lambda i: (cur(i), 0))
    prev = pl.BlockSpec((HALO, BW), lambda i: (jnp.maximum(cur(i) * nhalo - 1, 0), 0))
    nxt = pl.BlockSpec((HALO, BW), lambda i: (jnp.minimum((cur(i) + 1) * nhalo, last_halo), 0))
    const = lambda shape: pl.BlockSpec(shape, lambda i: (0,) * len(shape))
    return pl.pallas_call(
        functools.partial(_merge_body, n_tiles=n_tiles, nlat=nlat, tps=tps, tpc=tpc, s=dims["s"], ctx=dims["ctx"],
                          n_a=n_a),
        grid=(n_tiles + 1,),
        in_specs=[pl.BlockSpec((TM, D), lambda i: (jnp.minimum(cur(i), n_a - 1), 0)),
                  pl.BlockSpec((TM, D), lambda i: (jnp.clip(cur(i) - n_a, 0, n_b - 1), 0)),
                  pl.BlockSpec((None, 8, D), lambda i: (jnp.minimum(cur(i) // tps, nb), 0, 0)),
                  const((1, D)), tile(BW), tile(BW), prev, nxt, tile(BW), tile(BW), tile(BW), prev, nxt,
                  const((8, BW)), const((4, LANES, LANES)), const((1, BW)), const((4, BW, D)),
                  const((D, 4 * D)), const((1, 4 * D)), const((D, D)), const((1, D)),
                  const((D, LANES)), const((1, LANES))],
        out_specs=[tile(D), tile(D), pl.BlockSpec((TM, LANES), lambda i: (jnp.maximum(i - 1, 0), 0)),
                   pl.BlockSpec((8, LANES), lambda i: (0, 0))],
        out_shape=[jax.ShapeDtypeStruct((rows, D), F32), jax.ShapeDtypeStruct((rows, D), F32),
                   jax.ShapeDtypeStruct((rows, LANES), F32), jax.ShapeDtypeStruct((8, LANES), F32)],
        scratch_shapes=[pltpu.VMEM((8, LANES), F32), pltpu.VMEM((TM, D), BF)],
        compiler_params=_cparams(("arbitrary",), VMEM_LIMIT),
    )(xa, xb, mods, g1, a, u, u, u, sb, yf, px, px, px, wconv, wpool, pscale, wb, wg, bg, wo, g2, wr, br)


def _wait_row_copies(src_ref, dst_ref, sem, n):
    for _ in range(n):
        pltpu.make_async_copy(src_ref.at[pl.ds(0, 1)], dst_ref.at[pl.ds(0, 1)], sem).wait()


def _dispatch_body(dest_ref, h_ref, xs_ref, sem):
    def body(t, c):
        for r in range(TOP_K):
            pltpu.make_async_copy(h_ref.at[pl.ds(t, 1)], xs_ref.at[pl.ds(dest_ref[0, t * TOP_K + r], 1)], sem).start()
        return c
    lax.fori_loop(0, h_ref.shape[0], body, 0, unroll=8)
    _wait_row_copies(h_ref, xs_ref, sem, h_ref.shape[0] * TOP_K)


def _dispatch(dest3, h2, n_slots):
    rows = h2.shape[0]
    tmv = dest3.shape[2] // TOP_K
    return pl.pallas_call(
        _dispatch_body,
        grid=(rows // tmv,),
        in_specs=[pl.BlockSpec((None, 1, tmv * TOP_K), lambda i: (i, 0, 0), memory_space=pltpu.SMEM),
                  pl.BlockSpec((tmv, D), lambda i: (i, 0))],
        out_specs=pl.BlockSpec(memory_space=pl.ANY),
        out_shape=jax.ShapeDtypeStruct((n_slots, D), F32),
        scratch_shapes=[pltpu.SemaphoreType.DMA(())],
        compiler_params=_cparams(("arbitrary",), VMEM_LIMIT),
    )(dest3, h2)


def _expert_body(te_ref, na_ref, nv_ref, xs_ref, wg_ref, bg_ref, wu_ref, bu_ref, wd_ref, bd_ref, y_ref, wgb, wub, wdb):
    j = pl.program_id(0)

    @pl.when(j < na_ref[0])
    def _():
        @pl.when((j == 0) | (te_ref[j] != te_ref[jnp.maximum(j - 1, 0)]))
        def _():
            wgb[...] = wg_ref[...].astype(BF)
            wub[...] = wu_ref[...].astype(BF)
            wdb[...] = wd_ref[...].astype(BF)

        x = xs_ref[...].astype(BF)
        a = jnp.minimum(jnp.dot(x, wgb[...], preferred_element_type=F32) + bg_ref[...], SWIGLU_LIMIT)
        u = jnp.clip(jnp.dot(x, wub[...], preferred_element_type=F32) + bu_ref[...], -SWIGLU_LIMIT, SWIGLU_LIMIT)
        act = a * _sigmoid(SWIGLU_ALPHA * a) * (u + 1.0)
        y = jnp.dot(act.astype(BF), wdb[...], preferred_element_type=F32) + bd_ref[...]
        rowi = lax.broadcasted_iota(jnp.int32, (TME, 1), 0)
        y_ref[...] = jnp.where(rowi < nv_ref[j], y, 0.0)


def _experts(tile_expert, n_active, n_valid, xs, layer, wg, bg, wu, bu, wd, bd):
    n_slots = xs.shape[0]
    f = wg.shape[3]
    slot = lambda j, te, na, nv: (jnp.minimum(j, na[0] - 1), 0)
    wsel = lambda j, te, na, nv: (layer, te[j], 0, 0)
    return pl.pallas_call(
        _expert_body,
        grid_spec=pltpu.PrefetchScalarGridSpec(
            num_scalar_prefetch=3,
            grid=(n_slots // TME,),
            in_specs=[pl.BlockSpec((TME, D), slot),
                      pl.BlockSpec((None, None, D, f), wsel), pl.BlockSpec((None, None, 1, f), wsel),
                      pl.BlockSpec((None, None, D, f), wsel), pl.BlockSpec((None, None, 1, f), wsel),
                      pl.BlockSpec((None, None, f, D), wsel), pl.BlockSpec((None, None, 1, D), wsel)],
            out_specs=pl.BlockSpec((TME, D), slot),
            scratch_shapes=[pltpu.VMEM((D, f), BF), pltpu.VMEM((D, f), BF), pltpu.VMEM((f, D), BF)]),
        out_shape=jax.ShapeDtypeStruct((n_slots, D), F32),
        compiler_params=_cparams(("arbitrary",), VMEM_LIMIT),
    )(tile_expert, n_active, n_valid, xs, wg, bg, wu, bu, wd, bd)


def _combine_body(dest_ref, x_ref, mod_ref, route_ref, gf_ref, y_ref, o_ref, buf, sem, *, final):
    def body(t, c):
        for r in range(TOP_K):
            pltpu.make_async_copy(y_ref.at[pl.ds(dest_ref[0, t * TOP_K + r], 1)], buf.at[r, pl.ds(t, 1)], sem).start()
        return c
    tmv = x_ref.shape[0]
    lax.fori_loop(0, tmv, body, 0, unroll=8)
    _wait_row_copies(y_ref, buf.at[0], sem, tmv * TOP_K)
    route = route_ref[...]
    acc = jnp.zeros((tmv, D), F32)
    for r in range(TOP_K):
        acc = acc + route[:, TOP_K + r:TOP_K + r + 1] * buf[r]
    xn = x_ref[...] + mod_ref[5:6, :] * acc
    if final:
        xn = xn * lax.rsqrt(jnp.mean(xn * xn, axis=-1, keepdims=True) + EPS) * gf_ref[...]
    o_ref[...] = xn


def _combine(dest3, x_mid, mods, route, g_final, y, dims, final):
    rows = x_mid.shape[0]
    tmv = dest3.shape[2] // TOP_K
    tps, nb = dims["s"] // tmv, dims["nb"]
    return pl.pallas_call(
        functools.partial(_combine_body, final=final),
        grid=(rows // tmv,),
        in_specs=[pl.BlockSpec((None, 1, tmv * TOP_K), lambda i: (i, 0, 0), memory_space=pltpu.SMEM),
                  pl.BlockSpec((tmv, D), lambda i: (i, 0)),
                  pl.BlockSpec((None, 8, D), lambda i: (jnp.minimum(i // tps, nb), 0, 0)),
                  pl.BlockSpec((tmv, LANES), lambda i: (i, 0)),
                  pl.BlockSpec((1, D), lambda i: (0, 0)),
                  pl.BlockSpec(memory_space=pl.ANY)],
        out_specs=pl.BlockSpec((tmv, D), lambda i: (i, 0)),
        out_shape=jax.ShapeDtypeStruct((rows, D), F32),
        scratch_shapes=[pltpu.VMEM((TOP_K, tmv, D), F32), pltpu.SemaphoreType.DMA(())],
        compiler_params=_cparams(("arbitrary",), VMEM_LIMIT),
    )(dest3, x_mid, mods, route, g_final, y)


def _combine_blocks_body(tab_ref, x_ref, mod_ref, route_ref, gpos_ref, gf_ref, y_ref, o_ref, g_ref, sem, *, final, n_exp):
    @pl.when(pl.program_id(0) == 0)
    def _():
        g_ref[...] = jnp.zeros_like(g_ref)

    for e in range(n_exp):
        first_blk, n_blk, g_blk = tab_ref[0, e], tab_ref[0, n_exp + e], tab_ref[0, 2 * n_exp + e]

        def fetch(k, c):
            pltpu.make_async_copy(y_ref.at[pl.ds(first_blk + k, 1)], g_ref.at[pl.ds(g_blk + k, 1)], sem).start()
            return c
        lax.fori_loop(0, n_blk, fetch, 0)

    def drain(k, c):
        pltpu.make_async_copy(y_ref.at[pl.ds(0, 1)], g_ref.at[pl.ds(0, 1)], sem).wait()
        return c
    lax.fori_loop(0, tab_ref[0, 3 * n_exp], drain, 0)

    cap = g_ref.shape[0] * g_ref.shape[1]
    route, gpos = route_ref[...], gpos_ref[...]
    col = lax.broadcasted_iota(jnp.int32, (TM, cap), 1)
    wm = jnp.zeros((TM, cap), F32)
    for r in range(TOP_K):
        wm = wm + jnp.where(col == gpos[:, r:r + 1], route[:, TOP_K + r:TOP_K + r + 1], 0.0)
    acc = jnp.dot(wm.astype(BF), g_ref[...].reshape(cap, D).astype(BF), preferred_element_type=F32)
    xn = x_ref[...] + mod_ref[5:6, :] * acc
    if final:
        xn = xn * lax.rsqrt(jnp.mean(xn * xn, axis=-1, keepdims=True) + EPS) * gf_ref[...]
    o_ref[...] = xn


def _combine_blocks(tab, gpos, x_mid, mods, route, g_final, y, dims, final, n_exp):
    rows = x_mid.shape[0]
    tps, nb = dims["tps"], dims["nb"]
    cap_blocks = (TM * TOP_K + n_exp * 2 * ROW_BLK) // ROW_BLK
    return pl.pallas_call(
        functools.partial(_combine_blocks_body, final=final, n_exp=n_exp),
        grid=(rows // TM,),
        in_specs=[pl.BlockSpec((None, 1, LANES), lambda i: (i, 0, 0), memory_space=pltpu.SMEM),
                  pl.BlockSpec((TM, D), lambda i: (i, 0)),
                  pl.BlockSpec((None, 8, D), lambda i: (jnp.minimum(i // tps, nb), 0, 0)),
                  pl.BlockSpec((TM, LANES), lambda i: (i, 0)),
                  pl.BlockSpec((TM, TOP_K), lambda i: (i, 0)),
                  pl.BlockSpec((1, D), lambda i: (0, 0)),
                  pl.BlockSpec(memory_space=pl.ANY)],
        out_specs=pl.BlockSpec((TM, D), lambda i: (i, 0)),
        out_shape=jax.ShapeDtypeStruct((rows, D), F32),
        scratch_shapes=[pltpu.VMEM((cap_blocks, ROW_BLK, D), F32), pltpu.SemaphoreType.DMA(())],
        compiler_params=_cparams(("arbitrary",), VMEM_LIMIT),
    )(tab, x_mid, mods, route, gpos, g_final, y.reshape(-1, ROW_BLK, D))


def _moe(x_mid, h2, route, counts, mods, g_final, layer, wg, bg, wu, bu, wd, bd, dims, final):
    rows = x_mid.shape[0]
    n_exp = wg.shape[1]
    n_slots = rows * TOP_K + n_exp * TME
    e = route[:, 0:TOP_K].astype(jnp.int32)
    rank = route[:, 2 * TOP_K:3 * TOP_K].astype(jnp.int32)
    cnt = counts[0, :n_exp].astype(jnp.int32)
    cnt_pad = ((cnt + TME - 1) // TME) * TME
    offs_end = jnp.cumsum(cnt_pad)
    offs = offs_end - cnt_pad
    onehot = e[:, :, None] == jnp.arange(n_exp, dtype=jnp.int32)[None, None, :]
    dest = jnp.sum(jnp.where(onehot, offs[None, None, :], 0), axis=-1) + rank
    tmv = max(t for t in (TM_MOVE, TM) if dims["s"] % t == 0 and dims["t_lat"] % t == 0 and rows % t == 0)
    dest3 = dest.reshape(rows // tmv, 1, tmv * TOP_K)
    tile_start = jnp.arange(n_slots // TME, dtype=jnp.int32) * TME
    tile_expert = jnp.minimum(jnp.sum(tile_start[:, None] >= offs_end[None, :], axis=1), n_exp - 1).astype(jnp.int32)
    n_active = (offs_end[-1:] // TME).astype(jnp.int32)
    n_valid = jnp.clip(cnt[tile_expert] - (tile_start - offs[tile_expert]), 0, TME).astype(jnp.int32)
    xs = _dispatch(dest3, h2, n_slots)
    y = _experts(tile_expert, n_active, n_valid, xs, layer, wg, bg, wu, bu, wd, bd)

    nt = rows // TM
    per_tile = jnp.sum(onehot.reshape(nt, TM * TOP_K, n_exp), axis=1).astype(jnp.int32)
    start = offs[None, :] + jnp.cumsum(per_tile, axis=0) - per_tile
    first_blk = start // ROW_BLK
    n_blk = jnp.where(per_tile > 0, (start + per_tile + ROW_BLK - 1) // ROW_BLK - first_blk, 0)
    g_blk = jnp.cumsum(n_blk, axis=1) - n_blk
    tab = jnp.concatenate([first_blk, n_blk, g_blk, jnp.sum(n_blk, axis=1, keepdims=True),
                           jnp.zeros((nt, LANES - 3 * n_exp - 1), jnp.int32)], axis=1).reshape(nt, 1, LANES)
    shift = (g_blk - first_blk) * ROW_BLK
    gpos = dest + jnp.sum(jnp.where(onehot.reshape(nt, TM, TOP_K, n_exp), shift[:, None, None, :], 0),
                          axis=-1).reshape(rows, TOP_K)
    return _combine_blocks(tab, gpos, x_mid, mods, route, g_final, y, dims, final, n_exp)


def _rope_tables(s, ctx):
    rows = s // GRID_W
    row = np.repeat(np.arange(rows), GRID_W).astype(np.float32)
    col = np.tile(np.arange(GRID_W), rows).astype(np.float32)
    half = HEAD_DIM // 2
    inv = (np.float32(ROPE_BASE) ** (-np.arange(0, half, 2, dtype=np.float32) / half)).astype(np.float32)
    ar = row[:, None] * inv
    ac = col[:, None] * inv
    ang = np.concatenate([ar, ar, ac, ac], axis=-1)
    cos = np.concatenate([np.cos(ang), np.ones((ctx, HEAD_DIM), np.float32)], axis=0)
    sin = np.concatenate([np.sin(ang), np.zeros((ctx, HEAD_DIM), np.float32)], axis=0)
    return (jnp.asarray(np.tile(cos, (1, 2)), F32), jnp.asarray(np.tile(sin, (1, 2)), F32))


def _rot_columns(w):
    d, n = w.shape
    w4 = w.reshape(d, n // HEAD_DIM, 4, HEAD_DIM // 4)
    return jnp.stack([-w4[:, :, 1], w4[:, :, 0], -w4[:, :, 3], w4[:, :, 2]], axis=2).reshape(d, n)


def kernel(x, c, ctx, c_ctx, w_ada, b_ada, g_norm1, w_in, da_lambda, g_subln, w_conv, w_pool, pool_scale,
           w_branch, w_mgate, b_mgate, w_out, g_norm2, w_router, b_router, w_e_gate, b_e_gate, w_e_up,
           b_e_up, w_e_down, b_e_down, g_final):
    nb, s, _ = x.shape
    nctx = ctx.shape[1]
    depth = w_in.shape[0]
    n_exp = w_router.shape[2]
    t_lat, t_ctx = nb * s, nb * nctx
    dims = dict(nb=nb, s=s, ctx=nctx, t_lat=t_lat, t_all=t_lat + t_ctx, nlat=t_lat // TM, tps=s // TM, tpc=nctx // TM)
    assert s % TK == 0 and s // TK >= 2 and s % (FFT_N2 * 8) == 0 and nctx % TM == 0 and nb < 16
    assert TM == VT_CHUNK and nctx % VT_CHUNK == 0 and ATT_UNROLL % 2 == 0

    cos_t, sin_t = _rope_tables(s, nctx)
    tabs = _fft_tables(s, nctx)
    c16 = jnp.concatenate([c, c_ctx[None, :], jnp.zeros((15 - nb, D), F32)], axis=0)
    mods_all = _adaln(c16, w_ada, b_ada).reshape(depth, 16, 6, D)
    mods_all = jnp.concatenate([mods_all, jnp.zeros((depth, 16, 2, D), F32)], axis=2)

    xa, xb = x.reshape(t_lat, D), ctx.reshape(t_ctx, D)
    for l in range(depth):
        last = l == depth - 1
        lam_init = 0.8 - 0.6 * math.exp(-0.3 * l)
        mods = mods_all[l]
        w_l = w_in[l]
        w_ext = jnp.concatenate([w_l, _rot_columns(w_l[:, :BW]), _rot_columns(w_l[:, BW:2 * BW])], axis=1).astype(BF)
        qt, k, vt, u, sb, p, qq, px = _inproj(xa, xb, mods, g_norm1[l][None, :], w_ext, cos_t, sin_t, tabs["cs"], dims)

        rows = t_lat if last else t_lat + t_ctx
        a = _attention(qt, k, vt, g_subln[l], da_lambda[l], lam_init, dims, latent=True, rows=rows)
        yf = _fourier_latent(p, qq, tabs, dims, rows)
        if not last:
            a = _attention(qt, k, vt, g_subln[l], da_lambda[l], lam_init, dims, latent=False, fill=a)
            yf = _fourier_ctx(p, qq, tabs, dims, yf)

        wconv = jnp.concatenate([w_conv[l], jnp.zeros((5, BW), F32)], axis=0)
        wr = jnp.concatenate([w_router[l], jnp.zeros((D, LANES - n_exp), F32)], axis=1).astype(BF)
        br = jnp.concatenate([b_router[l], jnp.full((LANES - n_exp,), -1e30, F32)])[None, :]
        x_mid, h2, route, counts = _merge(
            xa, xb, mods, g_norm1[l][None, :], a, u, sb, yf, px, wconv, w_pool[l].astype(BF), pool_scale[l][None, :],
            w_branch[l].astype(BF), w_mgate[l].astype(BF), b_mgate[l][None, :], w_out[l].astype(BF),
            g_norm2[l][None, :], wr, br, dims, rows)
        x_new = _moe(x_mid, h2, route, counts, mods, g_final[None, :], l,
                     w_e_gate, b_e_gate[:, :, None, :], w_e_up, b_e_up[:, :, None, :],
                     w_e_down, b_e_down[:, :, None, :], dims, final=last)
        if last:
            return x_new.reshape(nb, s, D)
        xa = xb = x_new
```

```python
import functools
import math

import numpy as np
import jax
import jax.numpy as jnp
from jax import lax
from jax.experimental import pallas as pl
from jax.experimental.pallas import tpu as pltpu

F32 = jnp.float32
BF = jnp.bfloat16

D = 1024
HEADS = 4
HEAD_DIM = 64
HEAD_W = 128
BW = 512
GRID_W = 64
EPS = 1e-6
ROPE_BASE = 10000.0
POOL_WINDOWS = (2, 4, 8, 16)
TOP_K = 4
SWIGLU_LIMIT = 7.0
SWIGLU_ALPHA = 1.702
FFT_N2 = 128

TM = 256
TME = 512
TM_MOVE = 1024
TQ = 512
TK = 512
ATT_UNROLL = 2
VT_CHUNK = 256
HALO = 16
LANES = 128
ROW_BLK = 8
QSCALE = (HEAD_DIM ** -0.5) * math.log2(math.e)
EXP_HEADROOM = 64.0
NORM_MARGIN = 1.05
VMEM_LIMIT = 56 * 2 ** 20


def _cparams(sem, vmem=None):
    return pltpu.CompilerParams(dimension_semantics=sem, vmem_limit_bytes=vmem)


def _modnorm(x, g, shift, scale):
    y = x * lax.rsqrt(jnp.mean(x * x, axis=-1, keepdims=True) + EPS) * g
    return y * (1.0 + scale) + shift


def _sigmoid(x):
    return 1.0 / (1.0 + jnp.exp(-x))


def _adaln_body(c_ref, w_ref, b_ref, o_ref):
    c = c_ref[...]
    s = c * _sigmoid(c)
    o_ref[...] = jnp.dot(s.astype(BF), w_ref[...].astype(BF), preferred_element_type=F32) + b_ref[...]


def _adaln(c16, w_ada, b_ada):
    nl = w_ada.shape[0]
    tn = 1536
    return pl.pallas_call(
        _adaln_body,
        grid=(nl, 6 * D // tn),
        in_specs=[pl.BlockSpec((16, D), lambda l, j: (0, 0)),
                  pl.BlockSpec((None, D, tn), lambda l, j: (l, 0, j)),
                  pl.BlockSpec((None, 1, tn), lambda l, j: (l, 0, j))],
        out_specs=pl.BlockSpec((None, 16, tn), lambda l, j: (l, 0, j)),
        out_shape=jax.ShapeDtypeStruct((nl, 16, 6 * D), F32),
        compiler_params=_cparams(("arbitrary", "arbitrary"), VMEM_LIMIT),
    )(c16, w_ada, b_ada.reshape(nl, 1, 6 * D))


def _token_tile(xa_ref, xb_ref, n_a):
    return jnp.where(pl.program_id(0) < n_a, xa_ref[...], xb_ref[...])


def _token_specs(xa, xb):
    n_a, n_b = xa.shape[0] // TM, xb.shape[0] // TM
    return [pl.BlockSpec((TM, D), lambda i: (jnp.minimum(i, n_a - 1), 0)),
            pl.BlockSpec((TM, D), lambda i: (jnp.clip(i - n_a, 0, n_b - 1), 0))]


def _inproj_body(xa_ref, xb_ref, mod_ref, g_ref, w_ref, cos_ref, sin_ref, cs_ref,
                 q_ref, k_ref, v_ref, u_ref, sb_ref, p_ref, qq_ref, px_ref, *, n_a):
    h = _modnorm(_token_tile(xa_ref, xb_ref, n_a), g_ref[...], mod_ref[0:1, :], mod_ref[1:2, :]).astype(BF)

    def proj(c0):
        return jnp.dot(h, w_ref[:, c0:c0 + BW], preferred_element_type=F32)

    cos = jnp.concatenate([cos_ref[...]] * 4, axis=1)
    sin = jnp.concatenate([sin_ref[...]] * 4, axis=1)
    q_ref[...] = ((proj(0) * cos + proj(8 * BW) * sin) * QSCALE).T.astype(BF)
    k_ref[...] = (proj(BW) * cos + proj(9 * BW) * sin).astype(BF)
    v_ref[...] = proj(2 * BW).T.astype(BF)
    u_ref[...] = (proj(5 * BW) * proj(3 * BW)).astype(BF)
    sb_ref[...] = proj(4 * BW).astype(BF)
    px_ref[...] = proj(7 * BW).astype(BF)
    fx = proj(6 * BW).astype(BF)
    for g in range(4):
        pq = jnp.dot(fx[:, g * LANES:(g + 1) * LANES], cs_ref[...], preferred_element_type=F32)
        p_ref[:, g * LANES:(g + 1) * LANES] = pq[:, :LANES].astype(BF)
        qq_ref[:, g * LANES:(g + 1) * LANES] = pq[:, LANES:].astype(BF)


def _inproj(xa, xb, mods, g1, w_ext, cos_t, sin_t, cs_tab, dims):
    t_all, nlat, tps, tpc, nb = dims["t_all"], dims["nlat"], dims["tps"], dims["tpc"], dims["nb"]

    def tab_idx(i):
        return (jnp.where(i < nlat, i % tps, tps + (i - nlat) % tpc), 0)

    out = jax.ShapeDtypeStruct((t_all, BW), BF)
    row = pl.BlockSpec((TM, BW), lambda i: (i, 0))
    return pl.pallas_call(
        functools.partial(_inproj_body, n_a=xa.shape[0] // TM),
        grid=(t_all // TM,),
        in_specs=_token_specs(xa, xb) + [
                  pl.BlockSpec((None, 8, D), lambda i: (jnp.minimum(i // tps, nb), 0, 0)),
                  pl.BlockSpec((1, D), lambda i: (0, 0)),
                  pl.BlockSpec((D, 10 * BW), lambda i: (0, 0)),
                  pl.BlockSpec((TM, LANES), tab_idx),
                  pl.BlockSpec((TM, LANES), tab_idx),
                  pl.BlockSpec((LANES, 2 * LANES), lambda i: (0, 0))],
        out_specs=[pl.BlockSpec((BW, TM), lambda i: (0, i)), row,
                   pl.BlockSpec((None, BW, VT_CHUNK), lambda i: (i, 0, 0))] + [row] * 5,
        out_shape=[jax.ShapeDtypeStruct((BW, t_all), BF), out,
                   jax.ShapeDtypeStruct((t_all // VT_CHUNK, BW, VT_CHUNK), BF)] + [out] * 5,
        compiler_params=_cparams(("arbitrary",), VMEM_LIMIT),
    )(xa, xb, mods, g1, w_ext, cos_t, sin_t, cs_tab)


def _attn_body(*refs, n_ctx_chunks, n_lat_chunks, lam_init):
    if n_lat_chunks:
        qt_ref, kc_ref, vtc_ref, kl_ref, vtl_ref, g_ref, dl_ref, o_ref, acc_ref, s_buf, p_buf, l_ref, knorm_ref = refs
    else:
        qt_ref, kc_ref, vtc_ref, g_ref, dl_ref, _, o_ref, acc_ref = refs
    qt = qt_ref[...]
    tq = qt.shape[1]
    row = lax.broadcasted_iota(jnp.int32, qt.shape, 0)
    zero = jnp.zeros_like(qt)
    q2 = jnp.concatenate([jnp.where(row < HEAD_DIM, qt, zero), jnp.where(row >= HEAD_DIM, qt, zero)], axis=1)

    def scores(kc):
        s = jnp.dot(kc, q2, preferred_element_type=F32)
        return s, jnp.max(s, axis=0, keepdims=True)

    def softmax(s, smax, m, l):
        m_new = jnp.maximum(m, smax)
        p = jnp.exp2(s - m_new)
        alpha = jnp.exp2(m - m_new)
        return p.astype(BF), m_new, alpha * l + jnp.sum(p, axis=0, keepdims=True), alpha

    def accumulate(alpha, vt, p):
        acc_ref[...] = alpha * acc_ref[...] + jnp.dot(vt, p, preferred_element_type=F32)

    m = jnp.full((1, 2 * tq), -1e30, F32)
    l = jnp.zeros((1, 2 * tq), F32)
    acc_ref[...] = jnp.zeros_like(acc_ref)
    ctx_scores = [scores(kc_ref[c * VT_CHUNK:(c + 1) * VT_CHUNK, :]) for c in range(n_ctx_chunks)]
    if n_lat_chunks:
        per = TK // VT_CHUNK
        n = n_lat_chunks

        def keys(j):
            return kl_ref[pl.ds(pl.multiple_of(j * TK, TK), TK), :]

        s_buf[0] = jnp.dot(keys(0), q2, preferred_element_type=F32)
    for c in range(n_ctx_chunks):
        p, m, l, alpha = softmax(*ctx_scores[c], m, l)
        accumulate(alpha, vtc_ref[c], p)
    if n_lat_chunks:

        def values(j):
            parts = [vtl_ref[j * per + r] for r in range(per)]
            return parts[0] if per == 1 else jnp.concatenate(parts, axis=1)

        @pl.when(pl.program_id(2) == 0)
        def _():
            d_i = lax.broadcasted_iota(jnp.int32, (HEAD_W, LANES), 0)
            c_i = lax.broadcasted_iota(jnp.int32, (HEAD_W, LANES), 1)
            sel = jnp.where((d_i >= HEAD_DIM) == (c_i == 1), 1.0, 0.0) * jnp.where(c_i < 2, 1.0, 0.0)
            best = jnp.zeros((1, LANES), F32)
            for j in range(n):
                kf = keys(j).astype(F32)
                n2 = jnp.dot((kf * kf).astype(BF), sel.astype(BF), preferred_element_type=F32)
                best = jnp.maximum(best, jnp.max(n2, axis=0, keepdims=True))
            lane2 = lax.broadcasted_iota(jnp.int32, (1, 2 * tq), 1)
            knorm_ref[...] = jnp.sqrt(jnp.where(lane2 < tq, best[:, 0:1], best[:, 1:2]))

        qf = q2.astype(F32)
        bound = jnp.sqrt(jnp.sum(qf * qf, axis=0, keepdims=True)) * knorm_ref[...] * NORM_MARGIN
        gap = jnp.max(bound - m)
        l_ref[...] = l

        @pl.when(gap <= EXP_HEADROOM)
        def _():
            lsum = l
            s = s_buf[0]
            for j in range(n):
                s_next = jnp.dot(keys(j + 1), q2, preferred_element_type=F32) if j + 1 < n else None
                p = jnp.exp2(s - m)
                lsum = lsum + jnp.sum(p, axis=0, keepdims=True)
                acc_ref[...] += jnp.dot(values(j), p.astype(BF), preferred_element_type=F32)
                s = s_next
            l_ref[...] = lsum

        @pl.when(gap > EXP_HEADROOM)
        def _():
            _attn_exact_latent(m, l, n, keys, values, scores, softmax, accumulate, s_buf, p_buf, l_ref)

        l = l_ref[...]
    o = acc_ref[...] / l
    dl = dl_ref[...]
    lam = (jnp.exp(jnp.sum(dl[0:1] * dl[1:2], axis=1, keepdims=True))
           - jnp.exp(jnp.sum(dl[2:3] * dl[3:4], axis=1, keepdims=True)) + lam_init)
    d = o[:, :tq] - lam * o[:, tq:]
    y = d * lax.rsqrt(jnp.mean(d * d, axis=0, keepdims=True) + EPS) * g_ref[...] * (1.0 - lam_init)
    o_ref[...] = y.T.astype(BF)


def _attn_exact_latent(m, l, n, keys, values, scores, softmax, accumulate, s_buf, p_buf, l_ref):
    def half_step(j, prev, cur, carry):
        m, l, alpha_prev, smax = carry
        accumulate(alpha_prev, values(j - 1), p_buf[prev])
        if isinstance(j, int) and j + 1 >= n:
            smax_next = smax
        else:
            s_buf[prev], smax_next = scores(keys(jnp.minimum(j + 1, n - 1)))
        p, m, l, alpha = softmax(s_buf[cur], smax, m, l)
        p_buf[cur] = p
        return m, l, alpha, smax_next

    s0 = s_buf[0]
    p, m, l, alpha = softmax(s0, jnp.max(s0, axis=0, keepdims=True), m, l)
    p_buf[0] = p
    s_buf[1], smax = scores(keys(1))

    def body(i, carry):
        for r in range(ATT_UNROLL):
            carry = half_step(ATT_UNROLL * i + r + 1, r % 2, 1 - r % 2, carry)
        return carry

    n_body = (n - 1) // ATT_UNROLL
    carry = lax.fori_loop(0, n_body, body, (m, l, alpha, smax))
    for j in range(n_body * ATT_UNROLL + 1, n):
        carry = half_step(j, (j - 1) % 2, j % 2, carry)
    m, l, alpha, _ = carry
    accumulate(alpha, values(n - 1), p_buf[(n - 1) % 2])
    l_ref[...] = l


def _attention(qt, k, vt, g_sub, da_lambda, lam_init, dims, latent, rows=None, fill=None):
    nb, s, ctx, t_lat = dims["nb"], dims["s"], dims["ctx"], dims["t_lat"]
    cpb = ctx // VT_CHUNK
    tq = TQ if latent else min(TQ, ctx)
    g_sub_b = jnp.broadcast_to(g_sub[:, None], (HEAD_W, tq))
    common_tail = [pl.BlockSpec((HEAD_W, tq), lambda b, h, qi: (0, 0)),
                   pl.BlockSpec((4, HEAD_DIM), lambda b, h, qi: (0, 0))]
    kc_spec = pl.BlockSpec((ctx, HEAD_W), lambda b, h, qi: (t_lat // ctx + b, h))
    vtc_spec = pl.BlockSpec((cpb, HEAD_W, VT_CHUNK), lambda b, h, qi: (t_lat // ctx + b, h, 0))
    scratch = [pltpu.VMEM((HEAD_W, 2 * tq), F32)]
    if latent:
        nq = s // tq
        in_specs = [pl.BlockSpec((HEAD_W, tq), lambda b, h, qi: (h, b * nq + qi)), kc_spec, vtc_spec,
                    pl.BlockSpec((s, HEAD_W), lambda b, h, qi: (b, h)),
                    pl.BlockSpec((s // VT_CHUNK, HEAD_W, VT_CHUNK), lambda b, h, qi: (b, h, 0))] + common_tail
        args = (qt, k, vt, k, vt, g_sub_b, da_lambda)
        row0, n_lat_chunks, aliases = 0, s // TK, {}
        scratch += [pltpu.VMEM((2, TK, 2 * tq), F32), pltpu.VMEM((2, TK, 2 * tq), BF),
                    pltpu.VMEM((1, 2 * tq), F32), pltpu.VMEM((1, 2 * tq), F32)]
    else:
        nq = ctx // tq
        in_specs = [pl.BlockSpec((HEAD_W, tq), lambda b, h, qi: (h, t_lat // tq + b * nq + qi)),
                    kc_spec, vtc_spec] + common_tail
        in_specs.append(pl.BlockSpec(memory_space=pl.ANY))
        args = (qt, k, vt, g_sub_b, da_lambda, fill)
        row0, n_lat_chunks, rows, aliases = t_lat // tq, 0, fill.shape[0], {5: 0}
    return pl.pallas_call(
        functools.partial(_attn_body, n_ctx_chunks=cpb, n_lat_chunks=n_lat_chunks, lam_init=lam_init),
        grid=(nb, HEADS, nq),
        in_specs=in_specs,
        out_specs=pl.BlockSpec((tq, HEAD_W), lambda b, h, qi: (row0 + b * nq + qi, h)),
        out_shape=jax.ShapeDtypeStruct((rows, HEADS * HEAD_W), BF),
        input_output_aliases=aliases,
        scratch_shapes=scratch,
        compiler_params=_cparams(("arbitrary", "arbitrary", "arbitrary"), VMEM_LIMIT),
    )(*args)


def _fft_a_body(fa_ref, p_ref, q_ref, zr_ref, zi_ref):
    x = jnp.concatenate([p_ref[...], q_ref[...]], axis=0)
    z = jnp.dot(fa_ref[...], x, preferred_element_type=F32)
    zr_ref[...] = z[:FFT_N2].astype(BF)
    zi_ref[...] = z[FFT_N2:].astype(BF)


def _fft_b_body(fb_ref, tc_ref, ts_ref, zr_ref, zi_ref, o_ref):
    nk = zr_ref.shape[0]
    parts_r, parts_i = [], []
    for j in range(nk):
        zr = zr_ref[j].astype(F32)
        zi = zi_ref[j].astype(F32)
        tc = jnp.concatenate([tc_ref[j]] * 4, axis=1)
        ts = jnp.concatenate([ts_ref[j]] * 4, axis=1)
        parts_r.append((zr * tc - zi * ts).astype(BF))
        parts_i.append((zr * ts + zi * tc).astype(BF))
    rhs = jnp.concatenate([jnp.concatenate(parts_r, axis=1), jnp.concatenate(parts_i, axis=1)], axis=0)
    o_ref[...] = jnp.dot(fb_ref[...], rhs, preferred_element_type=F32).astype(BF)


def _fft_ctx_body(f_ref, p_ref, q_ref, _, o_ref):
    x = jnp.concatenate([p_ref[...], q_ref[...]], axis=0)
    o_ref[...] = jnp.dot(f_ref[...], x, preferred_element_type=F32).astype(BF)


def _fft_tables(s, ctx):
    n1, n2 = s // FFT_N2, FFT_N2
    norm = 1.0 / math.sqrt(s * LANES)
    a = 2 * np.pi * np.outer(np.arange(n2), np.arange(n2)) / n2
    c, sn = np.cos(a), np.sin(a)
    fa = np.block([[c, -sn], [sn, c]]) * norm
    tw = 2 * np.pi * np.outer(np.arange(n2), np.arange(n1)) / s
    tc = np.repeat(np.cos(tw)[:, :, None], LANES, axis=2)
    ts = np.repeat(np.sin(tw)[:, :, None], LANES, axis=2)
    b = 2 * np.pi * np.outer(np.arange(n1), np.arange(n1)) / n1
    fb = np.concatenate([np.cos(b), -np.sin(b)], axis=1)
    ch = 2 * np.pi * np.outer(np.arange(LANES), np.arange(LANES)) / LANES
    cs = np.concatenate([np.cos(ch), np.sin(ch)], axis=1)
    cx = 2 * np.pi * np.outer(np.arange(ctx), np.arange(ctx)) / ctx
    fctx = np.concatenate([np.cos(cx), -np.sin(cx)], axis=1) / math.sqrt(ctx * LANES)
    return dict(fa=jnp.asarray(fa, BF), tc=jnp.asarray(tc, F32), ts=jnp.asarray(ts, F32),
                fb=jnp.asarray(fb, BF), cs=jnp.asarray(cs, BF), fctx=jnp.asarray(fctx, BF))


def _fourier_latent(p, q, tabs, dims, rows):
    nb, s = dims["nb"], dims["s"]
    n1 = s // FFT_N2
    wide = n1 * BW
    wc = min(4096, wide)
    p2 = p.reshape(-1, wide)
    q2 = q.reshape(-1, wide)
    zshape = jax.ShapeDtypeStruct((nb * FFT_N2, wide), BF)
    zr, zi = pl.pallas_call(
        _fft_a_body,
        grid=(nb, wide // wc),
        in_specs=[pl.BlockSpec((2 * FFT_N2, 2 * FFT_N2), lambda b, j: (0, 0)),
                  pl.BlockSpec((FFT_N2, wc), lambda b, j: (b, j)),
                  pl.BlockSpec((FFT_N2, wc), lambda b, j: (b, j))],
        out_specs=[pl.BlockSpec((FFT_N2, wc), lambda b, j: (b, j))] * 2,
        out_shape=[zshape, zshape],
        compiler_params=_cparams(("arbitrary", "arbitrary"), VMEM_LIMIT),
    )(tabs["fa"], p2, q2)
    kb = 8
    zr4 = zr.reshape(nb, FFT_N2, n1, BW)
    zi4 = zi.reshape(nb, FFT_N2, n1, BW)
    y = pl.pallas_call(
        _fft_b_body,
        grid=(FFT_N2 // kb, nb),
        in_specs=[pl.BlockSpec((n1, 2 * n1), lambda kk, b: (0, 0)),
                  pl.BlockSpec((kb, n1, LANES), lambda kk, b: (kk, 0, 0)),
                  pl.BlockSpec((kb, n1, LANES), lambda kk, b: (kk, 0, 0)),
                  pl.BlockSpec((None, kb, n1, BW), lambda kk, b: (b, kk, 0, 0)),
                  pl.BlockSpec((None, kb, n1, BW), lambda kk, b: (b, kk, 0, 0))],
        out_specs=pl.BlockSpec((n1, kb * BW), lambda kk, b: (b, kk)),
        out_shape=jax.ShapeDtypeStruct((rows // FFT_N2, FFT_N2 * BW), BF),
        compiler_params=_cparams(("arbitrary", "arbitrary"), VMEM_LIMIT),
    )(tabs["fb"], tabs["tc"], tabs["ts"], zr4, zi4)
    return y.reshape(rows, BW)


def _fourier_ctx(p, q, tabs, dims, fill):
    nb, ctx, t_lat = dims["nb"], dims["ctx"], dims["t_lat"]
    return pl.pallas_call(
        _fft_ctx_body,
        grid=(nb,),
        in_specs=[pl.BlockSpec((ctx, 2 * ctx), lambda b: (0, 0)),
                  pl.BlockSpec((ctx, BW), lambda b: (t_lat // ctx + b, 0)),
                  pl.BlockSpec((ctx, BW), lambda b: (t_lat // ctx + b, 0)),
                  pl.BlockSpec(memory_space=pl.ANY)],
        out_specs=pl.BlockSpec((ctx, BW), lambda b: (t_lat // ctx + b, 0)),
        out_shape=jax.ShapeDtypeStruct(fill.shape, BF),
        input_output_aliases={3: 0},
        compiler_params=_cparams(("arbitrary",), VMEM_LIMIT),
    )(tabs["fctx"], p, q, fill)


def _route_topk(h2b, wr_ref, br_ref):
    logits = jnp.dot(h2b, wr_ref[...], preferred_element_type=F32) + br_ref[...]
    lane = lax.broadcasted_iota(jnp.int32, (TM, LANES), 1).astype(F32)
    cur = logits
    vals, idxs = [], []
    for _ in range(TOP_K):
        mx = jnp.max(cur, axis=1, keepdims=True)
        idx = jnp.min(jnp.where(cur == mx, lane, float(LANES)), axis=1, keepdims=True)
        vals.append(mx)
        idxs.append(idx)
        cur = jnp.where(lane == idx, -jnp.inf, cur)
    ex = [jnp.exp(v - vals[0]) for v in vals]
    den = ex[0] + ex[1] + ex[2] + ex[3]
    member = jnp.zeros((TM, LANES), F32)
    for idx in idxs:
        member = member + jnp.where(lane == idx, 1.0, 0.0)
    return idxs, [e / den for e in ex], member


def _route_ranks(idxs, weights, member, carry_ref, live):
    lane = lax.broadcasted_iota(jnp.int32, (TM, LANES), 1).astype(F32)
    tri = (lax.broadcasted_iota(jnp.int32, (TM, TM), 0) > lax.broadcasted_iota(jnp.int32, (TM, TM), 1))
    before = jnp.dot(jnp.where(tri, 1.0, 0.0).astype(BF), member.astype(BF), preferred_element_type=F32)
    rank_all = carry_ref[0:1, :] + before
    route = jnp.zeros((TM, LANES), F32)
    for r in range(TOP_K):
        rank = jnp.sum(jnp.where(lane == idxs[r], rank_all, 0.0), axis=1, keepdims=True)
        route = (route + jnp.where(lane == float(r), idxs[r], 0.0)
                 + jnp.where(lane == float(TOP_K + r), weights[r], 0.0)
                 + jnp.where(lane == float(2 * TOP_K + r), rank, 0.0))
    carry_ref[...] = carry_ref[...] + live * jnp.sum(member, axis=0, keepdims=True)
    return route


def _merge_body(xa_ref, xb_ref, mod_ref, g1_ref, a_ref, u_ref, up_ref, un_ref, sb_ref, yf_ref, px_ref, pp_ref, pn_ref,
                wconv_ref, wpool_ref, pscale_ref, wb_ref, wg_ref, bg_ref, wo_ref, g2_ref, wr_ref, br_ref,
                xo_ref, h2_ref, route_ref, cnt_ref, carry_ref, h2p_ref, *, n_tiles, nlat, tps, tpc, s, ctx, n_a):
    step = pl.program_id(0)
    i = jnp.minimum(step, n_tiles - 1)
    is_lat = i < nlat
    j = jnp.where(is_lat, i % tps, (i - nlat) % tpc)
    first = j == 0
    last = j == jnp.where(is_lat, tps, tpc) - 1
    seq_len = jnp.where(is_lat, s, ctx)

    @pl.when(step == 0)
    def _():
        carry_ref[...] = jnp.zeros_like(carry_ref)
        h2p_ref[...] = jnp.zeros_like(h2p_ref)

    topk_prev = _route_topk(h2p_ref[...], wr_ref, br_ref)

    x = jnp.where(i < n_a, xa_ref[...], xb_ref[...])
    h = _modnorm(x, g1_ref[...], mod_ref[0:1, :], mod_ref[1:2, :]).astype(BF)
    rowi = lax.broadcasted_iota(jnp.int32, (TM, 1), 0)

    u = u_ref[...].astype(F32)
    u_prev = jnp.where(first, 0.0, up_ref[...].astype(F32)[HALO - 1:HALO, :])
    u_next = jnp.where(last, 0.0, un_ref[...].astype(F32)[0:1, :])
    u_dn = jnp.where(rowi == 0, u_prev, pltpu.roll(u, 1, axis=0))
    u_up = jnp.where(rowi == TM - 1, u_next, pltpu.roll(u, TM - 1, axis=0))
    wc = wconv_ref[...]
    y_conv = sb_ref[...].astype(F32) * (wc[0:1] * u_dn + wc[1:2] * u + wc[2:3] * u_up)

    px = px_ref[...].astype(F32)
    ext = jnp.concatenate([jnp.where(first, 0.0, pp_ref[...].astype(F32)), px,
                           jnp.where(last, 0.0, pn_ref[...].astype(F32))], axis=0)
    ext_len = TM + 2 * HALO
    pos = j * TM + rowi
    pooled = []
    for g, wd in enumerate(POOL_WINDOWS):
        e = ext[:, g * LANES:(g + 1) * LANES]
        win = e + pltpu.roll(e, 1, axis=0)
        half = 1
        while 2 * half < wd:
            win = pltpu.roll(win, half, axis=0) + pltpu.roll(win, ext_len - half, axis=0)
            half *= 2
        win = win[HALO:HALO + TM]
        cnt = (jnp.minimum(pos + wd // 2, seq_len) - jnp.maximum(pos - wd // 2, 0)).astype(F32)
        pooled.append((win / cnt - px[:, g * LANES:(g + 1) * LANES]).astype(BF))

    gates = [_sigmoid(jnp.dot(h, wg_ref[:, kk * D:(kk + 1) * D], preferred_element_type=F32)
                      + bg_ref[:, kk * D:(kk + 1) * D]) for kk in range(4)]

    def branch(kk, y):
        return gates[kk] * jnp.dot(y, wb_ref[kk], preferred_element_type=F32)

    acc = branch(0, a_ref[...]) + branch(2, yf_ref[...]) + branch(1, y_conv.astype(BF))
    y_pool = jnp.concatenate([jnp.dot(pooled[g], wpool_ref[g], preferred_element_type=F32)
                              for g in range(len(POOL_WINDOWS))], axis=1) * pscale_ref[...]
    acc = acc + branch(3, y_pool.astype(BF))

    xn = x + mod_ref[2:3, :] * jnp.dot(acc.astype(BF), wo_ref[...], preferred_element_type=F32)
    xo_ref[...] = xn
    h2 = _modnorm(xn, g2_ref[...], mod_ref[3:4, :], mod_ref[4:5, :])
    h2_ref[...] = h2
    h2p_ref[...] = h2.astype(BF)
    route_ref[...] = _route_ranks(*topk_prev, carry_ref, jnp.where(step > 0, 1.0, 0.0))
    cnt_ref[...] = carry_ref[...]


def _merge(xa, xb, mods, g1, a, u, sb, yf, px, wconv, wpool, pscale, wb, wg, bg, wo, g2, wr, br, dims, rows):
    nlat, tps, tpc, nb = dims["nlat"], dims["tps"], dims["tpc"], dims["nb"]
    n_tiles = rows // TM
    n_a, n_b = xa.shape[0] // TM, xb.shape[0] // TM
    nhalo = TM // HALO
    last_halo = u.shape[0] // HALO - 1
    cur = lambda i: jnp.minimum(i, n_tiles - 1)
    tile = lambda w: pl.BlockSpec((TM, w), lambda i: (cur(i), 0))
    prev = pl.BlockSpec((HALO, BW), lambda i: (jnp.maximum(cur(i) * nhalo - 1, 0), 0))
    nxt = pl.BlockSpec((HALO, BW), lambda i: (jnp.minimum((cur(i) + 1) * nhalo, last_halo), 0))
    const = lambda shape: pl.BlockSpec(shape, lambda i: (0,) * len(shape))
    return pl.pallas_call(
        functools.partial(_merge_body, n_tiles=n_tiles, nlat=nlat, tps=tps, tpc=tpc, s=dims["s"], ctx=dims["ctx"],
                          n_a=n_a),
        grid=(n_tiles + 1,),
        in_specs=[pl.BlockSpec((TM, D), lambda i: (jnp.minimum(cur(i), n_a - 1), 0)),
                  pl.BlockSpec((TM, D), lambda i: (jnp.clip(cur(i) - n_a, 0, n_b - 1), 0)),
                  pl.BlockSpec((None, 8, D), lambda i: (jnp.minimum(cur(i) // tps, nb), 0, 0)),
                  const((1, D)), tile(BW), tile(BW), prev, nxt, tile(BW), tile(BW), tile(BW), prev, nxt,
                  const((8, BW)), const((4, LANES, LANES)), const((1, BW)), const((4, BW, D)),
                  const((D, 4 * D)), const((1, 4 * D)), const((D, D)), const((1, D)),
                  const((D, LANES)), const((1, LANES))],
        out_specs=[tile(D), tile(D), pl.BlockSpec((TM, LANES), lambda i: (jnp.maximum(i - 1, 0), 0)),
                   pl.BlockSpec((8, LANES), lambda i: (0, 0))],
        out_shape=[jax.ShapeDtypeStruct((rows, D), F32), jax.ShapeDtypeStruct((rows, D), F32),
                   jax.ShapeDtypeStruct((rows, LANES), F32), jax.ShapeDtypeStruct((8, LANES), F32)],
        scratch_shapes=[pltpu.VMEM((8, LANES), F32), pltpu.VMEM((TM, D), BF)],
        compiler_params=_cparams(("arbitrary",), VMEM_LIMIT),
    )(xa, xb, mods, g1, a, u, u, u, sb, yf, px, px, px, wconv, wpool, pscale, wb, wg, bg, wo, g2, wr, br)


def _wait_row_copies(src_ref, dst_ref, sem, n):
    for _ in range(n):
        pltpu.make_async_copy(src_ref.at[pl.ds(0, 1)], dst_ref.at[pl.ds(0, 1)], sem).wait()


def _dispatch_body(dest_ref, h_ref, xs_ref, sem):
    def body(t, c):
        for r in range(TOP_K):
            pltpu.make_async_copy(h_ref.at[pl.ds(t, 1)], xs_ref.at[pl.ds(dest_ref[0, t * TOP_K + r], 1)], sem).start()
        return c
    lax.fori_loop(0, h_ref.shape[0], body, 0, unroll=8)
    _wait_row_copies(h_ref, xs_ref, sem, h_ref.shape[0] * TOP_K)


def _dispatch(dest3, h2, n_slots):
    rows = h2.shape[0]
    tmv = dest3.shape[2] // TOP_K
    return pl.pallas_call(
        _dispatch_body,
        grid=(rows // tmv,),
        in_specs=[pl.BlockSpec((None, 1, tmv * TOP_K), lambda i: (i, 0, 0), memory_space=pltpu.SMEM),
                  pl.BlockSpec((tmv, D), lambda i: (i, 0))],
        out_specs=pl.BlockSpec(memory_space=pl.ANY),
        out_shape=jax.ShapeDtypeStruct((n_slots, D), F32),
        scratch_shapes=[pltpu.SemaphoreType.DMA(())],
        compiler_params=_cparams(("arbitrary",), VMEM_LIMIT),
    )(dest3, h2)


def _expert_body(te_ref, na_ref, nv_ref, xs_ref, wg_ref, bg_ref, wu_ref, bu_ref, wd_ref, bd_ref, y_ref, wgb, wub, wdb):
    j = pl.program_id(0)

    @pl.when(j < na_ref[0])
    def _():
        @pl.when((j == 0) | (te_ref[j] != te_ref[jnp.maximum(j - 1, 0)]))
        def _():
            wgb[...] = wg_ref[...].astype(BF)
            wub[...] = wu_ref[...].astype(BF)
            wdb[...] = wd_ref[...].astype(BF)

        x = xs_ref[...].astype(BF)
        a = jnp.minimum(jnp.dot(x, wgb[...], preferred_element_type=F32) + bg_ref[...], SWIGLU_LIMIT)
        u = jnp.clip(jnp.dot(x, wub[...], preferred_element_type=F32) + bu_ref[...], -SWIGLU_LIMIT, SWIGLU_LIMIT)
        act = a * _sigmoid(SWIGLU_ALPHA * a) * (u + 1.0)
        y = jnp.dot(act.astype(BF), wdb[...], preferred_element_type=F32) + bd_ref[...]
        rowi = lax.broadcasted_iota(jnp.int32, (TME, 1), 0)
        y_ref[...] = jnp.where(rowi < nv_ref[j], y, 0.0)


def _experts(tile_expert, n_active, n_valid, xs, layer, wg, bg, wu, bu, wd, bd):
    n_slots = xs.shape[0]
    f = wg.shape[3]
    slot = lambda j, te, na, nv: (jnp.minimum(j, na[0] - 1), 0)
    wsel = lambda j, te, na, nv: (layer, te[j], 0, 0)
    return pl.pallas_call(
        _expert_body,
        grid_spec=pltpu.PrefetchScalarGridSpec(
            num_scalar_prefetch=3,
            grid=(n_slots // TME,),
            in_specs=[pl.BlockSpec((TME, D), slot),
                      pl.BlockSpec((None, None, D, f), wsel), pl.BlockSpec((None, None, 1, f), wsel),
                      pl.BlockSpec((None, None, D, f), wsel), pl.BlockSpec((None, None, 1, f), wsel),
                      pl.BlockSpec((None, None, f, D), wsel), pl.BlockSpec((None, None, 1, D), wsel)],
            out_specs=pl.BlockSpec((TME, D), slot),
            scratch_shapes=[pltpu.VMEM((D, f), BF), pltpu.VMEM((D, f), BF), pltpu.VMEM((f, D), BF)]),
        out_shape=jax.ShapeDtypeStruct((n_slots, D), F32),
        compiler_params=_cparams(("arbitrary",), VMEM_LIMIT),
    )(tile_expert, n_active, n_valid, xs, wg, bg, wu, bu, wd, bd)


def _combine_body(dest_ref, x_ref, mod_ref, route_ref, gf_ref, y_ref, o_ref, buf, sem, *, final):
    def body(t, c):
        for r in range(TOP_K):
            pltpu.make_async_copy(y_ref.at[pl.ds(dest_ref[0, t * TOP_K + r], 1)], buf.at[r, pl.ds(t, 1)], sem).start()
        return c
    tmv = x_ref.shape[0]
    lax.fori_loop(0, tmv, body, 0, unroll=8)
    _wait_row_copies(y_ref, buf.at[0], sem, tmv * TOP_K)
    route = route_ref[...]
    acc = jnp.zeros((tmv, D), F32)
    for r in range(TOP_K):
        acc = acc + route[:, TOP_K + r:TOP_K + r + 1] * buf[r]
    xn = x_ref[...] + mod_ref[5:6, :] * acc
    if final:
        xn = xn * lax.rsqrt(jnp.mean(xn * xn, axis=-1, keepdims=True) + EPS) * gf_ref[...]
    o_ref[...] = xn


def _combine(dest3, x_mid, mods, route, g_final, y, dims, final):
    rows = x_mid.shape[0]
    tmv = dest3.shape[2] // TOP_K
    tps, nb = dims["s"] // tmv, dims["nb"]
    return pl.pallas_call(
        functools.partial(_combine_body, final=final),
        grid=(rows // tmv,),
        in_specs=[pl.BlockSpec((None, 1, tmv * TOP_K), lambda i: (i, 0, 0), memory_space=pltpu.SMEM),
                  pl.BlockSpec((tmv, D), lambda i: (i, 0)),
                  pl.BlockSpec((None, 8, D), lambda i: (jnp.minimum(i // tps, nb), 0, 0)),
                  pl.BlockSpec((tmv, LANES), lambda i: (i, 0)),
                  pl.BlockSpec((1, D), lambda i: (0, 0)),
                  pl.BlockSpec(memory_space=pl.ANY)],
        out_specs=pl.BlockSpec((tmv, D), lambda i: (i, 0)),
        out_shape=jax.ShapeDtypeStruct((rows, D), F32),
        scratch_shapes=[pltpu.VMEM((TOP_K, tmv, D), F32), pltpu.SemaphoreType.DMA(())],
        compiler_params=_cparams(("arbitrary",), VMEM_LIMIT),
    )(dest3, x_mid, mods, route, g_final, y)


def _combine_blocks_body(tab_ref, x_ref, mod_ref, route_ref, gpos_ref, gf_ref, y_ref, o_ref, g_ref, sem, *, final, n_exp):
    @pl.when(pl.program_id(0) == 0)
    def _():
        g_ref[...] = jnp.zeros_like(g_ref)

    for e in range(n_exp):
        first_blk, n_blk, g_blk = tab_ref[0, e], tab_ref[0, n_exp + e], tab_ref[0, 2 * n_exp + e]

        def fetch(k, c):
            pltpu.make_async_copy(y_ref.at[pl.ds(first_blk + k, 1)], g_ref.at[pl.ds(g_blk + k, 1)], sem).start()
            return c
        lax.fori_loop(0, n_blk, fetch, 0)

    cap = g_ref.shape[0] * g_ref.shape[1]
    route, gpos = route_ref[...], gpos_ref[...]
    col = lax.broadcasted_iota(jnp.int32, (TM, cap), 1)
    wm = jnp.zeros((TM, cap), F32)
    for r in range(TOP_K):
        wm = wm + jnp.where(col == gpos[:, r:r + 1], route[:, TOP_K + r:TOP_K + r + 1], 0.0)
    wm = wm.astype(BF)

    def drain(k, c):
        pltpu.make_async_copy(y_ref.at[pl.ds(0, 1)], g_ref.at[pl.ds(0, 1)], sem).wait()
        return c
    lax.fori_loop(0, tab_ref[0, 3 * n_exp], drain, 0)
    acc = jnp.dot(wm, g_ref[...].reshape(cap, D).astype(BF), preferred_element_type=F32)
    xn = x_ref[...] + mod_ref[5:6, :] * acc
    if final:
        xn = xn * lax.rsqrt(jnp.mean(xn * xn, axis=-1, keepdims=True) + EPS) * gf_ref[...]
    o_ref[...] = xn


def _combine_blocks(tab, gpos, x_mid, mods, route, g_final, y, dims, final, n_exp):
    rows = x_mid.shape[0]
    tps, nb = dims["tps"], dims["nb"]
    cap_blocks = (TM * TOP_K + n_exp * 2 * ROW_BLK) // ROW_BLK
    return pl.pallas_call(
        functools.partial(_combine_blocks_body, final=final, n_exp=n_exp),
        grid=(rows // TM,),
        in_specs=[pl.BlockSpec((None, 1, LANES), lambda i: (i, 0, 0), memory_space=pltpu.SMEM),
                  pl.BlockSpec((TM, D), lambda i: (i, 0)),
                  pl.BlockSpec((None, 8, D), lambda i: (jnp.minimum(i // tps, nb), 0, 0)),
                  pl.BlockSpec((TM, LANES), lambda i: (i, 0)),
                  pl.BlockSpec((TM, TOP_K), lambda i: (i, 0)),
                  pl.BlockSpec((1, D), lambda i: (0, 0)),
                  pl.BlockSpec(memory_space=pl.ANY)],
        out_specs=pl.BlockSpec((TM, D), lambda i: (i, 0)),
        out_shape=jax.ShapeDtypeStruct((rows, D), F32),
        scratch_shapes=[pltpu.VMEM((cap_blocks, ROW_BLK, D), F32), pltpu.SemaphoreType.DMA(())],
        compiler_params=_cparams(("arbitrary",), VMEM_LIMIT),
    )(tab, x_mid, mods, route, gpos, g_final, y.reshape(-1, ROW_BLK, D))


def _moe(x_mid, h2, route, counts, mods, g_final, layer, wg, bg, wu, bu, wd, bd, dims, final):
    rows = x_mid.shape[0]
    n_exp = wg.shape[1]
    n_slots = rows * TOP_K + n_exp * TME
    e = route[:, 0:TOP_K].astype(jnp.int32)
    rank = route[:, 2 * TOP_K:3 * TOP_K].astype(jnp.int32)
    cnt = counts[0, :n_exp].astype(jnp.int32)
    cnt_pad = ((cnt + TME - 1) // TME) * TME
    offs_end = jnp.cumsum(cnt_pad)
    offs = offs_end - cnt_pad
    onehot = e[:, :, None] == jnp.arange(n_exp, dtype=jnp.int32)[None, None, :]
    dest = jnp.sum(jnp.where(onehot, offs[None, None, :], 0), axis=-1) + rank
    tmv = max(t for t in (TM_MOVE, TM) if dims["s"] % t == 0 and dims["t_lat"] % t == 0 and rows % t == 0)
    dest3 = dest.reshape(rows // tmv, 1, tmv * TOP_K)
    tile_start = jnp.arange(n_slots // TME, dtype=jnp.int32) * TME
    tile_expert = jnp.minimum(jnp.sum(tile_start[:, None] >= offs_end[None, :], axis=1), n_exp - 1).astype(jnp.int32)
    n_active = (offs_end[-1:] // TME).astype(jnp.int32)
    n_valid = jnp.clip(cnt[tile_expert] - (tile_start - offs[tile_expert]), 0, TME).astype(jnp.int32)
    xs = _dispatch(dest3, h2, n_slots)
    y = _experts(tile_expert, n_active, n_valid, xs, layer, wg, bg, wu, bu, wd, bd)

    nt = rows // TM
    per_tile = jnp.sum(onehot.reshape(nt, TM * TOP_K, n_exp), axis=1).astype(jnp.int32)
    start = offs[None, :] + jnp.cumsum(per_tile, axis=0) - per_tile
    first_blk = start // ROW_BLK
    n_blk = jnp.where(per_tile > 0, (start + per_tile + ROW_BLK - 1) // ROW_BLK - first_blk, 0)
    g_blk = jnp.cumsum(n_blk, axis=1) - n_blk
    tab = jnp.concatenate([first_blk, n_blk, g_blk, jnp.sum(n_blk, axis=1, keepdims=True),
                           jnp.zeros((nt, LANES - 3 * n_exp - 1), jnp.int32)], axis=1).reshape(nt, 1, LANES)
    shift = (g_blk - first_blk) * ROW_BLK
    gpos = dest + jnp.sum(jnp.where(onehot.reshape(nt, TM, TOP_K, n_exp), shift[:, None, None, :], 0),
                          axis=-1).reshape(rows, TOP_K)
    return _combine_blocks(tab, gpos, x_mid, mods, route, g_final, y, dims, final, n_exp)


def _rope_tables(s, ctx):
    rows = s // GRID_W
    row = np.repeat(np.arange(rows), GRID_W).astype(np.float32)
    col = np.tile(np.arange(GRID_W), rows).astype(np.float32)
    half = HEAD_DIM // 2
    inv = (np.float32(ROPE_BASE) ** (-np.arange(0, half, 2, dtype=np.float32) / half)).astype(np.float32)
    ar = row[:, None] * inv
    ac = col[:, None] * inv
    ang = np.concatenate([ar, ar, ac, ac], axis=-1)
    cos = np.concatenate([np.cos(ang), np.ones((ctx, HEAD_DIM), np.float32)], axis=0)
    sin = np.concatenate([np.sin(ang), np.zeros((ctx, HEAD_DIM), np.float32)], axis=0)
    return (jnp.asarray(np.tile(cos, (1, 2)), F32), jnp.asarray(np.tile(sin, (1, 2)), F32))


def _rot_columns(w):
    d, n = w.shape
    w4 = w.reshape(d, n // HEAD_DIM, 4, HEAD_DIM // 4)
    return jnp.stack([-w4[:, :, 1], w4[:, :, 0], -w4[:, :, 3], w4[:, :, 2]], axis=2).reshape(d, n)


def kernel(x, c, ctx, c_ctx, w_ada, b_ada, g_norm1, w_in, da_lambda, g_subln, w_conv, w_pool, pool_scale,
           w_branch, w_mgate, b_mgate, w_out, g_norm2, w_router, b_router, w_e_gate, b_e_gate, w_e_up,
           b_e_up, w_e_down, b_e_down, g_final):
    nb, s, _ = x.shape
    nctx = ctx.shape[1]
    depth = w_in.shape[0]
    n_exp = w_router.shape[2]
    t_lat, t_ctx = nb * s, nb * nctx
    dims = dict(nb=nb, s=s, ctx=nctx, t_lat=t_lat, t_all=t_lat + t_ctx, nlat=t_lat // TM, tps=s // TM, tpc=nctx // TM)
    assert s % TK == 0 and s // TK >= 2 and s % (FFT_N2 * 8) == 0 and nctx % TM == 0 and nb < 16
    assert TM == VT_CHUNK and nctx % VT_CHUNK == 0 and ATT_UNROLL % 2 == 0

    cos_t, sin_t = _rope_tables(s, nctx)
    tabs = _fft_tables(s, nctx)
    c16 = jnp.concatenate([c, c_ctx[None, :], jnp.zeros((15 - nb, D), F32)], axis=0)
    mods_all = _adaln(c16, w_ada, b_ada).reshape(depth, 16, 6, D)
    mods_all = jnp.concatenate([mods_all, jnp.zeros((depth, 16, 2, D), F32)], axis=2)

    xa, xb = x.reshape(t_lat, D), ctx.reshape(t_ctx, D)
    for l in range(depth):
        last = l == depth - 1
        lam_init = 0.8 - 0.6 * math.exp(-0.3 * l)
        mods = mods_all[l]
        w_l = w_in[l]
        w_ext = jnp.concatenate([w_l, _rot_columns(w_l[:, :BW]), _rot_columns(w_l[:, BW:2 * BW])], axis=1).astype(BF)
        qt, k, vt, u, sb, p, qq, px = _inproj(xa, xb, mods, g_norm1[l][None, :], w_ext, cos_t, sin_t, tabs["cs"], dims)

        rows = t_lat if last else t_lat + t_ctx
        a = _attention(qt, k, vt, g_subln[l], da_lambda[l], lam_init, dims, latent=True, rows=rows)
        yf = _fourier_latent(p, qq, tabs, dims, rows)
        if not last:
            a = _attention(qt, k, vt, g_subln[l], da_lambda[l], lam_init, dims, latent=False, fill=a)
            yf = _fourier_ctx(p, qq, tabs, dims, yf)

        wconv = jnp.concatenate([w_conv[l], jnp.zeros((5, BW), F32)], axis=0)
        wr = jnp.concatenate([w_router[l], jnp.zeros((D, LANES - n_exp), F32)], axis=1).astype(BF)
        br = jnp.concatenate([b_router[l], jnp.full((LANES - n_exp,), -1e30, F32)])[None, :]
        x_mid, h2, route, counts = _merge(
            xa, xb, mods, g_norm1[l][None, :], a, u, sb, yf, px, wconv, w_pool[l].astype(BF), pool_scale[l][None, :],
            w_branch[l].astype(BF), w_mgate[l].astype(BF), b_mgate[l][None, :], w_out[l].astype(BF),
            g_norm2[l][None, :], wr, br, dims, rows)
        x_new = _moe(x_mid, h2, route, counts, mods, g_final[None, :], l,
                     w_e_gate, b_e_gate[:, :, None, :], w_e_up, b_e_up[:, :, None, :],
                     w_e_down, b_e_down[:, :, None, :], dims, final=last)
        if last:
            return x_new.reshape(nb, s, D)
        xa = xb = x_new
```

```python
import functools
import math

import numpy as np
import jax
import jax.numpy as jnp
from jax import lax
from jax.experimental import pallas as pl
from jax.experimental.pallas import tpu as pltpu

F32 = jnp.float32
BF = jnp.bfloat16

D = 1024
HEADS = 4
HEAD_DIM = 64
HEAD_W = 128
BW = 512
GRID_W = 64
EPS = 1e-6
ROPE_BASE = 10000.0
POOL_WINDOWS = (2, 4, 8, 16)
TOP_K = 4
SWIGLU_LIMIT = 7.0
SWIGLU_ALPHA = 1.702
FFT_N2 = 128

TM = 256
TME = 512
TM_MOVE = 1024
TQ = 512
TK = 512
ATT_UNROLL = 2
VT_CHUNK = 256
HALO = 16
LANES = 128
ROW_BLK = 16
QSCALE = (HEAD_DIM ** -0.5) * math.log2(math.e)
EXP_HEADROOM = 64.0
NORM_MARGIN = 1.05
VMEM_LIMIT = 56 * 2 ** 20


def _cparams(sem, vmem=None):
    return pltpu.CompilerParams(dimension_semantics=sem, vmem_limit_bytes=vmem)


def _modnorm(x, g, shift, scale):
    y = x * lax.rsqrt(jnp.mean(x * x, axis=-1, keepdims=True) + EPS) * g
    return y * (1.0 + scale) + shift


def _sigmoid(x):
    return 1.0 / (1.0 + jnp.exp(-x))


def _adaln_body(c_ref, w_ref, b_ref, o_ref):
    c = c_ref[...]
    s = c * _sigmoid(c)
    o_ref[...] = jnp.dot(s.astype(BF), w_ref[...].astype(BF), preferred_element_type=F32) + b_ref[...]


def _adaln(c16, w_ada, b_ada):
    nl = w_ada.shape[0]
    tn = 1536
    return pl.pallas_call(
        _adaln_body,
        grid=(nl, 6 * D // tn),
        in_specs=[pl.BlockSpec((16, D), lambda l, j: (0, 0)),
                  pl.BlockSpec((None, D, tn), lambda l, j: (l, 0, j)),
                  pl.BlockSpec((None, 1, tn), lambda l, j: (l, 0, j))],
        out_specs=pl.BlockSpec((None, 16, tn), lambda l, j: (l, 0, j)),
        out_shape=jax.ShapeDtypeStruct((nl, 16, 6 * D), F32),
        compiler_params=_cparams(("arbitrary", "arbitrary"), VMEM_LIMIT),
    )(c16, w_ada, b_ada.reshape(nl, 1, 6 * D))


def _token_tile(xa_ref, xb_ref, n_a):
    return jnp.where(pl.program_id(0) < n_a, xa_ref[...], xb_ref[...])


def _token_specs(xa, xb):
    n_a, n_b = xa.shape[0] // TM, xb.shape[0] // TM
    return [pl.BlockSpec((TM, D), lambda i: (jnp.minimum(i, n_a - 1), 0)),
            pl.BlockSpec((TM, D), lambda i: (jnp.clip(i - n_a, 0, n_b - 1), 0))]


def _inproj_body(xa_ref, xb_ref, mod_ref, g_ref, w_ref, cos_ref, sin_ref, cs_ref,
                 q_ref, k_ref, v_ref, u_ref, sb_ref, p_ref, qq_ref, px_ref, *, n_a):
    h = _modnorm(_token_tile(xa_ref, xb_ref, n_a), g_ref[...], mod_ref[0:1, :], mod_ref[1:2, :]).astype(BF)

    def proj(c0):
        return jnp.dot(h, w_ref[:, c0:c0 + BW], preferred_element_type=F32)

    cos = jnp.concatenate([cos_ref[...]] * 4, axis=1)
    sin = jnp.concatenate([sin_ref[...]] * 4, axis=1)
    q_ref[...] = ((proj(0) * cos + proj(8 * BW) * sin) * QSCALE).T.astype(BF)
    k_ref[...] = (proj(BW) * cos + proj(9 * BW) * sin).astype(BF)
    v_ref[...] = proj(2 * BW).T.astype(BF)
    u_ref[...] = (proj(5 * BW) * proj(3 * BW)).astype(BF)
    sb_ref[...] = proj(4 * BW).astype(BF)
    px_ref[...] = proj(7 * BW).astype(BF)
    fx = proj(6 * BW).astype(BF)
    for g in range(4):
        pq = jnp.dot(fx[:, g * LANES:(g + 1) * LANES], cs_ref[...], preferred_element_type=F32)
        p_ref[:, g * LANES:(g + 1) * LANES] = pq[:, :LANES].astype(BF)
        qq_ref[:, g * LANES:(g + 1) * LANES] = pq[:, LANES:].astype(BF)


def _inproj(xa, xb, mods, g1, w_ext, cos_t, sin_t, cs_tab, dims):
    t_all, nlat, tps, tpc, nb = dims["t_all"], dims["nlat"], dims["tps"], dims["tpc"], dims["nb"]

    def tab_idx(i):
        return (jnp.where(i < nlat, i % tps, tps + (i - nlat) % tpc), 0)

    out = jax.ShapeDtypeStruct((t_all, BW), BF)
    row = pl.BlockSpec((TM, BW), lambda i: (i, 0))
    return pl.pallas_call(
        functools.partial(_inproj_body, n_a=xa.shape[0] // TM),
        grid=(t_all // TM,),
        in_specs=_token_specs(xa, xb) + [
                  pl.BlockSpec((None, 8, D), lambda i: (jnp.minimum(i // tps, nb), 0, 0)),
                  pl.BlockSpec((1, D), lambda i: (0, 0)),
                  pl.BlockSpec((D, 10 * BW), lambda i: (0, 0)),
                  pl.BlockSpec((TM, LANES), tab_idx),
                  pl.BlockSpec((TM, LANES), tab_idx),
                  pl.BlockSpec((LANES, 2 * LANES), lambda i: (0, 0))],
        out_specs=[pl.BlockSpec((BW, TM), lambda i: (0, i)), row,
                   pl.BlockSpec((None, BW, VT_CHUNK), lambda i: (i, 0, 0))] + [row] * 5,
        out_shape=[jax.ShapeDtypeStruct((BW, t_all), BF), out,
                   jax.ShapeDtypeStruct((t_all // VT_CHUNK, BW, VT_CHUNK), BF)] + [out] * 5,
        compiler_params=_cparams(("arbitrary",), VMEM_LIMIT),
    )(xa, xb, mods, g1, w_ext, cos_t, sin_t, cs_tab)


def _attn_body(*refs, n_ctx_chunks, n_lat_chunks, lam_init):
    if n_lat_chunks:
        qt_ref, kc_ref, vtc_ref, kl_ref, vtl_ref, g_ref, dl_ref, o_ref, acc_ref, s_buf, p_buf, l_ref, knorm_ref = refs
    else:
        qt_ref, kc_ref, vtc_ref, g_ref, dl_ref, _, o_ref, acc_ref = refs
    qt = qt_ref[...]
    tq = qt.shape[1]
    row = lax.broadcasted_iota(jnp.int32, qt.shape, 0)
    zero = jnp.zeros_like(qt)
    q2 = jnp.concatenate([jnp.where(row < HEAD_DIM, qt, zero), jnp.where(row >= HEAD_DIM, qt, zero)], axis=1)

    def scores(kc):
        s = jnp.dot(kc, q2, preferred_element_type=F32)
        return s, jnp.max(s, axis=0, keepdims=True)

    def softmax(s, smax, m, l):
        m_new = jnp.maximum(m, smax)
        p = jnp.exp2(s - m_new)
        alpha = jnp.exp2(m - m_new)
        return p.astype(BF), m_new, alpha * l + jnp.sum(p, axis=0, keepdims=True), alpha

    def accumulate(alpha, vt, p):
        acc_ref[...] = alpha * acc_ref[...] + jnp.dot(vt, p, preferred_element_type=F32)

    m = jnp.full((1, 2 * tq), -1e30, F32)
    l = jnp.zeros((1, 2 * tq), F32)
    acc_ref[...] = jnp.zeros_like(acc_ref)
    ctx_scores = [scores(kc_ref[c * VT_CHUNK:(c + 1) * VT_CHUNK, :]) for c in range(n_ctx_chunks)]
    if n_lat_chunks:
        per = TK // VT_CHUNK
        n = n_lat_chunks

        def keys(j):
            return kl_ref[pl.ds(pl.multiple_of(j * TK, TK), TK), :]

        s_buf[0] = jnp.dot(keys(0), q2, preferred_element_type=F32)
    for c in range(n_ctx_chunks):
        p, m, l, alpha = softmax(*ctx_scores[c], m, l)
        accumulate(alpha, vtc_ref[c], p)
    if n_lat_chunks:

        def values(j):
            parts = [vtl_ref[j * per + r] for r in range(per)]
            return parts[0] if per == 1 else jnp.concatenate(parts, axis=1)

        @pl.when(pl.program_id(2) == 0)
        def _():
            d_i = lax.broadcasted_iota(jnp.int32, (HEAD_W, LANES), 0)
            c_i = lax.broadcasted_iota(jnp.int32, (HEAD_W, LANES), 1)
            sel = jnp.where((d_i >= HEAD_DIM) == (c_i == 1), 1.0, 0.0) * jnp.where(c_i < 2, 1.0, 0.0)
            best = jnp.zeros((1, LANES), F32)
            for j in range(n):
                kf = keys(j).astype(F32)
                n2 = jnp.dot((kf * kf).astype(BF), sel.astype(BF), preferred_element_type=F32)
                best = jnp.maximum(best, jnp.max(n2, axis=0, keepdims=True))
            lane2 = lax.broadcasted_iota(jnp.int32, (1, 2 * tq), 1)
            knorm_ref[...] = jnp.sqrt(jnp.where(lane2 < tq, best[:, 0:1], best[:, 1:2]))

        qf = q2.astype(F32)
        bound = jnp.sqrt(jnp.sum(qf * qf, axis=0, keepdims=True)) * knorm_ref[...] * NORM_MARGIN
        gap = jnp.max(bound - m)
        l_ref[...] = l

        @pl.when(gap <= EXP_HEADROOM)
        def _():
            lsum = l
            s = s_buf[0]
            for j in range(n):
                s_next = jnp.dot(keys(j + 1), q2, preferred_element_type=F32) if j + 1 < n else None
                p = jnp.exp2(s - m)
                lsum = lsum + jnp.sum(p, axis=0, keepdims=True)
                acc_ref[...] += jnp.dot(values(j), p.astype(BF), preferred_element_type=F32)
                s = s_next
            l_ref[...] = lsum

        @pl.when(gap > EXP_HEADROOM)
        def _():
            _attn_exact_latent(m, l, n, keys, values, scores, softmax, accumulate, s_buf, p_buf, l_ref)

        l = l_ref[...]
    o = acc_ref[...] / l
    dl = dl_ref[...]
    lam = (jnp.exp(jnp.sum(dl[0:1] * dl[1:2], axis=1, keepdims=True))
           - jnp.exp(jnp.sum(dl[2:3] * dl[3:4], axis=1, keepdims=True)) + lam_init)
    d = o[:, :tq] - lam * o[:, tq:]
    y = d * lax.rsqrt(jnp.mean(d * d, axis=0, keepdims=True) + EPS) * g_ref[...] * (1.0 - lam_init)
    o_ref[...] = y.T.astype(BF)


def _attn_exact_latent(m, l, n, keys, values, scores, softmax, accumulate, s_buf, p_buf, l_ref):
    def half_step(j, prev, cur, carry):
        m, l, alpha_prev, smax = carry
        accumulate(alpha_prev, values(j - 1), p_buf[prev])
        if isinstance(j, int) and j + 1 >= n:
            smax_next = smax
        else:
            s_buf[prev], smax_next = scores(keys(jnp.minimum(j + 1, n - 1)))
        p, m, l, alpha = softmax(s_buf[cur], smax, m, l)
        p_buf[cur] = p
        return m, l, alpha, smax_next

    s0 = s_buf[0]
    p, m, l, alpha = softmax(s0, jnp.max(s0, axis=0, keepdims=True), m, l)
    p_buf[0] = p
    s_buf[1], smax = scores(keys(1))

    def body(i, carry):
        for r in range(ATT_UNROLL):
            carry = half_step(ATT_UNROLL * i + r + 1, r % 2, 1 - r % 2, carry)
        return carry

    n_body = (n - 1) // ATT_UNROLL
    carry = lax.fori_loop(0, n_body, body, (m, l, alpha, smax))
    for j in range(n_body * ATT_UNROLL + 1, n):
        carry = half_step(j, (j - 1) % 2, j % 2, carry)
    m, l, alpha, _ = carry
    accumulate(alpha, values(n - 1), p_buf[(n - 1) % 2])
    l_ref[...] = l


def _attention(qt, k, vt, g_sub, da_lambda, lam_init, dims, latent, rows=None, fill=None):
    nb, s, ctx, t_lat = dims["nb"], dims["s"], dims["ctx"], dims["t_lat"]
    cpb = ctx // VT_CHUNK
    tq = TQ if latent else min(TQ, ctx)
    g_sub_b = jnp.broadcast_to(g_sub[:, None], (HEAD_W, tq))
    common_tail = [pl.BlockSpec((HEAD_W, tq), lambda b, h, qi: (0, 0)),
                   pl.BlockSpec((4, HEAD_DIM), lambda b, h, qi: (0, 0))]
    kc_spec = pl.BlockSpec((ctx, HEAD_W), lambda b, h, qi: (t_lat // ctx + b, h))
    vtc_spec = pl.BlockSpec((cpb, HEAD_W, VT_CHUNK), lambda b, h, qi: (t_lat // ctx + b, h, 0))
    scratch = [pltpu.VMEM((HEAD_W, 2 * tq), F32)]
    if latent:
        nq = s // tq
        in_specs = [pl.BlockSpec((HEAD_W, tq), lambda b, h, qi: (h, b * nq + qi)), kc_spec, vtc_spec,
                    pl.BlockSpec((s, HEAD_W), lambda b, h, qi: (b, h)),
                    pl.BlockSpec((s // VT_CHUNK, HEAD_W, VT_CHUNK), lambda b, h, qi: (b, h, 0))] + common_tail
        args = (qt, k, vt, k, vt, g_sub_b, da_lambda)
        row0, n_lat_chunks, aliases = 0, s // TK, {}
        scratch += [pltpu.VMEM((2, TK, 2 * tq), F32), pltpu.VMEM((2, TK, 2 * tq), BF),
                    pltpu.VMEM((1, 2 * tq), F32), pltpu.VMEM((1, 2 * tq), F32)]
    else:
        nq = ctx // tq
        in_specs = [pl.BlockSpec((HEAD_W, tq), lambda b, h, qi: (h, t_lat // tq + b * nq + qi)),
                    kc_spec, vtc_spec] + common_tail
        in_specs.append(pl.BlockSpec(memory_space=pl.ANY))
        args = (qt, k, vt, g_sub_b, da_lambda, fill)
        row0, n_lat_chunks, rows, aliases = t_lat // tq, 0, fill.shape[0], {5: 0}
    return pl.pallas_call(
        functools.partial(_attn_body, n_ctx_chunks=cpb, n_lat_chunks=n_lat_chunks, lam_init=lam_init),
        grid=(nb, HEADS, nq),
        in_specs=in_specs,
        out_specs=pl.BlockSpec((tq, HEAD_W), lambda b, h, qi: (row0 + b * nq + qi, h)),
        out_shape=jax.ShapeDtypeStruct((rows, HEADS * HEAD_W), BF),
        input_output_aliases=aliases,
        scratch_shapes=scratch,
        compiler_params=_cparams(("arbitrary", "arbitrary", "arbitrary"), VMEM_LIMIT),
    )(*args)


def _fft_a_body(fa_ref, p_ref, q_ref, zr_ref, zi_ref):
    x = jnp.concatenate([p_ref[...], q_ref[...]], axis=0)
    z = jnp.dot(fa_ref[...], x, preferred_element_type=F32)
    zr_ref[...] = z[:FFT_N2].astype(BF)
    zi_ref[...] = z[FFT_N2:].astype(BF)


def _fft_b_body(fb_ref, tc_ref, ts_ref, zr_ref, zi_ref, o_ref):
    nk = zr_ref.shape[0]
    parts_r, parts_i = [], []
    for j in range(nk):
        zr = zr_ref[j].astype(F32)
        zi = zi_ref[j].astype(F32)
        tc = jnp.concatenate([tc_ref[j]] * 4, axis=1)
        ts = jnp.concatenate([ts_ref[j]] * 4, axis=1)
        parts_r.append((zr * tc - zi * ts).astype(BF))
        parts_i.append((zr * ts + zi * tc).astype(BF))
    rhs = jnp.concatenate([jnp.concatenate(parts_r, axis=1), jnp.concatenate(parts_i, axis=1)], axis=0)
    o_ref[...] = jnp.dot(fb_ref[...], rhs, preferred_element_type=F32).astype(BF)


def _fft_ctx_body(f_ref, p_ref, q_ref, _, o_ref):
    x = jnp.concatenate([p_ref[...], q_ref[...]], axis=0)
    o_ref[...] = jnp.dot(f_ref[...], x, preferred_element_type=F32).astype(BF)


def _fft_tables(s, ctx):
    n1, n2 = s // FFT_N2, FFT_N2
    norm = 1.0 / math.sqrt(s * LANES)
    a = 2 * np.pi * np.outer(np.arange(n2), np.arange(n2)) / n2
    c, sn = np.cos(a), np.sin(a)
    fa = np.block([[c, -sn], [sn, c]]) * norm
    tw = 2 * np.pi * np.outer(np.arange(n2), np.arange(n1)) / s
    tc = np.repeat(np.cos(tw)[:, :, None], LANES, axis=2)
    ts = np.repeat(np.sin(tw)[:, :, None], LANES, axis=2)
    b = 2 * np.pi * np.outer(np.arange(n1), np.arange(n1)) / n1
    fb = np.concatenate([np.cos(b), -np.sin(b)], axis=1)
    ch = 2 * np.pi * np.outer(np.arange(LANES), np.arange(LANES)) / LANES
    cs = np.concatenate([np.cos(ch), np.sin(ch)], axis=1)
    cx = 2 * np.pi * np.outer(np.arange(ctx), np.arange(ctx)) / ctx
    fctx = np.concatenate([np.cos(cx), -np.sin(cx)], axis=1) / math.sqrt(ctx * LANES)
    return dict(fa=jnp.asarray(fa, BF), tc=jnp.asarray(tc, F32), ts=jnp.asarray(ts, F32),
                fb=jnp.asarray(fb, BF), cs=jnp.asarray(cs, BF), fctx=jnp.asarray(fctx, BF))


def _fourier_latent(p, q, tabs, dims, rows):
    nb, s = dims["nb"], dims["s"]
    n1 = s // FFT_N2
    wide = n1 * BW
    wc = min(4096, wide)
    p2 = p.reshape(-1, wide)
    q2 = q.reshape(-1, wide)
    zshape = jax.ShapeDtypeStruct((nb * FFT_N2, wide), BF)
    zr, zi = pl.pallas_call(
        _fft_a_body,
        grid=(nb, wide // wc),
        in_specs=[pl.BlockSpec((2 * FFT_N2, 2 * FFT_N2), lambda b, j: (0, 0)),
                  pl.BlockSpec((FFT_N2, wc), lambda b, j: (b, j)),
                  pl.BlockSpec((FFT_N2, wc), lambda b, j: (b, j))],
        out_specs=[pl.BlockSpec((FFT_N2, wc), lambda b, j: (b, j))] * 2,
        out_shape=[zshape, zshape],
        compiler_params=_cparams(("arbitrary", "arbitrary"), VMEM_LIMIT),
    )(tabs["fa"], p2, q2)
    kb = 8
    zr4 = zr.reshape(nb, FFT_N2, n1, BW)
    zi4 = zi.reshape(nb, FFT_N2, n1, BW)
    y = pl.pallas_call(
        _fft_b_body,
        grid=(FFT_N2 // kb, nb),
        in_specs=[pl.BlockSpec((n1, 2 * n1), lambda kk, b: (0, 0)),
                  pl.BlockSpec((kb, n1, LANES), lambda kk, b: (kk, 0, 0)),
                  pl.BlockSpec((kb, n1, LANES), lambda kk, b: (kk, 0, 0)),
                  pl.BlockSpec((None, kb, n1, BW), lambda kk, b: (b, kk, 0, 0)),
                  pl.BlockSpec((None, kb, n1, BW), lambda kk, b: (b, kk, 0, 0))],
        out_specs=pl.BlockSpec((n1, kb * BW), lambda kk, b: (b, kk)),
        out_shape=jax.ShapeDtypeStruct((rows // FFT_N2, FFT_N2 * BW), BF),
        compiler_params=_cparams(("arbitrary", "arbitrary"), VMEM_LIMIT),
    )(tabs["fb"], tabs["tc"], tabs["ts"], zr4, zi4)
    return y.reshape(rows, BW)


def _fourier_ctx(p, q, tabs, dims, fill):
    nb, ctx, t_lat = dims["nb"], dims["ctx"], dims["t_lat"]
    return pl.pallas_call(
        _fft_ctx_body,
        grid=(nb,),
        in_specs=[pl.BlockSpec((ctx, 2 * ctx), lambda b: (0, 0)),
                  pl.BlockSpec((ctx, BW), lambda b: (t_lat // ctx + b, 0)),
                  pl.BlockSpec((ctx, BW), lambda b: (t_lat // ctx + b, 0)),
                  pl.BlockSpec(memory_space=pl.ANY)],
        out_specs=pl.BlockSpec((ctx, BW), lambda b: (t_lat // ctx + b, 0)),
        out_shape=jax.ShapeDtypeStruct(fill.shape, BF),
        input_output_aliases={3: 0},
        compiler_params=_cparams(("arbitrary",), VMEM_LIMIT),
    )(tabs["fctx"], p, q, fill)


def _route_topk(h2b, wr_ref, br_ref):
    logits = jnp.dot(h2b, wr_ref[...], preferred_element_type=F32) + br_ref[...]
    lane = lax.broadcasted_iota(jnp.int32, (TM, LANES), 1).astype(F32)
    cur = logits
    vals, idxs = [], []
    for _ in range(TOP_K):
        mx = jnp.max(cur, axis=1, keepdims=True)
        idx = jnp.min(jnp.where(cur == mx, lane, float(LANES)), axis=1, keepdims=True)
        vals.append(mx)
        idxs.append(idx)
        cur = jnp.where(lane == idx, -jnp.inf, cur)
    ex = [jnp.exp(v - vals[0]) for v in vals]
    den = ex[0] + ex[1] + ex[2] + ex[3]
    member = jnp.zeros((TM, LANES), F32)
    for idx in idxs:
        member = member + jnp.where(lane == idx, 1.0, 0.0)
    return idxs, [e / den for e in ex], member


def _route_ranks(idxs, weights, member, carry_ref, live):
    lane = lax.broadcasted_iota(jnp.int32, (TM, LANES), 1).astype(F32)
    tri = (lax.broadcasted_iota(jnp.int32, (TM, TM), 0) > lax.broadcasted_iota(jnp.int32, (TM, TM), 1))
    before = jnp.dot(jnp.where(tri, 1.0, 0.0).astype(BF), member.astype(BF), preferred_element_type=F32)
    rank_all = carry_ref[0:1, :] + before
    route = jnp.zeros((TM, LANES), F32)
    for r in range(TOP_K):
        rank = jnp.sum(jnp.where(lane == idxs[r], rank_all, 0.0), axis=1, keepdims=True)
        route = (route + jnp.where(lane == float(r), idxs[r], 0.0)
                 + jnp.where(lane == float(TOP_K + r), weights[r], 0.0)
                 + jnp.where(lane == float(2 * TOP_K + r), rank, 0.0))
    carry_ref[...] = carry_ref[...] + live * jnp.sum(member, axis=0, keepdims=True)
    return route


def _merge_body(xa_ref, xb_ref, mod_ref, g1_ref, a_ref, u_ref, up_ref, un_ref, sb_ref, yf_ref, px_ref, pp_ref, pn_ref,
                wconv_ref, wpool_ref, pscale_ref, wb_ref, wg_ref, bg_ref, wo_ref, g2_ref, wr_ref, br_ref,
                xo_ref, h2_ref, route_ref, cnt_ref, carry_ref, h2p_ref, *, n_tiles, nlat, tps, tpc, s, ctx, n_a):
    step = pl.program_id(0)
    i = jnp.minimum(step, n_tiles - 1)
    is_lat = i < nlat
    j = jnp.where(is_lat, i % tps, (i - nlat) % tpc)
    first = j == 0
    last = j == jnp.where(is_lat, tps, tpc) - 1
    seq_len = jnp.where(is_lat, s, ctx)

    @pl.when(step == 0)
    def _():
        carry_ref[...] = jnp.zeros_like(carry_ref)
        h2p_ref[...] = jnp.zeros_like(h2p_ref)

    topk_prev = _route_topk(h2p_ref[...], wr_ref, br_ref)

    x = jnp.where(i < n_a, xa_ref[...], xb_ref[...])
    h = _modnorm(x, g1_ref[...], mod_ref[0:1, :], mod_ref[1:2, :]).astype(BF)
    rowi = lax.broadcasted_iota(jnp.int32, (TM, 1), 0)

    u = u_ref[...].astype(F32)
    u_prev = jnp.where(first, 0.0, up_ref[...].astype(F32)[HALO - 1:HALO, :])
    u_next = jnp.where(last, 0.0, un_ref[...].astype(F32)[0:1, :])
    u_dn = jnp.where(rowi == 0, u_prev, pltpu.roll(u, 1, axis=0))
    u_up = jnp.where(rowi == TM - 1, u_next, pltpu.roll(u, TM - 1, axis=0))
    wc = wconv_ref[...]
    y_conv = sb_ref[...].astype(F32) * (wc[0:1] * u_dn + wc[1:2] * u + wc[2:3] * u_up)

    px = px_ref[...].astype(F32)
    ext = jnp.concatenate([jnp.where(first, 0.0, pp_ref[...].astype(F32)), px,
                           jnp.where(last, 0.0, pn_ref[...].astype(F32))], axis=0)
    ext_len = TM + 2 * HALO
    pos = j * TM + rowi
    pooled = []
    for g, wd in enumerate(POOL_WINDOWS):
        e = ext[:, g * LANES:(g + 1) * LANES]
        win = e + pltpu.roll(e, 1, axis=0)
        half = 1
        while 2 * half < wd:
            win = pltpu.roll(win, half, axis=0) + pltpu.roll(win, ext_len - half, axis=0)
            half *= 2
        win = win[HALO:HALO + TM]
        cnt = (jnp.minimum(pos + wd // 2, seq_len) - jnp.maximum(pos - wd // 2, 0)).astype(F32)
        pooled.append((win / cnt - px[:, g * LANES:(g + 1) * LANES]).astype(BF))

    gates = [_sigmoid(jnp.dot(h, wg_ref[:, kk * D:(kk + 1) * D], preferred_element_type=F32)
                      + bg_ref[:, kk * D:(kk + 1) * D]) for kk in range(4)]

    def branch(kk, y):
        return gates[kk] * jnp.dot(y, wb_ref[kk], preferred_element_type=F32)

    acc = branch(0, a_ref[...]) + branch(2, yf_ref[...]) + branch(1, y_conv.astype(BF))
    y_pool = jnp.concatenate([jnp.dot(pooled[g], wpool_ref[g], preferred_element_type=F32)
                              for g in range(len(POOL_WINDOWS))], axis=1) * pscale_ref[...]
    acc = acc + branch(3, y_pool.astype(BF))

    xn = x + mod_ref[2:3, :] * jnp.dot(acc.astype(BF), wo_ref[...], preferred_element_type=F32)
    xo_ref[...] = xn
    h2 = _modnorm(xn, g2_ref[...], mod_ref[3:4, :], mod_ref[4:5, :])
    h2_ref[...] = h2
    h2p_ref[...] = h2.astype(BF)
    route_ref[...] = _route_ranks(*topk_prev, carry_ref, jnp.where(step > 0, 1.0, 0.0))
    cnt_ref[...] = carry_ref[...]


def _merge(xa, xb, mods, g1, a, u, sb, yf, px, wconv, wpool, pscale, wb, wg, bg, wo, g2, wr, br, dims, rows):
    nlat, tps, tpc, nb = dims["nlat"], dims["tps"], dims["tpc"], dims["nb"]
    n_tiles = rows // TM
    n_a, n_b = xa.shape[0] // TM, xb.shape[0] // TM
    nhalo = TM // HALO
    last_halo = u.shape[0] // HALO - 1
    cur = lambda i: jnp.minimum(i, n_tiles - 1)
    tile = lambda w: pl.BlockSpec((TM, w), lambda i: (cur(i), 0))
    prev = pl.BlockSpec((HALO, BW), lambda i: (jnp.maximum(cur(i) * nhalo - 1, 0), 0))
    nxt = pl.BlockSpec((HALO, BW), lambda i: (jnp.minimum((cur(i) + 1) * nhalo, last_halo), 0))
    const = lambda shape: pl.BlockSpec(shape, lambda i: (0,) * len(shape))
    return pl.pallas_call(
        functools.partial(_merge_body, n_tiles=n_tiles, nlat=nlat, tps=tps, tpc=tpc, s=dims["s"], ctx=dims["ctx"],
                          n_a=n_a),
        grid=(n_tiles + 1,),
        in_specs=[pl.BlockSpec((TM, D), lambda i: (jnp.minimum(cur(i), n_a - 1), 0)),
                  pl.BlockSpec((TM, D), lambda i: (jnp.clip(cur(i) - n_a, 0, n_b - 1), 0)),
                  pl.BlockSpec((None, 8, D), lambda i: (jnp.minimum(cur(i) // tps, nb), 0, 0)),
                  const((1, D)), tile(BW), tile(BW), prev, nxt, tile(BW), tile(BW), tile(BW), prev, nxt,
                  const((8, BW)), const((4, LANES, LANES)), const((1, BW)), const((4, BW, D)),
                  const((D, 4 * D)), const((1, 4 * D)), const((D, D)), const((1, D)),
                  const((D, LANES)), const((1, LANES))],
        out_specs=[tile(D), tile(D), pl.BlockSpec((TM, LANES), lambda i: (jnp.maximum(i - 1, 0), 0)),
                   pl.BlockSpec((8, LANES), lambda i: (0, 0))],
        out_shape=[jax.ShapeDtypeStruct((rows, D), F32), jax.ShapeDtypeStruct((rows, D), F32),
                   jax.ShapeDtypeStruct((rows, LANES), F32), jax.ShapeDtypeStruct((8, LANES), F32)],
        scratch_shapes=[pltpu.VMEM((8, LANES), F32), pltpu.VMEM((TM, D), BF)],
        compiler_params=_cparams(("arbitrary",), VMEM_LIMIT),
    )(xa, xb, mods, g1, a, u, u, u, sb, yf, px, px, px, wconv, wpool, pscale, wb, wg, bg, wo, g2, wr, br)


def _wait_row_copies(src_ref, dst_ref, sem, n):
    for _ in range(n):
        pltpu.make_async_copy(src_ref.at[pl.ds(0, 1)], dst_ref.at[pl.ds(0, 1)], sem).wait()


def _dispatch_body(dest_ref, h_ref, xs_ref, sem):
    def body(t, c):
        for r in range(TOP_K):
            pltpu.make_async_copy(h_ref.at[pl.ds(t, 1)], xs_ref.at[pl.ds(dest_ref[0, t * TOP_K + r], 1)], sem).start()
        return c
    lax.fori_loop(0, h_ref.shape[0], body, 0, unroll=8)
    _wait_row_copies(h_ref, xs_ref, sem, h_ref.shape[0] * TOP_K)


def _dispatch(dest3, h2, n_slots):
    rows = h2.shape[0]
    tmv = dest3.shape[2] // TOP_K
    return pl.pallas_call(
        _dispatch_body,
        grid=(rows // tmv,),
        in_specs=[pl.BlockSpec((None, 1, tmv * TOP_K), lambda i: (i, 0, 0), memory_space=pltpu.SMEM),
                  pl.BlockSpec((tmv, D), lambda i: (i, 0))],
        out_specs=pl.BlockSpec(memory_space=pl.ANY),
        out_shape=jax.ShapeDtypeStruct((n_slots, D), F32),
        scratch_shapes=[pltpu.SemaphoreType.DMA(())],
        compiler_params=_cparams(("arbitrary",), VMEM_LIMIT),
    )(dest3, h2)


def _expert_body(te_ref, na_ref, nv_ref, xs_ref, wg_ref, bg_ref, wu_ref, bu_ref, wd_ref, bd_ref, y_ref, wgb, wub, wdb):
    j = pl.program_id(0)

    @pl.when(j < na_ref[0])
    def _():
        @pl.when((j == 0) | (te_ref[j] != te_ref[jnp.maximum(j - 1, 0)]))
        def _():
            wgb[...] = wg_ref[...].astype(BF)
            wub[...] = wu_ref[...].astype(BF)
            wdb[...] = wd_ref[...].astype(BF)

        x = xs_ref[...].astype(BF)
        a = jnp.minimum(jnp.dot(x, wgb[...], preferred_element_type=F32) + bg_ref[...], SWIGLU_LIMIT)
        u = jnp.clip(jnp.dot(x, wub[...], preferred_element_type=F32) + bu_ref[...], -SWIGLU_LIMIT, SWIGLU_LIMIT)
        act = a * _sigmoid(SWIGLU_ALPHA * a) * (u + 1.0)
        y = jnp.dot(act.astype(BF), wdb[...], preferred_element_type=F32) + bd_ref[...]
        rowi = lax.broadcasted_iota(jnp.int32, (TME, 1), 0)
        y_ref[...] = jnp.where(rowi < nv_ref[j], y, 0.0).astype(BF)


def _experts(tile_expert, n_active, n_valid, xs, layer, wg, bg, wu, bu, wd, bd):
    n_slots = xs.shape[0]
    f = wg.shape[3]
    slot = lambda j, te, na, nv: (jnp.minimum(j, na[0] - 1), 0)
    wsel = lambda j, te, na, nv: (layer, te[j], 0, 0)
    return pl.pallas_call(
        _expert_body,
        grid_spec=pltpu.PrefetchScalarGridSpec(
            num_scalar_prefetch=3,
            grid=(n_slots // TME,),
            in_specs=[pl.BlockSpec((TME, D), slot),
                      pl.BlockSpec((None, None, D, f), wsel), pl.BlockSpec((None, None, 1, f), wsel),
                      pl.BlockSpec((None, None, D, f), wsel), pl.BlockSpec((None, None, 1, f), wsel),
                      pl.BlockSpec((None, None, f, D), wsel), pl.BlockSpec((None, None, 1, D), wsel)],
            out_specs=pl.BlockSpec((TME, D), slot),
            scratch_shapes=[pltpu.VMEM((D, f), BF), pltpu.VMEM((D, f), BF), pltpu.VMEM((f, D), BF)]),
        out_shape=jax.ShapeDtypeStruct((n_slots, D), BF),
        compiler_params=_cparams(("arbitrary",), VMEM_LIMIT),
    )(tile_expert, n_active, n_valid, xs, wg, bg, wu, bu, wd, bd)


def _combine_blocks_body(tab_ref, x_ref, mod_ref, route_ref, gpos_ref, gf_ref, y_ref, o_ref, g_ref, sem, *, final, n_exp):
    @pl.when(pl.program_id(0) == 0)
    def _():
        g_ref[...] = jnp.zeros_like(g_ref)

    for e in range(n_exp):
        first_blk, n_blk, g_blk = tab_ref[0, e], tab_ref[0, n_exp + e], tab_ref[0, 2 * n_exp + e]

        def fetch(k, c):
            pltpu.make_async_copy(y_ref.at[pl.ds(first_blk + k, 1)], g_ref.at[pl.ds(g_blk + k, 1)], sem).start()
            return c
        lax.fori_loop(0, n_blk, fetch, 0)

    cap = g_ref.shape[0] * g_ref.shape[1]
    route, gpos = route_ref[...], gpos_ref[...]
    col = lax.broadcasted_iota(jnp.int32, (TM, cap), 1)
    wm = jnp.zeros((TM, cap), F32)
    for r in range(TOP_K):
        wm = wm + jnp.where(col == gpos[:, r:r + 1], route[:, TOP_K + r:TOP_K + r + 1], 0.0)
    wm = wm.astype(BF)

    def drain(k, c):
        pltpu.make_async_copy(y_ref.at[pl.ds(0, 1)], g_ref.at[pl.ds(0, 1)], sem).wait()
        return c
    lax.fori_loop(0, tab_ref[0, 3 * n_exp], drain, 0)
    acc = jnp.dot(wm, g_ref[...].reshape(cap, D), preferred_element_type=F32)
    xn = x_ref[...] + mod_ref[5:6, :] * acc
    if final:
        xn = xn * lax.rsqrt(jnp.mean(xn * xn, axis=-1, keepdims=True) + EPS) * gf_ref[...]
    o_ref[...] = xn


def _combine_blocks(tab, gpos, x_mid, mods, route, g_final, y, dims, final, n_exp):
    rows = x_mid.shape[0]
    tps, nb = dims["tps"], dims["nb"]
    cap_blocks = (TM * TOP_K + n_exp * 2 * ROW_BLK) // ROW_BLK
    return pl.pallas_call(
        functools.partial(_combine_blocks_body, final=final, n_exp=n_exp),
        grid=(rows // TM,),
        in_specs=[pl.BlockSpec((None, 1, LANES), lambda i: (i, 0, 0), memory_space=pltpu.SMEM),
                  pl.BlockSpec((TM, D), lambda i: (i, 0)),
                  pl.BlockSpec((None, 8, D), lambda i: (jnp.minimum(i // tps, nb), 0, 0)),
                  pl.BlockSpec((TM, LANES), lambda i: (i, 0)),
                  pl.BlockSpec((TM, TOP_K), lambda i: (i, 0)),
                  pl.BlockSpec((1, D), lambda i: (0, 0)),
                  pl.BlockSpec(memory_space=pl.ANY)],
        out_specs=pl.BlockSpec((TM, D), lambda i: (i, 0)),
        out_shape=jax.ShapeDtypeStruct((rows, D), F32),
        scratch_shapes=[pltpu.VMEM((cap_blocks, ROW_BLK, D), BF), pltpu.SemaphoreType.DMA(())],
        compiler_params=_cparams(("arbitrary",), VMEM_LIMIT),
    )(tab, x_mid, mods, route, gpos, g_final, y.reshape(-1, ROW_BLK, D))


def _moe(x_mid, h2, route, counts, mods, g_final, layer, wg, bg, wu, bu, wd, bd, dims, final):
    rows = x_mid.shape[0]
    n_exp = wg.shape[1]
    n_slots = rows * TOP_K + n_exp * TME
    e = route[:, 0:TOP_K].astype(jnp.int32)
    rank = route[:, 2 * TOP_K:3 * TOP_K].astype(jnp.int32)
    cnt = counts[0, :n_exp].astype(jnp.int32)
    cnt_pad = ((cnt + TME - 1) // TME) * TME
    offs_end = jnp.cumsum(cnt_pad)
    offs = offs_end - cnt_pad
    onehot = e[:, :, None] == jnp.arange(n_exp, dtype=jnp.int32)[None, None, :]
    dest = jnp.sum(jnp.where(onehot, offs[None, None, :], 0), axis=-1) + rank
    tmv = max(t for t in (TM_MOVE, TM) if dims["s"] % t == 0 and dims["t_lat"] % t == 0 and rows % t == 0)
    dest3 = dest.reshape(rows // tmv, 1, tmv * TOP_K)
    tile_start = jnp.arange(n_slots // TME, dtype=jnp.int32) * TME
    tile_expert = jnp.minimum(jnp.sum(tile_start[:, None] >= offs_end[None, :], axis=1), n_exp - 1).astype(jnp.int32)
    n_active = (offs_end[-1:] // TME).astype(jnp.int32)
    n_valid = jnp.clip(cnt[tile_expert] - (tile_start - offs[tile_expert]), 0, TME).astype(jnp.int32)
    xs = _dispatch(dest3, h2, n_slots)
    y = _experts(tile_expert, n_active, n_valid, xs, layer, wg, bg, wu, bu, wd, bd)

    nt = rows // TM
    per_tile = jnp.sum(onehot.reshape(nt, TM * TOP_K, n_exp), axis=1).astype(jnp.int32)
    start = offs[None, :] + jnp.cumsum(per_tile, axis=0) - per_tile
    first_blk = start // ROW_BLK
    n_blk = jnp.where(per_tile > 0, (start + per_tile + ROW_BLK - 1) // ROW_BLK - first_blk, 0)
    g_blk = jnp.cumsum(n_blk, axis=1) - n_blk
    tab = jnp.concatenate([first_blk, n_blk, g_blk, jnp.sum(n_blk, axis=1, keepdims=True),
                           jnp.zeros((nt, LANES - 3 * n_exp - 1), jnp.int32)], axis=1).reshape(nt, 1, LANES)
    shift = (g_blk - first_blk) * ROW_BLK
    gpos = dest + jnp.sum(jnp.where(onehot.reshape(nt, TM, TOP_K, n_exp), shift[:, None, None, :], 0),
                          axis=-1).reshape(rows, TOP_K)
    return _combine_blocks(tab, gpos, x_mid, mods, route, g_final, y, dims, final, n_exp)


def _rope_tables(s, ctx):
    rows = s // GRID_W
    row = np.repeat(np.arange(rows), GRID_W).astype(np.float32)
    col = np.tile(np.arange(GRID_W), rows).astype(np.float32)
    half = HEAD_DIM // 2
    inv = (np.float32(ROPE_BASE) ** (-np.arange(0, half, 2, dtype=np.float32) / half)).astype(np.float32)
    ar = row[:, None] * inv
    ac = col[:, None] * inv
    ang = np.concatenate([ar, ar, ac, ac], axis=-1)
    cos = np.concatenate([np.cos(ang), np.ones((ctx, HEAD_DIM), np.float32)], axis=0)
    sin = np.concatenate([np.sin(ang), np.zeros((ctx, HEAD_DIM), np.float32)], axis=0)
    return (jnp.asarray(np.tile(cos, (1, 2)), F32), jnp.asarray(np.tile(sin, (1, 2)), F32))


def _rot_columns(w):
    d, n = w.shape
    w4 = w.reshape(d, n // HEAD_DIM, 4, HEAD_DIM // 4)
    return jnp.stack([-w4[:, :, 1], w4[:, :, 0], -w4[:, :, 3], w4[:, :, 2]], axis=2).reshape(d, n)


def kernel(x, c, ctx, c_ctx, w_ada, b_ada, g_norm1, w_in, da_lambda, g_subln, w_conv, w_pool, pool_scale,
           w_branch, w_mgate, b_mgate, w_out, g_norm2, w_router, b_router, w_e_gate, b_e_gate, w_e_up,
           b_e_up, w_e_down, b_e_down, g_final):
    nb, s, _ = x.shape
    nctx = ctx.shape[1]
    depth = w_in.shape[0]
    n_exp = w_router.shape[2]
    t_lat, t_ctx = nb * s, nb * nctx
    dims = dict(nb=nb, s=s, ctx=nctx, t_lat=t_lat, t_all=t_lat + t_ctx, nlat=t_lat // TM, tps=s // TM, tpc=nctx // TM)
    assert s % TK == 0 and s // TK >= 2 and s % (FFT_N2 * 8) == 0 and nctx % TM == 0 and nb < 16
    assert TM == VT_CHUNK and nctx % VT_CHUNK == 0 and ATT_UNROLL % 2 == 0

    cos_t, sin_t = _rope_tables(s, nctx)
    tabs = _fft_tables(s, nctx)
    c16 = jnp.concatenate([c, c_ctx[None, :], jnp.zeros((15 - nb, D), F32)], axis=0)
    mods_all = _adaln(c16, w_ada, b_ada).reshape(depth, 16, 6, D)
    mods_all = jnp.concatenate([mods_all, jnp.zeros((depth, 16, 2, D), F32)], axis=2)

    xa, xb = x.reshape(t_lat, D), ctx.reshape(t_ctx, D)
    for l in range(depth):
        last = l == depth - 1
        lam_init = 0.8 - 0.6 * math.exp(-0.3 * l)
        mods = mods_all[l]
        w_l = w_in[l]
        w_ext = jnp.concatenate([w_l, _rot_columns(w_l[:, :BW]), _rot_columns(w_l[:, BW:2 * BW])], axis=1).astype(BF)
        qt, k, vt, u, sb, p, qq, px = _inproj(xa, xb, mods, g_norm1[l][None, :], w_ext, cos_t, sin_t, tabs["cs"], dims)

        rows = t_lat if last else t_lat + t_ctx
        a = _attention(qt, k, vt, g_subln[l], da_lambda[l], lam_init, dims, latent=True, rows=rows)
        yf = _fourier_latent(p, qq, tabs, dims, rows)
        if not last:
            a = _attention(qt, k, vt, g_subln[l], da_lambda[l], lam_init, dims, latent=False, fill=a)
            yf = _fourier_ctx(p, qq, tabs, dims, yf)

        wconv = jnp.concatenate([w_conv[l], jnp.zeros((5, BW), F32)], axis=0)
        wr = jnp.concatenate([w_router[l], jnp.zeros((D, LANES - n_exp), F32)], axis=1).astype(BF)
        br = jnp.concatenate([b_router[l], jnp.full((LANES - n_exp,), -1e30, F32)])[None, :]
        x_mid, h2, route, counts = _merge(
            xa, xb, mods, g_norm1[l][None, :], a, u, sb, yf, px, wconv, w_pool[l].astype(BF), pool_scale[l][None, :],
            w_branch[l].astype(BF), w_mgate[l].astype(BF), b_mgate[l][None, :], w_out[l].astype(BF),
            g_norm2[l][None, :], wr, br, dims, rows)
        x_new = _moe(x_mid, h2, route, counts, mods, g_final[None, :], l,
                     w_e_gate, b_e_gate[:, :, None, :], w_e_up, b_e_up[:, :, None, :],
                     w_e_down, b_e_down[:, :, None, :], dims, final=last)
        if last:
            return x_new.reshape(nb, s, D)
        xa = xb = x_new
```

```python
import functools
import math

import numpy as np
import jax
import jax.numpy as jnp
from jax import lax
from jax.experimental import pallas as pl
from jax.experimental.pallas import tpu as pltpu

F32 = jnp.float32
BF = jnp.bfloat16

D = 1024
HEADS = 4
HEAD_DIM = 64
HEAD_W = 128
BW = 512
GRID_W = 64
EPS = 1e-6
ROPE_BASE = 10000.0
POOL_WINDOWS = (2, 4, 8, 16)
TOP_K = 4
SWIGLU_LIMIT = 7.0
SWIGLU_ALPHA = 1.702
FFT_N2 = 128

TM = 256
TME = 512
TM_MOVE = 1024
TQ = 512
TK = 512
ATT_UNROLL = 2
VT_CHUNK = 256
HALO = 16
LANES = 128
COND_ROWS = 16
MOD_ROWS = 8
ROW_BLK = 16
QSCALE = (HEAD_DIM ** -0.5) * math.log2(math.e)
EXP_HEADROOM = 64.0
NORM_MARGIN = 1.05
VMEM_LIMIT = 56 * 2 ** 20


def _cparams(sem, vmem=None):
    return pltpu.CompilerParams(dimension_semantics=sem, vmem_limit_bytes=vmem)


def _modnorm(x, g, shift, scale):
    y = x * lax.rsqrt(jnp.mean(x * x, axis=-1, keepdims=True) + EPS) * g
    return y * (1.0 + scale) + shift


def _sigmoid(x):
    return 1.0 / (1.0 + jnp.exp(-x))


def _adaln_body(c_ref, w_ref, b_ref, o_ref):
    c = c_ref[...]
    s = c * _sigmoid(c)
    o_ref[...] = jnp.dot(s.astype(BF), w_ref[...].astype(BF), preferred_element_type=F32) + b_ref[...]


def _adaln(c16, w_ada, b_ada):
    nl = w_ada.shape[0]
    tn = 1536
    return pl.pallas_call(
        _adaln_body,
        grid=(nl, 6 * D // tn),
        in_specs=[pl.BlockSpec((COND_ROWS, D), lambda l, j: (0, 0)),
                  pl.BlockSpec((None, D, tn), lambda l, j: (l, 0, j)),
                  pl.BlockSpec((None, 1, tn), lambda l, j: (l, 0, j))],
        out_specs=pl.BlockSpec((None, COND_ROWS, tn), lambda l, j: (l, 0, j)),
        out_shape=jax.ShapeDtypeStruct((nl, COND_ROWS, 6 * D), F32),
        compiler_params=_cparams(("arbitrary", "arbitrary"), VMEM_LIMIT),
    )(c16, w_ada, b_ada.reshape(nl, 1, 6 * D))


def _token_tile(xa_ref, xb_ref, n_a):
    return jnp.where(pl.program_id(0) < n_a, xa_ref[...], xb_ref[...])


def _token_specs(xa, xb):
    n_a, n_b = xa.shape[0] // TM, xb.shape[0] // TM
    return [pl.BlockSpec((TM, D), lambda i: (jnp.minimum(i, n_a - 1), 0)),
            pl.BlockSpec((TM, D), lambda i: (jnp.clip(i - n_a, 0, n_b - 1), 0))]


def _inproj_body(xa_ref, xb_ref, mod_ref, g_ref, w_ref, cos_ref, sin_ref,
                 q_ref, k_ref, v_ref, u_ref, sb_ref, fx_ref, px_ref, *, n_a):
    h = _modnorm(_token_tile(xa_ref, xb_ref, n_a), g_ref[...], mod_ref[0:1, :], mod_ref[1:2, :]).astype(BF)

    def proj(c0):
        return jnp.dot(h, w_ref[:, c0:c0 + BW], preferred_element_type=F32)

    cos = jnp.concatenate([cos_ref[...]] * 4, axis=1)
    sin = jnp.concatenate([sin_ref[...]] * 4, axis=1)
    q_ref[...] = ((proj(0) * cos + proj(8 * BW) * sin) * QSCALE).T.astype(BF)
    k_ref[...] = (proj(BW) * cos + proj(9 * BW) * sin).astype(BF)
    v_ref[...] = proj(2 * BW).T.astype(BF)
    u_ref[...] = (proj(5 * BW) * proj(3 * BW)).astype(BF)
    sb_ref[...] = proj(4 * BW).astype(BF)
    px_ref[...] = proj(7 * BW).astype(BF)
    fx_ref[...] = proj(6 * BW).astype(BF)


def _inproj(xa, xb, mods, g1, w_ext, cos_t, sin_t, dims):
    t_all, nlat, tps, tpc, nb = dims["t_all"], dims["nlat"], dims["tps"], dims["tpc"], dims["nb"]

    def tab_idx(i):
        return (jnp.where(i < nlat, i % tps, tps + (i - nlat) % tpc), 0)

    out = jax.ShapeDtypeStruct((t_all, BW), BF)
    row = pl.BlockSpec((TM, BW), lambda i: (i, 0))
    return pl.pallas_call(
        functools.partial(_inproj_body, n_a=xa.shape[0] // TM),
        grid=(t_all // TM,),
        in_specs=_token_specs(xa, xb) + [
                  pl.BlockSpec((None, MOD_ROWS, D), lambda i: (jnp.minimum(i // tps, nb), 0, 0)),
                  pl.BlockSpec((1, D), lambda i: (0, 0)),
                  pl.BlockSpec((D, 10 * BW), lambda i: (0, 0)),
                  pl.BlockSpec((TM, LANES), tab_idx),
                  pl.BlockSpec((TM, LANES), tab_idx)],
        out_specs=[pl.BlockSpec((BW, TM), lambda i: (0, i)), row,
                   pl.BlockSpec((None, BW, VT_CHUNK), lambda i: (i, 0, 0))] + [row] * 4,
        out_shape=[jax.ShapeDtypeStruct((BW, t_all), BF), out,
                   jax.ShapeDtypeStruct((t_all // VT_CHUNK, BW, VT_CHUNK), BF)] + [out] * 4,
        compiler_params=_cparams(("arbitrary",), VMEM_LIMIT),
    )(xa, xb, mods, g1, w_ext, cos_t, sin_t)


def _attn_body(*refs, n_ctx_chunks, n_lat_chunks, lam_init):
    if n_lat_chunks:
        qt_ref, kc_ref, vtc_ref, kl_ref, vtl_ref, g_ref, dl_ref, o_ref, acc_ref, s_buf, p_buf, l_ref, knorm_ref = refs
    else:
        qt_ref, kc_ref, vtc_ref, g_ref, dl_ref, _, o_ref, acc_ref = refs
    qt = qt_ref[...]
    tq = qt.shape[1]
    row = lax.broadcasted_iota(jnp.int32, qt.shape, 0)
    zero = jnp.zeros_like(qt)
    q2 = jnp.concatenate([jnp.where(row < HEAD_DIM, qt, zero), jnp.where(row >= HEAD_DIM, qt, zero)], axis=1)

    def scores(kc):
        s = jnp.dot(kc, q2, preferred_element_type=F32)
        return s, jnp.max(s, axis=0, keepdims=True)

    def softmax(s, smax, m, l):
        m_new = jnp.maximum(m, smax)
        p = jnp.exp2(s - m_new)
        alpha = jnp.exp2(m - m_new)
        return p.astype(BF), m_new, alpha * l + jnp.sum(p, axis=0, keepdims=True), alpha

    def accumulate(alpha, vt, p):
        acc_ref[...] = alpha * acc_ref[...] + jnp.dot(vt, p, preferred_element_type=F32)

    m = jnp.full((1, 2 * tq), -1e30, F32)
    l = jnp.zeros((1, 2 * tq), F32)
    acc_ref[...] = jnp.zeros_like(acc_ref)
    ctx_scores = [scores(kc_ref[c * VT_CHUNK:(c + 1) * VT_CHUNK, :]) for c in range(n_ctx_chunks)]
    if n_lat_chunks:
        per = TK // VT_CHUNK
        n = n_lat_chunks

        def keys(j):
            return kl_ref[pl.ds(pl.multiple_of(j * TK, TK), TK), :]

        s_buf[0] = jnp.dot(keys(0), q2, preferred_element_type=F32)
    for c in range(n_ctx_chunks):
        p, m, l, alpha = softmax(*ctx_scores[c], m, l)
        accumulate(alpha, vtc_ref[c], p)
    if n_lat_chunks:

        def values(j):
            parts = [vtl_ref[j * per + r] for r in range(per)]
            return parts[0] if per == 1 else jnp.concatenate(parts, axis=1)

        @pl.when(pl.program_id(2) == 0)
        def _():
            d_i = lax.broadcasted_iota(jnp.int32, (HEAD_W, LANES), 0)
            c_i = lax.broadcasted_iota(jnp.int32, (HEAD_W, LANES), 1)
            sel = jnp.where((d_i >= HEAD_DIM) == (c_i == 1), 1.0, 0.0) * jnp.where(c_i < 2, 1.0, 0.0)
            best = jnp.zeros((1, LANES), F32)
            for j in range(n):
                kf = keys(j).astype(F32)
                n2 = jnp.dot((kf * kf).astype(BF), sel.astype(BF), preferred_element_type=F32)
                best = jnp.maximum(best, jnp.max(n2, axis=0, keepdims=True))
            lane2 = lax.broadcasted_iota(jnp.int32, (1, 2 * tq), 1)
            knorm_ref[...] = jnp.sqrt(jnp.where(lane2 < tq, best[:, 0:1], best[:, 1:2]))

        qf = q2.astype(F32)
        bound = jnp.sqrt(jnp.sum(qf * qf, axis=0, keepdims=True)) * knorm_ref[...] * NORM_MARGIN
        gap = jnp.max(bound - m)
        l_ref[...] = l

        @pl.when(gap <= EXP_HEADROOM)
        def _():
            lsum = l
            s = s_buf[0]
            for j in range(n):
                s_next = jnp.dot(keys(j + 1), q2, preferred_element_type=F32) if j + 1 < n else None
                p = jnp.exp2(s - m)
                lsum = lsum + jnp.sum(p, axis=0, keepdims=True)
                acc_ref[...] += jnp.dot(values(j), p.astype(BF), preferred_element_type=F32)
                s = s_next
            l_ref[...] = lsum

        @pl.when(gap > EXP_HEADROOM)
        def _():
            _attn_exact_latent(m, l, n, keys, values, scores, softmax, accumulate, s_buf, p_buf, l_ref)

        l = l_ref[...]
    o = acc_ref[...] / l
    dl = dl_ref[...]
    lam = (jnp.exp(jnp.sum(dl[0:1] * dl[1:2], axis=1, keepdims=True))
           - jnp.exp(jnp.sum(dl[2:3] * dl[3:4], axis=1, keepdims=True)) + lam_init)
    d = o[:, :tq] - lam * o[:, tq:]
    y = d * lax.rsqrt(jnp.mean(d * d, axis=0, keepdims=True) + EPS) * g_ref[...] * (1.0 - lam_init)
    o_ref[...] = y.T.astype(BF)


def _attn_exact_latent(m, l, n, keys, values, scores, softmax, accumulate, s_buf, p_buf, l_ref):
    def half_step(j, prev, cur, carry):
        m, l, alpha_prev, smax = carry
        accumulate(alpha_prev, values(j - 1), p_buf[prev])
        if isinstance(j, int) and j + 1 >= n:
            smax_next = smax
        else:
            s_buf[prev], smax_next = scores(keys(jnp.minimum(j + 1, n - 1)))
        p, m, l, alpha = softmax(s_buf[cur], smax, m, l)
        p_buf[cur] = p
        return m, l, alpha, smax_next

    s0 = s_buf[0]
    p, m, l, alpha = softmax(s0, jnp.max(s0, axis=0, keepdims=True), m, l)
    p_buf[0] = p
    s_buf[1], smax = scores(keys(1))

    def body(i, carry):
        for r in range(ATT_UNROLL):
            carry = half_step(ATT_UNROLL * i + r + 1, r % 2, 1 - r % 2, carry)
        return carry

    n_body = (n - 1) // ATT_UNROLL
    carry = lax.fori_loop(0, n_body, body, (m, l, alpha, smax))
    for j in range(n_body * ATT_UNROLL + 1, n):
        carry = half_step(j, (j - 1) % 2, j % 2, carry)
    m, l, alpha, _ = carry
    accumulate(alpha, values(n - 1), p_buf[(n - 1) % 2])
    l_ref[...] = l


def _attention(qt, k, vt, g_sub, da_lambda, lam_init, dims, latent, rows=None, fill=None):
    nb, s, ctx, t_lat = dims["nb"], dims["s"], dims["ctx"], dims["t_lat"]
    cpb = ctx // VT_CHUNK
    tq = TQ if latent else min(TQ, ctx)
    g_sub_b = jnp.broadcast_to(g_sub[:, None], (HEAD_W, tq))
    common_tail = [pl.BlockSpec((HEAD_W, tq), lambda b, h, qi: (0, 0)),
                   pl.BlockSpec((4, HEAD_DIM), lambda b, h, qi: (0, 0))]
    kc_spec = pl.BlockSpec((ctx, HEAD_W), lambda b, h, qi: (t_lat // ctx + b, h))
    vtc_spec = pl.BlockSpec((cpb, HEAD_W, VT_CHUNK), lambda b, h, qi: (t_lat // ctx + b, h, 0))
    scratch = [pltpu.VMEM((HEAD_W, 2 * tq), F32)]
    if latent:
        nq = s // tq
        in_specs = [pl.BlockSpec((HEAD_W, tq), lambda b, h, qi: (h, b * nq + qi)), kc_spec, vtc_spec,
                    pl.BlockSpec((s, HEAD_W), lambda b, h, qi: (b, h)),
                    pl.BlockSpec((s // VT_CHUNK, HEAD_W, VT_CHUNK), lambda b, h, qi: (b, h, 0))] + common_tail
        args = (qt, k, vt, k, vt, g_sub_b, da_lambda)
        row0, n_lat_chunks, aliases = 0, s // TK, {}
        scratch += [pltpu.VMEM((2, TK, 2 * tq), F32), pltpu.VMEM((2, TK, 2 * tq), BF),
                    pltpu.VMEM((1, 2 * tq), F32), pltpu.VMEM((1, 2 * tq), F32)]
    else:
        nq = ctx // tq
        in_specs = [pl.BlockSpec((HEAD_W, tq), lambda b, h, qi: (h, t_lat // tq + b * nq + qi)),
                    kc_spec, vtc_spec] + common_tail
        in_specs.append(pl.BlockSpec(memory_space=pl.ANY))
        args = (qt, k, vt, g_sub_b, da_lambda, fill)
        row0, n_lat_chunks, rows, aliases = t_lat // tq, 0, fill.shape[0], {5: 0}
    return pl.pallas_call(
        functools.partial(_attn_body, n_ctx_chunks=cpb, n_lat_chunks=n_lat_chunks, lam_init=lam_init),
        grid=(nb, HEADS, nq),
        in_specs=in_specs,
        out_specs=pl.BlockSpec((tq, HEAD_W), lambda b, h, qi: (row0 + b * nq + qi, h)),
        out_shape=jax.ShapeDtypeStruct((rows, HEADS * HEAD_W), BF),
        input_output_aliases=aliases,
        scratch_shapes=scratch,
        compiler_params=_cparams(("arbitrary", "arbitrary", "arbitrary"), VMEM_LIMIT),
    )(*args)


def _channel_dft(x, cs):
    parts = [jnp.dot(x[:, g * LANES:(g + 1) * LANES], cs, preferred_element_type=F32).astype(BF)
             for g in range(x.shape[1] // LANES)]
    return jnp.concatenate([jnp.concatenate([pq[:, :LANES] for pq in parts], axis=1),
                            jnp.concatenate([pq[:, LANES:] for pq in parts], axis=1)], axis=0)


def _fft_a_body(fa_ref, cs_ref, x_ref, zr_ref, zi_ref):
    x = _channel_dft(x_ref[...], cs_ref[...])
    z = jnp.dot(fa_ref[...], x, preferred_element_type=F32)
    zr_ref[...] = z[:FFT_N2].astype(BF)
    zi_ref[...] = z[FFT_N2:].astype(BF)


def _fft_b_body(fb_ref, tc_ref, ts_ref, zr_ref, zi_ref, o_ref):
    nk = zr_ref.shape[0]
    parts_r, parts_i = [], []
    for j in range(nk):
        zr = zr_ref[j].astype(F32)
        zi = zi_ref[j].astype(F32)
        tc = jnp.concatenate([tc_ref[j]] * 4, axis=1)
        ts = jnp.concatenate([ts_ref[j]] * 4, axis=1)
        parts_r.append((zr * tc - zi * ts).astype(BF))
        parts_i.append((zr * ts + zi * tc).astype(BF))
    rhs = jnp.concatenate([jnp.concatenate(parts_r, axis=1), jnp.concatenate(parts_i, axis=1)], axis=0)
    o_ref[...] = jnp.dot(fb_ref[...], rhs, preferred_element_type=F32).astype(BF)


def _fft_ctx_body(f_ref, cs_ref, x_ref, _, o_ref):
    x = _channel_dft(x_ref[...], cs_ref[...])
    o_ref[...] = jnp.dot(f_ref[...], x, preferred_element_type=F32).astype(BF)


def _fft_tables(s, ctx):
    n1, n2 = s // FFT_N2, FFT_N2
    norm = 1.0 / math.sqrt(s * LANES)
    a = 2 * np.pi * np.outer(np.arange(n2), np.arange(n2)) / n2
    c, sn = np.cos(a), np.sin(a)
    fa = np.block([[c, -sn], [sn, c]]) * norm
    tw = 2 * np.pi * np.outer(np.arange(n2), np.arange(n1)) / s
    tc = np.repeat(np.cos(tw)[:, :, None], LANES, axis=2)
    ts = np.repeat(np.sin(tw)[:, :, None], LANES, axis=2)
    b = 2 * np.pi * np.outer(np.arange(n1), np.arange(n1)) / n1
    fb = np.concatenate([np.cos(b), -np.sin(b)], axis=1)
    ch = 2 * np.pi * np.outer(np.arange(LANES), np.arange(LANES)) / LANES
    cs = np.concatenate([np.cos(ch), np.sin(ch)], axis=1)
    cx = 2 * np.pi * np.outer(np.arange(ctx), np.arange(ctx)) / ctx
    fctx = np.concatenate([np.cos(cx), -np.sin(cx)], axis=1) / math.sqrt(ctx * LANES)
    return dict(fa=jnp.asarray(fa, BF), tc=jnp.asarray(tc, F32), ts=jnp.asarray(ts, F32),
                fb=jnp.asarray(fb, BF), cs=jnp.asarray(cs, BF), fctx=jnp.asarray(fctx, BF))


def _fourier_latent(fx, tabs, dims, rows):
    nb, s = dims["nb"], dims["s"]
    n1 = s // FFT_N2
    wide = n1 * BW
    wc = min(4096, wide)
    fx2 = fx.reshape(-1, wide)
    zshape = jax.ShapeDtypeStruct((nb * FFT_N2, wide), BF)
    zr, zi = pl.pallas_call(
        _fft_a_body,
        grid=(nb, wide // wc),
        in_specs=[pl.BlockSpec((2 * FFT_N2, 2 * FFT_N2), lambda b, j: (0, 0)),
                  pl.BlockSpec((LANES, 2 * LANES), lambda b, j: (0, 0)),
                  pl.BlockSpec((FFT_N2, wc), lambda b, j: (b, j))],
        out_specs=[pl.BlockSpec((FFT_N2, wc), lambda b, j: (b, j))] * 2,
        out_shape=[zshape, zshape],
        compiler_params=_cparams(("arbitrary", "arbitrary"), VMEM_LIMIT),
    )(tabs["fa"], tabs["cs"], fx2)
    kb = 8
    zr4 = zr.reshape(nb, FFT_N2, n1, BW)
    zi4 = zi.reshape(nb, FFT_N2, n1, BW)
    y = pl.pallas_call(
        _fft_b_body,
        grid=(FFT_N2 // kb, nb),
        in_specs=[pl.BlockSpec((n1, 2 * n1), lambda kk, b: (0, 0)),
                  pl.BlockSpec((kb, n1, LANES), lambda kk, b: (kk, 0, 0)),
                  pl.BlockSpec((kb, n1, LANES), lambda kk, b: (kk, 0, 0)),
                  pl.BlockSpec((None, kb, n1, BW), lambda kk, b: (b, kk, 0, 0)),
                  pl.BlockSpec((None, kb, n1, BW), lambda kk, b: (b, kk, 0, 0))],
        out_specs=pl.BlockSpec((n1, kb * BW), lambda kk, b: (b, kk)),
        out_shape=jax.ShapeDtypeStruct((rows // FFT_N2, FFT_N2 * BW), BF),
        compiler_params=_cparams(("arbitrary", "arbitrary"), VMEM_LIMIT),
    )(tabs["fb"], tabs["tc"], tabs["ts"], zr4, zi4)
    return y.reshape(rows, BW)


def _fourier_ctx(fx, tabs, dims, fill):
    nb, ctx, t_lat = dims["nb"], dims["ctx"], dims["t_lat"]
    return pl.pallas_call(
        _fft_ctx_body,
        grid=(nb,),
        in_specs=[pl.BlockSpec((ctx, 2 * ctx), lambda b: (0, 0)),
                  pl.BlockSpec((LANES, 2 * LANES), lambda b: (0, 0)),
                  pl.BlockSpec((ctx, BW), lambda b: (t_lat // ctx + b, 0)),
                  pl.BlockSpec(memory_space=pl.ANY)],
        out_specs=pl.BlockSpec((ctx, BW), lambda b: (t_lat // ctx + b, 0)),
        out_shape=jax.ShapeDtypeStruct(fill.shape, BF),
        input_output_aliases={3: 0},
        compiler_params=_cparams(("arbitrary",), VMEM_LIMIT),
    )(tabs["fctx"], tabs["cs"], fx, fill)


def _route_topk(h2b, wr_ref, br_ref):
    logits = jnp.dot(h2b, wr_ref[...], preferred_element_type=F32) + br_ref[...]
    lane = lax.broadcasted_iota(jnp.int32, (TM, LANES), 1).astype(F32)
    cur = logits
    vals, idxs = [], []
    for _ in range(TOP_K):
        mx = jnp.max(cur, axis=1, keepdims=True)
        idx = jnp.min(jnp.where(cur == mx, lane, float(LANES)), axis=1, keepdims=True)
        vals.append(mx)
        idxs.append(idx)
        cur = jnp.where(lane == idx, -jnp.inf, cur)
    ex = [jnp.exp(v - vals[0]) for v in vals]
    den = ex[0] + ex[1] + ex[2] + ex[3]
    member = jnp.zeros((TM, LANES), F32)
    for idx in idxs:
        member = member + jnp.where(lane == idx, 1.0, 0.0)
    return idxs, [e / den for e in ex], member


def _route_ranks(idxs, weights, member, carry_ref, live):
    lane = lax.broadcasted_iota(jnp.int32, (TM, LANES), 1).astype(F32)
    tri = (lax.broadcasted_iota(jnp.int32, (TM, TM), 0) > lax.broadcasted_iota(jnp.int32, (TM, TM), 1))
    before = jnp.dot(jnp.where(tri, 1.0, 0.0).astype(BF), member.astype(BF), preferred_element_type=F32)
    rank_all = carry_ref[0:1, :] + before
    route = jnp.zeros((TM, LANES), F32)
    for r in range(TOP_K):
        rank = jnp.sum(jnp.where(lane == idxs[r], rank_all, 0.0), axis=1, keepdims=True)
        route = (route + jnp.where(lane == float(r), idxs[r], 0.0)
                 + jnp.where(lane == float(TOP_K + r), weights[r], 0.0)
                 + jnp.where(lane == float(2 * TOP_K + r), rank, 0.0))
    carry_ref[...] = carry_ref[...] + live * jnp.sum(member, axis=0, keepdims=True)
    return route


def _merge_body(xa_ref, xb_ref, mod_ref, g1_ref, a_ref, u_ref, up_ref, un_ref, sb_ref, yf_ref, px_ref, pp_ref, pn_ref,
                wconv_ref, wpool_ref, pscale_ref, wb_ref, wg_ref, bg_ref, wo_ref, g2_ref, wr_ref, br_ref,
                xo_ref, h2_ref, route_ref, cnt_ref, carry_ref, h2p_ref, *, n_tiles, nlat, tps, tpc, s, ctx, n_a):
    step = pl.program_id(0)
    i = jnp.minimum(step, n_tiles - 1)
    is_lat = i < nlat
    j = jnp.where(is_lat, i % tps, (i - nlat) % tpc)
    first = j == 0
    last = j == jnp.where(is_lat, tps, tpc) - 1
    seq_len = jnp.where(is_lat, s, ctx)

    @pl.when(step == 0)
    def _():
        carry_ref[...] = jnp.zeros_like(carry_ref)
        h2p_ref[...] = jnp.zeros_like(h2p_ref)

    topk_prev = _route_topk(h2p_ref[...], wr_ref, br_ref)

    x = jnp.where(i < n_a, xa_ref[...], xb_ref[...])
    h = _modnorm(x, g1_ref[...], mod_ref[0:1, :], mod_ref[1:2, :]).astype(BF)
    rowi = lax.broadcasted_iota(jnp.int32, (TM, 1), 0)

    u = u_ref[...].astype(F32)
    u_prev = jnp.where(first, 0.0, up_ref[...].astype(F32)[HALO - 1:HALO, :])
    u_next = jnp.where(last, 0.0, un_ref[...].astype(F32)[0:1, :])
    u_dn = jnp.where(rowi == 0, u_prev, pltpu.roll(u, 1, axis=0))
    u_up = jnp.where(rowi == TM - 1, u_next, pltpu.roll(u, TM - 1, axis=0))
    wc = wconv_ref[...]
    y_conv = sb_ref[...].astype(F32) * (wc[0:1] * u_dn + wc[1:2] * u + wc[2:3] * u_up)

    px = px_ref[...].astype(F32)
    ext = jnp.concatenate([jnp.where(first, 0.0, pp_ref[...].astype(F32)), px,
                           jnp.where(last, 0.0, pn_ref[...].astype(F32))], axis=0)
    ext_len = TM + 2 * HALO
    pos = j * TM + rowi
    pooled = []
    for g, wd in enumerate(POOL_WINDOWS):
        e = ext[:, g * LANES:(g + 1) * LANES]
        win = e + pltpu.roll(e, 1, axis=0)
        half = 1
        while 2 * half < wd:
            win = pltpu.roll(win, half, axis=0) + pltpu.roll(win, ext_len - half, axis=0)
            half *= 2
        win = win[HALO:HALO + TM]
        cnt = (jnp.minimum(pos + wd // 2, seq_len) - jnp.maximum(pos - wd // 2, 0)).astype(F32)
        pooled.append((win / cnt - px[:, g * LANES:(g + 1) * LANES]).astype(BF))

    gates = [_sigmoid(jnp.dot(h, wg_ref[:, kk * D:(kk + 1) * D], preferred_element_type=F32)
                      + bg_ref[:, kk * D:(kk + 1) * D]) for kk in range(4)]

    def branch(kk, y):
        return gates[kk] * jnp.dot(y, wb_ref[kk], preferred_element_type=F32)

    acc = branch(0, a_ref[...]) + branch(2, yf_ref[...]) + branch(1, y_conv.astype(BF))
    y_pool = jnp.concatenate([jnp.dot(pooled[g], wpool_ref[g], preferred_element_type=F32)
                              for g in range(len(POOL_WINDOWS))], axis=1) * pscale_ref[...]
    acc = acc + branch(3, y_pool.astype(BF))

    xn = x + mod_ref[2:3, :] * jnp.dot(acc.astype(BF), wo_ref[...], preferred_element_type=F32)
    xo_ref[...] = xn
    h2 = _modnorm(xn, g2_ref[...], mod_ref[3:4, :], mod_ref[4:5, :])
    h2_ref[...] = h2
    h2p_ref[...] = h2.astype(BF)
    route_ref[...] = _route_ranks(*topk_prev, carry_ref, jnp.where(step > 0, 1.0, 0.0))
    cnt_ref[...] = carry_ref[...]


def _merge(xa, xb, mods, g1, a, u, sb, yf, px, wconv, wpool, pscale, wb, wg, bg, wo, g2, wr, br, dims, rows):
    nlat, tps, tpc, nb = dims["nlat"], dims["tps"], dims["tpc"], dims["nb"]
    n_tiles = rows // TM
    n_a, n_b = xa.shape[0] // TM, xb.shape[0] // TM
    nhalo = TM // HALO
    last_halo = u.shape[0] // HALO - 1
    cur = lambda i: jnp.minimum(i, n_tiles - 1)
    tile = lambda w: pl.BlockSpec((TM, w), lambda i: (cur(i), 0))
    prev = pl.BlockSpec((HALO, BW), lambda i: (jnp.maximum(cur(i) * nhalo - 1, 0), 0))
    nxt = pl.BlockSpec((HALO, BW), lambda i: (jnp.minimum((cur(i) + 1) * nhalo, last_halo), 0))
    const = lambda shape: pl.BlockSpec(shape, lambda i: (0,) * len(shape))
    return pl.pallas_call(
        functools.partial(_merge_body, n_tiles=n_tiles, nlat=nlat, tps=tps, tpc=tpc, s=dims["s"], ctx=dims["ctx"],
                          n_a=n_a),
        grid=(n_tiles + 1,),
        in_specs=[pl.BlockSpec((TM, D), lambda i: (jnp.minimum(cur(i), n_a - 1), 0)),
                  pl.BlockSpec((TM, D), lambda i: (jnp.clip(cur(i) - n_a, 0, n_b - 1), 0)),
                  pl.BlockSpec((None, MOD_ROWS, D), lambda i: (jnp.minimum(cur(i) // tps, nb), 0, 0)),
                  const((1, D)), tile(BW), tile(BW), prev, nxt, tile(BW), tile(BW), tile(BW), prev, nxt,
                  const((8, BW)), const((4, LANES, LANES)), const((1, BW)), const((4, BW, D)),
                  const((D, 4 * D)), const((1, 4 * D)), const((D, D)), const((1, D)),
                  const((D, LANES)), const((1, LANES))],
        out_specs=[tile(D), tile(D), pl.BlockSpec((TM, LANES), lambda i: (jnp.maximum(i - 1, 0), 0)),
                   pl.BlockSpec((8, LANES), lambda i: (0, 0))],
        out_shape=[jax.ShapeDtypeStruct((rows, D), F32), jax.ShapeDtypeStruct((rows, D), F32),
                   jax.ShapeDtypeStruct((rows, LANES), F32), jax.ShapeDtypeStruct((8, LANES), F32)],
        scratch_shapes=[pltpu.VMEM((8, LANES), F32), pltpu.VMEM((TM, D), BF)],
        compiler_params=_cparams(("arbitrary",), VMEM_LIMIT),
    )(xa, xb, mods, g1, a, u, u, u, sb, yf, px, px, px, wconv, wpool, pscale, wb, wg, bg, wo, g2, wr, br)


def _wait_row_copies(src_ref, dst_ref, sem, n):
    for _ in range(n):
        pltpu.make_async_copy(src_ref.at[pl.ds(0, 1)], dst_ref.at[pl.ds(0, 1)], sem).wait()


def _dispatch_body(dest_ref, h_ref, xs_ref, sem):
    def body(t, c):
        for r in range(TOP_K):
            pltpu.make_async_copy(h_ref.at[pl.ds(t, 1)], xs_ref.at[pl.ds(dest_ref[0, t * TOP_K + r], 1)], sem).start()
        return c
    lax.fori_loop(0, h_ref.shape[0], body, 0, unroll=8)
    _wait_row_copies(h_ref, xs_ref, sem, h_ref.shape[0] * TOP_K)


def _dispatch(dest3, h2, n_slots):
    rows = h2.shape[0]
    tmv = dest3.shape[2] // TOP_K
    return pl.pallas_call(
        _dispatch_body,
        grid=(rows // tmv,),
        in_specs=[pl.BlockSpec((None, 1, tmv * TOP_K), lambda i: (i, 0, 0), memory_space=pltpu.SMEM),
                  pl.BlockSpec((tmv, D), lambda i: (i, 0))],
        out_specs=pl.BlockSpec(memory_space=pl.ANY),
        out_shape=jax.ShapeDtypeStruct((n_slots, D), F32),
        scratch_shapes=[pltpu.SemaphoreType.DMA(())],
        compiler_params=_cparams(("arbitrary",), VMEM_LIMIT),
    )(dest3, h2)


def _expert_body(te_ref, na_ref, nv_ref, xs_ref, wg_ref, bg_ref, wu_ref, bu_ref, wd_ref, bd_ref, y_ref, wgb, wub, wdb):
    j = pl.program_id(0)

    @pl.when(j < na_ref[0])
    def _():
        @pl.when((j == 0) | (te_ref[j] != te_ref[jnp.maximum(j - 1, 0)]))
        def _():
            wgb[...] = wg_ref[...].astype(BF)
            wub[...] = wu_ref[...].astype(BF)
            wdb[...] = wd_ref[...].astype(BF)

        x = xs_ref[...].astype(BF)
        a = jnp.minimum(jnp.dot(x, wgb[...], preferred_element_type=F32) + bg_ref[...], SWIGLU_LIMIT)
        u = jnp.clip(jnp.dot(x, wub[...], preferred_element_type=F32) + bu_ref[...], -SWIGLU_LIMIT, SWIGLU_LIMIT)
        act = a * _sigmoid(SWIGLU_ALPHA * a) * (u + 1.0)
        y = jnp.dot(act.astype(BF), wdb[...], preferred_element_type=F32) + bd_ref[...]
        rowi = lax.broadcasted_iota(jnp.int32, (TME, 1), 0)
        y_ref[...] = jnp.where(rowi < nv_ref[j], y, 0.0).astype(BF)


def _experts(tile_expert, n_active, n_valid, xs, layer, wg, bg, wu, bu, wd, bd):
    n_slots = xs.shape[0]
    f = wg.shape[3]
    slot = lambda j, te, na, nv: (jnp.minimum(j, na[0] - 1), 0)
    wsel = lambda j, te, na, nv: (layer, te[j], 0, 0)
    return pl.pallas_call(
        _expert_body,
        grid_spec=pltpu.PrefetchScalarGridSpec(
            num_scalar_prefetch=3,
            grid=(n_slots // TME,),
            in_specs=[pl.BlockSpec((TME, D), slot),
                      pl.BlockSpec((None, None, D, f), wsel), pl.BlockSpec((None, None, 1, f), wsel),
                      pl.BlockSpec((None, None, D, f), wsel), pl.BlockSpec((None, None, 1, f), wsel),
                      pl.BlockSpec((None, None, f, D), wsel), pl.BlockSpec((None, None, 1, D), wsel)],
            out_specs=pl.BlockSpec((TME, D), slot),
            scratch_shapes=[pltpu.VMEM((D, f), BF), pltpu.VMEM((D, f), BF), pltpu.VMEM((f, D), BF)]),
        out_shape=jax.ShapeDtypeStruct((n_slots, D), BF),
        compiler_params=_cparams(("arbitrary",), VMEM_LIMIT),
    )(tile_expert, n_active, n_valid, xs, wg, bg, wu, bu, wd, bd)


def _combine_blocks_body(tab_ref, x_ref, mod_ref, route_ref, gpos_ref, gf_ref, y_ref, o_ref, g_ref, sem, *, final, n_exp):
    @pl.when(pl.program_id(0) == 0)
    def _():
        g_ref[...] = jnp.zeros_like(g_ref)

    for e in range(n_exp):
        first_blk, n_blk, g_blk = tab_ref[0, e], tab_ref[0, n_exp + e], tab_ref[0, 2 * n_exp + e]

        def fetch(k, c):
            pltpu.make_async_copy(y_ref.at[pl.ds(first_blk + k, 1)], g_ref.at[pl.ds(g_blk + k, 1)], sem).start()
            return c
        lax.fori_loop(0, n_blk, fetch, 0)

    cap = g_ref.shape[0] * g_ref.shape[1]
    route, gpos = route_ref[...], gpos_ref[...]
    col = lax.broadcasted_iota(jnp.int32, (TM, cap), 1)
    wm = jnp.zeros((TM, cap), F32)
    for r in range(TOP_K):
        wm = wm + jnp.where(col == gpos[:, r:r + 1], route[:, TOP_K + r:TOP_K + r + 1], 0.0)
    wm = wm.astype(BF)

    def drain(k, c):
        pltpu.make_async_copy(y_ref.at[pl.ds(0, 1)], g_ref.at[pl.ds(0, 1)], sem).wait()
        return c
    lax.fori_loop(0, tab_ref[0, 3 * n_exp], drain, 0)
    acc = jnp.dot(wm, g_ref[...].reshape(cap, D), preferred_element_type=F32)
    xn = x_ref[...] + mod_ref[5:6, :] * acc
    if final:
        xn = xn * lax.rsqrt(jnp.mean(xn * xn, axis=-1, keepdims=True) + EPS) * gf_ref[...]
    o_ref[...] = xn


def _combine_blocks(tab, gpos, x_mid, mods, route, g_final, y, dims, final, n_exp):
    rows = x_mid.shape[0]
    tps, nb = dims["tps"], dims["nb"]
    cap_blocks = (TM * TOP_K + n_exp * 2 * ROW_BLK) // ROW_BLK
    return pl.pallas_call(
        functools.partial(_combine_blocks_body, final=final, n_exp=n_exp),
        grid=(rows // TM,),
        in_specs=[pl.BlockSpec((None, 1, LANES), lambda i: (i, 0, 0), memory_space=pltpu.SMEM),
                  pl.BlockSpec((TM, D), lambda i: (i, 0)),
                  pl.BlockSpec((None, MOD_ROWS, D), lambda i: (jnp.minimum(i // tps, nb), 0, 0)),
                  pl.BlockSpec((TM, LANES), lambda i: (i, 0)),
                  pl.BlockSpec((TM, TOP_K), lambda i: (i, 0)),
                  pl.BlockSpec((1, D), lambda i: (0, 0)),
                  pl.BlockSpec(memory_space=pl.ANY)],
        out_specs=pl.BlockSpec((TM, D), lambda i: (i, 0)),
        out_shape=jax.ShapeDtypeStruct((rows, D), F32),
        scratch_shapes=[pltpu.VMEM((cap_blocks, ROW_BLK, D), BF), pltpu.SemaphoreType.DMA(())],
        compiler_params=_cparams(("arbitrary",), VMEM_LIMIT),
    )(tab, x_mid, mods, route, gpos, g_final, y.reshape(-1, ROW_BLK, D))


def _moe(x_mid, h2, route, counts, mods, g_final, layer, wg, bg, wu, bu, wd, bd, dims, final):
    rows = x_mid.shape[0]
    n_exp = wg.shape[1]
    n_slots = rows * TOP_K + n_exp * TME
    e = route[:, 0:TOP_K].astype(jnp.int32)
    rank = route[:, 2 * TOP_K:3 * TOP_K].astype(jnp.int32)
    cnt = counts[0, :n_exp].astype(jnp.int32)
    cnt_pad = ((cnt + TME - 1) // TME) * TME
    offs_end = jnp.cumsum(cnt_pad)
    offs = offs_end - cnt_pad
    onehot = e[:, :, None] == jnp.arange(n_exp, dtype=jnp.int32)[None, None, :]
    dest = jnp.sum(jnp.where(onehot, offs[None, None, :], 0), axis=-1) + rank
    tmv = max(t for t in (TM_MOVE, TM) if dims["s"] % t == 0 and dims["t_lat"] % t == 0 and rows % t == 0)
    dest3 = dest.reshape(rows // tmv, 1, tmv * TOP_K)
    tile_start = jnp.arange(n_slots // TME, dtype=jnp.int32) * TME
    tile_expert = jnp.minimum(jnp.sum(tile_start[:, None] >= offs_end[None, :], axis=1), n_exp - 1).astype(jnp.int32)
    n_active = (offs_end[-1:] // TME).astype(jnp.int32)
    n_valid = jnp.clip(cnt[tile_expert] - (tile_start - offs[tile_expert]), 0, TME).astype(jnp.int32)
    xs = _dispatch(dest3, h2, n_slots)
    y = _experts(tile_expert, n_active, n_valid, xs, layer, wg, bg, wu, bu, wd, bd)

    nt = rows // TM
    per_tile = jnp.sum(onehot.reshape(nt, TM * TOP_K, n_exp), axis=1).astype(jnp.int32)
    start = offs[None, :] + jnp.cumsum(per_tile, axis=0) - per_tile
    first_blk = start // ROW_BLK
    n_blk = jnp.where(per_tile > 0, (start + per_tile + ROW_BLK - 1) // ROW_BLK - first_blk, 0)
    g_blk = jnp.cumsum(n_blk, axis=1) - n_blk
    tab = jnp.concatenate([first_blk, n_blk, g_blk, jnp.sum(n_blk, axis=1, keepdims=True),
                           jnp.zeros((nt, LANES - 3 * n_exp - 1), jnp.int32)], axis=1).reshape(nt, 1, LANES)
    shift = (g_blk - first_blk) * ROW_BLK
    gpos = dest + jnp.sum(jnp.where(onehot.reshape(nt, TM, TOP_K, n_exp), shift[:, None, None, :], 0),
                          axis=-1).reshape(rows, TOP_K)
    return _combine_blocks(tab, gpos, x_mid, mods, route, g_final, y, dims, final, n_exp)


def _rope_tables(s, ctx):
    rows = s // GRID_W
    row = np.repeat(np.arange(rows), GRID_W).astype(np.float32)
    col = np.tile(np.arange(GRID_W), rows).astype(np.float32)
    half = HEAD_DIM // 2
    inv = (np.float32(ROPE_BASE) ** (-np.arange(0, half, 2, dtype=np.float32) / half)).astype(np.float32)
    ar = row[:, None] * inv
    ac = col[:, None] * inv
    ang = np.concatenate([ar, ar, ac, ac], axis=-1)
    cos = np.concatenate([np.cos(ang), np.ones((ctx, HEAD_DIM), np.float32)], axis=0)
    sin = np.concatenate([np.sin(ang), np.zeros((ctx, HEAD_DIM), np.float32)], axis=0)
    return (jnp.asarray(np.tile(cos, (1, 2)), F32), jnp.asarray(np.tile(sin, (1, 2)), F32))


def _rot_columns(w):
    d, n = w.shape
    w4 = w.reshape(d, n // HEAD_DIM, 4, HEAD_DIM // 4)
    return jnp.stack([-w4[:, :, 1], w4[:, :, 0], -w4[:, :, 3], w4[:, :, 2]], axis=2).reshape(d, n)


def kernel(x, c, ctx, c_ctx, w_ada, b_ada, g_norm1, w_in, da_lambda, g_subln, w_conv, w_pool, pool_scale,
           w_branch, w_mgate, b_mgate, w_out, g_norm2, w_router, b_router, w_e_gate, b_e_gate, w_e_up,
           b_e_up, w_e_down, b_e_down, g_final):
    nb, s, _ = x.shape
    nctx = ctx.shape[1]
    depth = w_in.shape[0]
    n_exp = w_router.shape[2]
    t_lat, t_ctx = nb * s, nb * nctx
    dims = dict(nb=nb, s=s, ctx=nctx, t_lat=t_lat, t_all=t_lat + t_ctx, nlat=t_lat // TM, tps=s // TM, tpc=nctx // TM)
    assert s % TK == 0 and s // TK >= 2 and s % (FFT_N2 * 8) == 0 and nctx % TM == 0 and nb < COND_ROWS
    assert TM == VT_CHUNK and nctx % VT_CHUNK == 0 and ATT_UNROLL % 2 == 0

    cos_t, sin_t = _rope_tables(s, nctx)
    tabs = _fft_tables(s, nctx)
    cond = jnp.concatenate([c, c_ctx[None, :], jnp.zeros((COND_ROWS - 1 - nb, D), F32)], axis=0)
    mods_all = _adaln(cond, w_ada, b_ada).reshape(depth, COND_ROWS, 6, D)
    mods_all = jnp.concatenate([mods_all, jnp.zeros((depth, COND_ROWS, MOD_ROWS - 6, D), F32)], axis=2)

    xa, xb = x.reshape(t_lat, D), ctx.reshape(t_ctx, D)
    for l in range(depth):
        last = l == depth - 1
        lam_init = 0.8 - 0.6 * math.exp(-0.3 * l)
        mods = mods_all[l]
        w_l = w_in[l]
        w_ext = jnp.concatenate([w_l, _rot_columns(w_l[:, :BW]), _rot_columns(w_l[:, BW:2 * BW])], axis=1).astype(BF)
        qt, k, vt, u, sb, fx, px = _inproj(xa, xb, mods, g_norm1[l][None, :], w_ext, cos_t, sin_t, dims)

        rows = t_lat if last else t_lat + t_ctx
        a = _attention(qt, k, vt, g_subln[l], da_lambda[l], lam_init, dims, latent=True, rows=rows)
        yf = _fourier_latent(fx, tabs, dims, rows)
        if not last:
            a = _attention(qt, k, vt, g_subln[l], da_lambda[l], lam_init, dims, latent=False, fill=a)
            yf = _fourier_ctx(fx, tabs, dims, yf)

        wconv = jnp.concatenate([w_conv[l], jnp.zeros((5, BW), F32)], axis=0)
        wr = jnp.concatenate([w_router[l], jnp.zeros((D, LANES - n_exp), F32)], axis=1).astype(BF)
        br = jnp.concatenate([b_router[l], jnp.full((LANES - n_exp,), -1e30, F32)])[None, :]
        x_mid, h2, route, counts = _merge(
            xa, xb, mods, g_norm1[l][None, :], a, u, sb, yf, px, wconv, w_pool[l].astype(BF), pool_scale[l][None, :],
            w_branch[l].astype(BF), w_mgate[l].astype(BF), b_mgate[l][None, :], w_out[l].astype(BF),
            g_norm2[l][None, :], wr, br, dims, rows)
        x_new = _moe(x_mid, h2, route, counts, mods, g_final[None, :], l,
                     w_e_gate, b_e_gate[:, :, None, :], w_e_up, b_e_up[:, :, None, :],
                     w_e_down, b_e_down[:, :, None, :], dims, final=last)
        if last:
            return x_new.reshape(nb, s, D)
        xa = xb = x_new
```

```python
import functools
import math

import numpy as np
import jax
import jax.numpy as jnp
from jax import lax
from jax.experimental import pallas as pl
from jax.experimental.pallas import tpu as pltpu

F32 = jnp.float32
BF = jnp.bfloat16

D = 1024
HEADS = 4
HEAD_DIM = 64
HEAD_W = 128
BW = 512
GRID_W = 64
EPS = 1e-6
ROPE_BASE = 10000.0
POOL_WINDOWS = (2, 4, 8, 16)
TOP_K = 4
SWIGLU_LIMIT = 7.0
SWIGLU_ALPHA = 1.702
FFT_N2 = 128

TM = 256
TME = 512
TM_MOVE = 1024
TQ = 512
TK = 512
ATT_UNROLL = 2
VT_CHUNK = 256
HALO = 16
LANES = 128
COND_ROWS = 16
MOD_ROWS = 8
ROW_BLK = 16
QSCALE = (HEAD_DIM ** -0.5) * math.log2(math.e)
EXP_HEADROOM = 64.0
NORM_MARGIN = 1.05
VMEM_LIMIT = 56 * 2 ** 20


def _cparams(sem, vmem=None):
    return pltpu.CompilerParams(dimension_semantics=sem, vmem_limit_bytes=vmem)


def _modnorm(x, g, shift, scale):
    y = x * lax.rsqrt(jnp.mean(x * x, axis=-1, keepdims=True) + EPS) * g
    return y * (1.0 + scale) + shift


def _sigmoid(x):
    return 1.0 / (1.0 + jnp.exp(-x))


def _adaln_body(c_ref, w_ref, b_ref, o_ref):
    c = c_ref[...]
    s = c * _sigmoid(c)
    o_ref[...] = jnp.dot(s.astype(BF), w_ref[...].astype(BF), preferred_element_type=F32) + b_ref[...]


def _adaln(c16, w_ada, b_ada):
    nl = w_ada.shape[0]
    tn = 1536
    return pl.pallas_call(
        _adaln_body,
        grid=(nl, 6 * D // tn),
        in_specs=[pl.BlockSpec((COND_ROWS, D), lambda l, j: (0, 0)),
                  pl.BlockSpec((None, D, tn), lambda l, j: (l, 0, j)),
                  pl.BlockSpec((None, 1, tn), lambda l, j: (l, 0, j))],
        out_specs=pl.BlockSpec((None, COND_ROWS, tn), lambda l, j: (l, 0, j)),
        out_shape=jax.ShapeDtypeStruct((nl, COND_ROWS, 6 * D), F32),
        compiler_params=_cparams(("arbitrary", "arbitrary"), VMEM_LIMIT),
    )(c16, w_ada, b_ada.reshape(nl, 1, 6 * D))


def _token_tile(xa_ref, xb_ref, n_a):
    return jnp.where(pl.program_id(0) < n_a, xa_ref[...], xb_ref[...])


def _token_specs(xa, xb):
    n_a, n_b = xa.shape[0] // TM, xb.shape[0] // TM
    return [pl.BlockSpec((TM, D), lambda i: (jnp.minimum(i, n_a - 1), 0)),
            pl.BlockSpec((TM, D), lambda i: (jnp.clip(i - n_a, 0, n_b - 1), 0))]


def _inproj_body(xa_ref, xb_ref, mod_ref, g_ref, w_ref, cos_ref, sin_ref,
                 q_ref, k_ref, v_ref, u_ref, sb_ref, fx_ref, px_ref, *, n_a):
    h = _modnorm(_token_tile(xa_ref, xb_ref, n_a), g_ref[...], mod_ref[0:1, :], mod_ref[1:2, :]).astype(BF)

    def proj(c0):
        return jnp.dot(h, w_ref[:, c0:c0 + BW], preferred_element_type=F32)

    cos = jnp.concatenate([cos_ref[...]] * 4, axis=1)
    sin = jnp.concatenate([sin_ref[...]] * 4, axis=1)
    q_ref[...] = ((proj(0) * cos + proj(8 * BW) * sin) * QSCALE).T.astype(BF)
    k_ref[...] = (proj(BW) * cos + proj(9 * BW) * sin).astype(BF)
    v_ref[...] = proj(2 * BW).T.astype(BF)
    u_ref[...] = (proj(5 * BW) * proj(3 * BW)).astype(BF)
    sb_ref[...] = proj(4 * BW).astype(BF)
    px_ref[...] = proj(7 * BW).astype(BF)
    fx_ref[...] = proj(6 * BW).astype(BF)


def _inproj(xa, xb, mods, g1, w_ext, cos_t, sin_t, dims):
    t_all, nlat, tps, tpc, nb = dims["t_all"], dims["nlat"], dims["tps"], dims["tpc"], dims["nb"]

    def tab_idx(i):
        return (jnp.where(i < nlat, i % tps, tps + (i - nlat) % tpc), 0)

    out = jax.ShapeDtypeStruct((t_all, BW), BF)
    row = pl.BlockSpec((TM, BW), lambda i: (i, 0))
    return pl.pallas_call(
        functools.partial(_inproj_body, n_a=xa.shape[0] // TM),
        grid=(t_all // TM,),
        in_specs=_token_specs(xa, xb) + [
                  pl.BlockSpec((None, MOD_ROWS, D), lambda i: (jnp.minimum(i // tps, nb), 0, 0)),
                  pl.BlockSpec((1, D), lambda i: (0, 0)),
                  pl.BlockSpec((D, 10 * BW), lambda i: (0, 0)),
                  pl.BlockSpec((TM, LANES), tab_idx),
                  pl.BlockSpec((TM, LANES), tab_idx)],
        out_specs=[pl.BlockSpec((BW, TM), lambda i: (0, i)), row,
                   pl.BlockSpec((None, BW, VT_CHUNK), lambda i: (i, 0, 0))] + [row] * 4,
        out_shape=[jax.ShapeDtypeStruct((BW, t_all), BF), out,
                   jax.ShapeDtypeStruct((t_all // VT_CHUNK, BW, VT_CHUNK), BF)] + [out] * 4,
        compiler_params=_cparams(("arbitrary",), VMEM_LIMIT),
    )(xa, xb, mods, g1, w_ext, cos_t, sin_t)


def _attn_body(*refs, n_ctx_chunks, n_lat_chunks, lam_init):
    if n_lat_chunks:
        qt_ref, kc_ref, vtc_ref, kl_ref, vtl_ref, g_ref, dl_ref, o_ref, acc_ref, s_buf, p_buf, l_ref, knorm_ref = refs
    else:
        qt_ref, kc_ref, vtc_ref, g_ref, dl_ref, _, o_ref, acc_ref = refs
    qt = qt_ref[...]
    tq = qt.shape[1]
    row = lax.broadcasted_iota(jnp.int32, qt.shape, 0)
    zero = jnp.zeros_like(qt)
    q2 = jnp.concatenate([jnp.where(row < HEAD_DIM, qt, zero), jnp.where(row >= HEAD_DIM, qt, zero)], axis=1)

    def scores(kc):
        s = jnp.dot(kc, q2, preferred_element_type=F32)
        return s, jnp.max(s, axis=0, keepdims=True)

    def softmax(s, smax, m, l):
        m_new = jnp.maximum(m, smax)
        p = jnp.exp2(s - m_new)
        alpha = jnp.exp2(m - m_new)
        return p.astype(BF), m_new, alpha * l + jnp.sum(p, axis=0, keepdims=True), alpha

    def accumulate(alpha, vt, p):
        acc_ref[...] = alpha * acc_ref[...] + jnp.dot(vt, p, preferred_element_type=F32)

    m = jnp.full((1, 2 * tq), -1e30, F32)
    l = jnp.zeros((1, 2 * tq), F32)
    acc_ref[...] = jnp.zeros_like(acc_ref)
    ctx_scores = [scores(kc_ref[c * VT_CHUNK:(c + 1) * VT_CHUNK, :]) for c in range(n_ctx_chunks)]
    if n_lat_chunks:
        per = TK // VT_CHUNK
        n = n_lat_chunks

        def keys(j):
            return kl_ref[pl.ds(pl.multiple_of(j * TK, TK), TK), :]

        s_buf[0] = jnp.dot(keys(0), q2, preferred_element_type=F32)
    for c in range(n_ctx_chunks):
        p, m, l, alpha = softmax(*ctx_scores[c], m, l)
        accumulate(alpha, vtc_ref[c], p)
    if n_lat_chunks:

        def values(j):
            parts = [vtl_ref[j * per + r] for r in range(per)]
            return parts[0] if per == 1 else jnp.concatenate(parts, axis=1)

        @pl.when(pl.program_id(2) == 0)
        def _():
            d_i = lax.broadcasted_iota(jnp.int32, (HEAD_W, LANES), 0)
            c_i = lax.broadcasted_iota(jnp.int32, (HEAD_W, LANES), 1)
            sel = jnp.where((d_i >= HEAD_DIM) == (c_i == 1), 1.0, 0.0) * jnp.where(c_i < 2, 1.0, 0.0)
            best = jnp.zeros((1, LANES), F32)
            for j in range(n):
                kf = keys(j).astype(F32)
                n2 = jnp.dot((kf * kf).astype(BF), sel.astype(BF), preferred_element_type=F32)
                best = jnp.maximum(best, jnp.max(n2, axis=0, keepdims=True))
            lane2 = lax.broadcasted_iota(jnp.int32, (1, 2 * tq), 1)
            knorm_ref[...] = jnp.sqrt(jnp.where(lane2 < tq, best[:, 0:1], best[:, 1:2]))

        qf = q2.astype(F32)
        bound = jnp.sqrt(jnp.sum(qf * qf, axis=0, keepdims=True)) * knorm_ref[...] * NORM_MARGIN
        gap = jnp.max(bound - m)
        l_ref[...] = l

        @pl.when(gap <= EXP_HEADROOM)
        def _():
            lsum = l
            s = s_buf[0]
            for j in range(n):
                s_next = jnp.dot(keys(j + 1), q2, preferred_element_type=F32) if j + 1 < n else None
                p = jnp.exp2(s - m)
                lsum = lsum + jnp.sum(p, axis=0, keepdims=True)
                acc_ref[...] += jnp.dot(values(j), p.astype(BF), preferred_element_type=F32)
                s = s_next
            l_ref[...] = lsum

        @pl.when(gap > EXP_HEADROOM)
        def _():
            _attn_exact_latent(m, l, n, keys, values, scores, softmax, accumulate, s_buf, p_buf, l_ref)

        l = l_ref[...]
    o = acc_ref[...] / l
    dl = dl_ref[...]
    lam = (jnp.exp(jnp.sum(dl[0:1] * dl[1:2], axis=1, keepdims=True))
           - jnp.exp(jnp.sum(dl[2:3] * dl[3:4], axis=1, keepdims=True)) + lam_init)
    d = o[:, :tq] - lam * o[:, tq:]
    y = d * lax.rsqrt(jnp.mean(d * d, axis=0, keepdims=True) + EPS) * g_ref[...] * (1.0 - lam_init)
    o_ref[...] = y.T.astype(BF)


def _attn_exact_latent(m, l, n, keys, values, scores, softmax, accumulate, s_buf, p_buf, l_ref):
    def half_step(j, prev, cur, carry):
        m, l, alpha_prev, smax = carry
        accumulate(alpha_prev, values(j - 1), p_buf[prev])
        if isinstance(j, int) and j + 1 >= n:
            smax_next = smax
        else:
            s_buf[prev], smax_next = scores(keys(jnp.minimum(j + 1, n - 1)))
        p, m, l, alpha = softmax(s_buf[cur], smax, m, l)
        p_buf[cur] = p
        return m, l, alpha, smax_next

    s0 = s_buf[0]
    p, m, l, alpha = softmax(s0, jnp.max(s0, axis=0, keepdims=True), m, l)
    p_buf[0] = p
    s_buf[1], smax = scores(keys(1))

    def body(i, carry):
        for r in range(ATT_UNROLL):
            carry = half_step(ATT_UNROLL * i + r + 1, r % 2, 1 - r % 2, carry)
        return carry

    n_body = (n - 1) // ATT_UNROLL
    carry = lax.fori_loop(0, n_body, body, (m, l, alpha, smax))
    for j in range(n_body * ATT_UNROLL + 1, n):
        carry = half_step(j, (j - 1) % 2, j % 2, carry)
    m, l, alpha, _ = carry
    accumulate(alpha, values(n - 1), p_buf[(n - 1) % 2])
    l_ref[...] = l


def _attention(qt, k, vt, g_sub, da_lambda, lam_init, dims, latent, rows=None, fill=None):
    nb, s, ctx, t_lat = dims["nb"], dims["s"], dims["ctx"], dims["t_lat"]
    cpb = ctx // VT_CHUNK
    tq = TQ if latent else min(TQ, ctx)
    g_sub_b = jnp.broadcast_to(g_sub[:, None], (HEAD_W, tq))
    common_tail = [pl.BlockSpec((HEAD_W, tq), lambda b, h, qi: (0, 0)),
                   pl.BlockSpec((4, HEAD_DIM), lambda b, h, qi: (0, 0))]
    kc_spec = pl.BlockSpec((ctx, HEAD_W), lambda b, h, qi: (t_lat // ctx + b, h))
    vtc_spec = pl.BlockSpec((cpb, HEAD_W, VT_CHUNK), lambda b, h, qi: (t_lat // ctx + b, h, 0))
    scratch = [pltpu.VMEM((HEAD_W, 2 * tq), F32)]
    if latent:
        nq = s // tq
        in_specs = [pl.BlockSpec((HEAD_W, tq), lambda b, h, qi: (h, b * nq + qi)), kc_spec, vtc_spec,
                    pl.BlockSpec((s, HEAD_W), lambda b, h, qi: (b, h)),
                    pl.BlockSpec((s // VT_CHUNK, HEAD_W, VT_CHUNK), lambda b, h, qi: (b, h, 0))] + common_tail
        args = (qt, k, vt, k, vt, g_sub_b, da_lambda)
        row0, n_lat_chunks, aliases = 0, s // TK, {}
        scratch += [pltpu.VMEM((2, TK, 2 * tq), F32), pltpu.VMEM((2, TK, 2 * tq), BF),
                    pltpu.VMEM((1, 2 * tq), F32), pltpu.VMEM((1, 2 * tq), F32)]
    else:
        nq = ctx // tq
        in_specs = [pl.BlockSpec((HEAD_W, tq), lambda b, h, qi: (h, t_lat // tq + b * nq + qi)),
                    kc_spec, vtc_spec] + common_tail
        in_specs.append(pl.BlockSpec(memory_space=pl.ANY))
        args = (qt, k, vt, g_sub_b, da_lambda, fill)
        row0, n_lat_chunks, rows, aliases = t_lat // tq, 0, fill.shape[0], {5: 0}
    return pl.pallas_call(
        functools.partial(_attn_body, n_ctx_chunks=cpb, n_lat_chunks=n_lat_chunks, lam_init=lam_init),
        grid=(nb, HEADS, nq),
        in_specs=in_specs,
        out_specs=pl.BlockSpec((tq, HEAD_W), lambda b, h, qi: (row0 + b * nq + qi, h)),
        out_shape=jax.ShapeDtypeStruct((rows, HEADS * HEAD_W), BF),
        input_output_aliases=aliases,
        scratch_shapes=scratch,
        compiler_params=_cparams(("arbitrary", "arbitrary", "arbitrary"), VMEM_LIMIT),
    )(*args)


def _channel_dft(x, cs):
    parts = [jnp.dot(x[:, g * LANES:(g + 1) * LANES], cs, preferred_element_type=F32).astype(BF)
             for g in range(x.shape[1] // LANES)]
    return jnp.concatenate([jnp.concatenate([pq[:, :LANES] for pq in parts], axis=1),
                            jnp.concatenate([pq[:, LANES:] for pq in parts], axis=1)], axis=0)


def _fft_a_body(fa_ref, cs_ref, x_ref, zr_ref, zi_ref):
    x = _channel_dft(x_ref[...], cs_ref[...])
    z = jnp.dot(fa_ref[...], x, preferred_element_type=F32)
    zr_ref[...] = z[:FFT_N2].astype(BF)
    zi_ref[...] = z[FFT_N2:].astype(BF)


def _fft_b_body(fb_ref, tc_ref, ts_ref, zr_ref, zi_ref, o_ref):
    nk = zr_ref.shape[0]
    parts_r, parts_i = [], []
    for j in range(nk):
        zr = zr_ref[j].astype(F32)
        zi = zi_ref[j].astype(F32)
        tc = jnp.concatenate([tc_ref[j]] * 4, axis=1)
        ts = jnp.concatenate([ts_ref[j]] * 4, axis=1)
        parts_r.append((zr * tc - zi * ts).astype(BF))
        parts_i.append((zr * ts + zi * tc).astype(BF))
    rhs = jnp.concatenate([jnp.concatenate(parts_r, axis=1), jnp.concatenate(parts_i, axis=1)], axis=0)
    o_ref[...] = jnp.dot(fb_ref[...], rhs, preferred_element_type=F32).astype(BF)


def _fft_ctx_body(f_ref, cs_ref, x_ref, _, o_ref):
    x = _channel_dft(x_ref[...], cs_ref[...])
    o_ref[...] = jnp.dot(f_ref[...], x, preferred_element_type=F32).astype(BF)


def _fft_tables(s, ctx):
    n1, n2 = s // FFT_N2, FFT_N2
    norm = 1.0 / math.sqrt(s * LANES)
    a = 2 * np.pi * np.outer(np.arange(n2), np.arange(n2)) / n2
    c, sn = np.cos(a), np.sin(a)
    fa = np.block([[c, -sn], [sn, c]]) * norm
    tw = 2 * np.pi * np.outer(np.arange(n2), np.arange(n1)) / s
    tc = np.repeat(np.cos(tw)[:, :, None], LANES, axis=2)
    ts = np.repeat(np.sin(tw)[:, :, None], LANES, axis=2)
    b = 2 * np.pi * np.outer(np.arange(n1), np.arange(n1)) / n1
    fb = np.concatenate([np.cos(b), -np.sin(b)], axis=1)
    ch = 2 * np.pi * np.outer(np.arange(LANES), np.arange(LANES)) / LANES
    cs = np.concatenate([np.cos(ch), np.sin(ch)], axis=1)
    cx = 2 * np.pi * np.outer(np.arange(ctx), np.arange(ctx)) / ctx
    fctx = np.concatenate([np.cos(cx), -np.sin(cx)], axis=1) / math.sqrt(ctx * LANES)
    return dict(fa=jnp.asarray(fa, BF), tc=jnp.asarray(tc, F32), ts=jnp.asarray(ts, F32),
                fb=jnp.asarray(fb, BF), cs=jnp.asarray(cs, BF), fctx=jnp.asarray(fctx, BF))


def _fourier_latent(fx, tabs, dims, rows):
    nb, s = dims["nb"], dims["s"]
    n1 = s // FFT_N2
    wide = n1 * BW
    wc = min(4096, wide)
    fx2 = fx.reshape(-1, wide)
    zshape = jax.ShapeDtypeStruct((nb * FFT_N2, wide), BF)
    zr, zi = pl.pallas_call(
        _fft_a_body,
        grid=(nb, wide // wc),
        in_specs=[pl.BlockSpec((2 * FFT_N2, 2 * FFT_N2), lambda b, j: (0, 0)),
                  pl.BlockSpec((LANES, 2 * LANES), lambda b, j: (0, 0)),
                  pl.BlockSpec((FFT_N2, wc), lambda b, j: (b, j))],
        out_specs=[pl.BlockSpec((FFT_N2, wc), lambda b, j: (b, j))] * 2,
        out_shape=[zshape, zshape],
        compiler_params=_cparams(("arbitrary", "arbitrary"), VMEM_LIMIT),
    )(tabs["fa"], tabs["cs"], fx2)
    kb = 8
    zr4 = zr.reshape(nb, FFT_N2, n1, BW)
    zi4 = zi.reshape(nb, FFT_N2, n1, BW)
    y = pl.pallas_call(
        _fft_b_body,
        grid=(FFT_N2 // kb, nb),
        in_specs=[pl.BlockSpec((n1, 2 * n1), lambda kk, b: (0, 0)),
                  pl.BlockSpec((kb, n1, LANES), lambda kk, b: (kk, 0, 0)),
                  pl.BlockSpec((kb, n1, LANES), lambda kk, b: (kk, 0, 0)),
                  pl.BlockSpec((None, kb, n1, BW), lambda kk, b: (b, kk, 0, 0)),
                  pl.BlockSpec((None, kb, n1, BW), lambda kk, b: (b, kk, 0, 0))],
        out_specs=pl.BlockSpec((n1, kb * BW), lambda kk, b: (b, kk)),
        out_shape=jax.ShapeDtypeStruct((rows // FFT_N2, FFT_N2 * BW), BF),
        compiler_params=_cparams(("arbitrary", "arbitrary"), VMEM_LIMIT),
    )(tabs["fb"], tabs["tc"], tabs["ts"], zr4, zi4)
    return y.reshape(rows, BW)


def _fourier_ctx(fx, tabs, dims, fill):
    nb, ctx, t_lat = dims["nb"], dims["ctx"], dims["t_lat"]
    return pl.pallas_call(
        _fft_ctx_body,
        grid=(nb,),
        in_specs=[pl.BlockSpec((ctx, 2 * ctx), lambda b: (0, 0)),
                  pl.BlockSpec((LANES, 2 * LANES), lambda b: (0, 0)),
                  pl.BlockSpec((ctx, BW), lambda b: (t_lat // ctx + b, 0)),
                  pl.BlockSpec(memory_space=pl.ANY)],
        out_specs=pl.BlockSpec((ctx, BW), lambda b: (t_lat // ctx + b, 0)),
        out_shape=jax.ShapeDtypeStruct(fill.shape, BF),
        input_output_aliases={3: 0},
        compiler_params=_cparams(("arbitrary",), VMEM_LIMIT),
    )(tabs["fctx"], tabs["cs"], fx, fill)


def _route_topk(h2b, wr_ref, br_ref):
    logits = jnp.dot(h2b, wr_ref[...], preferred_element_type=F32) + br_ref[...]
    lane = lax.broadcasted_iota(jnp.int32, (TM, LANES), 1).astype(F32)
    cur = logits
    vals, idxs = [], []
    for _ in range(TOP_K):
        mx = jnp.max(cur, axis=1, keepdims=True)
        idx = jnp.min(jnp.where(cur == mx, lane, float(LANES)), axis=1, keepdims=True)
        vals.append(mx)
        idxs.append(idx)
        cur = jnp.where(lane == idx, -jnp.inf, cur)
    ex = [jnp.exp(v - vals[0]) for v in vals]
    den = ex[0] + ex[1] + ex[2] + ex[3]
    member = jnp.zeros((TM, LANES), F32)
    for idx in idxs:
        member = member + jnp.where(lane == idx, 1.0, 0.0)
    return idxs, [e / den for e in ex], member


def _route_ranks(idxs, weights, member, carry_ref, live):
    lane = lax.broadcasted_iota(jnp.int32, (TM, LANES), 1).astype(F32)
    tri = (lax.broadcasted_iota(jnp.int32, (TM, TM), 0) > lax.broadcasted_iota(jnp.int32, (TM, TM), 1))
    before = jnp.dot(jnp.where(tri, 1.0, 0.0).astype(BF), member.astype(BF), preferred_element_type=F32)
    rank_all = carry_ref[0:1, :] + before
    route = jnp.zeros((TM, LANES), F32)
    for r in range(TOP_K):
        rank = jnp.sum(jnp.where(lane == idxs[r], rank_all, 0.0), axis=1, keepdims=True)
        route = (route + jnp.where(lane == float(r), idxs[r], 0.0)
                 + jnp.where(lane == float(TOP_K + r), weights[r], 0.0)
                 + jnp.where(lane == float(2 * TOP_K + r), rank, 0.0))
    carry_ref[...] = carry_ref[...] + live * jnp.sum(member, axis=0, keepdims=True)
    return route


def _merge_body(xa_ref, xb_ref, mod_ref, g1_ref, a_ref, u_ref, up_ref, un_ref, sb_ref, yf_ref, px_ref, pp_ref, pn_ref,
                wconv_ref, wpool_ref, pscale_ref, wb_ref, wg_ref, bg_ref, wo_ref, g2_ref, wr_ref, br_ref,
                xo_ref, h2_ref, route_ref, cnt_ref, carry_ref, h2p_ref, *, n_tiles, nlat, tps, tpc, s, ctx, n_a):
    step = pl.program_id(0)
    i = jnp.minimum(step, n_tiles - 1)
    is_lat = i < nlat
    j = jnp.where(is_lat, i % tps, (i - nlat) % tpc)
    first = j == 0
    last = j == jnp.where(is_lat, tps, tpc) - 1
    seq_len = jnp.where(is_lat, s, ctx)

    @pl.when(step == 0)
    def _():
        carry_ref[...] = jnp.zeros_like(carry_ref)
        h2p_ref[...] = jnp.zeros_like(h2p_ref)

    topk_prev = _route_topk(h2p_ref[...], wr_ref, br_ref)

    x = jnp.where(i < n_a, xa_ref[...], xb_ref[...])
    h = _modnorm(x, g1_ref[...], mod_ref[0:1, :], mod_ref[1:2, :]).astype(BF)
    rowi = lax.broadcasted_iota(jnp.int32, (TM, 1), 0)

    u = u_ref[...].astype(F32)
    u_prev = jnp.where(first, 0.0, up_ref[...].astype(F32)[HALO - 1:HALO, :])
    u_next = jnp.where(last, 0.0, un_ref[...].astype(F32)[0:1, :])
    u_dn = jnp.where(rowi == 0, u_prev, pltpu.roll(u, 1, axis=0))
    u_up = jnp.where(rowi == TM - 1, u_next, pltpu.roll(u, TM - 1, axis=0))
    wc = wconv_ref[...]
    y_conv = sb_ref[...].astype(F32) * (wc[0:1] * u_dn + wc[1:2] * u + wc[2:3] * u_up)

    px = px_ref[...].astype(F32)
    ext = jnp.concatenate([jnp.where(first, 0.0, pp_ref[...].astype(F32)), px,
                           jnp.where(last, 0.0, pn_ref[...].astype(F32))], axis=0)
    ext_len = TM + 2 * HALO
    pos = j * TM + rowi
    pooled = []
    for g, wd in enumerate(POOL_WINDOWS):
        e = ext[:, g * LANES:(g + 1) * LANES]
        win = e + pltpu.roll(e, 1, axis=0)
        half = 1
        while 2 * half < wd:
            win = pltpu.roll(win, half, axis=0) + pltpu.roll(win, ext_len - half, axis=0)
            half *= 2
        win = win[HALO:HALO + TM]
        cnt = (jnp.minimum(pos + wd // 2, seq_len) - jnp.maximum(pos - wd // 2, 0)).astype(F32)
        pooled.append((win / cnt - px[:, g * LANES:(g + 1) * LANES]).astype(BF))

    gates = [_sigmoid(jnp.dot(h, wg_ref[:, kk * D:(kk + 1) * D], preferred_element_type=F32)
                      + bg_ref[:, kk * D:(kk + 1) * D]) for kk in range(4)]

    def branch(kk, y):
        return gates[kk] * jnp.dot(y, wb_ref[kk], preferred_element_type=F32)

    acc = branch(0, a_ref[...]) + branch(2, yf_ref[...]) + branch(1, y_conv.astype(BF))
    y_pool = jnp.concatenate([jnp.dot(pooled[g], wpool_ref[g], preferred_element_type=F32)
                              for g in range(len(POOL_WINDOWS))], axis=1) * pscale_ref[...]
    acc = acc + branch(3, y_pool.astype(BF))

    xn = x + mod_ref[2:3, :] * jnp.dot(acc.astype(BF), wo_ref[...], preferred_element_type=F32)
    xo_ref[...] = xn
    h2 = _modnorm(xn, g2_ref[...], mod_ref[3:4, :], mod_ref[4:5, :])
    h2_ref[...] = h2
    h2p_ref[...] = h2.astype(BF)
    route_ref[...] = _route_ranks(*topk_prev, carry_ref, jnp.where(step > 0, 1.0, 0.0))
    cnt_ref[...] = carry_ref[...]


def _merge(xa, xb, mods, g1, a, u, sb, yf, px, wconv, wpool, pscale, wb, wg, bg, wo, g2, wr, br, dims, rows):
    nlat, tps, tpc, nb = dims["nlat"], dims["tps"], dims["tpc"], dims["nb"]
    n_tiles = rows // TM
    n_a, n_b = xa.shape[0] // TM, xb.shape[0] // TM
    nhalo = TM // HALO
    last_halo = u.shape[0] // HALO - 1
    cur = lambda i: jnp.minimum(i, n_tiles - 1)
    tile = lambda w: pl.BlockSpec((TM, w), lambda i: (cur(i), 0))
    prev = pl.BlockSpec((HALO, BW), lambda i: (jnp.maximum(cur(i) * nhalo - 1, 0), 0))
    nxt = pl.BlockSpec((HALO, BW), lambda i: (jnp.minimum((cur(i) + 1) * nhalo, last_halo), 0))
    const = lambda shape: pl.BlockSpec(shape, lambda i: (0,) * len(shape))
    return pl.pallas_call(
        functools.partial(_merge_body, n_tiles=n_tiles, nlat=nlat, tps=tps, tpc=tpc, s=dims["s"], ctx=dims["ctx"],
                          n_a=n_a),
        grid=(n_tiles + 1,),
        in_specs=[pl.BlockSpec((TM, D), lambda i: (jnp.minimum(cur(i), n_a - 1), 0)),
                  pl.BlockSpec((TM, D), lambda i: (jnp.clip(cur(i) - n_a, 0, n_b - 1), 0)),
                  pl.BlockSpec((None, MOD_ROWS, D), lambda i: (jnp.minimum(cur(i) // tps, nb), 0, 0)),
                  const((1, D)), tile(BW), tile(BW), prev, nxt, tile(BW), tile(BW), tile(BW), prev, nxt,
                  const((8, BW)), const((4, LANES, LANES)), const((1, BW)), const((4, BW, D)),
                  const((D, 4 * D)), const((1, 4 * D)), const((D, D)), const((1, D)),
                  const((D, LANES)), const((1, LANES))],
        out_specs=[tile(D), tile(D), pl.BlockSpec((TM, LANES), lambda i: (jnp.maximum(i - 1, 0), 0)),
                   pl.BlockSpec((8, LANES), lambda i: (0, 0))],
        out_shape=[jax.ShapeDtypeStruct((rows, D), F32), jax.ShapeDtypeStruct((rows, D), F32),
                   jax.ShapeDtypeStruct((rows, LANES), F32), jax.ShapeDtypeStruct((8, LANES), F32)],
        scratch_shapes=[pltpu.VMEM((8, LANES), F32), pltpu.VMEM((TM, D), BF)],
        compiler_params=_cparams(("arbitrary",), VMEM_LIMIT),
    )(xa, xb, mods, g1, a, u, u, u, sb, yf, px, px, px, wconv, wpool, pscale, wb, wg, bg, wo, g2, wr, br)


def _wait_row_copies(src_ref, dst_ref, sem, n):
    for _ in range(n):
        pltpu.make_async_copy(src_ref.at[pl.ds(0, 1)], dst_ref.at[pl.ds(0, 1)], sem).wait()


def _dispatch_body(dest_ref, h_ref, xs_ref, sem):
    def body(t, c):
        for r in range(TOP_K):
            pltpu.make_async_copy(h_ref.at[pl.ds(t, 1)], xs_ref.at[pl.ds(dest_ref[0, t * TOP_K + r], 1)],
                                  sem).start(priority=r % 2)
        return c
    lax.fori_loop(0, h_ref.shape[0], body, 0, unroll=8)
    _wait_row_copies(h_ref, xs_ref, sem, h_ref.shape[0] * TOP_K)


def _dispatch(dest3, h2, n_slots):
    rows = h2.shape[0]
    tmv = dest3.shape[2] // TOP_K
    return pl.pallas_call(
        _dispatch_body,
        grid=(rows // tmv,),
        in_specs=[pl.BlockSpec((None, 1, tmv * TOP_K), lambda i: (i, 0, 0), memory_space=pltpu.SMEM),
                  pl.BlockSpec((tmv, D), lambda i: (i, 0))],
        out_specs=pl.BlockSpec(memory_space=pl.ANY),
        out_shape=jax.ShapeDtypeStruct((n_slots, D), F32),
        scratch_shapes=[pltpu.SemaphoreType.DMA(())],
        compiler_params=_cparams(("arbitrary",), VMEM_LIMIT),
    )(dest3, h2)


def _expert_body(te_ref, na_ref, nv_ref, xs_ref, wg_ref, bg_ref, wu_ref, bu_ref, wd_ref, bd_ref, y_ref, wgb, wub, wdb):
    j = pl.program_id(0)

    @pl.when(j < na_ref[0])
    def _():
        @pl.when((j == 0) | (te_ref[j] != te_ref[jnp.maximum(j - 1, 0)]))
        def _():
            wgb[...] = wg_ref[...].astype(BF)
            wub[...] = wu_ref[...].astype(BF)
            wdb[...] = wd_ref[...].astype(BF)

        x = xs_ref[...].astype(BF)
        a = jnp.minimum(jnp.dot(x, wgb[...], preferred_element_type=F32) + bg_ref[...], SWIGLU_LIMIT)
        u = jnp.clip(jnp.dot(x, wub[...], preferred_element_type=F32) + bu_ref[...], -SWIGLU_LIMIT, SWIGLU_LIMIT)
        act = a * _sigmoid(SWIGLU_ALPHA * a) * (u + 1.0)
        y = jnp.dot(act.astype(BF), wdb[...], preferred_element_type=F32) + bd_ref[...]
        rowi = lax.broadcasted_iota(jnp.int32, (TME, 1), 0)
        y_ref[...] = jnp.where(rowi < nv_ref[j], y, 0.0).astype(BF)


def _experts(tile_expert, n_active, n_valid, xs, layer, wg, bg, wu, bu, wd, bd):
    n_slots = xs.shape[0]
    f = wg.shape[3]
    slot = lambda j, te, na, nv: (jnp.minimum(j, na[0] - 1), 0)
    wsel = lambda j, te, na, nv: (layer, te[j], 0, 0)
    return pl.pallas_call(
        _expert_body,
        grid_spec=pltpu.PrefetchScalarGridSpec(
            num_scalar_prefetch=3,
            grid=(n_slots // TME,),
            in_specs=[pl.BlockSpec((TME, D), slot),
                      pl.BlockSpec((None, None, D, f), wsel), pl.BlockSpec((None, None, 1, f), wsel),
                      pl.BlockSpec((None, None, D, f), wsel), pl.BlockSpec((None, None, 1, f), wsel),
                      pl.BlockSpec((None, None, f, D), wsel), pl.BlockSpec((None, None, 1, D), wsel)],
            out_specs=pl.BlockSpec((TME, D), slot),
            scratch_shapes=[pltpu.VMEM((D, f), BF), pltpu.VMEM((D, f), BF), pltpu.VMEM((f, D), BF)]),
        out_shape=jax.ShapeDtypeStruct((n_slots, D), BF),
        compiler_params=_cparams(("arbitrary",), VMEM_LIMIT),
    )(tile_expert, n_active, n_valid, xs, wg, bg, wu, bu, wd, bd)


def _combine_blocks_body(tab_ref, x_ref, mod_ref, route_ref, gpos_ref, gf_ref, y_ref, o_ref, g_ref, sem, *, final, n_exp):
    @pl.when(pl.program_id(0) == 0)
    def _():
        g_ref[...] = jnp.zeros_like(g_ref)

    for e in range(n_exp):
        first_blk, n_blk, g_blk = tab_ref[0, e], tab_ref[0, n_exp + e], tab_ref[0, 2 * n_exp + e]

        def fetch(k, c):
            pltpu.make_async_copy(y_ref.at[pl.ds(first_blk + k, 1)], g_ref.at[pl.ds(g_blk + k, 1)], sem).start()
            return c
        lax.fori_loop(0, n_blk, fetch, 0)

    cap = g_ref.shape[0] * g_ref.shape[1]
    route, gpos = route_ref[...], gpos_ref[...]
    col = lax.broadcasted_iota(jnp.int32, (TM, cap), 1)
    wm = jnp.zeros((TM, cap), F32)
    for r in range(TOP_K):
        wm = wm + jnp.where(col == gpos[:, r:r + 1], route[:, TOP_K + r:TOP_K + r + 1], 0.0)
    wm = wm.astype(BF)

    def drain(k, c):
        pltpu.make_async_copy(y_ref.at[pl.ds(0, 1)], g_ref.at[pl.ds(0, 1)], sem).wait()
        return c
    lax.fori_loop(0, tab_ref[0, 3 * n_exp], drain, 0)
    acc = jnp.dot(wm, g_ref[...].reshape(cap, D), preferred_element_type=F32)
    xn = x_ref[...] + mod_ref[5:6, :] * acc
    if final:
        xn = xn * lax.rsqrt(jnp.mean(xn * xn, axis=-1, keepdims=True) + EPS) * gf_ref[...]
    o_ref[...] = xn


def _combine_blocks(tab, gpos, x_mid, mods, route, g_final, y, dims, final, n_exp):
    rows = x_mid.shape[0]
    tps, nb = dims["tps"], dims["nb"]
    cap_blocks = (TM * TOP_K + n_exp * 2 * ROW_BLK) // ROW_BLK
    return pl.pallas_call(
        functools.partial(_combine_blocks_body, final=final, n_exp=n_exp),
        grid=(rows // TM,),
        in_specs=[pl.BlockSpec((None, 1, LANES), lambda i: (i, 0, 0), memory_space=pltpu.SMEM),
                  pl.BlockSpec((TM, D), lambda i: (i, 0)),
                  pl.BlockSpec((None, MOD_ROWS, D), lambda i: (jnp.minimum(i // tps, nb), 0, 0)),
                  pl.BlockSpec((TM, LANES), lambda i: (i, 0)),
                  pl.BlockSpec((TM, TOP_K), lambda i: (i, 0)),
                  pl.BlockSpec((1, D), lambda i: (0, 0)),
                  pl.BlockSpec(memory_space=pl.ANY)],
        out_specs=pl.BlockSpec((TM, D), lambda i: (i, 0)),
        out_shape=jax.ShapeDtypeStruct((rows, D), F32),
        scratch_shapes=[pltpu.VMEM((cap_blocks, ROW_BLK, D), BF), pltpu.SemaphoreType.DMA(())],
        compiler_params=_cparams(("arbitrary",), VMEM_LIMIT),
    )(tab, x_mid, mods, route, gpos, g_final, y.reshape(-1, ROW_BLK, D))


def _moe(x_mid, h2, route, counts, mods, g_final, layer, wg, bg, wu, bu, wd, bd, dims, final):
    rows = x_mid.shape[0]
    n_exp = wg.shape[1]
    n_slots = rows * TOP_K + n_exp * TME
    e = route[:, 0:TOP_K].astype(jnp.int32)
    rank = route[:, 2 * TOP_K:3 * TOP_K].astype(jnp.int32)
    cnt = counts[0, :n_exp].astype(jnp.int32)
    cnt_pad = ((cnt + TME - 1) // TME) * TME
    offs_end = jnp.cumsum(cnt_pad)
    offs = offs_end - cnt_pad
    onehot = e[:, :, None] == jnp.arange(n_exp, dtype=jnp.int32)[None, None, :]
    dest = jnp.sum(jnp.where(onehot, offs[None, None, :], 0), axis=-1) + rank
    tmv = max(t for t in (TM_MOVE, TM) if dims["s"] % t == 0 and dims["t_lat"] % t == 0 and rows % t == 0)
    dest3 = dest.reshape(rows // tmv, 1, tmv * TOP_K)
    tile_start = jnp.arange(n_slots // TME, dtype=jnp.int32) * TME
    tile_expert = jnp.minimum(jnp.sum(tile_start[:, None] >= offs_end[None, :], axis=1), n_exp - 1).astype(jnp.int32)
    n_active = (offs_end[-1:] // TME).astype(jnp.int32)
    n_valid = jnp.clip(cnt[tile_expert] - (tile_start - offs[tile_expert]), 0, TME).astype(jnp.int32)
    xs = _dispatch(dest3, h2, n_slots)
    y = _experts(tile_expert, n_active, n_valid, xs, layer, wg, bg, wu, bu, wd, bd)

    nt = rows // TM
    per_tile = jnp.sum(onehot.reshape(nt, TM * TOP_K, n_exp), axis=1).astype(jnp.int32)
    start = offs[None, :] + jnp.cumsum(per_tile, axis=0) - per_tile
    first_blk = start // ROW_BLK
    n_blk = jnp.where(per_tile > 0, (start + per_tile + ROW_BLK - 1) // ROW_BLK - first_blk, 0)
    g_blk = jnp.cumsum(n_blk, axis=1) - n_blk
    tab = jnp.concatenate([first_blk, n_blk, g_blk, jnp.sum(n_blk, axis=1, keepdims=True),
                           jnp.zeros((nt, LANES - 3 * n_exp - 1), jnp.int32)], axis=1).reshape(nt, 1, LANES)
    shift = (g_blk - first_blk) * ROW_BLK
    gpos = dest + jnp.sum(jnp.where(onehot.reshape(nt, TM, TOP_K, n_exp), shift[:, None, None, :], 0),
                          axis=-1).reshape(rows, TOP_K)
    return _combine_blocks(tab, gpos, x_mid, mods, route, g_final, y, dims, final, n_exp)


def _rope_tables(s, ctx):
    rows = s // GRID_W
    row = np.repeat(np.arange(rows), GRID_W).astype(np.float32)
    col = np.tile(np.arange(GRID_W), rows).astype(np.float32)
    half = HEAD_DIM // 2
    inv = (np.float32(ROPE_BASE) ** (-np.arange(0, half, 2, dtype=np.float32) / half)).astype(np.float32)
    ar = row[:, None] * inv
    ac = col[:, None] * inv
    ang = np.concatenate([ar, ar, ac, ac], axis=-1)
    cos = np.concatenate([np.cos(ang), np.ones((ctx, HEAD_DIM), np.float32)], axis=0)
    sin = np.concatenate([np.sin(ang), np.zeros((ctx, HEAD_DIM), np.float32)], axis=0)
    return (jnp.asarray(np.tile(cos, (1, 2)), F32), jnp.asarray(np.tile(sin, (1, 2)), F32))


def _rot_columns(w):
    d, n = w.shape
    w4 = w.reshape(d, n // HEAD_DIM, 4, HEAD_DIM // 4)
    return jnp.stack([-w4[:, :, 1], w4[:, :, 0], -w4[:, :, 3], w4[:, :, 2]], axis=2).reshape(d, n)


def kernel(x, c, ctx, c_ctx, w_ada, b_ada, g_norm1, w_in, da_lambda, g_subln, w_conv, w_pool, pool_scale,
           w_branch, w_mgate, b_mgate, w_out, g_norm2, w_router, b_router, w_e_gate, b_e_gate, w_e_up,
           b_e_up, w_e_down, b_e_down, g_final):
    nb, s, _ = x.shape
    nctx = ctx.shape[1]
    depth = w_in.shape[0]
    n_exp = w_router.shape[2]
    t_lat, t_ctx = nb * s, nb * nctx
    dims = dict(nb=nb, s=s, ctx=nctx, t_lat=t_lat, t_all=t_lat + t_ctx, nlat=t_lat // TM, tps=s // TM, tpc=nctx // TM)
    assert s % TK == 0 and s // TK >= 2 and s % (FFT_N2 * 8) == 0 and nctx % TM == 0 and nb < COND_ROWS
    assert TM == VT_CHUNK and nctx % VT_CHUNK == 0 and ATT_UNROLL % 2 == 0

    cos_t, sin_t = _rope_tables(s, nctx)
    tabs = _fft_tables(s, nctx)
    cond = jnp.concatenate([c, c_ctx[None, :], jnp.zeros((COND_ROWS - 1 - nb, D), F32)], axis=0)
    mods_all = _adaln(cond, w_ada, b_ada).reshape(depth, COND_ROWS, 6, D)
    mods_all = jnp.concatenate([mods_all, jnp.zeros((depth, COND_ROWS, MOD_ROWS - 6, D), F32)], axis=2)

    xa, xb = x.reshape(t_lat, D), ctx.reshape(t_ctx, D)
    for l in range(depth):
        last = l == depth - 1
        lam_init = 0.8 - 0.6 * math.exp(-0.3 * l)
        mods = mods_all[l]
        w_l = w_in[l]
        w_ext = jnp.concatenate([w_l, _rot_columns(w_l[:, :BW]), _rot_columns(w_l[:, BW:2 * BW])], axis=1).astype(BF)
        qt, k, vt, u, sb, fx, px = _inproj(xa, xb, mods, g_norm1[l][None, :], w_ext, cos_t, sin_t, dims)

        rows = t_lat if last else t_lat + t_ctx
        a = _attention(qt, k, vt, g_subln[l], da_lambda[l], lam_init, dims, latent=True, rows=rows)
        yf = _fourier_latent(fx, tabs, dims, rows)
        if not last:
            a = _attention(qt, k, vt, g_subln[l], da_lambda[l], lam_init, dims, latent=False, fill=a)
            yf = _fourier_ctx(fx, tabs, dims, yf)

        wconv = jnp.concatenate([w_conv[l], jnp.zeros((5, BW), F32)], axis=0)
        wr = jnp.concatenate([w_router[l], jnp.zeros((D, LANES - n_exp), F32)], axis=1).astype(BF)
        br = jnp.concatenate([b_router[l], jnp.full((LANES - n_exp,), -1e30, F32)])[None, :]
        x_mid, h2, route, counts = _merge(
            xa, xb, mods, g_norm1[l][None, :], a, u, sb, yf, px, wconv, w_pool[l].astype(BF), pool_scale[l][None, :],
            w_branch[l].astype(BF), w_mgate[l].astype(BF), b_mgate[l][None, :], w_out[l].astype(BF),
            g_norm2[l][None, :], wr, br, dims, rows)
        x_new = _moe(x_mid, h2, route, counts, mods, g_final[None, :], l,
                     w_e_gate, b_e_gate[:, :, None, :], w_e_up, b_e_up[:, :, None, :],
                     w_e_down, b_e_down[:, :, None, :], dims, final=last)
        if last:
            return x_new.reshape(nb, s, D)
        xa = xb = x_new
```

```python
import functools
import math

import numpy as np
import jax
import jax.numpy as jnp
from jax import lax
from jax.experimental import pallas as pl
from jax.experimental.pallas import tpu as pltpu

F32 = jnp.float32
BF = jnp.bfloat16

D = 1024
HEADS = 4
HEAD_DIM = 64
HEAD_W = 128
BW = 512
GRID_W = 64
EPS = 1e-6
ROPE_BASE = 10000.0
POOL_WINDOWS = (2, 4, 8, 16)
TOP_K = 4
SWIGLU_LIMIT = 7.0
SWIGLU_ALPHA = 1.702
FFT_N2 = 128

TM = 256
TME = 512
TM_MOVE = 1024
TQ = 512
TK = 512
ATT_UNROLL = 2
VT_CHUNK = 256
HALO = 16
LANES = 128
COND_ROWS = 16
MOD_ROWS = 8
ROW_BLK = 16
QSCALE = (HEAD_DIM ** -0.5) * math.log2(math.e)
EXP_HEADROOM = 64.0
NORM_MARGIN = 1.05
VMEM_LIMIT = 56 * 2 ** 20


def _cparams(sem, vmem=None):
    return pltpu.CompilerParams(dimension_semantics=sem, vmem_limit_bytes=vmem)


def _modnorm(x, g, shift, scale):
    y = x * lax.rsqrt(jnp.mean(x * x, axis=-1, keepdims=True) + EPS) * g
    return y * (1.0 + scale) + shift


def _sigmoid(x):
    return 1.0 / (1.0 + jnp.exp(-x))


def _adaln_body(c_ref, w_ref, b_ref, o_ref):
    c = c_ref[...]
    s = c * _sigmoid(c)
    o_ref[...] = jnp.dot(s.astype(BF), w_ref[...].astype(BF), preferred_element_type=F32) + b_ref[...]


def _adaln(c16, w_ada, b_ada):
    nl = w_ada.shape[0]
    tn = 1536
    return pl.pallas_call(
        _adaln_body,
        grid=(nl, 6 * D // tn),
        in_specs=[pl.BlockSpec((COND_ROWS, D), lambda l, j: (0, 0)),
                  pl.BlockSpec((None, D, tn), lambda l, j: (l, 0, j)),
                  pl.BlockSpec((None, 1, tn), lambda l, j: (l, 0, j))],
        out_specs=pl.BlockSpec((None, COND_ROWS, tn), lambda l, j: (l, 0, j)),
        out_shape=jax.ShapeDtypeStruct((nl, COND_ROWS, 6 * D), F32),
        compiler_params=_cparams(("arbitrary", "arbitrary"), VMEM_LIMIT),
    )(c16, w_ada, b_ada.reshape(nl, 1, 6 * D))


def _token_tile(xa_ref, xb_ref, n_a):
    return jnp.where(pl.program_id(0) < n_a, xa_ref[...], xb_ref[...])


def _token_specs(xa, xb):
    n_a, n_b = xa.shape[0] // TM, xb.shape[0] // TM
    return [pl.BlockSpec((TM, D), lambda i: (jnp.minimum(i, n_a - 1), 0)),
            pl.BlockSpec((TM, D), lambda i: (jnp.clip(i - n_a, 0, n_b - 1), 0))]


def _inproj_body(xa_ref, xb_ref, mod_ref, g_ref, w_ref, cos_ref, sin_ref,
                 q_ref, k_ref, v_ref, u_ref, sb_ref, fx_ref, px_ref, *, n_a):
    h = _modnorm(_token_tile(xa_ref, xb_ref, n_a), g_ref[...], mod_ref[0:1, :], mod_ref[1:2, :]).astype(BF)

    def proj(c0):
        return jnp.dot(h, w_ref[:, c0:c0 + BW], preferred_element_type=F32)

    cos = jnp.concatenate([cos_ref[...]] * 4, axis=1)
    sin = jnp.concatenate([sin_ref[...]] * 4, axis=1)
    q_ref[...] = ((proj(0) * cos + proj(8 * BW) * sin) * QSCALE).T.astype(BF)
    k_ref[...] = (proj(BW) * cos + proj(9 * BW) * sin).astype(BF)
    v_ref[...] = proj(2 * BW).T.astype(BF)
    u_ref[...] = (proj(5 * BW) * proj(3 * BW)).astype(BF)
    sb_ref[...] = proj(4 * BW).astype(BF)
    px_ref[...] = proj(7 * BW).astype(BF)
    fx_ref[...] = proj(6 * BW).astype(BF)


def _inproj(xa, xb, mods, g1, w_ext, cos_t, sin_t, dims):
    t_all, nlat, tps, tpc, nb = dims["t_all"], dims["nlat"], dims["tps"], dims["tpc"], dims["nb"]

    def tab_idx(i):
        return (jnp.where(i < nlat, i % tps, tps + (i - nlat) % tpc), 0)

    out = jax.ShapeDtypeStruct((t_all, BW), BF)
    row = pl.BlockSpec((TM, BW), lambda i: (i, 0))
    return pl.pallas_call(
        functools.partial(_inproj_body, n_a=xa.shape[0] // TM),
        grid=(t_all // TM,),
        in_specs=_token_specs(xa, xb) + [
                  pl.BlockSpec((None, MOD_ROWS, D), lambda i: (jnp.minimum(i // tps, nb), 0, 0)),
                  pl.BlockSpec((1, D), lambda i: (0, 0)),
                  pl.BlockSpec((D, 10 * BW), lambda i: (0, 0)),
                  pl.BlockSpec((TM, LANES), tab_idx),
                  pl.BlockSpec((TM, LANES), tab_idx)],
        out_specs=[pl.BlockSpec((BW, TM), lambda i: (0, i)), row,
                   pl.BlockSpec((None, BW, VT_CHUNK), lambda i: (i, 0, 0))] + [row] * 4,
        out_shape=[jax.ShapeDtypeStruct((BW, t_all), BF), out,
                   jax.ShapeDtypeStruct((t_all // VT_CHUNK, BW, VT_CHUNK), BF)] + [out] * 4,
        compiler_params=_cparams(("arbitrary",), VMEM_LIMIT),
    )(xa, xb, mods, g1, w_ext, cos_t, sin_t)


def _attn_body(*refs, n_ctx_chunks, n_lat_chunks, lam_init):
    if n_lat_chunks:
        qt_ref, kc_ref, vtc_ref, kl_ref, vtl_ref, g_ref, dl_ref, o_ref, acc_ref, s_buf, p_buf, l_ref, knorm_ref = refs
    else:
        qt_ref, kc_ref, vtc_ref, g_ref, dl_ref, _, o_ref, acc_ref = refs
    qt = qt_ref[...]
    tq = qt.shape[1]
    row = lax.broadcasted_iota(jnp.int32, qt.shape, 0)
    zero = jnp.zeros_like(qt)
    q2 = jnp.concatenate([jnp.where(row < HEAD_DIM, qt, zero), jnp.where(row >= HEAD_DIM, qt, zero)], axis=1)

    def scores(kc):
        s = jnp.dot(kc, q2, preferred_element_type=F32)
        return s, jnp.max(s, axis=0, keepdims=True)

    def softmax(s, smax, m, l):
        m_new = jnp.maximum(m, smax)
        p = jnp.exp2(s - m_new)
        alpha = jnp.exp2(m - m_new)
        return p.astype(BF), m_new, alpha * l + jnp.sum(p, axis=0, keepdims=True), alpha

    def accumulate(alpha, vt, p):
        acc_ref[...] = alpha * acc_ref[...] + jnp.dot(vt, p, preferred_element_type=F32)

    m = jnp.full((1, 2 * tq), -1e30, F32)
    l = jnp.zeros((1, 2 * tq), F32)
    acc_ref[...] = jnp.zeros_like(acc_ref)
    ctx_scores = [scores(kc_ref[c * VT_CHUNK:(c + 1) * VT_CHUNK, :]) for c in range(n_ctx_chunks)]
    if n_lat_chunks:
        per = TK // VT_CHUNK
        n = n_lat_chunks

        def keys(j):
            return kl_ref[pl.ds(pl.multiple_of(j * TK, TK), TK), :]

        s_buf[0] = jnp.dot(keys(0), q2, preferred_element_type=F32)
    for c in range(n_ctx_chunks):
        p, m, l, alpha = softmax(*ctx_scores[c], m, l)
        accumulate(alpha, vtc_ref[c], p)
    if n_lat_chunks:

        def values(j):
            parts = [vtl_ref[j * per + r] for r in range(per)]
            return parts[0] if per == 1 else jnp.concatenate(parts, axis=1)

        @pl.when(pl.program_id(2) == 0)
        def _():
            d_i = lax.broadcasted_iota(jnp.int32, (HEAD_W, LANES), 0)
            c_i = lax.broadcasted_iota(jnp.int32, (HEAD_W, LANES), 1)
            sel = jnp.where((d_i >= HEAD_DIM) == (c_i == 1), 1.0, 0.0) * jnp.where(c_i < 2, 1.0, 0.0)
            best = jnp.zeros((1, LANES), F32)
            for j in range(n):
                kf = keys(j).astype(F32)
                n2 = jnp.dot((kf * kf).astype(BF), sel.astype(BF), preferred_element_type=F32)
                best = jnp.maximum(best, jnp.max(n2, axis=0, keepdims=True))
            lane2 = lax.broadcasted_iota(jnp.int32, (1, 2 * tq), 1)
            knorm_ref[...] = jnp.sqrt(jnp.where(lane2 < tq, best[:, 0:1], best[:, 1:2]))

        qf = q2.astype(F32)
        bound = jnp.sqrt(jnp.sum(qf * qf, axis=0, keepdims=True)) * knorm_ref[...] * NORM_MARGIN
        gap = jnp.max(bound - m)
        l_ref[...] = l

        @pl.when(gap <= EXP_HEADROOM)
        def _():
            lsum = l
            s = s_buf[0]
            for j in range(n):
                s_next = jnp.dot(keys(j + 1), q2, preferred_element_type=F32) if j + 1 < n else None
                p = jnp.exp2(s - m)
                lsum = lsum + jnp.sum(p, axis=0, keepdims=True)
                acc_ref[...] += jnp.dot(values(j), p.astype(BF), preferred_element_type=F32)
                s = s_next
            l_ref[...] = lsum

        @pl.when(gap > EXP_HEADROOM)
        def _():
            _attn_exact_latent(m, l, n, keys, values, scores, softmax, accumulate, s_buf, p_buf, l_ref)

        l = l_ref[...]
    o = acc_ref[...] / l
    dl = dl_ref[...]
    lam = (jnp.exp(jnp.sum(dl[0:1] * dl[1:2], axis=1, keepdims=True))
           - jnp.exp(jnp.sum(dl[2:3] * dl[3:4], axis=1, keepdims=True)) + lam_init)
    d = o[:, :tq] - lam * o[:, tq:]
    y = d * lax.rsqrt(jnp.mean(d * d, axis=0, keepdims=True) + EPS) * g_ref[...] * (1.0 - lam_init)
    o_ref[...] = y.T.astype(BF)


def _attn_exact_latent(m, l, n, keys, values, scores, softmax, accumulate, s_buf, p_buf, l_ref):
    def half_step(j, prev, cur, carry):
        m, l, alpha_prev, smax = carry
        accumulate(alpha_prev, values(j - 1), p_buf[prev])
        if isinstance(j, int) and j + 1 >= n:
            smax_next = smax
        else:
            s_buf[prev], smax_next = scores(keys(jnp.minimum(j + 1, n - 1)))
        p, m, l, alpha = softmax(s_buf[cur], smax, m, l)
        p_buf[cur] = p
        return m, l, alpha, smax_next

    s0 = s_buf[0]
    p, m, l, alpha = softmax(s0, jnp.max(s0, axis=0, keepdims=True), m, l)
    p_buf[0] = p
    s_buf[1], smax = scores(keys(1))

    def body(i, carry):
        for r in range(ATT_UNROLL):
            carry = half_step(ATT_UNROLL * i + r + 1, r % 2, 1 - r % 2, carry)
        return carry

    n_body = (n - 1) // ATT_UNROLL
    carry = lax.fori_loop(0, n_body, body, (m, l, alpha, smax))
    for j in range(n_body * ATT_UNROLL + 1, n):
        carry = half_step(j, (j - 1) % 2, j % 2, carry)
    m, l, alpha, _ = carry
    accumulate(alpha, values(n - 1), p_buf[(n - 1) % 2])
    l_ref[...] = l


def _attention(qt, k, vt, g_sub, da_lambda, lam_init, dims, latent, rows=None, fill=None):
    nb, s, ctx, t_lat = dims["nb"], dims["s"], dims["ctx"], dims["t_lat"]
    cpb = ctx // VT_CHUNK
    tq = TQ if latent else min(TQ, ctx)
    g_sub_b = jnp.broadcast_to(g_sub[:, None], (HEAD_W, tq))
    common_tail = [pl.BlockSpec((HEAD_W, tq), lambda b, h, qi: (0, 0)),
                   pl.BlockSpec((4, HEAD_DIM), lambda b, h, qi: (0, 0))]
    kc_spec = pl.BlockSpec((ctx, HEAD_W), lambda b, h, qi: (t_lat // ctx + b, h))
    vtc_spec = pl.BlockSpec((cpb, HEAD_W, VT_CHUNK), lambda b, h, qi: (t_lat // ctx + b, h, 0))
    scratch = [pltpu.VMEM((HEAD_W, 2 * tq), F32)]
    if latent:
        nq = s // tq
        in_specs = [pl.BlockSpec((HEAD_W, tq), lambda b, h, qi: (h, b * nq + qi)), kc_spec, vtc_spec,
                    pl.BlockSpec((s, HEAD_W), lambda b, h, qi: (b, h)),
                    pl.BlockSpec((s // VT_CHUNK, HEAD_W, VT_CHUNK), lambda b, h, qi: (b, h, 0))] + common_tail
        args = (qt, k, vt, k, vt, g_sub_b, da_lambda)
        row0, n_lat_chunks, aliases = 0, s // TK, {}
        scratch += [pltpu.VMEM((2, TK, 2 * tq), F32), pltpu.VMEM((2, TK, 2 * tq), BF),
                    pltpu.VMEM((1, 2 * tq), F32), pltpu.VMEM((1, 2 * tq), F32)]
    else:
        nq = ctx // tq
        in_specs = [pl.BlockSpec((HEAD_W, tq), lambda b, h, qi: (h, t_lat // tq + b * nq + qi)),
                    kc_spec, vtc_spec] + common_tail
        in_specs.append(pl.BlockSpec(memory_space=pl.ANY))
        args = (qt, k, vt, g_sub_b, da_lambda, fill)
        row0, n_lat_chunks, rows, aliases = t_lat // tq, 0, fill.shape[0], {5: 0}
    return pl.pallas_call(
        functools.partial(_attn_body, n_ctx_chunks=cpb, n_lat_chunks=n_lat_chunks, lam_init=lam_init),
        grid=(nb, HEADS, nq),
        in_specs=in_specs,
        out_specs=pl.BlockSpec((tq, HEAD_W), lambda b, h, qi: (row0 + b * nq + qi, h)),
        out_shape=jax.ShapeDtypeStruct((rows, HEADS * HEAD_W), BF),
        input_output_aliases=aliases,
        scratch_shapes=scratch,
        compiler_params=_cparams(("arbitrary", "arbitrary", "arbitrary"), VMEM_LIMIT),
    )(*args)


def _channel_dft(x, cs):
    parts = [jnp.dot(x[:, g * LANES:(g + 1) * LANES], cs, preferred_element_type=F32).astype(BF)
             for g in range(x.shape[1] // LANES)]
    return jnp.concatenate([jnp.concatenate([pq[:, :LANES] for pq in parts], axis=1),
                            jnp.concatenate([pq[:, LANES:] for pq in parts], axis=1)], axis=0)


def _fft_a_body(fa_ref, cs_ref, x_ref, zr_ref, zi_ref):
    x = _channel_dft(x_ref[...], cs_ref[...])
    z = jnp.dot(fa_ref[...], x, preferred_element_type=F32)
    zr_ref[...] = z[:FFT_N2].astype(BF)
    zi_ref[...] = z[FFT_N2:].astype(BF)


def _fft_b_body(fb_ref, tc_ref, ts_ref, zr_ref, zi_ref, o_ref):
    nk = zr_ref.shape[0]
    parts_r, parts_i = [], []
    for j in range(nk):
        zr = zr_ref[j].astype(F32)
        zi = zi_ref[j].astype(F32)
        tc = jnp.concatenate([tc_ref[j]] * 4, axis=1)
        ts = jnp.concatenate([ts_ref[j]] * 4, axis=1)
        parts_r.append((zr * tc - zi * ts).astype(BF))
        parts_i.append((zr * ts + zi * tc).astype(BF))
    rhs = jnp.concatenate([jnp.concatenate(parts_r, axis=1), jnp.concatenate(parts_i, axis=1)], axis=0)
    o_ref[...] = jnp.dot(fb_ref[...], rhs, preferred_element_type=F32).astype(BF)


def _fft_ctx_body(f_ref, cs_ref, x_ref, _, o_ref):
    x = _channel_dft(x_ref[...], cs_ref[...])
    o_ref[...] = jnp.dot(f_ref[...], x, preferred_element_type=F32).astype(BF)


def _fft_tables(s, ctx):
    n1, n2 = s // FFT_N2, FFT_N2
    norm = 1.0 / math.sqrt(s * LANES)
    a = 2 * np.pi * np.outer(np.arange(n2), np.arange(n2)) / n2
    c, sn = np.cos(a), np.sin(a)
    fa = np.block([[c, -sn], [sn, c]]) * norm
    tw = 2 * np.pi * np.outer(np.arange(n2), np.arange(n1)) / s
    tc = np.repeat(np.cos(tw)[:, :, None], LANES, axis=2)
    ts = np.repeat(np.sin(tw)[:, :, None], LANES, axis=2)
    b = 2 * np.pi * np.outer(np.arange(n1), np.arange(n1)) / n1
    fb = np.concatenate([np.cos(b), -np.sin(b)], axis=1)
    ch = 2 * np.pi * np.outer(np.arange(LANES), np.arange(LANES)) / LANES
    cs = np.concatenate([np.cos(ch), np.sin(ch)], axis=1)
    cx = 2 * np.pi * np.outer(np.arange(ctx), np.arange(ctx)) / ctx
    fctx = np.concatenate([np.cos(cx), -np.sin(cx)], axis=1) / math.sqrt(ctx * LANES)
    return dict(fa=jnp.asarray(fa, BF), tc=jnp.asarray(tc, F32), ts=jnp.asarray(ts, F32),
                fb=jnp.asarray(fb, BF), cs=jnp.asarray(cs, BF), fctx=jnp.asarray(fctx, BF))


def _fourier_latent(fx, tabs, dims, rows):
    nb, s = dims["nb"], dims["s"]
    n1 = s // FFT_N2
    wide = n1 * BW
    wc = min(4096, wide)
    fx2 = fx.reshape(-1, wide)
    zshape = jax.ShapeDtypeStruct((nb * FFT_N2, wide), BF)
    zr, zi = pl.pallas_call(
        _fft_a_body,
        grid=(nb, wide // wc),
        in_specs=[pl.BlockSpec((2 * FFT_N2, 2 * FFT_N2), lambda b, j: (0, 0)),
                  pl.BlockSpec((LANES, 2 * LANES), lambda b, j: (0, 0)),
                  pl.BlockSpec((FFT_N2, wc), lambda b, j: (b, j))],
        out_specs=[pl.BlockSpec((FFT_N2, wc), lambda b, j: (b, j))] * 2,
        out_shape=[zshape, zshape],
        compiler_params=_cparams(("arbitrary", "arbitrary"), VMEM_LIMIT),
    )(tabs["fa"], tabs["cs"], fx2)
    kb = 8
    zr4 = zr.reshape(nb, FFT_N2, n1, BW)
    zi4 = zi.reshape(nb, FFT_N2, n1, BW)
    y = pl.pallas_call(
        _fft_b_body,
        grid=(FFT_N2 // kb, nb),
        in_specs=[pl.BlockSpec((n1, 2 * n1), lambda kk, b: (0, 0)),
                  pl.BlockSpec((kb, n1, LANES), lambda kk, b: (kk, 0, 0)),
                  pl.BlockSpec((kb, n1, LANES), lambda kk, b: (kk, 0, 0)),
                  pl.BlockSpec((None, kb, n1, BW), lambda kk, b: (b, kk, 0, 0)),
                  pl.BlockSpec((None, kb, n1, BW), lambda kk, b: (b, kk, 0, 0))],
        out_specs=pl.BlockSpec((n1, kb * BW), lambda kk, b: (b, kk)),
        out_shape=jax.ShapeDtypeStruct((rows // FFT_N2, FFT_N2 * BW), BF),
        compiler_params=_cparams(("arbitrary", "arbitrary"), VMEM_LIMIT),
    )(tabs["fb"], tabs["tc"], tabs["ts"], zr4, zi4)
    return y.reshape(rows, BW)


def _fourier_ctx(fx, tabs, dims, fill):
    nb, ctx, t_lat = dims["nb"], dims["ctx"], dims["t_lat"]
    return pl.pallas_call(
        _fft_ctx_body,
        grid=(nb,),
        in_specs=[pl.BlockSpec((ctx, 2 * ctx), lambda b: (0, 0)),
                  pl.BlockSpec((LANES, 2 * LANES), lambda b: (0, 0)),
                  pl.BlockSpec((ctx, BW), lambda b: (t_lat // ctx + b, 0)),
                  pl.BlockSpec(memory_space=pl.ANY)],
        out_specs=pl.BlockSpec((ctx, BW), lambda b: (t_lat // ctx + b, 0)),
        out_shape=jax.ShapeDtypeStruct(fill.shape, BF),
        input_output_aliases={3: 0},
        compiler_params=_cparams(("arbitrary",), VMEM_LIMIT),
    )(tabs["fctx"], tabs["cs"], fx, fill)


def _route_topk(h2b, wr_ref, br_ref):
    logits = jnp.dot(h2b, wr_ref[...], preferred_element_type=F32) + br_ref[...]
    lane = lax.broadcasted_iota(jnp.int32, (TM, LANES), 1).astype(F32)
    cur = logits
    vals, idxs = [], []
    for _ in range(TOP_K):
        mx = jnp.max(cur, axis=1, keepdims=True)
        idx = jnp.min(jnp.where(cur == mx, lane, float(LANES)), axis=1, keepdims=True)
        vals.append(mx)
        idxs.append(idx)
        cur = jnp.where(lane == idx, -jnp.inf, cur)
    ex = [jnp.exp(v - vals[0]) for v in vals]
    den = ex[0] + ex[1] + ex[2] + ex[3]
    member = jnp.zeros((TM, LANES), F32)
    for idx in idxs:
        member = member + jnp.where(lane == idx, 1.0, 0.0)
    return idxs, [e / den for e in ex], member


def _route_ranks(idxs, weights, member, carry_ref, live):
    lane = lax.broadcasted_iota(jnp.int32, (TM, LANES), 1).astype(F32)
    tri = (lax.broadcasted_iota(jnp.int32, (TM, TM), 0) > lax.broadcasted_iota(jnp.int32, (TM, TM), 1))
    before = jnp.dot(jnp.where(tri, 1.0, 0.0).astype(BF), member.astype(BF), preferred_element_type=F32)
    rank_all = carry_ref[0:1, :] + before
    route = jnp.zeros((TM, LANES), F32)
    for r in range(TOP_K):
        rank = jnp.sum(jnp.where(lane == idxs[r], rank_all, 0.0), axis=1, keepdims=True)
        route = (route + jnp.where(lane == float(r), idxs[r], 0.0)
                 + jnp.where(lane == float(TOP_K + r), weights[r], 0.0)
                 + jnp.where(lane == float(2 * TOP_K + r), rank, 0.0))
    carry_ref[...] = carry_ref[...] + live * jnp.sum(member, axis=0, keepdims=True)
    return route


def _merge_body(xa_ref, xb_ref, mod_ref, g1_ref, a_ref, u_ref, up_ref, un_ref, sb_ref, yf_ref, px_ref, pp_ref, pn_ref,
                wconv_ref, wpool_ref, pscale_ref, wb_ref, wg_ref, bg_ref, wo_ref, g2_ref, wr_ref, br_ref,
                xo_ref, h2_ref, route_ref, cnt_ref, carry_ref, h2p_ref, *, n_tiles, nlat, tps, tpc, s, ctx, n_a):
    step = pl.program_id(0)
    i = jnp.minimum(step, n_tiles - 1)
    is_lat = i < nlat
    j = jnp.where(is_lat, i % tps, (i - nlat) % tpc)
    first = j == 0
    last = j == jnp.where(is_lat, tps, tpc) - 1
    seq_len = jnp.where(is_lat, s, ctx)

    @pl.when(step == 0)
    def _():
        carry_ref[...] = jnp.zeros_like(carry_ref)
        h2p_ref[...] = jnp.zeros_like(h2p_ref)

    topk_prev = _route_topk(h2p_ref[...], wr_ref, br_ref)

    x = jnp.where(i < n_a, xa_ref[...], xb_ref[...])
    h = _modnorm(x, g1_ref[...], mod_ref[0:1, :], mod_ref[1:2, :]).astype(BF)
    rowi = lax.broadcasted_iota(jnp.int32, (TM, 1), 0)

    u = u_ref[...].astype(F32)
    u_prev = jnp.where(first, 0.0, up_ref[...].astype(F32)[HALO - 1:HALO, :])
    u_next = jnp.where(last, 0.0, un_ref[...].astype(F32)[0:1, :])
    u_dn = jnp.where(rowi == 0, u_prev, pltpu.roll(u, 1, axis=0))
    u_up = jnp.where(rowi == TM - 1, u_next, pltpu.roll(u, TM - 1, axis=0))
    wc = wconv_ref[...]
    y_conv = sb_ref[...].astype(F32) * (wc[0:1] * u_dn + wc[1:2] * u + wc[2:3] * u_up)

    px = px_ref[...].astype(F32)
    ext = jnp.concatenate([jnp.where(first, 0.0, pp_ref[...].astype(F32)), px,
                           jnp.where(last, 0.0, pn_ref[...].astype(F32))], axis=0)
    ext_len = TM + 2 * HALO
    pos = j * TM + rowi
    pooled = []
    for g, wd in enumerate(POOL_WINDOWS):
        e = ext[:, g * LANES:(g + 1) * LANES]
        win = e + pltpu.roll(e, 1, axis=0)
        half = 1
        while 2 * half < wd:
            win = pltpu.roll(win, half, axis=0) + pltpu.roll(win, ext_len - half, axis=0)
            half *= 2
        win = win[HALO:HALO + TM]
        cnt = (jnp.minimum(pos + wd // 2, seq_len) - jnp.maximum(pos - wd // 2, 0)).astype(F32)
        pooled.append((win / cnt - px[:, g * LANES:(g + 1) * LANES]).astype(BF))

    gates = [_sigmoid(jnp.dot(h, wg_ref[:, kk * D:(kk + 1) * D], preferred_element_type=F32)
                      + bg_ref[:, kk * D:(kk + 1) * D]) for kk in range(4)]

    def branch(kk, y):
        return gates[kk] * jnp.dot(y, wb_ref[kk], preferred_element_type=F32)

    acc = branch(0, a_ref[...]) + branch(2, yf_ref[...]) + branch(1, y_conv.astype(BF))
    y_pool = jnp.concatenate([jnp.dot(pooled[g], wpool_ref[g], preferred_element_type=F32)
                              for g in range(len(POOL_WINDOWS))], axis=1) * pscale_ref[...]
    acc = acc + branch(3, y_pool.astype(BF))

    xn = x + mod_ref[2:3, :] * jnp.dot(acc.astype(BF), wo_ref[...], preferred_element_type=F32)
    xo_ref[...] = xn
    h2 = _modnorm(xn, g2_ref[...], mod_ref[3:4, :], mod_ref[4:5, :])
    h2_ref[...] = h2
    h2p_ref[...] = h2.astype(BF)
    route_ref[...] = _route_ranks(*topk_prev, carry_ref, jnp.where(step > 0, 1.0, 0.0))
    cnt_ref[...] = carry_ref[...]


def _merge(xa, xb, mods, g1, a, u, sb, yf, px, wconv, wpool, pscale, wb, wg, bg, wo, g2, wr, br, dims, rows):
    nlat, tps, tpc, nb = dims["nlat"], dims["tps"], dims["tpc"], dims["nb"]
    n_tiles = rows // TM
    n_a, n_b = xa.shape[0] // TM, xb.shape[0] // TM
    nhalo = TM // HALO
    last_halo = u.shape[0] // HALO - 1
    cur = lambda i: jnp.minimum(i, n_tiles - 1)
    tile = lambda w: pl.BlockSpec((TM, w), lambda i: (cur(i), 0))
    prev = pl.BlockSpec((HALO, BW), lambda i: (jnp.maximum(cur(i) * nhalo - 1, 0), 0))
    nxt = pl.BlockSpec((HALO, BW), lambda i: (jnp.minimum((cur(i) + 1) * nhalo, last_halo), 0))
    const = lambda shape: pl.BlockSpec(shape, lambda i: (0,) * len(shape))
    return pl.pallas_call(
        functools.partial(_merge_body, n_tiles=n_tiles, nlat=nlat, tps=tps, tpc=tpc, s=dims["s"], ctx=dims["ctx"],
                          n_a=n_a),
        grid=(n_tiles + 1,),
        in_specs=[pl.BlockSpec((TM, D), lambda i: (jnp.minimum(cur(i), n_a - 1), 0)),
                  pl.BlockSpec((TM, D), lambda i: (jnp.clip(cur(i) - n_a, 0, n_b - 1), 0)),
                  pl.BlockSpec((None, MOD_ROWS, D), lambda i: (jnp.minimum(cur(i) // tps, nb), 0, 0)),
                  const((1, D)), tile(BW), tile(BW), prev, nxt, tile(BW), tile(BW), tile(BW), prev, nxt,
                  const((8, BW)), const((4, LANES, LANES)), const((1, BW)), const((4, BW, D)),
                  const((D, 4 * D)), const((1, 4 * D)), const((D, D)), const((1, D)),
                  const((D, LANES)), const((1, LANES))],
        out_specs=[tile(D), tile(D), pl.BlockSpec((TM, LANES), lambda i: (jnp.maximum(i - 1, 0), 0)),
                   pl.BlockSpec((8, LANES), lambda i: (0, 0))],
        out_shape=[jax.ShapeDtypeStruct((rows, D), F32), jax.ShapeDtypeStruct((rows, D), F32),
                   jax.ShapeDtypeStruct((rows, LANES), F32), jax.ShapeDtypeStruct((8, LANES), F32)],
        scratch_shapes=[pltpu.VMEM((8, LANES), F32), pltpu.VMEM((TM, D), BF)],
        compiler_params=_cparams(("arbitrary",), VMEM_LIMIT),
    )(xa, xb, mods, g1, a, u, u, u, sb, yf, px, px, px, wconv, wpool, pscale, wb, wg, bg, wo, g2, wr, br)


def _wait_row_copies(src_ref, dst_ref, sem, n):
    for _ in range(n):
        pltpu.make_async_copy(src_ref.at[pl.ds(0, 1)], dst_ref.at[pl.ds(0, 1)], sem).wait()


def _dispatch_body(dest_ref, h_ref, xs_ref, sem):
    def body(t, c):
        for r in range(TOP_K):
            pltpu.make_async_copy(h_ref.at[pl.ds(t, 1)], xs_ref.at[pl.ds(dest_ref[0, t * TOP_K + r], 1)],
                                  sem).start(priority=r % 2)
        return c
    lax.fori_loop(0, h_ref.shape[0], body, 0, unroll=8)
    _wait_row_copies(h_ref, xs_ref, sem, h_ref.shape[0] * TOP_K)


def _dispatch(dest3, h2, n_slots):
    rows = h2.shape[0]
    tmv = dest3.shape[2] // TOP_K
    return pl.pallas_call(
        _dispatch_body,
        grid=(rows // tmv,),
        in_specs=[pl.BlockSpec((None, 1, tmv * TOP_K), lambda i: (i, 0, 0), memory_space=pltpu.SMEM),
                  pl.BlockSpec((tmv, D), lambda i: (i, 0))],
        out_specs=pl.BlockSpec(memory_space=pl.ANY),
        out_shape=jax.ShapeDtypeStruct((n_slots, D), F32),
        scratch_shapes=[pltpu.SemaphoreType.DMA(())],
        compiler_params=_cparams(("arbitrary",), VMEM_LIMIT),
    )(dest3, h2)


def _expert_body(te_ref, na_ref, nv_ref, xs_ref, wg_ref, bg_ref, wu_ref, bu_ref, wd_ref, bd_ref, y_ref, wgb, wub, wdb):
    j = pl.program_id(0)

    @pl.when(j < na_ref[0])
    def _():
        @pl.when((j == 0) | (te_ref[j] != te_ref[jnp.maximum(j - 1, 0)]))
        def _():
            wgb[...] = wg_ref[...].astype(BF)
            wub[...] = wu_ref[...].astype(BF)
            wdb[...] = wd_ref[...].astype(BF)

        x = xs_ref[...].astype(BF)
        a = jnp.minimum(jnp.dot(x, wgb[...], preferred_element_type=F32) + bg_ref[...], SWIGLU_LIMIT)
        u = jnp.clip(jnp.dot(x, wub[...], preferred_element_type=F32) + bu_ref[...], -SWIGLU_LIMIT, SWIGLU_LIMIT)
        act = a * _sigmoid(SWIGLU_ALPHA * a) * (u + 1.0)
        y = jnp.dot(act.astype(BF), wdb[...], preferred_element_type=F32) + bd_ref[...]
        rowi = lax.broadcasted_iota(jnp.int32, (TME, 1), 0)
        y_ref[...] = jnp.where(rowi < nv_ref[j], y, 0.0).astype(BF)


def _experts(tile_expert, n_active, n_valid, xs, layer, wg, bg, wu, bu, wd, bd):
    n_slots = xs.shape[0]
    f = wg.shape[3]
    slot = lambda j, te, na, nv: (jnp.minimum(j, na[0] - 1), 0)
    wsel = lambda j, te, na, nv: (layer, te[j], 0, 0)
    return pl.pallas_call(
        _expert_body,
        grid_spec=pltpu.PrefetchScalarGridSpec(
            num_scalar_prefetch=3,
            grid=(n_slots // TME,),
            in_specs=[pl.BlockSpec((TME, D), slot),
                      pl.BlockSpec((None, None, D, f), wsel), pl.BlockSpec((None, None, 1, f), wsel),
                      pl.BlockSpec((None, None, D, f), wsel), pl.BlockSpec((None, None, 1, f), wsel),
                      pl.BlockSpec((None, None, f, D), wsel), pl.BlockSpec((None, None, 1, D), wsel)],
            out_specs=pl.BlockSpec((TME, D), slot),
            scratch_shapes=[pltpu.VMEM((D, f), BF), pltpu.VMEM((D, f), BF), pltpu.VMEM((f, D), BF)]),
        out_shape=jax.ShapeDtypeStruct((n_slots, D), BF),
        compiler_params=_cparams(("arbitrary",), VMEM_LIMIT),
    )(tile_expert, n_active, n_valid, xs, wg, bg, wu, bu, wd, bd)


def _combine_blocks_body(tab_ref, x_ref, mod_ref, route_ref, gpos_ref, gf_ref, y_ref, o_ref, g_ref, sem, *, final, n_exp):
    @pl.when(pl.program_id(0) == 0)
    def _():
        g_ref[...] = jnp.zeros_like(g_ref)

    for e in range(n_exp):
        first_blk, n_blk, g_blk = tab_ref[0, e], tab_ref[0, n_exp + e], tab_ref[0, 2 * n_exp + e]

        def fetch(k, c):
            pltpu.make_async_copy(y_ref.at[pl.ds(first_blk + k, 1)], g_ref.at[pl.ds(g_blk + k, 1)],
                                  sem).start(priority=e % 2)
            return c
        lax.fori_loop(0, n_blk, fetch, 0)

    cap = g_ref.shape[0] * g_ref.shape[1]
    route, gpos = route_ref[...], gpos_ref[...]
    col = lax.broadcasted_iota(jnp.int32, (TM, cap), 1)
    wm = jnp.zeros((TM, cap), F32)
    for r in range(TOP_K):
        wm = wm + jnp.where(col == gpos[:, r:r + 1], route[:, TOP_K + r:TOP_K + r + 1], 0.0)
    wm = wm.astype(BF)

    def drain(k, c):
        pltpu.make_async_copy(y_ref.at[pl.ds(0, 1)], g_ref.at[pl.ds(0, 1)], sem).wait()
        return c
    lax.fori_loop(0, tab_ref[0, 3 * n_exp], drain, 0)
    acc = jnp.dot(wm, g_ref[...].reshape(cap, D), preferred_element_type=F32)
    xn = x_ref[...] + mod_ref[5:6, :] * acc
    if final:
        xn = xn * lax.rsqrt(jnp.mean(xn * xn, axis=-1, keepdims=True) + EPS) * gf_ref[...]
    o_ref[...] = xn


def _combine_blocks(tab, gpos, x_mid, mods, route, g_final, y, dims, final, n_exp):
    rows = x_mid.shape[0]
    tps, nb = dims["tps"], dims["nb"]
    cap_blocks = (TM * TOP_K + n_exp * 2 * ROW_BLK) // ROW_BLK
    return pl.pallas_call(
        functools.partial(_combine_blocks_body, final=final, n_exp=n_exp),
        grid=(rows // TM,),
        in_specs=[pl.BlockSpec((None, 1, LANES), lambda i: (i, 0, 0), memory_space=pltpu.SMEM),
                  pl.BlockSpec((TM, D), lambda i: (i, 0)),
                  pl.BlockSpec((None, MOD_ROWS, D), lambda i: (jnp.minimum(i // tps, nb), 0, 0)),
                  pl.BlockSpec((TM, LANES), lambda i: (i, 0)),
                  pl.BlockSpec((TM, TOP_K), lambda i: (i, 0)),
                  pl.BlockSpec((1, D), lambda i: (0, 0)),
                  pl.BlockSpec(memory_space=pl.ANY)],
        out_specs=pl.BlockSpec((TM, D), lambda i: (i, 0)),
        out_shape=jax.ShapeDtypeStruct((rows, D), F32),
        scratch_shapes=[pltpu.VMEM((cap_blocks, ROW_BLK, D), BF), pltpu.SemaphoreType.DMA(())],
        compiler_params=_cparams(("arbitrary",), VMEM_LIMIT),
    )(tab, x_mid, mods, route, gpos, g_final, y.reshape(-1, ROW_BLK, D))


def _moe(x_mid, h2, route, counts, mods, g_final, layer, wg, bg, wu, bu, wd, bd, dims, final):
    rows = x_mid.shape[0]
    n_exp = wg.shape[1]
    n_slots = rows * TOP_K + n_exp * TME
    e = route[:, 0:TOP_K].astype(jnp.int32)
    rank = route[:, 2 * TOP_K:3 * TOP_K].astype(jnp.int32)
    cnt = counts[0, :n_exp].astype(jnp.int32)
    cnt_pad = ((cnt + TME - 1) // TME) * TME
    offs_end = jnp.cumsum(cnt_pad)
    offs = offs_end - cnt_pad
    onehot = e[:, :, None] == jnp.arange(n_exp, dtype=jnp.int32)[None, None, :]
    dest = jnp.sum(jnp.where(onehot, offs[None, None, :], 0), axis=-1) + rank
    tmv = max(t for t in (TM_MOVE, TM) if dims["s"] % t == 0 and dims["t_lat"] % t == 0 and rows % t == 0)
    dest3 = dest.reshape(rows // tmv, 1, tmv * TOP_K)
    tile_start = jnp.arange(n_slots // TME, dtype=jnp.int32) * TME
    tile_expert = jnp.minimum(jnp.sum(tile_start[:, None] >= offs_end[None, :], axis=1), n_exp - 1).astype(jnp.int32)
    n_active = (offs_end[-1:] // TME).astype(jnp.int32)
    n_valid = jnp.clip(cnt[tile_expert] - (tile_start - offs[tile_expert]), 0, TME).astype(jnp.int32)
    xs = _dispatch(dest3, h2, n_slots)
    y = _experts(tile_expert, n_active, n_valid, xs, layer, wg, bg, wu, bu, wd, bd)

    nt = rows // TM
    per_tile = jnp.sum(onehot.reshape(nt, TM * TOP_K, n_exp), axis=1).astype(jnp.int32)
    start = offs[None, :] + jnp.cumsum(per_tile, axis=0) - per_tile
    first_blk = start // ROW_BLK
    n_blk = jnp.where(per_tile > 0, (start + per_tile + ROW_BLK - 1) // ROW_BLK - first_blk, 0)
    g_blk = jnp.cumsum(n_blk, axis=1) - n_blk
    tab = jnp.concatenate([first_blk, n_blk, g_blk, jnp.sum(n_blk, axis=1, keepdims=True),
                           jnp.zeros((nt, LANES - 3 * n_exp - 1), jnp.int32)], axis=1).reshape(nt, 1, LANES)
    shift = (g_blk - first_blk) * ROW_BLK
    gpos = dest + jnp.sum(jnp.where(onehot.reshape(nt, TM, TOP_K, n_exp), shift[:, None, None, :], 0),
                          axis=-1).reshape(rows, TOP_K)
    return _combine_blocks(tab, gpos, x_mid, mods, route, g_final, y, dims, final, n_exp)


def _rope_tables(s, ctx):
    rows = s // GRID_W
    row = np.repeat(np.arange(rows), GRID_W).astype(np.float32)
    col = np.tile(np.arange(GRID_W), rows).astype(np.float32)
    half = HEAD_DIM // 2
    inv = (np.float32(ROPE_BASE) ** (-np.arange(0, half, 2, dtype=np.float32) / half)).astype(np.float32)
    ar = row[:, None] * inv
    ac = col[:, None] * inv
    ang = np.concatenate([ar, ar, ac, ac], axis=-1)
    cos = np.concatenate([np.cos(ang), np.ones((ctx, HEAD_DIM), np.float32)], axis=0)
    sin = np.concatenate([np.sin(ang), np.zeros((ctx, HEAD_DIM), np.float32)], axis=0)
    return (jnp.asarray(np.tile(cos, (1, 2)), F32), jnp.asarray(np.tile(sin, (1, 2)), F32))


def _rot_columns(w):
    d, n = w.shape
    w4 = w.reshape(d, n // HEAD_DIM, 4, HEAD_DIM // 4)
    return jnp.stack([-w4[:, :, 1], w4[:, :, 0], -w4[:, :, 3], w4[:, :, 2]], axis=2).reshape(d, n)


def kernel(x, c, ctx, c_ctx, w_ada, b_ada, g_norm1, w_in, da_lambda, g_subln, w_conv, w_pool, pool_scale,
           w_branch, w_mgate, b_mgate, w_out, g_norm2, w_router, b_router, w_e_gate, b_e_gate, w_e_up,
           b_e_up, w_e_down, b_e_down, g_final):
    nb, s, _ = x.shape
    nctx = ctx.shape[1]
    depth = w_in.shape[0]
    n_exp = w_router.shape[2]
    t_lat, t_ctx = nb * s, nb * nctx
    dims = dict(nb=nb, s=s, ctx=nctx, t_lat=t_lat, t_all=t_lat + t_ctx, nlat=t_lat // TM, tps=s // TM, tpc=nctx // TM)
    assert s % TK == 0 and s // TK >= 2 and s % (FFT_N2 * 8) == 0 and nctx % TM == 0 and nb < COND_ROWS
    assert TM == VT_CHUNK and nctx % VT_CHUNK == 0 and ATT_UNROLL % 2 == 0

    cos_t, sin_t = _rope_tables(s, nctx)
    tabs = _fft_tables(s, nctx)
    cond = jnp.concatenate([c, c_ctx[None, :], jnp.zeros((COND_ROWS - 1 - nb, D), F32)], axis=0)
    mods_all = _adaln(cond, w_ada, b_ada).reshape(depth, COND_ROWS, 6, D)
    mods_all = jnp.concatenate([mods_all, jnp.zeros((depth, COND_ROWS, MOD_ROWS - 6, D), F32)], axis=2)

    xa, xb = x.reshape(t_lat, D), ctx.reshape(t_ctx, D)
    for l in range(depth):
        last = l == depth - 1
        lam_init = 0.8 - 0.6 * math.exp(-0.3 * l)
        mods = mods_all[l]
        w_l = w_in[l]
        w_ext = jnp.concatenate([w_l, _rot_columns(w_l[:, :BW]), _rot_columns(w_l[:, BW:2 * BW])], axis=1).astype(BF)
        qt, k, vt, u, sb, fx, px = _inproj(xa, xb, mods, g_norm1[l][None, :], w_ext, cos_t, sin_t, dims)

        rows = t_lat if last else t_lat + t_ctx
        a = _attention(qt, k, vt, g_subln[l], da_lambda[l], lam_init, dims, latent=True, rows=rows)
        yf = _fourier_latent(fx, tabs, dims, rows)
        if not last:
            a = _attention(qt, k, vt, g_subln[l], da_lambda[l], lam_init, dims, latent=False, fill=a)
            yf = _fourier_ctx(fx, tabs, dims, yf)

        wconv = jnp.concatenate([w_conv[l], jnp.zeros((5, BW), F32)], axis=0)
        wr = jnp.concatenate([w_router[l], jnp.zeros((D, LANES - n_exp), F32)], axis=1).astype(BF)
        br = jnp.concatenate([b_router[l], jnp.full((LANES - n_exp,), -1e30, F32)])[None, :]
        x_mid, h2, route, counts = _merge(
            xa, xb, mods, g_norm1[l][None, :], a, u, sb, yf, px, wconv, w_pool[l].astype(BF), pool_scale[l][None, :],
            w_branch[l].astype(BF), w_mgate[l].astype(BF), b_mgate[l][None, :], w_out[l].astype(BF),
            g_norm2[l][None, :], wr, br, dims, rows)
        x_new = _moe(x_mid, h2, route, counts, mods, g_final[None, :], l,
                     w_e_gate, b_e_gate[:, :, None, :], w_e_up, b_e_up[:, :, None, :],
                     w_e_down, b_e_down[:, :, None, :], dims, final=last)
        if last:
            return x_new.reshape(nb, s, D)
        xa = xb = x_new
```

```python
import functools
import math

import numpy as np
import jax
import jax.numpy as jnp
from jax import lax
from jax.experimental import pallas as pl
from jax.experimental.pallas import tpu as pltpu

F32 = jnp.float32
BF = jnp.bfloat16

D = 1024
HEADS = 4
HEAD_DIM = 64
HEAD_W = 128
BW = 512
GRID_W = 64
EPS = 1e-6
ROPE_BASE = 10000.0
POOL_WINDOWS = (2, 4, 8, 16)
TOP_K = 4
SWIGLU_LIMIT = 7.0
SWIGLU_ALPHA = 1.702
FFT_N2 = 128

TM = 256
TME = 512
TM_MOVE = 1024
TQ = 512
TK = 512
ATT_UNROLL = 2
VT_CHUNK = 256
HALO = 16
LANES = 128
COND_ROWS = 16
MOD_ROWS = 8
ROW_BLK = 16
QSCALE = (HEAD_DIM ** -0.5) * math.log2(math.e)
EXP_HEADROOM = 64.0
NORM_MARGIN = 1.05
VMEM_LIMIT = 56 * 2 ** 20


def _cparams(sem, vmem=None):
    return pltpu.CompilerParams(dimension_semantics=sem, vmem_limit_bytes=vmem)


def _modnorm(x, g, shift, scale):
    y = x * lax.rsqrt(jnp.mean(x * x, axis=-1, keepdims=True) + EPS) * g
    return y * (1.0 + scale) + shift


def _sigmoid(x):
    return 1.0 / (1.0 + jnp.exp(-x))


def _adaln_body(c_ref, w_ref, b_ref, o_ref):
    c = c_ref[...]
    s = c * _sigmoid(c)
    o_ref[...] = jnp.dot(s.astype(BF), w_ref[...].astype(BF), preferred_element_type=F32) + b_ref[...]


def _adaln(c16, w_ada, b_ada):
    nl = w_ada.shape[0]
    tn = 1536
    return pl.pallas_call(
        _adaln_body,
        grid=(nl, 6 * D // tn),
        in_specs=[pl.BlockSpec((COND_ROWS, D), lambda l, j: (0, 0)),
                  pl.BlockSpec((None, D, tn), lambda l, j: (l, 0, j)),
                  pl.BlockSpec((None, 1, tn), lambda l, j: (l, 0, j))],
        out_specs=pl.BlockSpec((None, COND_ROWS, tn), lambda l, j: (l, 0, j)),
        out_shape=jax.ShapeDtypeStruct((nl, COND_ROWS, 6 * D), F32),
        compiler_params=_cparams(("arbitrary", "arbitrary"), VMEM_LIMIT),
    )(c16, w_ada, b_ada.reshape(nl, 1, 6 * D))


def _token_tile(xa_ref, xb_ref, n_a):
    return jnp.where(pl.program_id(0) < n_a, xa_ref[...], xb_ref[...])


def _token_specs(xa, xb):
    n_a, n_b = xa.shape[0] // TM, xb.shape[0] // TM
    return [pl.BlockSpec((TM, D), lambda i: (jnp.minimum(i, n_a - 1), 0)),
            pl.BlockSpec((TM, D), lambda i: (jnp.clip(i - n_a, 0, n_b - 1), 0))]


def _inproj_body(xa_ref, xb_ref, mod_ref, g_ref, w_ref, cos_ref, sin_ref,
                 q_ref, k_ref, v_ref, u_ref, sb_ref, fx_ref, px_ref, *, n_a):
    h = _modnorm(_token_tile(xa_ref, xb_ref, n_a), g_ref[...], mod_ref[0:1, :], mod_ref[1:2, :]).astype(BF)

    def proj(c0):
        return jnp.dot(h, w_ref[:, c0:c0 + BW], preferred_element_type=F32)

    cos = jnp.concatenate([cos_ref[...]] * 4, axis=1)
    sin = jnp.concatenate([sin_ref[...]] * 4, axis=1)
    q_ref[...] = ((proj(0) * cos + proj(8 * BW) * sin) * QSCALE).T.astype(BF)
    k_ref[...] = (proj(BW) * cos + proj(9 * BW) * sin).astype(BF)
    v_ref[...] = proj(2 * BW).T.astype(BF)
    u_ref[...] = (proj(5 * BW) * proj(3 * BW)).astype(BF)
    sb_ref[...] = proj(4 * BW).astype(BF)
    px_ref[...] = proj(7 * BW).astype(BF)
    fx_ref[...] = proj(6 * BW).astype(BF)


def _inproj(xa, xb, mods, g1, w_ext, cos_t, sin_t, dims):
    t_all, nlat, tps, tpc, nb = dims["t_all"], dims["nlat"], dims["tps"], dims["tpc"], dims["nb"]

    def tab_idx(i):
        return (jnp.where(i < nlat, i % tps, tps + (i - nlat) % tpc), 0)

    out = jax.ShapeDtypeStruct((t_all, BW), BF)
    row = pl.BlockSpec((TM, BW), lambda i: (i, 0))
    return pl.pallas_call(
        functools.partial(_inproj_body, n_a=xa.shape[0] // TM),
        grid=(t_all // TM,),
        in_specs=_token_specs(xa, xb) + [
                  pl.BlockSpec((None, MOD_ROWS, D), lambda i: (jnp.minimum(i // tps, nb), 0, 0)),
                  pl.BlockSpec((1, D), lambda i: (0, 0)),
                  pl.BlockSpec((D, 10 * BW), lambda i: (0, 0)),
                  pl.BlockSpec((TM, LANES), tab_idx),
                  pl.BlockSpec((TM, LANES), tab_idx)],
        out_specs=[pl.BlockSpec((BW, TM), lambda i: (0, i)), row,
                   pl.BlockSpec((None, BW, VT_CHUNK), lambda i: (i, 0, 0))] + [row] * 4,
        out_shape=[jax.ShapeDtypeStruct((BW, t_all), BF), out,
                   jax.ShapeDtypeStruct((t_all // VT_CHUNK, BW, VT_CHUNK), BF)] + [out] * 4,
        compiler_params=_cparams(("arbitrary",), VMEM_LIMIT),
    )(xa, xb, mods, g1, w_ext, cos_t, sin_t)


def _attn_body(*refs, n_ctx_chunks, n_lat_chunks, lam_init):
    if n_lat_chunks:
        qt_ref, kc_ref, vtc_ref, kl_ref, vtl_ref, g_ref, dl_ref, o_ref, acc_ref, s_buf, p_buf, l_ref, knorm_ref = refs
    else:
        qt_ref, kc_ref, vtc_ref, g_ref, dl_ref, _, o_ref, acc_ref = refs
    qt = qt_ref[...]
    tq = qt.shape[1]
    row = lax.broadcasted_iota(jnp.int32, qt.shape, 0)
    zero = jnp.zeros_like(qt)
    q2 = jnp.concatenate([jnp.where(row < HEAD_DIM, qt, zero), jnp.where(row >= HEAD_DIM, qt, zero)], axis=1)

    def scores(kc):
        s = jnp.dot(kc, q2, preferred_element_type=F32)
        return s, jnp.max(s, axis=0, keepdims=True)

    def softmax(s, smax, m, l):
        m_new = jnp.maximum(m, smax)
        p = jnp.exp2(s - m_new)
        alpha = jnp.exp2(m - m_new)
        return p.astype(BF), m_new, alpha * l + jnp.sum(p, axis=0, keepdims=True), alpha

    def accumulate(alpha, vt, p):
        acc_ref[...] = alpha * acc_ref[...] + jnp.dot(vt, p, preferred_element_type=F32)

    m = jnp.full((1, 2 * tq), -1e30, F32)
    l = jnp.zeros((1, 2 * tq), F32)
    acc_ref[...] = jnp.zeros_like(acc_ref)
    ctx_scores = [scores(kc_ref[c * VT_CHUNK:(c + 1) * VT_CHUNK, :]) for c in range(n_ctx_chunks)]
    if n_lat_chunks:
        per = TK // VT_CHUNK
        n = n_lat_chunks

        def keys(j):
            return kl_ref[pl.ds(pl.multiple_of(j * TK, TK), TK), :]

        s_buf[0] = jnp.dot(keys(0), q2, preferred_element_type=F32)
    for c in range(n_ctx_chunks):
        p, m, l, alpha = softmax(*ctx_scores[c], m, l)
        accumulate(alpha, vtc_ref[c], p)
    if n_lat_chunks:

        def values(j):
            parts = [vtl_ref[j * per + r] for r in range(per)]
            return parts[0] if per == 1 else jnp.concatenate(parts, axis=1)

        @pl.when(pl.program_id(2) == 0)
        def _():
            d_i = lax.broadcasted_iota(jnp.int32, (HEAD_W, LANES), 0)
            c_i = lax.broadcasted_iota(jnp.int32, (HEAD_W, LANES), 1)
            sel = jnp.where((d_i >= HEAD_DIM) == (c_i == 1), 1.0, 0.0) * jnp.where(c_i < 2, 1.0, 0.0)
            best = jnp.zeros((1, LANES), F32)
            for j in range(n):
                kf = keys(j).astype(F32)
                n2 = jnp.dot((kf * kf).astype(BF), sel.astype(BF), preferred_element_type=F32)
                best = jnp.maximum(best, jnp.max(n2, axis=0, keepdims=True))
            lane2 = lax.broadcasted_iota(jnp.int32, (1, 2 * tq), 1)
            knorm_ref[...] = jnp.sqrt(jnp.where(lane2 < tq, best[:, 0:1], best[:, 1:2]))

        qf = q2.astype(F32)
        bound = jnp.sqrt(jnp.sum(qf * qf, axis=0, keepdims=True)) * knorm_ref[...] * NORM_MARGIN
        gap = jnp.max(bound - m)
        l_ref[...] = l

        @pl.when(gap <= EXP_HEADROOM)
        def _():
            lsum = l
            s = s_buf[0]
            for j in range(n):
                s_next = jnp.dot(keys(j + 1), q2, preferred_element_type=F32) if j + 1 < n else None
                p = jnp.exp2(s - m)
                lsum = lsum + jnp.sum(p, axis=0, keepdims=True)
                acc_ref[...] += jnp.dot(values(j), p.astype(BF), preferred_element_type=F32)
                s = s_next
            l_ref[...] = lsum

        @pl.when(gap > EXP_HEADROOM)
        def _():
            _attn_exact_latent(m, l, n, keys, values, scores, softmax, accumulate, s_buf, p_buf, l_ref)

        l = l_ref[...]
    o = acc_ref[...] / l
    dl = dl_ref[...]
    lam = (jnp.exp(jnp.sum(dl[0:1] * dl[1:2], axis=1, keepdims=True))
           - jnp.exp(jnp.sum(dl[2:3] * dl[3:4], axis=1, keepdims=True)) + lam_init)
    d = o[:, :tq] - lam * o[:, tq:]
    y = d * lax.rsqrt(jnp.mean(d * d, axis=0, keepdims=True) + EPS) * g_ref[...] * (1.0 - lam_init)
    o_ref[...] = y.T.astype(BF)


def _attn_exact_latent(m, l, n, keys, values, scores, softmax, accumulate, s_buf, p_buf, l_ref):
    def half_step(j, prev, cur, carry):
        m, l, alpha_prev, smax = carry
        accumulate(alpha_prev, values(j - 1), p_buf[prev])
        if isinstance(j, int) and j + 1 >= n:
            smax_next = smax
        else:
            s_buf[prev], smax_next = scores(keys(jnp.minimum(j + 1, n - 1)))
        p, m, l, alpha = softmax(s_buf[cur], smax, m, l)
        p_buf[cur] = p
        return m, l, alpha, smax_next

    s0 = s_buf[0]
    p, m, l, alpha = softmax(s0, jnp.max(s0, axis=0, keepdims=True), m, l)
    p_buf[0] = p
    s_buf[1], smax = scores(keys(1))

    def body(i, carry):
        for r in range(ATT_UNROLL):
            carry = half_step(ATT_UNROLL * i + r + 1, r % 2, 1 - r % 2, carry)
        return carry

    n_body = (n - 1) // ATT_UNROLL
    carry = lax.fori_loop(0, n_body, body, (m, l, alpha, smax))
    for j in range(n_body * ATT_UNROLL + 1, n):
        carry = half_step(j, (j - 1) % 2, j % 2, carry)
    m, l, alpha, _ = carry
    accumulate(alpha, values(n - 1), p_buf[(n - 1) % 2])
    l_ref[...] = l


def _attention(qt, k, vt, g_sub, da_lambda, lam_init, dims, latent, rows=None, fill=None):
    nb, s, ctx, t_lat = dims["nb"], dims["s"], dims["ctx"], dims["t_lat"]
    cpb = ctx // VT_CHUNK
    tq = TQ if latent else min(TQ, ctx)
    g_sub_b = jnp.broadcast_to(g_sub[:, None], (HEAD_W, tq))
    common_tail = [pl.BlockSpec((HEAD_W, tq), lambda b, h, qi: (0, 0)),
                   pl.BlockSpec((4, HEAD_DIM), lambda b, h, qi: (0, 0))]
    kc_spec = pl.BlockSpec((ctx, HEAD_W), lambda b, h, qi: (t_lat // ctx + b, h))
    vtc_spec = pl.BlockSpec((cpb, HEAD_W, VT_CHUNK), lambda b, h, qi: (t_lat // ctx + b, h, 0))
    scratch = [pltpu.VMEM((HEAD_W, 2 * tq), F32)]
    if latent:
        nq = s // tq
        in_specs = [pl.BlockSpec((HEAD_W, tq), lambda b, h, qi: (h, b * nq + qi)), kc_spec, vtc_spec,
                    pl.BlockSpec((s, HEAD_W), lambda b, h, qi: (b, h)),
                    pl.BlockSpec((s // VT_CHUNK, HEAD_W, VT_CHUNK), lambda b, h, qi: (b, h, 0))] + common_tail
        args = (qt, k, vt, k, vt, g_sub_b, da_lambda)
        row0, n_lat_chunks, aliases = 0, s // TK, {}
        scratch += [pltpu.VMEM((2, TK, 2 * tq), F32), pltpu.VMEM((2, TK, 2 * tq), BF),
                    pltpu.VMEM((1, 2 * tq), F32), pltpu.VMEM((1, 2 * tq), F32)]
    else:
        nq = ctx // tq
        in_specs = [pl.BlockSpec((HEAD_W, tq), lambda b, h, qi: (h, t_lat // tq + b * nq + qi)),
                    kc_spec, vtc_spec] + common_tail
        in_specs.append(pl.BlockSpec(memory_space=pl.ANY))
        args = (qt, k, vt, g_sub_b, da_lambda, fill)
        row0, n_lat_chunks, rows, aliases = t_lat // tq, 0, fill.shape[0], {5: 0}
    return pl.pallas_call(
        functools.partial(_attn_body, n_ctx_chunks=cpb, n_lat_chunks=n_lat_chunks, lam_init=lam_init),
        grid=(nb, HEADS, nq),
        in_specs=in_specs,
        out_specs=pl.BlockSpec((tq, HEAD_W), lambda b, h, qi: (row0 + b * nq + qi, h)),
        out_shape=jax.ShapeDtypeStruct((rows, HEADS * HEAD_W), BF),
        input_output_aliases=aliases,
        scratch_shapes=scratch,
        compiler_params=_cparams(("arbitrary", "arbitrary", "arbitrary"), VMEM_LIMIT),
    )(*args)


def _channel_dft(x, cs):
    parts = [jnp.dot(x[:, g * LANES:(g + 1) * LANES], cs, preferred_element_type=F32).astype(BF)
             for g in range(x.shape[1] // LANES)]
    return jnp.concatenate([jnp.concatenate([pq[:, :LANES] for pq in parts], axis=1),
                            jnp.concatenate([pq[:, LANES:] for pq in parts], axis=1)], axis=0)


def _fft_a_body(fa_ref, cs_ref, x_ref, zr_ref, zi_ref):
    x = x_ref[...]
    x = _channel_dft(x.reshape(x.shape[0], x.shape[1] * x.shape[2]), cs_ref[...])
    z = jnp.dot(fa_ref[...], x, preferred_element_type=F32)
    zr_ref[...] = z[:FFT_N2].astype(BF).reshape(zr_ref.shape)
    zi_ref[...] = z[FFT_N2:].astype(BF).reshape(zi_ref.shape)


def _fft_b_body(fb_ref, tc_ref, ts_ref, zr_ref, zi_ref, o_ref):
    nk = zr_ref.shape[0]
    parts_r, parts_i = [], []
    for j in range(nk):
        zr = zr_ref[j].astype(F32)
        zi = zi_ref[j].astype(F32)
        tc = jnp.concatenate([tc_ref[j]] * 4, axis=1)
        ts = jnp.concatenate([ts_ref[j]] * 4, axis=1)
        parts_r.append((zr * tc - zi * ts).astype(BF))
        parts_i.append((zr * ts + zi * tc).astype(BF))
    rhs = jnp.concatenate([jnp.concatenate(parts_r, axis=1), jnp.concatenate(parts_i, axis=1)], axis=0)
    o_ref[...] = jnp.dot(fb_ref[...], rhs, preferred_element_type=F32).astype(BF).reshape(o_ref.shape)


def _fft_ctx_body(f_ref, cs_ref, x_ref, _, o_ref):
    x = _channel_dft(x_ref[...], cs_ref[...])
    o_ref[...] = jnp.dot(f_ref[...], x, preferred_element_type=F32).astype(BF)


def _fft_tables(s, ctx):
    n1, n2 = s // FFT_N2, FFT_N2
    norm = 1.0 / math.sqrt(s * LANES)
    a = 2 * np.pi * np.outer(np.arange(n2), np.arange(n2)) / n2
    c, sn = np.cos(a), np.sin(a)
    fa = np.block([[c, -sn], [sn, c]]) * norm
    tw = 2 * np.pi * np.outer(np.arange(n2), np.arange(n1)) / s
    tc = np.repeat(np.cos(tw)[:, :, None], LANES, axis=2)
    ts = np.repeat(np.sin(tw)[:, :, None], LANES, axis=2)
    b = 2 * np.pi * np.outer(np.arange(n1), np.arange(n1)) / n1
    fb = np.concatenate([np.cos(b), -np.sin(b)], axis=1)
    ch = 2 * np.pi * np.outer(np.arange(LANES), np.arange(LANES)) / LANES
    cs = np.concatenate([np.cos(ch), np.sin(ch)], axis=1)
    cx = 2 * np.pi * np.outer(np.arange(ctx), np.arange(ctx)) / ctx
    fctx = np.concatenate([np.cos(cx), -np.sin(cx)], axis=1) / math.sqrt(ctx * LANES)
    return dict(fa=jnp.asarray(fa, BF), tc=jnp.asarray(tc, F32), ts=jnp.asarray(ts, F32),
                fb=jnp.asarray(fb, BF), cs=jnp.asarray(cs, BF), fctx=jnp.asarray(fctx, BF))


def _fourier_latent(fx, tabs, dims, rows):
    nb, s = dims["nb"], dims["s"]
    n1 = s // FFT_N2
    wide = n1 * BW
    nj = min(16, n1)
    wc = nj * BW
    fx3 = fx.reshape(-1, n1, BW)
    zshape = jax.ShapeDtypeStruct((nb, FFT_N2, n1, BW), BF)
    zr4, zi4 = pl.pallas_call(
        _fft_a_body,
        grid=(nb, wide // wc),
        in_specs=[pl.BlockSpec((2 * FFT_N2, 2 * FFT_N2), lambda b, j: (0, 0)),
                  pl.BlockSpec((LANES, 2 * LANES), lambda b, j: (0, 0)),
                  pl.BlockSpec((FFT_N2, nj, BW), lambda b, j: (b, j, 0))],
        out_specs=[pl.BlockSpec((None, FFT_N2, nj, BW), lambda b, j: (b, 0, j, 0))] * 2,
        out_shape=[zshape, zshape],
        compiler_params=_cparams(("arbitrary", "arbitrary"), VMEM_LIMIT),
    )(tabs["fa"], tabs["cs"], fx3)
    kb = 16
    y = pl.pallas_call(
        _fft_b_body,
        grid=(FFT_N2 // kb, nb),
        in_specs=[pl.BlockSpec((n1, 2 * n1), lambda kk, b: (0, 0)),
                  pl.BlockSpec((kb, n1, LANES), lambda kk, b: (kk, 0, 0)),
                  pl.BlockSpec((kb, n1, LANES), lambda kk, b: (kk, 0, 0)),
                  pl.BlockSpec((None, kb, n1, BW), lambda kk, b: (b, kk, 0, 0)),
                  pl.BlockSpec((None, kb, n1, BW), lambda kk, b: (b, kk, 0, 0))],
        out_specs=pl.BlockSpec((n1, kb, BW), lambda kk, b: (b, kk, 0)),
        out_shape=jax.ShapeDtypeStruct((rows // FFT_N2, FFT_N2, BW), BF),
        compiler_params=_cparams(("arbitrary", "arbitrary"), VMEM_LIMIT),
    )(tabs["fb"], tabs["tc"], tabs["ts"], zr4, zi4)
    return y.reshape(rows, BW)


def _fourier_ctx(fx, tabs, dims, fill):
    nb, ctx, t_lat = dims["nb"], dims["ctx"], dims["t_lat"]
    return pl.pallas_call(
        _fft_ctx_body,
        grid=(nb,),
        in_specs=[pl.BlockSpec((ctx, 2 * ctx), lambda b: (0, 0)),
                  pl.BlockSpec((LANES, 2 * LANES), lambda b: (0, 0)),
                  pl.BlockSpec((ctx, BW), lambda b: (t_lat // ctx + b, 0)),
                  pl.BlockSpec(memory_space=pl.ANY)],
        out_specs=pl.BlockSpec((ctx, BW), lambda b: (t_lat // ctx + b, 0)),
        out_shape=jax.ShapeDtypeStruct(fill.shape, BF),
        input_output_aliases={3: 0},
        compiler_params=_cparams(("arbitrary",), VMEM_LIMIT),
    )(tabs["fctx"], tabs["cs"], fx, fill)


def _route_topk(h2b, wr_ref, br_ref):
    logits = jnp.dot(h2b, wr_ref[...], preferred_element_type=F32) + br_ref[...]
    lane = lax.broadcasted_iota(jnp.int32, (TM, LANES), 1).astype(F32)
    cur = logits
    vals, idxs = [], []
    for _ in range(TOP_K):
        mx = jnp.max(cur, axis=1, keepdims=True)
        idx = jnp.min(jnp.where(cur == mx, lane, float(LANES)), axis=1, keepdims=True)
        vals.append(mx)
        idxs.append(idx)
        cur = jnp.where(lane == idx, -jnp.inf, cur)
    ex = [jnp.exp(v - vals[0]) for v in vals]
    den = ex[0] + ex[1] + ex[2] + ex[3]
    member = jnp.zeros((TM, LANES), F32)
    for idx in idxs:
        member = member + jnp.where(lane == idx, 1.0, 0.0)
    return idxs, [e / den for e in ex], member


def _route_ranks(idxs, weights, member, carry_ref, live):
    lane = lax.broadcasted_iota(jnp.int32, (TM, LANES), 1).astype(F32)
    tri = (lax.broadcasted_iota(jnp.int32, (TM, TM), 0) > lax.broadcasted_iota(jnp.int32, (TM, TM), 1))
    before = jnp.dot(jnp.where(tri, 1.0, 0.0).astype(BF), member.astype(BF), preferred_element_type=F32)
    rank_all = carry_ref[0:1, :] + before
    route = jnp.zeros((TM, LANES), F32)
    for r in range(TOP_K):
        rank = jnp.sum(jnp.where(lane == idxs[r], rank_all, 0.0), axis=1, keepdims=True)
        route = (route + jnp.where(lane == float(r), idxs[r], 0.0)
                 + jnp.where(lane == float(TOP_K + r), weights[r], 0.0)
                 + jnp.where(lane == float(2 * TOP_K + r), rank, 0.0))
    carry_ref[...] = carry_ref[...] + live * jnp.sum(member, axis=0, keepdims=True)
    return route


def _merge_body(xa_ref, xb_ref, mod_ref, g1_ref, a_ref, u_ref, up_ref, un_ref, sb_ref, yf_ref, px_ref, pp_ref, pn_ref,
                wconv_ref, wpool_ref, pscale_ref, wb_ref, wg_ref, bg_ref, wo_ref, g2_ref, wr_ref, br_ref,
                xo_ref, h2_ref, route_ref, cnt_ref, carry_ref, h2p_ref, *, n_tiles, nlat, tps, tpc, s, ctx, n_a):
    step = pl.program_id(0)
    i = jnp.minimum(step, n_tiles - 1)
    is_lat = i < nlat
    j = jnp.where(is_lat, i % tps, (i - nlat) % tpc)
    first = j == 0
    last = j == jnp.where(is_lat, tps, tpc) - 1
    seq_len = jnp.where(is_lat, s, ctx)

    @pl.when(step == 0)
    def _():
        carry_ref[...] = jnp.zeros_like(carry_ref)
        h2p_ref[...] = jnp.zeros_like(h2p_ref)

    topk_prev = _route_topk(h2p_ref[...], wr_ref, br_ref)

    x = jnp.where(i < n_a, xa_ref[...], xb_ref[...])
    h = _modnorm(x, g1_ref[...], mod_ref[0:1, :], mod_ref[1:2, :]).astype(BF)
    rowi = lax.broadcasted_iota(jnp.int32, (TM, 1), 0)

    u = u_ref[...].astype(F32)
    u_prev = jnp.where(first, 0.0, up_ref[...].astype(F32)[HALO - 1:HALO, :])
    u_next = jnp.where(last, 0.0, un_ref[...].astype(F32)[0:1, :])
    u_dn = jnp.where(rowi == 0, u_prev, pltpu.roll(u, 1, axis=0))
    u_up = jnp.where(rowi == TM - 1, u_next, pltpu.roll(u, TM - 1, axis=0))
    wc = wconv_ref[...]
    y_conv = sb_ref[...].astype(F32) * (wc[0:1] * u_dn + wc[1:2] * u + wc[2:3] * u_up)

    px = px_ref[...].astype(F32)
    ext = jnp.concatenate([jnp.where(first, 0.0, pp_ref[...].astype(F32)), px,
                           jnp.where(last, 0.0, pn_ref[...].astype(F32))], axis=0)
    ext_len = TM + 2 * HALO
    pos = j * TM + rowi
    pooled = []
    for g, wd in enumerate(POOL_WINDOWS):
        e = ext[:, g * LANES:(g + 1) * LANES]
        win = e + pltpu.roll(e, 1, axis=0)
        half = 1
        while 2 * half < wd:
            win = pltpu.roll(win, half, axis=0) + pltpu.roll(win, ext_len - half, axis=0)
            half *= 2
        win = win[HALO:HALO + TM]
        cnt = (jnp.minimum(pos + wd // 2, seq_len) - jnp.maximum(pos - wd // 2, 0)).astype(F32)
        pooled.append((win / cnt - px[:, g * LANES:(g + 1) * LANES]).astype(BF))

    gates = [_sigmoid(jnp.dot(h, wg_ref[:, kk * D:(kk + 1) * D], preferred_element_type=F32)
                      + bg_ref[:, kk * D:(kk + 1) * D]) for kk in range(4)]

    def branch(kk, y):
        return gates[kk] * jnp.dot(y, wb_ref[kk], preferred_element_type=F32)

    acc = branch(0, a_ref[...]) + branch(2, yf_ref[...]) + branch(1, y_conv.astype(BF))
    y_pool = jnp.concatenate([jnp.dot(pooled[g], wpool_ref[g], preferred_element_type=F32)
                              for g in range(len(POOL_WINDOWS))], axis=1) * pscale_ref[...]
    acc = acc + branch(3, y_pool.astype(BF))

    xn = x + mod_ref[2:3, :] * jnp.dot(acc.astype(BF), wo_ref[...], preferred_element_type=F32)
    xo_ref[...] = xn
    h2 = _modnorm(xn, g2_ref[...], mod_ref[3:4, :], mod_ref[4:5, :])
    h2_ref[...] = h2
    h2p_ref[...] = h2.astype(BF)
    route_ref[...] = _route_ranks(*topk_prev, carry_ref, jnp.where(step > 0, 1.0, 0.0))
    cnt_ref[...] = carry_ref[...]


def _merge(xa, xb, mods, g1, a, u, sb, yf, px, wconv, wpool, pscale, wb, wg, bg, wo, g2, wr, br, dims, rows):
    nlat, tps, tpc, nb = dims["nlat"], dims["tps"], dims["tpc"], dims["nb"]
    n_tiles = rows // TM
    n_a, n_b = xa.shape[0] // TM, xb.shape[0] // TM
    nhalo = TM // HALO
    last_halo = u.shape[0] // HALO - 1
    cur = lambda i: jnp.minimum(i, n_tiles - 1)
    tile = lambda w: pl.BlockSpec((TM, w), lambda i: (cur(i), 0))
    prev = pl.BlockSpec((HALO, BW), lambda i: (jnp.maximum(cur(i) * nhalo - 1, 0), 0))
    nxt = pl.BlockSpec((HALO, BW), lambda i: (jnp.minimum((cur(i) + 1) * nhalo, last_halo), 0))
    const = lambda shape: pl.BlockSpec(shape, lambda i: (0,) * len(shape))
    return pl.pallas_call(
        functools.partial(_merge_body, n_tiles=n_tiles, nlat=nlat, tps=tps, tpc=tpc, s=dims["s"], ctx=dims["ctx"],
                          n_a=n_a),
        grid=(n_tiles + 1,),
        in_specs=[pl.BlockSpec((TM, D), lambda i: (jnp.minimum(cur(i), n_a - 1), 0)),
                  pl.BlockSpec((TM, D), lambda i: (jnp.clip(cur(i) - n_a, 0, n_b - 1), 0)),
                  pl.BlockSpec((None, MOD_ROWS, D), lambda i: (jnp.minimum(cur(i) // tps, nb), 0, 0)),
                  const((1, D)), tile(BW), tile(BW), prev, nxt, tile(BW), tile(BW), tile(BW), prev, nxt,
                  const((8, BW)), const((4, LANES, LANES)), const((1, BW)), const((4, BW, D)),
                  const((D, 4 * D)), const((1, 4 * D)), const((D, D)), const((1, D)),
                  const((D, LANES)), const((1, LANES))],
        out_specs=[tile(D), tile(D), pl.BlockSpec((TM, LANES), lambda i: (jnp.maximum(i - 1, 0), 0)),
                   pl.BlockSpec((8, LANES), lambda i: (0, 0))],
        out_shape=[jax.ShapeDtypeStruct((rows, D), F32), jax.ShapeDtypeStruct((rows, D), F32),
                   jax.ShapeDtypeStruct((rows, LANES), F32), jax.ShapeDtypeStruct((8, LANES), F32)],
        scratch_shapes=[pltpu.VMEM((8, LANES), F32), pltpu.VMEM((TM, D), BF)],
        compiler_params=_cparams(("arbitrary",), VMEM_LIMIT),
    )(xa, xb, mods, g1, a, u, u, u, sb, yf, px, px, px, wconv, wpool, pscale, wb, wg, bg, wo, g2, wr, br)


def _wait_row_copies(src_ref, dst_ref, sem, n):
    for _ in range(n):
        pltpu.make_async_copy(src_ref.at[pl.ds(0, 1)], dst_ref.at[pl.ds(0, 1)], sem).wait()


def _dispatch_body(dest_ref, h_ref, xs_ref, sem):
    def body(t, c):
        for r in range(TOP_K):
            pltpu.make_async_copy(h_ref.at[pl.ds(t, 1)], xs_ref.at[pl.ds(dest_ref[0, t * TOP_K + r], 1)],
                                  sem).start(priority=r % 2)
        return c
    lax.fori_loop(0, h_ref.shape[0], body, 0, unroll=8)
    _wait_row_copies(h_ref, xs_ref, sem, h_ref.shape[0] * TOP_K)


def _dispatch(dest3, h2, n_slots):
    rows = h2.shape[0]
    tmv = dest3.shape[2] // TOP_K
    return pl.pallas_call(
        _dispatch_body,
        grid=(rows // tmv,),
        in_specs=[pl.BlockSpec((None, 1, tmv * TOP_K), lambda i: (i, 0, 0), memory_space=pltpu.SMEM),
                  pl.BlockSpec((tmv, D), lambda i: (i, 0))],
        out_specs=pl.BlockSpec(memory_space=pl.ANY),
        out_shape=jax.ShapeDtypeStruct((n_slots, D), F32),
        scratch_shapes=[pltpu.SemaphoreType.DMA(())],
        compiler_params=_cparams(("arbitrary",), VMEM_LIMIT),
    )(dest3, h2)


def _expert_body(te_ref, na_ref, nv_ref, xs_ref, wg_ref, bg_ref, wu_ref, bu_ref, wd_ref, bd_ref, y_ref, wgb, wub, wdb):
    j = pl.program_id(0)

    @pl.when(j < na_ref[0])
    def _():
        @pl.when((j == 0) | (te_ref[j] != te_ref[jnp.maximum(j - 1, 0)]))
        def _():
            wgb[...] = wg_ref[...].astype(BF)
            wub[...] = wu_ref[...].astype(BF)
            wdb[...] = wd_ref[...].astype(BF)

        x = xs_ref[...].astype(BF)
        a = jnp.minimum(jnp.dot(x, wgb[...], preferred_element_type=F32) + bg_ref[...], SWIGLU_LIMIT)
        u = jnp.clip(jnp.dot(x, wub[...], preferred_element_type=F32) + bu_ref[...], -SWIGLU_LIMIT, SWIGLU_LIMIT)
        act = a * _sigmoid(SWIGLU_ALPHA * a) * (u + 1.0)
        y = jnp.dot(act.astype(BF), wdb[...], preferred_element_type=F32) + bd_ref[...]
        rowi = lax.broadcasted_iota(jnp.int32, (TME, 1), 0)
        y_ref[...] = jnp.where(rowi < nv_ref[j], y, 0.0).astype(BF)


def _experts(tile_expert, n_active, n_valid, xs, layer, wg, bg, wu, bu, wd, bd):
    n_slots = xs.shape[0]
    f = wg.shape[3]
    slot = lambda j, te, na, nv: (jnp.minimum(j, na[0] - 1), 0)
    wsel = lambda j, te, na, nv: (layer, te[j], 0, 0)
    return pl.pallas_call(
        _expert_body,
        grid_spec=pltpu.PrefetchScalarGridSpec(
            num_scalar_prefetch=3,
            grid=(n_slots // TME,),
            in_specs=[pl.BlockSpec((TME, D), slot),
                      pl.BlockSpec((None, None, D, f), wsel), pl.BlockSpec((None, None, 1, f), wsel),
                      pl.BlockSpec((None, None, D, f), wsel), pl.BlockSpec((None, None, 1, f), wsel),
                      pl.BlockSpec((None, None, f, D), wsel), pl.BlockSpec((None, None, 1, D), wsel)],
            out_specs=pl.BlockSpec((TME, D), slot),
            scratch_shapes=[pltpu.VMEM((D, f), BF), pltpu.VMEM((D, f), BF), pltpu.VMEM((f, D), BF)]),
        out_shape=jax.ShapeDtypeStruct((n_slots, D), BF),
        compiler_params=_cparams(("arbitrary",), VMEM_LIMIT),
    )(tile_expert, n_active, n_valid, xs, wg, bg, wu, bu, wd, bd)


def _combine_blocks_body(tab_ref, x_ref, mod_ref, route_ref, gpos_ref, gf_ref, y_ref, o_ref, g_ref, sem, *, final, n_exp):
    @pl.when(pl.program_id(0) == 0)
    def _():
        g_ref[...] = jnp.zeros_like(g_ref)

    for e in range(n_exp):
        first_blk, n_blk, g_blk = tab_ref[0, e], tab_ref[0, n_exp + e], tab_ref[0, 2 * n_exp + e]

        def fetch(k, c):
            pltpu.make_async_copy(y_ref.at[pl.ds(first_blk + k, 1)], g_ref.at[pl.ds(g_blk + k, 1)], sem).start()
            return c
        lax.fori_loop(0, n_blk, fetch, 0)

    cap = g_ref.shape[0] * g_ref.shape[1]
    route, gpos = route_ref[...], gpos_ref[...]
    col = lax.broadcasted_iota(jnp.int32, (TM, cap), 1)
    wm = jnp.zeros((TM, cap), F32)
    for r in range(TOP_K):
        wm = wm + jnp.where(col == gpos[:, r:r + 1], route[:, TOP_K + r:TOP_K + r + 1], 0.0)
    wm = wm.astype(BF)

    def drain(k, c):
        pltpu.make_async_copy(y_ref.at[pl.ds(0, 1)], g_ref.at[pl.ds(0, 1)], sem).wait()
        return c
    lax.fori_loop(0, tab_ref[0, 3 * n_exp], drain, 0)
    acc = jnp.dot(wm, g_ref[...].reshape(cap, D), preferred_element_type=F32)
    xn = x_ref[...] + mod_ref[5:6, :] * acc
    if final:
        xn = xn * lax.rsqrt(jnp.mean(xn * xn, axis=-1, keepdims=True) + EPS) * gf_ref[...]
    o_ref[...] = xn


def _combine_blocks(tab, gpos, x_mid, mods, route, g_final, y, dims, final, n_exp):
    rows = x_mid.shape[0]
    tps, nb = dims["tps"], dims["nb"]
    cap_blocks = (TM * TOP_K + n_exp * 2 * ROW_BLK) // ROW_BLK
    return pl.pallas_call(
        functools.partial(_combine_blocks_body, final=final, n_exp=n_exp),
        grid=(rows // TM,),
        in_specs=[pl.BlockSpec((None, 1, LANES), lambda i: (i, 0, 0), memory_space=pltpu.SMEM),
                  pl.BlockSpec((TM, D), lambda i: (i, 0)),
                  pl.BlockSpec((None, MOD_ROWS, D), lambda i: (jnp.minimum(i // tps, nb), 0, 0)),
                  pl.BlockSpec((TM, LANES), lambda i: (i, 0)),
                  pl.BlockSpec((TM, TOP_K), lambda i: (i, 0)),
                  pl.BlockSpec((1, D), lambda i: (0, 0)),
                  pl.BlockSpec(memory_space=pl.ANY)],
        out_specs=pl.BlockSpec((TM, D), lambda i: (i, 0)),
        out_shape=jax.ShapeDtypeStruct((rows, D), F32),
        scratch_shapes=[pltpu.VMEM((cap_blocks, ROW_BLK, D), BF), pltpu.SemaphoreType.DMA(())],
        compiler_params=_cparams(("arbitrary",), VMEM_LIMIT),
    )(tab, x_mid, mods, route, gpos, g_final, y.reshape(-1, ROW_BLK, D))


def _moe(x_mid, h2, route, counts, mods, g_final, layer, wg, bg, wu, bu, wd, bd, dims, final):
    rows = x_mid.shape[0]
    n_exp = wg.shape[1]
    n_slots = rows * TOP_K + n_exp * TME
    e = route[:, 0:TOP_K].astype(jnp.int32)
    rank = route[:, 2 * TOP_K:3 * TOP_K].astype(jnp.int32)
    cnt = counts[0, :n_exp].astype(jnp.int32)
    cnt_pad = ((cnt + TME - 1) // TME) * TME
    offs_end = jnp.cumsum(cnt_pad)
    offs = offs_end - cnt_pad
    onehot = e[:, :, None] == jnp.arange(n_exp, dtype=jnp.int32)[None, None, :]
    dest = jnp.sum(jnp.where(onehot, offs[None, None, :], 0), axis=-1) + rank
    tmv = max(t for t in (TM_MOVE, TM) if dims["s"] % t == 0 and dims["t_lat"] % t == 0 and rows % t == 0)
    dest3 = dest.reshape(rows // tmv, 1, tmv * TOP_K)
    tile_start = jnp.arange(n_slots // TME, dtype=jnp.int32) * TME
    tile_expert = jnp.minimum(jnp.sum(tile_start[:, None] >= offs_end[None, :], axis=1), n_exp - 1).astype(jnp.int32)
    n_active = (offs_end[-1:] // TME).astype(jnp.int32)
    n_valid = jnp.clip(cnt[tile_expert] - (tile_start - offs[tile_expert]), 0, TME).astype(jnp.int32)
    xs = _dispatch(dest3, h2, n_slots)
    y = _experts(tile_expert, n_active, n_valid, xs, layer, wg, bg, wu, bu, wd, bd)

    nt = rows // TM
    per_tile = jnp.sum(onehot.reshape(nt, TM * TOP_K, n_exp), axis=1).astype(jnp.int32)
    start = offs[None, :] + jnp.cumsum(per_tile, axis=0) - per_tile
    first_blk = start // ROW_BLK
    n_blk = jnp.where(per_tile > 0, (start + per_tile + ROW_BLK - 1) // ROW_BLK - first_blk, 0)
    g_blk = jnp.cumsum(n_blk, axis=1) - n_blk
    tab = jnp.concatenate([first_blk, n_blk, g_blk, jnp.sum(n_blk, axis=1, keepdims=True),
                           jnp.zeros((nt, LANES - 3 * n_exp - 1), jnp.int32)], axis=1).reshape(nt, 1, LANES)
    shift = (g_blk - first_blk) * ROW_BLK
    gpos = dest + jnp.sum(jnp.where(onehot.reshape(nt, TM, TOP_K, n_exp), shift[:, None, None, :], 0),
                          axis=-1).reshape(rows, TOP_K)
    return _combine_blocks(tab, gpos, x_mid, mods, route, g_final, y, dims, final, n_exp)


def _rope_tables(s, ctx):
    rows = s // GRID_W
    row = np.repeat(np.arange(rows), GRID_W).astype(np.float32)
    col = np.tile(np.arange(GRID_W), rows).astype(np.float32)
    half = HEAD_DIM // 2
    inv = (np.float32(ROPE_BASE) ** (-np.arange(0, half, 2, dtype=np.float32) / half)).astype(np.float32)
    ar = row[:, None] * inv
    ac = col[:, None] * inv
    ang = np.concatenate([ar, ar, ac, ac], axis=-1)
    cos = np.concatenate([np.cos(ang), np.ones((ctx, HEAD_DIM), np.float32)], axis=0)
    sin = np.concatenate([np.sin(ang), np.zeros((ctx, HEAD_DIM), np.float32)], axis=0)
    return (jnp.asarray(np.tile(cos, (1, 2)), F32), jnp.asarray(np.tile(sin, (1, 2)), F32))


def _rot_columns(w):
    d, n = w.shape
    w4 = w.reshape(d, n // HEAD_DIM, 4, HEAD_DIM // 4)
    return jnp.stack([-w4[:, :, 1], w4[:, :, 0], -w4[:, :, 3], w4[:, :, 2]], axis=2).reshape(d, n)


def kernel(x, c, ctx, c_ctx, w_ada, b_ada, g_norm1, w_in, da_lambda, g_subln, w_conv, w_pool, pool_scale,
           w_branch, w_mgate, b_mgate, w_out, g_norm2, w_router, b_router, w_e_gate, b_e_gate, w_e_up,
           b_e_up, w_e_down, b_e_down, g_final):
    nb, s, _ = x.shape
    nctx = ctx.shape[1]
    depth = w_in.shape[0]
    n_exp = w_router.shape[2]
    t_lat, t_ctx = nb * s, nb * nctx
    dims = dict(nb=nb, s=s, ctx=nctx, t_lat=t_lat, t_all=t_lat + t_ctx, nlat=t_lat // TM, tps=s // TM, tpc=nctx // TM)
    assert s % TK == 0 and s // TK >= 2 and s % (FFT_N2 * 8) == 0 and nctx % TM == 0 and nb < COND_ROWS
    assert TM == VT_CHUNK and nctx % VT_CHUNK == 0 and ATT_UNROLL % 2 == 0

    cos_t, sin_t = _rope_tables(s, nctx)
    tabs = _fft_tables(s, nctx)
    cond = jnp.concatenate([c, c_ctx[None, :], jnp.zeros((COND_ROWS - 1 - nb, D), F32)], axis=0)
    mods_all = _adaln(cond, w_ada, b_ada).reshape(depth, COND_ROWS, 6, D)
    mods_all = jnp.concatenate([mods_all, jnp.zeros((depth, COND_ROWS, MOD_ROWS - 6, D), F32)], axis=2)

    xa, xb = x.reshape(t_lat, D), ctx.reshape(t_ctx, D)
    for l in range(depth):
        last = l == depth - 1
        lam_init = 0.8 - 0.6 * math.exp(-0.3 * l)
        mods = mods_all[l]
        w_l = w_in[l]
        w_ext = jnp.concatenate([w_l, _rot_columns(w_l[:, :BW]), _rot_columns(w_l[:, BW:2 * BW])], axis=1).astype(BF)
        qt, k, vt, u, sb, fx, px = _inproj(xa, xb, mods, g_norm1[l][None, :], w_ext, cos_t, sin_t, dims)

        rows = t_lat if last else t_lat + t_ctx
        a = _attention(qt, k, vt, g_subln[l], da_lambda[l], lam_init, dims, latent=True, rows=rows)
        yf = _fourier_latent(fx, tabs, dims, rows)
        if not last:
            a = _attention(qt, k, vt, g_subln[l], da_lambda[l], lam_init, dims, latent=False, fill=a)
            yf = _fourier_ctx(fx, tabs, dims, yf)

        wconv = jnp.concatenate([w_conv[l], jnp.zeros((5, BW), F32)], axis=0)
        wr = jnp.concatenate([w_router[l], jnp.zeros((D, LANES - n_exp), F32)], axis=1).astype(BF)
        br = jnp.concatenate([b_router[l], jnp.full((LANES - n_exp,), -1e30, F32)])[None, :]
        x_mid, h2, route, counts = _merge(
            xa, xb, mods, g_norm1[l][None, :], a, u, sb, yf, px, wconv, w_pool[l].astype(BF), pool_scale[l][None, :],
            w_branch[l].astype(BF), w_mgate[l].astype(BF), b_mgate[l][None, :], w_out[l].astype(BF),
            g_norm2[l][None, :], wr, br, dims, rows)
        x_new = _moe(x_mid, h2, route, counts, mods, g_final[None, :], l,
                     w_e_gate, b_e_gate[:, :, None, :], w_e_up, b_e_up[:, :, None, :],
                     w_e_down, b_e_down[:, :, None, :], dims, final=last)
        if last:
            return x_new.reshape(nb, s, D)
        xa = xb = x_new
```

```python
import functools
import math

import numpy as np
import jax
import jax.numpy as jnp
from jax import lax
from jax.experimental import pallas as pl
from jax.experimental.pallas import tpu as pltpu

F32 = jnp.float32
BF = jnp.bfloat16

D = 1024
HEADS = 4
HEAD_DIM = 64
HEAD_W = 128
BW = 512
GRID_W = 64
EPS = 1e-6
ROPE_BASE = 10000.0
POOL_WINDOWS = (2, 4, 8, 16)
TOP_K = 4
SWIGLU_LIMIT = 7.0
SWIGLU_ALPHA = 1.702
FFT_N2 = 128

TM = 256
TME = 512
TM_MOVE = 1024
TQ = 512
TK = 512
ATT_UNROLL = 2
VT_CHUNK = 256
HALO = 16
LANES = 128
COND_ROWS = 16
MOD_ROWS = 8
ROW_TILE = (D // LANES, LANES)
ROW_BLK = 16
QSCALE = (HEAD_DIM ** -0.5) * math.log2(math.e)
EXP_HEADROOM = 64.0
NORM_MARGIN = 1.05
VMEM_LIMIT = 56 * 2 ** 20


def _cparams(sem, vmem=None):
    return pltpu.CompilerParams(dimension_semantics=sem, vmem_limit_bytes=vmem)


def _modnorm(x, g, shift, scale):
    y = x * lax.rsqrt(jnp.mean(x * x, axis=-1, keepdims=True) + EPS) * g
    return y * (1.0 + scale) + shift


def _sigmoid(x):
    return 1.0 / (1.0 + jnp.exp(-x))


def _adaln_body(c_ref, w_ref, b_ref, o_ref):
    c = c_ref[...]
    s = c * _sigmoid(c)
    o_ref[...] = jnp.dot(s.astype(BF), w_ref[...].astype(BF), preferred_element_type=F32) + b_ref[...]


def _adaln(c16, w_ada, b_ada):
    nl = w_ada.shape[0]
    tn = 1536
    return pl.pallas_call(
        _adaln_body,
        grid=(nl, 6 * D // tn),
        in_specs=[pl.BlockSpec((COND_ROWS, D), lambda l, j: (0, 0)),
                  pl.BlockSpec((None, D, tn), lambda l, j: (l, 0, j)),
                  pl.BlockSpec((None, 1, tn), lambda l, j: (l, 0, j))],
        out_specs=pl.BlockSpec((None, COND_ROWS, tn), lambda l, j: (l, 0, j)),
        out_shape=jax.ShapeDtypeStruct((nl, COND_ROWS, 6 * D), F32),
        compiler_params=_cparams(("arbitrary", "arbitrary"), VMEM_LIMIT),
    )(c16, w_ada, b_ada.reshape(nl, 1, 6 * D))


def _token_tile(xa_ref, xb_ref, n_a):
    return jnp.where(pl.program_id(0) < n_a, xa_ref[...], xb_ref[...])


def _token_specs(xa, xb):
    n_a, n_b = xa.shape[0] // TM, xb.shape[0] // TM
    return [pl.BlockSpec((TM, D), lambda i: (jnp.minimum(i, n_a - 1), 0)),
            pl.BlockSpec((TM, D), lambda i: (jnp.clip(i - n_a, 0, n_b - 1), 0))]


def _inproj_body(xa_ref, xb_ref, mod_ref, g_ref, w_ref, cos_ref, sin_ref,
                 q_ref, k_ref, v_ref, u_ref, sb_ref, fx_ref, px_ref, *, n_a):
    h = _modnorm(_token_tile(xa_ref, xb_ref, n_a), g_ref[...], mod_ref[0:1, :], mod_ref[1:2, :]).astype(BF)

    def proj(c0):
        return jnp.dot(h, w_ref[:, c0:c0 + BW], preferred_element_type=F32)

    cos = jnp.concatenate([cos_ref[...]] * 4, axis=1)
    sin = jnp.concatenate([sin_ref[...]] * 4, axis=1)
    q_ref[...] = ((proj(0) * cos + proj(8 * BW) * sin) * QSCALE).T.astype(BF)
    k_ref[...] = (proj(BW) * cos + proj(9 * BW) * sin).astype(BF)
    v_ref[...] = proj(2 * BW).T.astype(BF)
    u_ref[...] = (proj(5 * BW) * proj(3 * BW)).astype(BF)
    sb_ref[...] = proj(4 * BW).astype(BF)
    px_ref[...] = proj(7 * BW).astype(BF)
    fx_ref[...] = proj(6 * BW).astype(BF)


def _inproj(xa, xb, mods, g1, w_ext, cos_t, sin_t, dims):
    t_all, nlat, tps, tpc, nb = dims["t_all"], dims["nlat"], dims["tps"], dims["tpc"], dims["nb"]

    def tab_idx(i):
        return (jnp.where(i < nlat, i % tps, tps + (i - nlat) % tpc), 0)

    out = jax.ShapeDtypeStruct((t_all, BW), BF)
    row = pl.BlockSpec((TM, BW), lambda i: (i, 0))
    return pl.pallas_call(
        functools.partial(_inproj_body, n_a=xa.shape[0] // TM),
        grid=(t_all // TM,),
        in_specs=_token_specs(xa, xb) + [
                  pl.BlockSpec((None, MOD_ROWS, D), lambda i: (jnp.minimum(i // tps, nb), 0, 0)),
                  pl.BlockSpec((1, D), lambda i: (0, 0)),
                  pl.BlockSpec((D, 10 * BW), lambda i: (0, 0)),
                  pl.BlockSpec((TM, LANES), tab_idx),
                  pl.BlockSpec((TM, LANES), tab_idx)],
        out_specs=[pl.BlockSpec((BW, TM), lambda i: (0, i)), row,
                   pl.BlockSpec((None, BW, VT_CHUNK), lambda i: (i, 0, 0))] + [row] * 4,
        out_shape=[jax.ShapeDtypeStruct((BW, t_all), BF), out,
                   jax.ShapeDtypeStruct((t_all // VT_CHUNK, BW, VT_CHUNK), BF)] + [out] * 4,
        compiler_params=_cparams(("arbitrary",), VMEM_LIMIT),
    )(xa, xb, mods, g1, w_ext, cos_t, sin_t)


def _attn_body(*refs, n_ctx_chunks, n_lat_chunks, lam_init):
    if n_lat_chunks:
        qt_ref, kc_ref, vtc_ref, kl_ref, vtl_ref, g_ref, dl_ref, o_ref, acc_ref, s_buf, p_buf, l_ref, knorm_ref = refs
    else:
        qt_ref, kc_ref, vtc_ref, g_ref, dl_ref, _, o_ref, acc_ref = refs
    qt = qt_ref[...]
    tq = qt.shape[1]
    row = lax.broadcasted_iota(jnp.int32, qt.shape, 0)
    zero = jnp.zeros_like(qt)
    q2 = jnp.concatenate([jnp.where(row < HEAD_DIM, qt, zero), jnp.where(row >= HEAD_DIM, qt, zero)], axis=1)

    def scores(kc):
        s = jnp.dot(kc, q2, preferred_element_type=F32)
        return s, jnp.max(s, axis=0, keepdims=True)

    def softmax(s, smax, m, l):
        m_new = jnp.maximum(m, smax)
        p = jnp.exp2(s - m_new)
        alpha = jnp.exp2(m - m_new)
        return p.astype(BF), m_new, alpha * l + jnp.sum(p, axis=0, keepdims=True), alpha

    def accumulate(alpha, vt, p):
        acc_ref[...] = alpha * acc_ref[...] + jnp.dot(vt, p, preferred_element_type=F32)

    m = jnp.full((1, 2 * tq), -1e30, F32)
    l = jnp.zeros((1, 2 * tq), F32)
    acc_ref[...] = jnp.zeros_like(acc_ref)
    ctx_scores = [scores(kc_ref[c * VT_CHUNK:(c + 1) * VT_CHUNK, :]) for c in range(n_ctx_chunks)]
    if n_lat_chunks:
        per = TK // VT_CHUNK
        n = n_lat_chunks

        def keys(j):
            return kl_ref[pl.ds(pl.multiple_of(j * TK, TK), TK), :]

        s_buf[0] = jnp.dot(keys(0), q2, preferred_element_type=F32)
    for c in range(n_ctx_chunks):
        p, m, l, alpha = softmax(*ctx_scores[c], m, l)
        accumulate(alpha, vtc_ref[c], p)
    if n_lat_chunks:

        def values(j):
            parts = [vtl_ref[j * per + r] for r in range(per)]
            return parts[0] if per == 1 else jnp.concatenate(parts, axis=1)

        @pl.when(pl.program_id(2) == 0)
        def _():
            d_i = lax.broadcasted_iota(jnp.int32, (HEAD_W, LANES), 0)
            c_i = lax.broadcasted_iota(jnp.int32, (HEAD_W, LANES), 1)
            sel = jnp.where((d_i >= HEAD_DIM) == (c_i == 1), 1.0, 0.0) * jnp.where(c_i < 2, 1.0, 0.0)
            best = jnp.zeros((1, LANES), F32)
            for j in range(n):
                kf = keys(j).astype(F32)
                n2 = jnp.dot((kf * kf).astype(BF), sel.astype(BF), preferred_element_type=F32)
                best = jnp.maximum(best, jnp.max(n2, axis=0, keepdims=True))
            lane2 = lax.broadcasted_iota(jnp.int32, (1, 2 * tq), 1)
            knorm_ref[...] = jnp.sqrt(jnp.where(lane2 < tq, best[:, 0:1], best[:, 1:2]))

        qf = q2.astype(F32)
        bound = jnp.sqrt(jnp.sum(qf * qf, axis=0, keepdims=True)) * knorm_ref[...] * NORM_MARGIN
        gap = jnp.max(bound - m)
        l_ref[...] = l

        @pl.when(gap <= EXP_HEADROOM)
        def _():
            lsum = l
            s = s_buf[0]
            for j in range(n):
                s_next = jnp.dot(keys(j + 1), q2, preferred_element_type=F32) if j + 1 < n else None
                p = jnp.exp2(s - m)
                lsum = lsum + jnp.sum(p, axis=0, keepdims=True)
                acc_ref[...] += jnp.dot(values(j), p.astype(BF), preferred_element_type=F32)
                s = s_next
            l_ref[...] = lsum

        @pl.when(gap > EXP_HEADROOM)
        def _():
            _attn_exact_latent(m, l, n, keys, values, scores, softmax, accumulate, s_buf, p_buf, l_ref)

        l = l_ref[...]
    o = acc_ref[...] / l
    dl = dl_ref[...]
    lam = (jnp.exp(jnp.sum(dl[0:1] * dl[1:2], axis=1, keepdims=True))
           - jnp.exp(jnp.sum(dl[2:3] * dl[3:4], axis=1, keepdims=True)) + lam_init)
    d = o[:, :tq] - lam * o[:, tq:]
    y = d * lax.rsqrt(jnp.mean(d * d, axis=0, keepdims=True) + EPS) * g_ref[...] * (1.0 - lam_init)
    o_ref[...] = y.T.astype(BF)


def _attn_exact_latent(m, l, n, keys, values, scores, softmax, accumulate, s_buf, p_buf, l_ref):
    def half_step(j, prev, cur, carry):
        m, l, alpha_prev, smax = carry
        accumulate(alpha_prev, values(j - 1), p_buf[prev])
        if isinstance(j, int) and j + 1 >= n:
            smax_next = smax
        else:
            s_buf[prev], smax_next = scores(keys(jnp.minimum(j + 1, n - 1)))
        p, m, l, alpha = softmax(s_buf[cur], smax, m, l)
        p_buf[cur] = p
        return m, l, alpha, smax_next

    s0 = s_buf[0]
    p, m, l, alpha = softmax(s0, jnp.max(s0, axis=0, keepdims=True), m, l)
    p_buf[0] = p
    s_buf[1], smax = scores(keys(1))

    def body(i, carry):
        for r in range(ATT_UNROLL):
            carry = half_step(ATT_UNROLL * i + r + 1, r % 2, 1 - r % 2, carry)
        return carry

    n_body = (n - 1) // ATT_UNROLL
    carry = lax.fori_loop(0, n_body, body, (m, l, alpha, smax))
    for j in range(n_body * ATT_UNROLL + 1, n):
        carry = half_step(j, (j - 1) % 2, j % 2, carry)
    m, l, alpha, _ = carry
    accumulate(alpha, values(n - 1), p_buf[(n - 1) % 2])
    l_ref[...] = l


def _attention(qt, k, vt, g_sub, da_lambda, lam_init, dims, latent, rows=None, fill=None):
    nb, s, ctx, t_lat = dims["nb"], dims["s"], dims["ctx"], dims["t_lat"]
    cpb = ctx // VT_CHUNK
    tq = TQ if latent else min(TQ, ctx)
    g_sub_b = jnp.broadcast_to(g_sub[:, None], (HEAD_W, tq))
    common_tail = [pl.BlockSpec((HEAD_W, tq), lambda b, h, qi: (0, 0)),
                   pl.BlockSpec((4, HEAD_DIM), lambda b, h, qi: (0, 0))]
    kc_spec = pl.BlockSpec((ctx, HEAD_W), lambda b, h, qi: (t_lat // ctx + b, h))
    vtc_spec = pl.BlockSpec((cpb, HEAD_W, VT_CHUNK), lambda b, h, qi: (t_lat // ctx + b, h, 0))
    scratch = [pltpu.VMEM((HEAD_W, 2 * tq), F32)]
    if latent:
        nq = s // tq
        in_specs = [pl.BlockSpec((HEAD_W, tq), lambda b, h, qi: (h, b * nq + qi)), kc_spec, vtc_spec,
                    pl.BlockSpec((s, HEAD_W), lambda b, h, qi: (b, h)),
                    pl.BlockSpec((s // VT_CHUNK, HEAD_W, VT_CHUNK), lambda b, h, qi: (b, h, 0))] + common_tail
        args = (qt, k, vt, k, vt, g_sub_b, da_lambda)
        row0, n_lat_chunks, aliases = 0, s // TK, {}
        scratch += [pltpu.VMEM((2, TK, 2 * tq), F32), pltpu.VMEM((2, TK, 2 * tq), BF),
                    pltpu.VMEM((1, 2 * tq), F32), pltpu.VMEM((1, 2 * tq), F32)]
    else:
        nq = ctx // tq
        in_specs = [pl.BlockSpec((HEAD_W, tq), lambda b, h, qi: (h, t_lat // tq + b * nq + qi)),
                    kc_spec, vtc_spec] + common_tail
        in_specs.append(pl.BlockSpec(memory_space=pl.ANY))
        args = (qt, k, vt, g_sub_b, da_lambda, fill)
        row0, n_lat_chunks, rows, aliases = t_lat // tq, 0, fill.shape[0], {5: 0}
    return pl.pallas_call(
        functools.partial(_attn_body, n_ctx_chunks=cpb, n_lat_chunks=n_lat_chunks, lam_init=lam_init),
        grid=(nb, HEADS, nq),
        in_specs=in_specs,
        out_specs=pl.BlockSpec((tq, HEAD_W), lambda b, h, qi: (row0 + b * nq + qi, h)),
        out_shape=jax.ShapeDtypeStruct((rows, HEADS * HEAD_W), BF),
        input_output_aliases=aliases,
        scratch_shapes=scratch,
        compiler_params=_cparams(("arbitrary", "arbitrary", "arbitrary"), VMEM_LIMIT),
    )(*args)


def _channel_dft(x, cs):
    parts = [jnp.dot(x[:, g * LANES:(g + 1) * LANES], cs, preferred_element_type=F32).astype(BF)
             for g in range(x.shape[1] // LANES)]
    return jnp.concatenate([jnp.concatenate([pq[:, :LANES] for pq in parts], axis=1),
                            jnp.concatenate([pq[:, LANES:] for pq in parts], axis=1)], axis=0)


def _fft_a_body(fa_ref, cs_ref, x_ref, zr_ref, zi_ref):
    x = x_ref[...]
    x = _channel_dft(x.reshape(x.shape[0], x.shape[1] * x.shape[2]), cs_ref[...])
    z = jnp.dot(fa_ref[...], x, preferred_element_type=F32)
    zr_ref[...] = z[:FFT_N2].astype(BF).reshape(zr_ref.shape)
    zi_ref[...] = z[FFT_N2:].astype(BF).reshape(zi_ref.shape)


def _fft_b_body(fb_ref, tc_ref, ts_ref, zr_ref, zi_ref, o_ref):
    nk = zr_ref.shape[0]
    parts_r, parts_i = [], []
    for j in range(nk):
        zr = zr_ref[j].astype(F32)
        zi = zi_ref[j].astype(F32)
        tc = jnp.concatenate([tc_ref[j]] * 4, axis=1)
        ts = jnp.concatenate([ts_ref[j]] * 4, axis=1)
        parts_r.append((zr * tc - zi * ts).astype(BF))
        parts_i.append((zr * ts + zi * tc).astype(BF))
    rhs = jnp.concatenate([jnp.concatenate(parts_r, axis=1), jnp.concatenate(parts_i, axis=1)], axis=0)
    o_ref[...] = jnp.dot(fb_ref[...], rhs, preferred_element_type=F32).astype(BF).reshape(o_ref.shape)


def _fft_ctx_body(f_ref, cs_ref, x_ref, _, o_ref):
    x = _channel_dft(x_ref[...], cs_ref[...])
    o_ref[...] = jnp.dot(f_ref[...], x, preferred_element_type=F32).astype(BF)


def _fft_tables(s, ctx):
    n1, n2 = s // FFT_N2, FFT_N2
    norm = 1.0 / math.sqrt(s * LANES)
    a = 2 * np.pi * np.outer(np.arange(n2), np.arange(n2)) / n2
    c, sn = np.cos(a), np.sin(a)
    fa = np.block([[c, -sn], [sn, c]]) * norm
    tw = 2 * np.pi * np.outer(np.arange(n2), np.arange(n1)) / s
    tc = np.repeat(np.cos(tw)[:, :, None], LANES, axis=2)
    ts = np.repeat(np.sin(tw)[:, :, None], LANES, axis=2)
    b = 2 * np.pi * np.outer(np.arange(n1), np.arange(n1)) / n1
    fb = np.concatenate([np.cos(b), -np.sin(b)], axis=1)
    ch = 2 * np.pi * np.outer(np.arange(LANES), np.arange(LANES)) / LANES
    cs = np.concatenate([np.cos(ch), np.sin(ch)], axis=1)
    cx = 2 * np.pi * np.outer(np.arange(ctx), np.arange(ctx)) / ctx
    fctx = np.concatenate([np.cos(cx), -np.sin(cx)], axis=1) / math.sqrt(ctx * LANES)
    return dict(fa=jnp.asarray(fa, BF), tc=jnp.asarray(tc, F32), ts=jnp.asarray(ts, F32),
                fb=jnp.asarray(fb, BF), cs=jnp.asarray(cs, BF), fctx=jnp.asarray(fctx, BF))


def _fourier_latent(fx, tabs, dims, rows):
    nb, s = dims["nb"], dims["s"]
    n1 = s // FFT_N2
    wide = n1 * BW
    nj = min(16, n1)
    wc = nj * BW
    fx3 = fx.reshape(-1, n1, BW)
    zshape = jax.ShapeDtypeStruct((nb, FFT_N2, n1, BW), BF)
    zr4, zi4 = pl.pallas_call(
        _fft_a_body,
        grid=(nb, wide // wc),
        in_specs=[pl.BlockSpec((2 * FFT_N2, 2 * FFT_N2), lambda b, j: (0, 0)),
                  pl.BlockSpec((LANES, 2 * LANES), lambda b, j: (0, 0)),
                  pl.BlockSpec((FFT_N2, nj, BW), lambda b, j: (b, j, 0))],
        out_specs=[pl.BlockSpec((None, FFT_N2, nj, BW), lambda b, j: (b, 0, j, 0))] * 2,
        out_shape=[zshape, zshape],
        compiler_params=_cparams(("arbitrary", "arbitrary"), VMEM_LIMIT),
    )(tabs["fa"], tabs["cs"], fx3)
    kb = 16
    y = pl.pallas_call(
        _fft_b_body,
        grid=(FFT_N2 // kb, nb),
        in_specs=[pl.BlockSpec((n1, 2 * n1), lambda kk, b: (0, 0)),
                  pl.BlockSpec((kb, n1, LANES), lambda kk, b: (kk, 0, 0)),
                  pl.BlockSpec((kb, n1, LANES), lambda kk, b: (kk, 0, 0)),
                  pl.BlockSpec((None, kb, n1, BW), lambda kk, b: (b, kk, 0, 0)),
                  pl.BlockSpec((None, kb, n1, BW), lambda kk, b: (b, kk, 0, 0))],
        out_specs=pl.BlockSpec((n1, kb, BW), lambda kk, b: (b, kk, 0)),
        out_shape=jax.ShapeDtypeStruct((rows // FFT_N2, FFT_N2, BW), BF),
        compiler_params=_cparams(("arbitrary", "arbitrary"), VMEM_LIMIT),
    )(tabs["fb"], tabs["tc"], tabs["ts"], zr4, zi4)
    return y.reshape(rows, BW)


def _fourier_ctx(fx, tabs, dims, fill):
    nb, ctx, t_lat = dims["nb"], dims["ctx"], dims["t_lat"]
    return pl.pallas_call(
        _fft_ctx_body,
        grid=(nb,),
        in_specs=[pl.BlockSpec((ctx, 2 * ctx), lambda b: (0, 0)),
                  pl.BlockSpec((LANES, 2 * LANES), lambda b: (0, 0)),
                  pl.BlockSpec((ctx, BW), lambda b: (t_lat // ctx + b, 0)),
                  pl.BlockSpec(memory_space=pl.ANY)],
        out_specs=pl.BlockSpec((ctx, BW), lambda b: (t_lat // ctx + b, 0)),
        out_shape=jax.ShapeDtypeStruct(fill.shape, BF),
        input_output_aliases={3: 0},
        compiler_params=_cparams(("arbitrary",), VMEM_LIMIT),
    )(tabs["fctx"], tabs["cs"], fx, fill)


def _route_topk(h2b, wr_ref, br_ref):
    logits = jnp.dot(h2b, wr_ref[...], preferred_element_type=F32) + br_ref[...]
    lane = lax.broadcasted_iota(jnp.int32, (TM, LANES), 1).astype(F32)
    cur = logits
    vals, idxs = [], []
    for _ in range(TOP_K):
        mx = jnp.max(cur, axis=1, keepdims=True)
        idx = jnp.min(jnp.where(cur == mx, lane, float(LANES)), axis=1, keepdims=True)
        vals.append(mx)
        idxs.append(idx)
        cur = jnp.where(lane == idx, -jnp.inf, cur)
    ex = [jnp.exp(v - vals[0]) for v in vals]
    den = ex[0] + ex[1] + ex[2] + ex[3]
    member = jnp.zeros((TM, LANES), F32)
    for idx in idxs:
        member = member + jnp.where(lane == idx, 1.0, 0.0)
    return idxs, [e / den for e in ex], member


def _route_ranks(idxs, weights, member, carry_ref, live):
    lane = lax.broadcasted_iota(jnp.int32, (TM, LANES), 1).astype(F32)
    tri = (lax.broadcasted_iota(jnp.int32, (TM, TM), 0) > lax.broadcasted_iota(jnp.int32, (TM, TM), 1))
    before = jnp.dot(jnp.where(tri, 1.0, 0.0).astype(BF), member.astype(BF), preferred_element_type=F32)
    rank_all = carry_ref[0:1, :] + before
    route = jnp.zeros((TM, LANES), F32)
    for r in range(TOP_K):
        rank = jnp.sum(jnp.where(lane == idxs[r], rank_all, 0.0), axis=1, keepdims=True)
        route = (route + jnp.where(lane == float(r), idxs[r], 0.0)
                 + jnp.where(lane == float(TOP_K + r), weights[r], 0.0)
                 + jnp.where(lane == float(2 * TOP_K + r), rank, 0.0))
    carry_ref[...] = carry_ref[...] + live * jnp.sum(member, axis=0, keepdims=True)
    return route


def _merge_body(xa_ref, xb_ref, mod_ref, g1_ref, a_ref, u_ref, up_ref, un_ref, sb_ref, yf_ref, px_ref, pp_ref, pn_ref,
                wconv_ref, wpool_ref, pscale_ref, wb_ref, wg_ref, bg_ref, wo_ref, g2_ref, wr_ref, br_ref,
                xo_ref, h2_ref, route_ref, cnt_ref, carry_ref, h2p_ref, *, n_tiles, nlat, tps, tpc, s, ctx, n_a):
    step = pl.program_id(0)
    i = jnp.minimum(step, n_tiles - 1)
    is_lat = i < nlat
    j = jnp.where(is_lat, i % tps, (i - nlat) % tpc)
    first = j == 0
    last = j == jnp.where(is_lat, tps, tpc) - 1
    seq_len = jnp.where(is_lat, s, ctx)

    @pl.when(step == 0)
    def _():
        carry_ref[...] = jnp.zeros_like(carry_ref)
        h2p_ref[...] = jnp.zeros_like(h2p_ref)

    topk_prev = _route_topk(h2p_ref[...], wr_ref, br_ref)

    x = jnp.where(i < n_a, xa_ref[...], xb_ref[...])
    h = _modnorm(x, g1_ref[...], mod_ref[0:1, :], mod_ref[1:2, :]).astype(BF)
    rowi = lax.broadcasted_iota(jnp.int32, (TM, 1), 0)

    u = u_ref[...].astype(F32)
    u_prev = jnp.where(first, 0.0, up_ref[...].astype(F32)[HALO - 1:HALO, :])
    u_next = jnp.where(last, 0.0, un_ref[...].astype(F32)[0:1, :])
    u_dn = jnp.where(rowi == 0, u_prev, pltpu.roll(u, 1, axis=0))
    u_up = jnp.where(rowi == TM - 1, u_next, pltpu.roll(u, TM - 1, axis=0))
    wc = wconv_ref[...]
    y_conv = sb_ref[...].astype(F32) * (wc[0:1] * u_dn + wc[1:2] * u + wc[2:3] * u_up)

    px = px_ref[...].astype(F32)
    ext = jnp.concatenate([jnp.where(first, 0.0, pp_ref[...].astype(F32)), px,
                           jnp.where(last, 0.0, pn_ref[...].astype(F32))], axis=0)
    ext_len = TM + 2 * HALO
    pos = j * TM + rowi
    pooled = []
    for g, wd in enumerate(POOL_WINDOWS):
        e = ext[:, g * LANES:(g + 1) * LANES]
        win = e + pltpu.roll(e, 1, axis=0)
        half = 1
        while 2 * half < wd:
            win = pltpu.roll(win, half, axis=0) + pltpu.roll(win, ext_len - half, axis=0)
            half *= 2
        win = win[HALO:HALO + TM]
        cnt = (jnp.minimum(pos + wd // 2, seq_len) - jnp.maximum(pos - wd // 2, 0)).astype(F32)
        pooled.append((win / cnt - px[:, g * LANES:(g + 1) * LANES]).astype(BF))

    gates = [_sigmoid(jnp.dot(h, wg_ref[:, kk * D:(kk + 1) * D], preferred_element_type=F32)
                      + bg_ref[:, kk * D:(kk + 1) * D]) for kk in range(4)]

    def branch(kk, y):
        return gates[kk] * jnp.dot(y, wb_ref[kk], preferred_element_type=F32)

    acc = branch(0, a_ref[...]) + branch(2, yf_ref[...]) + branch(1, y_conv.astype(BF))
    y_pool = jnp.concatenate([jnp.dot(pooled[g], wpool_ref[g], preferred_element_type=F32)
                              for g in range(len(POOL_WINDOWS))], axis=1) * pscale_ref[...]
    acc = acc + branch(3, y_pool.astype(BF))

    xn = x + mod_ref[2:3, :] * jnp.dot(acc.astype(BF), wo_ref[...], preferred_element_type=F32)
    xo_ref[...] = xn
    h2 = _modnorm(xn, g2_ref[...], mod_ref[3:4, :], mod_ref[4:5, :])
    h2_ref[...] = h2.reshape(h2_ref.shape)
    h2p_ref[...] = h2.astype(BF)
    route_ref[...] = _route_ranks(*topk_prev, carry_ref, jnp.where(step > 0, 1.0, 0.0))
    cnt_ref[...] = carry_ref[...]


def _merge(xa, xb, mods, g1, a, u, sb, yf, px, wconv, wpool, pscale, wb, wg, bg, wo, g2, wr, br, dims, rows):
    nlat, tps, tpc, nb = dims["nlat"], dims["tps"], dims["tpc"], dims["nb"]
    n_tiles = rows // TM
    n_a, n_b = xa.shape[0] // TM, xb.shape[0] // TM
    nhalo = TM // HALO
    last_halo = u.shape[0] // HALO - 1
    cur = lambda i: jnp.minimum(i, n_tiles - 1)
    tile = lambda w: pl.BlockSpec((TM, w), lambda i: (cur(i), 0))
    prev = pl.BlockSpec((HALO, BW), lambda i: (jnp.maximum(cur(i) * nhalo - 1, 0), 0))
    nxt = pl.BlockSpec((HALO, BW), lambda i: (jnp.minimum((cur(i) + 1) * nhalo, last_halo), 0))
    const = lambda shape: pl.BlockSpec(shape, lambda i: (0,) * len(shape))
    return pl.pallas_call(
        functools.partial(_merge_body, n_tiles=n_tiles, nlat=nlat, tps=tps, tpc=tpc, s=dims["s"], ctx=dims["ctx"],
                          n_a=n_a),
        grid=(n_tiles + 1,),
        in_specs=[pl.BlockSpec((TM, D), lambda i: (jnp.minimum(cur(i), n_a - 1), 0)),
                  pl.BlockSpec((TM, D), lambda i: (jnp.clip(cur(i) - n_a, 0, n_b - 1), 0)),
                  pl.BlockSpec((None, MOD_ROWS, D), lambda i: (jnp.minimum(cur(i) // tps, nb), 0, 0)),
                  const((1, D)), tile(BW), tile(BW), prev, nxt, tile(BW), tile(BW), tile(BW), prev, nxt,
                  const((8, BW)), const((4, LANES, LANES)), const((1, BW)), const((4, BW, D)),
                  const((D, 4 * D)), const((1, 4 * D)), const((D, D)), const((1, D)),
                  const((D, LANES)), const((1, LANES))],
        out_specs=[tile(D), pl.BlockSpec((TM,) + ROW_TILE, lambda i: (cur(i), 0, 0)),
                   pl.BlockSpec((TM, LANES), lambda i: (jnp.maximum(i - 1, 0), 0)),
                   pl.BlockSpec((8, LANES), lambda i: (0, 0))],
        out_shape=[jax.ShapeDtypeStruct((rows, D), F32), jax.ShapeDtypeStruct((rows,) + ROW_TILE, F32),
                   jax.ShapeDtypeStruct((rows, LANES), F32), jax.ShapeDtypeStruct((8, LANES), F32)],
        scratch_shapes=[pltpu.VMEM((8, LANES), F32), pltpu.VMEM((TM, D), BF)],
        compiler_params=_cparams(("arbitrary",), VMEM_LIMIT),
    )(xa, xb, mods, g1, a, u, u, u, sb, yf, px, px, px, wconv, wpool, pscale, wb, wg, bg, wo, g2, wr, br)


def _wait_row_copies(src_ref, dst_ref, sem, n):
    for _ in range(n):
        pltpu.make_async_copy(src_ref.at[pl.ds(0, 1)], dst_ref.at[pl.ds(0, 1)], sem).wait()


def _dispatch_body(dest_ref, h_ref, xs_ref, sem):
    def body(t, c):
        for r in range(TOP_K):
            pltpu.make_async_copy(h_ref.at[pl.ds(t, 1)], xs_ref.at[pl.ds(dest_ref[0, t * TOP_K + r], 1)],
                                  sem).start(priority=r % 2)
        return c
    lax.fori_loop(0, h_ref.shape[0], body, 0, unroll=8)
    _wait_row_copies(h_ref, xs_ref, sem, h_ref.shape[0] * TOP_K)


def _dispatch(dest3, h2, n_slots):
    rows = h2.shape[0]
    tmv = dest3.shape[2] // TOP_K
    return pl.pallas_call(
        _dispatch_body,
        grid=(rows // tmv,),
        in_specs=[pl.BlockSpec((None, 1, tmv * TOP_K), lambda i: (i, 0, 0), memory_space=pltpu.SMEM),
                  pl.BlockSpec((tmv,) + ROW_TILE, lambda i: (i, 0, 0))],
        out_specs=pl.BlockSpec(memory_space=pl.ANY),
        out_shape=jax.ShapeDtypeStruct((n_slots,) + ROW_TILE, F32),
        scratch_shapes=[pltpu.SemaphoreType.DMA(())],
        compiler_params=_cparams(("arbitrary",), VMEM_LIMIT),
    )(dest3, h2)


def _expert_body(te_ref, na_ref, nv_ref, xs_ref, wg_ref, bg_ref, wu_ref, bu_ref, wd_ref, bd_ref, y_ref, wgb, wub, wdb):
    j = pl.program_id(0)

    @pl.when(j < na_ref[0])
    def _():
        @pl.when((j == 0) | (te_ref[j] != te_ref[jnp.maximum(j - 1, 0)]))
        def _():
            wgb[...] = wg_ref[...].astype(BF)
            wub[...] = wu_ref[...].astype(BF)
            wdb[...] = wd_ref[...].astype(BF)

        x = xs_ref[...].reshape(TME, D).astype(BF)
        a = jnp.minimum(jnp.dot(x, wgb[...], preferred_element_type=F32) + bg_ref[...], SWIGLU_LIMIT)
        u = jnp.clip(jnp.dot(x, wub[...], preferred_element_type=F32) + bu_ref[...], -SWIGLU_LIMIT, SWIGLU_LIMIT)
        act = a * _sigmoid(SWIGLU_ALPHA * a) * (u + 1.0)
        y = jnp.dot(act.astype(BF), wdb[...], preferred_element_type=F32) + bd_ref[...]
        rowi = lax.broadcasted_iota(jnp.int32, (TME, 1), 0)
        y_ref[...] = jnp.where(rowi < nv_ref[j], y, 0.0).astype(BF)


def _experts(tile_expert, n_active, n_valid, xs, layer, wg, bg, wu, bu, wd, bd):
    n_slots = xs.shape[0]
    f = wg.shape[3]
    slot = lambda j, te, na, nv: (jnp.minimum(j, na[0] - 1), 0)
    wsel = lambda j, te, na, nv: (layer, te[j], 0, 0)
    return pl.pallas_call(
        _expert_body,
        grid_spec=pltpu.PrefetchScalarGridSpec(
            num_scalar_prefetch=3,
            grid=(n_slots // TME,),
            in_specs=[pl.BlockSpec((TME,) + ROW_TILE, lambda j, te, na, nv: (jnp.minimum(j, na[0] - 1), 0, 0)),
                      pl.BlockSpec((None, None, D, f), wsel), pl.BlockSpec((None, None, 1, f), wsel),
                      pl.BlockSpec((None, None, D, f), wsel), pl.BlockSpec((None, None, 1, f), wsel),
                      pl.BlockSpec((None, None, f, D), wsel), pl.BlockSpec((None, None, 1, D), wsel)],
            out_specs=pl.BlockSpec((TME, D), slot),
            scratch_shapes=[pltpu.VMEM((D, f), BF), pltpu.VMEM((D, f), BF), pltpu.VMEM((f, D), BF)]),
        out_shape=jax.ShapeDtypeStruct((n_slots, D), BF),
        compiler_params=_cparams(("arbitrary",), VMEM_LIMIT),
    )(tile_expert, n_active, n_valid, xs, wg, bg, wu, bu, wd, bd)


def _combine_blocks_body(tab_ref, x_ref, mod_ref, route_ref, gpos_ref, gf_ref, y_ref, o_ref, g_ref, sem, *, final, n_exp):
    @pl.when(pl.program_id(0) == 0)
    def _():
        g_ref[...] = jnp.zeros_like(g_ref)

    for e in range(n_exp):
        first_blk, n_blk, g_blk = tab_ref[0, e], tab_ref[0, n_exp + e], tab_ref[0, 2 * n_exp + e]

        def fetch(k, c):
            pltpu.make_async_copy(y_ref.at[pl.ds(first_blk + k, 1)], g_ref.at[pl.ds(g_blk + k, 1)], sem).start()
            return c
        lax.fori_loop(0, n_blk, fetch, 0)

    cap = g_ref.shape[0] * g_ref.shape[1]
    route, gpos = route_ref[...], gpos_ref[...]
    col = lax.broadcasted_iota(jnp.int32, (TM, cap), 1)
    wm = jnp.zeros((TM, cap), F32)
    for r in range(TOP_K):
        wm = wm + jnp.where(col == gpos[:, r:r + 1], route[:, TOP_K + r:TOP_K + r + 1], 0.0)
    wm = wm.astype(BF)

    def drain(k, c):
        pltpu.make_async_copy(y_ref.at[pl.ds(0, 1)], g_ref.at[pl.ds(0, 1)], sem).wait()
        return c
    lax.fori_loop(0, tab_ref[0, 3 * n_exp], drain, 0)
    acc = jnp.dot(wm, g_ref[...].reshape(cap, D), preferred_element_type=F32)
    xn = x_ref[...] + mod_ref[5:6, :] * acc
    if final:
        xn = xn * lax.rsqrt(jnp.mean(xn * xn, axis=-1, keepdims=True) + EPS) * gf_ref[...]
    o_ref[...] = xn


def _combine_blocks(tab, gpos, x_mid, mods, route, g_final, y, dims, final, n_exp):
    rows = x_mid.shape[0]
    tps, nb = dims["tps"], dims["nb"]
    cap_blocks = (TM * TOP_K + n_exp * 2 * ROW_BLK) // ROW_BLK
    return pl.pallas_call(
        functools.partial(_combine_blocks_body, final=final, n_exp=n_exp),
        grid=(rows // TM,),
        in_specs=[pl.BlockSpec((None, 1, LANES), lambda i: (i, 0, 0), memory_space=pltpu.SMEM),
                  pl.BlockSpec((TM, D), lambda i: (i, 0)),
                  pl.BlockSpec((None, MOD_ROWS, D), lambda i: (jnp.minimum(i // tps, nb), 0, 0)),
                  pl.BlockSpec((TM, LANES), lambda i: (i, 0)),
                  pl.BlockSpec((TM, TOP_K), lambda i: (i, 0)),
                  pl.BlockSpec((1, D), lambda i: (0, 0)),
                  pl.BlockSpec(memory_space=pl.ANY)],
        out_specs=pl.BlockSpec((TM, D), lambda i: (i, 0)),
        out_shape=jax.ShapeDtypeStruct((rows, D), F32),
        scratch_shapes=[pltpu.VMEM((cap_blocks, ROW_BLK, D), BF), pltpu.SemaphoreType.DMA(())],
        compiler_params=_cparams(("arbitrary",), VMEM_LIMIT),
    )(tab, x_mid, mods, route, gpos, g_final, y.reshape(-1, ROW_BLK, D))


def _moe(x_mid, h2, route, counts, mods, g_final, layer, wg, bg, wu, bu, wd, bd, dims, final):
    rows = x_mid.shape[0]
    n_exp = wg.shape[1]
    n_slots = rows * TOP_K + n_exp * TME
    e = route[:, 0:TOP_K].astype(jnp.int32)
    rank = route[:, 2 * TOP_K:3 * TOP_K].astype(jnp.int32)
    cnt = counts[0, :n_exp].astype(jnp.int32)
    cnt_pad = ((cnt + TME - 1) // TME) * TME
    offs_end = jnp.cumsum(cnt_pad)
    offs = offs_end - cnt_pad
    onehot = e[:, :, None] == jnp.arange(n_exp, dtype=jnp.int32)[None, None, :]
    dest = jnp.sum(jnp.where(onehot, offs[None, None, :], 0), axis=-1) + rank
    tmv = max(t for t in (TM_MOVE, TM) if dims["s"] % t == 0 and dims["t_lat"] % t == 0 and rows % t == 0)
    dest3 = dest.reshape(rows // tmv, 1, tmv * TOP_K)
    tile_start = jnp.arange(n_slots // TME, dtype=jnp.int32) * TME
    tile_expert = jnp.minimum(jnp.sum(tile_start[:, None] >= offs_end[None, :], axis=1), n_exp - 1).astype(jnp.int32)
    n_active = (offs_end[-1:] // TME).astype(jnp.int32)
    n_valid = jnp.clip(cnt[tile_expert] - (tile_start - offs[tile_expert]), 0, TME).astype(jnp.int32)
    xs = _dispatch(dest3, h2, n_slots)
    y = _experts(tile_expert, n_active, n_valid, xs, layer, wg, bg, wu, bu, wd, bd)

    nt = rows // TM
    per_tile = jnp.sum(onehot.reshape(nt, TM * TOP_K, n_exp), axis=1).astype(jnp.int32)
    start = offs[None, :] + jnp.cumsum(per_tile, axis=0) - per_tile
    first_blk = start // ROW_BLK
    n_blk = jnp.where(per_tile > 0, (start + per_tile + ROW_BLK - 1) // ROW_BLK - first_blk, 0)
    g_blk = jnp.cumsum(n_blk, axis=1) - n_blk
    tab = jnp.concatenate([first_blk, n_blk, g_blk, jnp.sum(n_blk, axis=1, keepdims=True),
                           jnp.zeros((nt, LANES - 3 * n_exp - 1), jnp.int32)], axis=1).reshape(nt, 1, LANES)
    shift = (g_blk - first_blk) * ROW_BLK
    gpos = dest + jnp.sum(jnp.where(onehot.reshape(nt, TM, TOP_K, n_exp), shift[:, None, None, :], 0),
                          axis=-1).reshape(rows, TOP_K)
    return _combine_blocks(tab, gpos, x_mid, mods, route, g_final, y, dims, final, n_exp)


def _rope_tables(s, ctx):
    rows = s // GRID_W
    row = np.repeat(np.arange(rows), GRID_W).astype(np.float32)
    col = np.tile(np.arange(GRID_W), rows).astype(np.float32)
    half = HEAD_DIM // 2
    inv = (np.float32(ROPE_BASE) ** (-np.arange(0, half, 2, dtype=np.float32) / half)).astype(np.float32)
    ar = row[:, None] * inv
    ac = col[:, None] * inv
    ang = np.concatenate([ar, ar, ac, ac], axis=-1)
    cos = np.concatenate([np.cos(ang), np.ones((ctx, HEAD_DIM), np.float32)], axis=0)
    sin = np.concatenate([np.sin(ang), np.zeros((ctx, HEAD_DIM), np.float32)], axis=0)
    return (jnp.asarray(np.tile(cos, (1, 2)), F32), jnp.asarray(np.tile(sin, (1, 2)), F32))


def _rot_columns(w):
    d, n = w.shape
    w4 = w.reshape(d, n // HEAD_DIM, 4, HEAD_DIM // 4)
    return jnp.stack([-w4[:, :, 1], w4[:, :, 0], -w4[:, :, 3], w4[:, :, 2]], axis=2).reshape(d, n)


def kernel(x, c, ctx, c_ctx, w_ada, b_ada, g_norm1, w_in, da_lambda, g_subln, w_conv, w_pool, pool_scale,
           w_branch, w_mgate, b_mgate, w_out, g_norm2, w_router, b_router, w_e_gate, b_e_gate, w_e_up,
           b_e_up, w_e_down, b_e_down, g_final):
    nb, s, _ = x.shape
    nctx = ctx.shape[1]
    depth = w_in.shape[0]
    n_exp = w_router.shape[2]
    t_lat, t_ctx = nb * s, nb * nctx
    dims = dict(nb=nb, s=s, ctx=nctx, t_lat=t_lat, t_all=t_lat + t_ctx, nlat=t_lat // TM, tps=s // TM, tpc=nctx // TM)
    assert s % TK == 0 and s // TK >= 2 and s % (FFT_N2 * 8) == 0 and nctx % TM == 0 and nb < COND_ROWS
    assert TM == VT_CHUNK and nctx % VT_CHUNK == 0 and ATT_UNROLL % 2 == 0

    cos_t, sin_t = _rope_tables(s, nctx)
    tabs = _fft_tables(s, nctx)
    cond = jnp.concatenate([c, c_ctx[None, :], jnp.zeros((COND_ROWS - 1 - nb, D), F32)], axis=0)
    mods_all = _adaln(cond, w_ada, b_ada).reshape(depth, COND_ROWS, 6, D)
    mods_all = jnp.concatenate([mods_all, jnp.zeros((depth, COND_ROWS, MOD_ROWS - 6, D), F32)], axis=2)

    xa, xb = x.reshape(t_lat, D), ctx.reshape(t_ctx, D)
    for l in range(depth):
        last = l == depth - 1
        lam_init = 0.8 - 0.6 * math.exp(-0.3 * l)
        mods = mods_all[l]
        w_l = w_in[l]
        w_ext = jnp.concatenate([w_l, _rot_columns(w_l[:, :BW]), _rot_columns(w_l[:, BW:2 * BW])], axis=1).astype(BF)
        qt, k, vt, u, sb, fx, px = _inproj(xa, xb, mods, g_norm1[l][None, :], w_ext, cos_t, sin_t, dims)

        rows = t_lat if last else t_lat + t_ctx
        a = _attention(qt, k, vt, g_subln[l], da_lambda[l], lam_init, dims, latent=True, rows=rows)
        yf = _fourier_latent(fx, tabs, dims, rows)
        if not last:
            a = _attention(qt, k, vt, g_subln[l], da_lambda[l], lam_init, dims, latent=False, fill=a)
            yf = _fourier_ctx(fx, tabs, dims, yf)

        wconv = jnp.concatenate([w_conv[l], jnp.zeros((5, BW), F32)], axis=0)
        wr = jnp.concatenate([w_router[l], jnp.zeros((D, LANES - n_exp), F32)], axis=1).astype(BF)
        br = jnp.concatenate([b_router[l], jnp.full((LANES - n_exp,), -1e30, F32)])[None, :]
        x_mid, h2, route, counts = _merge(
            xa, xb, mods, g_norm1[l][None, :], a, u, sb, yf, px, wconv, w_pool[l].astype(BF), pool_scale[l][None, :],
            w_branch[l].astype(BF), w_mgate[l].astype(BF), b_mgate[l][None, :], w_out[l].astype(BF),
            g_norm2[l][None, :], wr, br, dims, rows)
        x_new = _moe(x_mid, h2, route, counts, mods, g_final[None, :], l,
                     w_e_gate, b_e_gate[:, :, None, :], w_e_up, b_e_up[:, :, None, :],
                     w_e_down, b_e_down[:, :, None, :], dims, final=last)
        if last:
            return x_new.reshape(nb, s, D)
        xa = xb = x_new
```

```python
import functools
import math

import numpy as np
import jax
import jax.numpy as jnp
from jax import lax
from jax.experimental import pallas as pl
from jax.experimental.pallas import tpu as pltpu

F32 = jnp.float32
BF = jnp.bfloat16

D = 1024
HEADS = 4
HEAD_DIM = 64
HEAD_W = 128
BW = 512
GRID_W = 64
EPS = 1e-6
ROPE_BASE = 10000.0
POOL_WINDOWS = (2, 4, 8, 16)
TOP_K = 4
SWIGLU_LIMIT = 7.0
SWIGLU_ALPHA = 1.702
FFT_N2 = 128

TM = 256
TME = 512
TM_MOVE = 1024
TQ = 512
TK = 512
ATT_UNROLL = 2
VT_CHUNK = 256
HALO = 16
LANES = 128
COND_ROWS = 16
MOD_ROWS = 8
ROW_TILE = (D // LANES, LANES)
QSCALE = (HEAD_DIM ** -0.5) * math.log2(math.e)
EXP_HEADROOM = 64.0
NORM_MARGIN = 1.05
VMEM_LIMIT = 56 * 2 ** 20


def _cparams(sem, vmem=None):
    return pltpu.CompilerParams(dimension_semantics=sem, vmem_limit_bytes=vmem)


def _modnorm(x, g, shift, scale):
    y = x * lax.rsqrt(jnp.mean(x * x, axis=-1, keepdims=True) + EPS) * g
    return y * (1.0 + scale) + shift


def _sigmoid(x):
    return 1.0 / (1.0 + jnp.exp(-x))


def _adaln_body(c_ref, w_ref, b_ref, o_ref):
    c = c_ref[...]
    s = c * _sigmoid(c)
    o_ref[...] = jnp.dot(s.astype(BF), w_ref[...].astype(BF), preferred_element_type=F32) + b_ref[...]


def _adaln(c16, w_ada, b_ada):
    nl = w_ada.shape[0]
    tn = 1536
    return pl.pallas_call(
        _adaln_body,
        grid=(nl, 6 * D // tn),
        in_specs=[pl.BlockSpec((COND_ROWS, D), lambda l, j: (0, 0)),
                  pl.BlockSpec((None, D, tn), lambda l, j: (l, 0, j)),
                  pl.BlockSpec((None, 1, tn), lambda l, j: (l, 0, j))],
        out_specs=pl.BlockSpec((None, COND_ROWS, tn), lambda l, j: (l, 0, j)),
        out_shape=jax.ShapeDtypeStruct((nl, COND_ROWS, 6 * D), F32),
        compiler_params=_cparams(("arbitrary", "arbitrary"), VMEM_LIMIT),
    )(c16, w_ada, b_ada.reshape(nl, 1, 6 * D))


def _token_tile(xa_ref, xb_ref, n_a):
    return jnp.where(pl.program_id(0) < n_a, xa_ref[...], xb_ref[...])


def _token_specs(xa, xb):
    n_a, n_b = xa.shape[0] // TM, xb.shape[0] // TM
    return [pl.BlockSpec((TM, D), lambda i: (jnp.minimum(i, n_a - 1), 0)),
            pl.BlockSpec((TM, D), lambda i: (jnp.clip(i - n_a, 0, n_b - 1), 0))]


def _inproj_body(xa_ref, xb_ref, mod_ref, g_ref, w_ref, cos_ref, sin_ref,
                 q_ref, k_ref, v_ref, u_ref, sb_ref, fx_ref, px_ref, *, n_a):
    h = _modnorm(_token_tile(xa_ref, xb_ref, n_a), g_ref[...], mod_ref[0:1, :], mod_ref[1:2, :]).astype(BF)

    def proj(c0):
        return jnp.dot(h, w_ref[:, c0:c0 + BW], preferred_element_type=F32)

    cos = jnp.concatenate([cos_ref[...]] * 4, axis=1)
    sin = jnp.concatenate([sin_ref[...]] * 4, axis=1)
    q_ref[...] = ((proj(0) * cos + proj(8 * BW) * sin) * QSCALE).T.astype(BF)
    k_ref[...] = (proj(BW) * cos + proj(9 * BW) * sin).astype(BF)
    v_ref[...] = proj(2 * BW).T.astype(BF)
    u_ref[...] = (proj(5 * BW) * proj(3 * BW)).astype(BF)
    sb_ref[...] = proj(4 * BW).astype(BF)
    px_ref[...] = proj(7 * BW).astype(BF)
    fx_ref[...] = proj(6 * BW).astype(BF)


def _inproj(xa, xb, mods, g1, w_ext, cos_t, sin_t, dims):
    t_all, nlat, tps, tpc, nb = dims["t_all"], dims["nlat"], dims["tps"], dims["tpc"], dims["nb"]

    def tab_idx(i):
        return (jnp.where(i < nlat, i % tps, tps + (i - nlat) % tpc), 0)

    out = jax.ShapeDtypeStruct((t_all, BW), BF)
    row = pl.BlockSpec((TM, BW), lambda i: (i, 0))
    return pl.pallas_call(
        functools.partial(_inproj_body, n_a=xa.shape[0] // TM),
        grid=(t_all // TM,),
        in_specs=_token_specs(xa, xb) + [
                  pl.BlockSpec((None, MOD_ROWS, D), lambda i: (jnp.minimum(i // tps, nb), 0, 0)),
                  pl.BlockSpec((1, D), lambda i: (0, 0)),
                  pl.BlockSpec((D, 10 * BW), lambda i: (0, 0)),
                  pl.BlockSpec((TM, LANES), tab_idx),
                  pl.BlockSpec((TM, LANES), tab_idx)],
        out_specs=[pl.BlockSpec((BW, TM), lambda i: (0, i)), row,
                   pl.BlockSpec((None, BW, VT_CHUNK), lambda i: (i, 0, 0))] + [row] * 4,
        out_shape=[jax.ShapeDtypeStruct((BW, t_all), BF), out,
                   jax.ShapeDtypeStruct((t_all // VT_CHUNK, BW, VT_CHUNK), BF)] + [out] * 4,
        compiler_params=_cparams(("arbitrary",), VMEM_LIMIT),
    )(xa, xb, mods, g1, w_ext, cos_t, sin_t)


def _attn_body(*refs, n_ctx_chunks, n_lat_chunks, lam_init):
    if n_lat_chunks:
        qt_ref, kc_ref, vtc_ref, kl_ref, vtl_ref, g_ref, dl_ref, o_ref, acc_ref, s_buf, p_buf, l_ref, knorm_ref = refs
    else:
        qt_ref, kc_ref, vtc_ref, g_ref, dl_ref, _, o_ref, acc_ref = refs
    qt = qt_ref[...]
    tq = qt.shape[1]
    row = lax.broadcasted_iota(jnp.int32, qt.shape, 0)
    zero = jnp.zeros_like(qt)
    q2 = jnp.concatenate([jnp.where(row < HEAD_DIM, qt, zero), jnp.where(row >= HEAD_DIM, qt, zero)], axis=1)

    def scores(kc):
        s = jnp.dot(kc, q2, preferred_element_type=F32)
        return s, jnp.max(s, axis=0, keepdims=True)

    def softmax(s, smax, m, l):
        m_new = jnp.maximum(m, smax)
        p = jnp.exp2(s - m_new)
        alpha = jnp.exp2(m - m_new)
        return p.astype(BF), m_new, alpha * l + jnp.sum(p, axis=0, keepdims=True), alpha

    def accumulate(alpha, vt, p):
        acc_ref[...] = alpha * acc_ref[...] + jnp.dot(vt, p, preferred_element_type=F32)

    m = jnp.full((1, 2 * tq), -1e30, F32)
    l = jnp.zeros((1, 2 * tq), F32)
    acc_ref[...] = jnp.zeros_like(acc_ref)
    ctx_scores = [scores(kc_ref[c * VT_CHUNK:(c + 1) * VT_CHUNK, :]) for c in range(n_ctx_chunks)]
    if n_lat_chunks:
        per = TK // VT_CHUNK
        n = n_lat_chunks

        def keys(j):
            return kl_ref[pl.ds(pl.multiple_of(j * TK, TK), TK), :]

        s_buf[0] = jnp.dot(keys(0), q2, preferred_element_type=F32)
    for c in range(n_ctx_chunks):
        p, m, l, alpha = softmax(*ctx_scores[c], m, l)
        accumulate(alpha, vtc_ref[c], p)
    if n_lat_chunks:

        def values(j):
            parts = [vtl_ref[j * per + r] for r in range(per)]
            return parts[0] if per == 1 else jnp.concatenate(parts, axis=1)

        @pl.when(pl.program_id(2) == 0)
        def _():
            d_i = lax.broadcasted_iota(jnp.int32, (HEAD_W, LANES), 0)
            c_i = lax.broadcasted_iota(jnp.int32, (HEAD_W, LANES), 1)
            sel = jnp.where((d_i >= HEAD_DIM) == (c_i == 1), 1.0, 0.0) * jnp.where(c_i < 2, 1.0, 0.0)
            best = jnp.zeros((1, LANES), F32)
            for j in range(n):
                kf = keys(j).astype(F32)
                n2 = jnp.dot((kf * kf).astype(BF), sel.astype(BF), preferred_element_type=F32)
                best = jnp.maximum(best, jnp.max(n2, axis=0, keepdims=True))
            lane2 = lax.broadcasted_iota(jnp.int32, (1, 2 * tq), 1)
            knorm_ref[...] = jnp.sqrt(jnp.where(lane2 < tq, best[:, 0:1], best[:, 1:2]))

        qf = q2.astype(F32)
        bound = jnp.sqrt(jnp.sum(qf * qf, axis=0, keepdims=True)) * knorm_ref[...] * NORM_MARGIN
        gap = jnp.max(bound - m)
        l_ref[...] = l

        @pl.when(gap <= EXP_HEADROOM)
        def _():
            lsum = l
            s = s_buf[0]
            for j in range(n):
                s_next = jnp.dot(keys(j + 1), q2, preferred_element_type=F32) if j + 1 < n else None
                p = jnp.exp2(s - m)
                lsum = lsum + jnp.sum(p, axis=0, keepdims=True)
                acc_ref[...] += jnp.dot(values(j), p.astype(BF), preferred_element_type=F32)
                s = s_next
            l_ref[...] = lsum

        @pl.when(gap > EXP_HEADROOM)
        def _():
            _attn_exact_latent(m, l, n, keys, values, scores, softmax, accumulate, s_buf, p_buf, l_ref)

        l = l_ref[...]
    o = acc_ref[...] / l
    dl = dl_ref[...]
    lam = (jnp.exp(jnp.sum(dl[0:1] * dl[1:2], axis=1, keepdims=True))
           - jnp.exp(jnp.sum(dl[2:3] * dl[3:4], axis=1, keepdims=True)) + lam_init)
    d = o[:, :tq] - lam * o[:, tq:]
    y = d * lax.rsqrt(jnp.mean(d * d, axis=0, keepdims=True) + EPS) * g_ref[...] * (1.0 - lam_init)
    o_ref[...] = y.T.astype(BF)


def _attn_exact_latent(m, l, n, keys, values, scores, softmax, accumulate, s_buf, p_buf, l_ref):
    def half_step(j, prev, cur, carry):
        m, l, alpha_prev, smax = carry
        accumulate(alpha_prev, values(j - 1), p_buf[prev])
        if isinstance(j, int) and j + 1 >= n:
            smax_next = smax
        else:
            s_buf[prev], smax_next = scores(keys(jnp.minimum(j + 1, n - 1)))
        p, m, l, alpha = softmax(s_buf[cur], smax, m, l)
        p_buf[cur] = p
        return m, l, alpha, smax_next

    s0 = s_buf[0]
    p, m, l, alpha = softmax(s0, jnp.max(s0, axis=0, keepdims=True), m, l)
    p_buf[0] = p
    s_buf[1], smax = scores(keys(1))

    def body(i, carry):
        for r in range(ATT_UNROLL):
            carry = half_step(ATT_UNROLL * i + r + 1, r % 2, 1 - r % 2, carry)
        return carry

    n_body = (n - 1) // ATT_UNROLL
    carry = lax.fori_loop(0, n_body, body, (m, l, alpha, smax))
    for j in range(n_body * ATT_UNROLL + 1, n):
        carry = half_step(j, (j - 1) % 2, j % 2, carry)
    m, l, alpha, _ = carry
    accumulate(alpha, values(n - 1), p_buf[(n - 1) % 2])
    l_ref[...] = l


def _attention(qt, k, vt, g_sub, da_lambda, lam_init, dims, latent, rows=None, fill=None):
    nb, s, ctx, t_lat = dims["nb"], dims["s"], dims["ctx"], dims["t_lat"]
    cpb = ctx // VT_CHUNK
    tq = TQ if latent else min(TQ, ctx)
    g_sub_b = jnp.broadcast_to(g_sub[:, None], (HEAD_W, tq))
    common_tail = [pl.BlockSpec((HEAD_W, tq), lambda b, h, qi: (0, 0)),
                   pl.BlockSpec((4, HEAD_DIM), lambda b, h, qi: (0, 0))]
    kc_spec = pl.BlockSpec((ctx, HEAD_W), lambda b, h, qi: (t_lat // ctx + b, h))
    vtc_spec = pl.BlockSpec((cpb, HEAD_W, VT_CHUNK), lambda b, h, qi: (t_lat // ctx + b, h, 0))
    scratch = [pltpu.VMEM((HEAD_W, 2 * tq), F32)]
    if latent:
        nq = s // tq
        in_specs = [pl.BlockSpec((HEAD_W, tq), lambda b, h, qi: (h, b * nq + qi)), kc_spec, vtc_spec,
                    pl.BlockSpec((s, HEAD_W), lambda b, h, qi: (b, h)),
                    pl.BlockSpec((s // VT_CHUNK, HEAD_W, VT_CHUNK), lambda b, h, qi: (b, h, 0))] + common_tail
        args = (qt, k, vt, k, vt, g_sub_b, da_lambda)
        row0, n_lat_chunks, aliases = 0, s // TK, {}
        scratch += [pltpu.VMEM((2, TK, 2 * tq), F32), pltpu.VMEM((2, TK, 2 * tq), BF),
                    pltpu.VMEM((1, 2 * tq), F32), pltpu.VMEM((1, 2 * tq), F32)]
    else:
        nq = ctx // tq
        in_specs = [pl.BlockSpec((HEAD_W, tq), lambda b, h, qi: (h, t_lat // tq + b * nq + qi)),
                    kc_spec, vtc_spec] + common_tail
        in_specs.append(pl.BlockSpec(memory_space=pl.ANY))
        args = (qt, k, vt, g_sub_b, da_lambda, fill)
        row0, n_lat_chunks, rows, aliases = t_lat // tq, 0, fill.shape[0], {5: 0}
    return pl.pallas_call(
        functools.partial(_attn_body, n_ctx_chunks=cpb, n_lat_chunks=n_lat_chunks, lam_init=lam_init),
        grid=(nb, HEADS, nq),
        in_specs=in_specs,
        out_specs=pl.BlockSpec((tq, HEAD_W), lambda b, h, qi: (row0 + b * nq + qi, h)),
        out_shape=jax.ShapeDtypeStruct((rows, HEADS * HEAD_W), BF),
        input_output_aliases=aliases,
        scratch_shapes=scratch,
        compiler_params=_cparams(("arbitrary", "arbitrary", "arbitrary"), VMEM_LIMIT),
    )(*args)


def _channel_dft(x, cs):
    parts = [jnp.dot(x[:, g * LANES:(g + 1) * LANES], cs, preferred_element_type=F32).astype(BF)
             for g in range(x.shape[1] // LANES)]
    return jnp.concatenate([jnp.concatenate([pq[:, :LANES] for pq in parts], axis=1),
                            jnp.concatenate([pq[:, LANES:] for pq in parts], axis=1)], axis=0)


def _fft_a_body(fa_ref, cs_ref, x_ref, zr_ref, zi_ref):
    x = x_ref[...]
    x = _channel_dft(x.reshape(x.shape[0], x.shape[1] * x.shape[2]), cs_ref[...])
    z = jnp.dot(fa_ref[...], x, preferred_element_type=F32)
    zr_ref[...] = z[:FFT_N2].astype(BF).reshape(zr_ref.shape)
    zi_ref[...] = z[FFT_N2:].astype(BF).reshape(zi_ref.shape)


def _fft_b_body(fb_ref, tc_ref, ts_ref, zr_ref, zi_ref, o_ref):
    nk = zr_ref.shape[0]
    parts_r, parts_i = [], []
    for j in range(nk):
        zr = zr_ref[j].astype(F32)
        zi = zi_ref[j].astype(F32)
        tc = jnp.concatenate([tc_ref[j]] * 4, axis=1)
        ts = jnp.concatenate([ts_ref[j]] * 4, axis=1)
        parts_r.append((zr * tc - zi * ts).astype(BF))
        parts_i.append((zr * ts + zi * tc).astype(BF))
    rhs = jnp.concatenate([jnp.concatenate(parts_r, axis=1), jnp.concatenate(parts_i, axis=1)], axis=0)
    o_ref[...] = jnp.dot(fb_ref[...], rhs, preferred_element_type=F32).astype(BF).reshape(o_ref.shape)


def _fft_ctx_body(f_ref, cs_ref, x_ref, _, o_ref):
    x = _channel_dft(x_ref[...], cs_ref[...])
    o_ref[...] = jnp.dot(f_ref[...], x, preferred_element_type=F32).astype(BF)


def _fft_tables(s, ctx):
    n1, n2 = s // FFT_N2, FFT_N2
    norm = 1.0 / math.sqrt(s * LANES)
    a = 2 * np.pi * np.outer(np.arange(n2), np.arange(n2)) / n2
    c, sn = np.cos(a), np.sin(a)
    fa = np.block([[c, -sn], [sn, c]]) * norm
    tw = 2 * np.pi * np.outer(np.arange(n2), np.arange(n1)) / s
    tc = np.repeat(np.cos(tw)[:, :, None], LANES, axis=2)
    ts = np.repeat(np.sin(tw)[:, :, None], LANES, axis=2)
    b = 2 * np.pi * np.outer(np.arange(n1), np.arange(n1)) / n1
    fb = np.concatenate([np.cos(b), -np.sin(b)], axis=1)
    ch = 2 * np.pi * np.outer(np.arange(LANES), np.arange(LANES)) / LANES
    cs = np.concatenate([np.cos(ch), np.sin(ch)], axis=1)
    cx = 2 * np.pi * np.outer(np.arange(ctx), np.arange(ctx)) / ctx
    fctx = np.concatenate([np.cos(cx), -np.sin(cx)], axis=1) / math.sqrt(ctx * LANES)
    return dict(fa=jnp.asarray(fa, BF), tc=jnp.asarray(tc, F32), ts=jnp.asarray(ts, F32),
                fb=jnp.asarray(fb, BF), cs=jnp.asarray(cs, BF), fctx=jnp.asarray(fctx, BF))


def _fourier_latent(fx, tabs, dims, rows):
    nb, s = dims["nb"], dims["s"]
    n1 = s // FFT_N2
    wide = n1 * BW
    nj = min(16, n1)
    wc = nj * BW
    fx3 = fx.reshape(-1, n1, BW)
    zshape = jax.ShapeDtypeStruct((nb, FFT_N2, n1, BW), BF)
    zr4, zi4 = pl.pallas_call(
        _fft_a_body,
        grid=(nb, wide // wc),
        in_specs=[pl.BlockSpec((2 * FFT_N2, 2 * FFT_N2), lambda b, j: (0, 0)),
                  pl.BlockSpec((LANES, 2 * LANES), lambda b, j: (0, 0)),
                  pl.BlockSpec((FFT_N2, nj, BW), lambda b, j: (b, j, 0))],
        out_specs=[pl.BlockSpec((None, FFT_N2, nj, BW), lambda b, j: (b, 0, j, 0))] * 2,
        out_shape=[zshape, zshape],
        compiler_params=_cparams(("arbitrary", "arbitrary"), VMEM_LIMIT),
    )(tabs["fa"], tabs["cs"], fx3)
    kb = 16
    y = pl.pallas_call(
        _fft_b_body,
        grid=(FFT_N2 // kb, nb),
        in_specs=[pl.BlockSpec((n1, 2 * n1), lambda kk, b: (0, 0)),
                  pl.BlockSpec((kb, n1, LANES), lambda kk, b: (kk, 0, 0)),
                  pl.BlockSpec((kb, n1, LANES), lambda kk, b: (kk, 0, 0)),
                  pl.BlockSpec((None, kb, n1, BW), lambda kk, b: (b, kk, 0, 0)),
                  pl.BlockSpec((None, kb, n1, BW), lambda kk, b: (b, kk, 0, 0))],
        out_specs=pl.BlockSpec((n1, kb, BW), lambda kk, b: (b, kk, 0)),
        out_shape=jax.ShapeDtypeStruct((rows // FFT_N2, FFT_N2, BW), BF),
        compiler_params=_cparams(("arbitrary", "arbitrary"), VMEM_LIMIT),
    )(tabs["fb"], tabs["tc"], tabs["ts"], zr4, zi4)
    return y.reshape(rows, BW)


def _fourier_ctx(fx, tabs, dims, fill):
    nb, ctx, t_lat = dims["nb"], dims["ctx"], dims["t_lat"]
    return pl.pallas_call(
        _fft_ctx_body,
        grid=(nb,),
        in_specs=[pl.BlockSpec((ctx, 2 * ctx), lambda b: (0, 0)),
                  pl.BlockSpec((LANES, 2 * LANES), lambda b: (0, 0)),
                  pl.BlockSpec((ctx, BW), lambda b: (t_lat // ctx + b, 0)),
                  pl.BlockSpec(memory_space=pl.ANY)],
        out_specs=pl.BlockSpec((ctx, BW), lambda b: (t_lat // ctx + b, 0)),
        out_shape=jax.ShapeDtypeStruct(fill.shape, BF),
        input_output_aliases={3: 0},
        compiler_params=_cparams(("arbitrary",), VMEM_LIMIT),
    )(tabs["fctx"], tabs["cs"], fx, fill)


def _route_topk(h2b, wr_ref, br_ref):
    logits = jnp.dot(h2b, wr_ref[...], preferred_element_type=F32) + br_ref[...]
    lane = lax.broadcasted_iota(jnp.int32, (TM, LANES), 1).astype(F32)
    cur = logits
    vals, idxs = [], []
    for _ in range(TOP_K):
        mx = jnp.max(cur, axis=1, keepdims=True)
        idx = jnp.min(jnp.where(cur == mx, lane, float(LANES)), axis=1, keepdims=True)
        vals.append(mx)
        idxs.append(idx)
        cur = jnp.where(lane == idx, -jnp.inf, cur)
    ex = [jnp.exp(v - vals[0]) for v in vals]
    den = ex[0] + ex[1] + ex[2] + ex[3]
    member = jnp.zeros((TM, LANES), F32)
    for idx in idxs:
        member = member + jnp.where(lane == idx, 1.0, 0.0)
    return idxs, [e / den for e in ex], member


def _route_ranks(idxs, weights, member, carry_ref, live):
    lane = lax.broadcasted_iota(jnp.int32, (TM, LANES), 1).astype(F32)
    tri = (lax.broadcasted_iota(jnp.int32, (TM, TM), 0) > lax.broadcasted_iota(jnp.int32, (TM, TM), 1))
    before = jnp.dot(jnp.where(tri, 1.0, 0.0).astype(BF), member.astype(BF), preferred_element_type=F32)
    rank_all = carry_ref[0:1, :] + before
    route = jnp.zeros((TM, LANES), F32)
    for r in range(TOP_K):
        rank = jnp.sum(jnp.where(lane == idxs[r], rank_all, 0.0), axis=1, keepdims=True)
        route = (route + jnp.where(lane == float(r), idxs[r], 0.0)
                 + jnp.where(lane == float(TOP_K + r), weights[r], 0.0)
                 + jnp.where(lane == float(2 * TOP_K + r), rank, 0.0))
    carry_ref[...] = carry_ref[...] + live * jnp.sum(member, axis=0, keepdims=True)
    return route


def _merge_body(xa_ref, xb_ref, mod_ref, g1_ref, a_ref, u_ref, up_ref, un_ref, sb_ref, yf_ref, px_ref, pp_ref, pn_ref,
                wconv_ref, wpool_ref, pscale_ref, wb_ref, wg_ref, bg_ref, wo_ref, g2_ref, wr_ref, br_ref,
                xo_ref, h2_ref, route_ref, cnt_ref, carry_ref, h2p_ref, *, n_tiles, nlat, tps, tpc, s, ctx, n_a):
    step = pl.program_id(0)
    i = jnp.minimum(step, n_tiles - 1)
    is_lat = i < nlat
    j = jnp.where(is_lat, i % tps, (i - nlat) % tpc)
    first = j == 0
    last = j == jnp.where(is_lat, tps, tpc) - 1
    seq_len = jnp.where(is_lat, s, ctx)

    @pl.when(step == 0)
    def _():
        carry_ref[...] = jnp.zeros_like(carry_ref)
        h2p_ref[...] = jnp.zeros_like(h2p_ref)

    topk_prev = _route_topk(h2p_ref[...], wr_ref, br_ref)

    x = jnp.where(i < n_a, xa_ref[...], xb_ref[...])
    h = _modnorm(x, g1_ref[...], mod_ref[0:1, :], mod_ref[1:2, :]).astype(BF)
    rowi = lax.broadcasted_iota(jnp.int32, (TM, 1), 0)

    u = u_ref[...].astype(F32)
    u_prev = jnp.where(first, 0.0, up_ref[...].astype(F32)[HALO - 1:HALO, :])
    u_next = jnp.where(last, 0.0, un_ref[...].astype(F32)[0:1, :])
    u_dn = jnp.where(rowi == 0, u_prev, pltpu.roll(u, 1, axis=0))
    u_up = jnp.where(rowi == TM - 1, u_next, pltpu.roll(u, TM - 1, axis=0))
    wc = wconv_ref[...]
    y_conv = sb_ref[...].astype(F32) * (wc[0:1] * u_dn + wc[1:2] * u + wc[2:3] * u_up)

    px = px_ref[...].astype(F32)
    ext = jnp.concatenate([jnp.where(first, 0.0, pp_ref[...].astype(F32)), px,
                           jnp.where(last, 0.0, pn_ref[...].astype(F32))], axis=0)
    ext_len = TM + 2 * HALO
    pos = j * TM + rowi
    pooled = []
    for g, wd in enumerate(POOL_WINDOWS):
        e = ext[:, g * LANES:(g + 1) * LANES]
        win = e + pltpu.roll(e, 1, axis=0)
        half = 1
        while 2 * half < wd:
            win = pltpu.roll(win, half, axis=0) + pltpu.roll(win, ext_len - half, axis=0)
            half *= 2
        win = win[HALO:HALO + TM]
        cnt = (jnp.minimum(pos + wd // 2, seq_len) - jnp.maximum(pos - wd // 2, 0)).astype(F32)
        pooled.append((win / cnt - px[:, g * LANES:(g + 1) * LANES]).astype(BF))

    gates = [_sigmoid(jnp.dot(h, wg_ref[:, kk * D:(kk + 1) * D], preferred_element_type=F32)
                      + bg_ref[:, kk * D:(kk + 1) * D]) for kk in range(4)]

    def branch(kk, y):
        return gates[kk] * jnp.dot(y, wb_ref[kk], preferred_element_type=F32)

    acc = branch(0, a_ref[...]) + branch(2, yf_ref[...]) + branch(1, y_conv.astype(BF))
    y_pool = jnp.concatenate([jnp.dot(pooled[g], wpool_ref[g], preferred_element_type=F32)
                              for g in range(len(POOL_WINDOWS))], axis=1) * pscale_ref[...]
    acc = acc + branch(3, y_pool.astype(BF))

    xn = x + mod_ref[2:3, :] * jnp.dot(acc.astype(BF), wo_ref[...], preferred_element_type=F32)
    xo_ref[...] = xn
    h2 = _modnorm(xn, g2_ref[...], mod_ref[3:4, :], mod_ref[4:5, :])
    h2_ref[...] = h2.reshape(h2_ref.shape)
    h2p_ref[...] = h2.astype(BF)
    route_ref[...] = _route_ranks(*topk_prev, carry_ref, jnp.where(step > 0, 1.0, 0.0))
    cnt_ref[...] = carry_ref[...]


def _merge(xa, xb, mods, g1, a, u, sb, yf, px, wconv, wpool, pscale, wb, wg, bg, wo, g2, wr, br, dims, rows):
    nlat, tps, tpc, nb = dims["nlat"], dims["tps"], dims["tpc"], dims["nb"]
    n_tiles = rows // TM
    n_a, n_b = xa.shape[0] // TM, xb.shape[0] // TM
    nhalo = TM // HALO
    last_halo = u.shape[0] // HALO - 1
    cur = lambda i: jnp.minimum(i, n_tiles - 1)
    tile = lambda w: pl.BlockSpec((TM, w), lambda i: (cur(i), 0))
    prev = pl.BlockSpec((HALO, BW), lambda i: (jnp.maximum(cur(i) * nhalo - 1, 0), 0))
    nxt = pl.BlockSpec((HALO, BW), lambda i: (jnp.minimum((cur(i) + 1) * nhalo, last_halo), 0))
    const = lambda shape: pl.BlockSpec(shape, lambda i: (0,) * len(shape))
    return pl.pallas_call(
        functools.partial(_merge_body, n_tiles=n_tiles, nlat=nlat, tps=tps, tpc=tpc, s=dims["s"], ctx=dims["ctx"],
                          n_a=n_a),
        grid=(n_tiles + 1,),
        in_specs=[pl.BlockSpec((TM, D), lambda i: (jnp.minimum(cur(i), n_a - 1), 0)),
                  pl.BlockSpec((TM, D), lambda i: (jnp.clip(cur(i) - n_a, 0, n_b - 1), 0)),
                  pl.BlockSpec((None, MOD_ROWS, D), lambda i: (jnp.minimum(cur(i) // tps, nb), 0, 0)),
                  const((1, D)), tile(BW), tile(BW), prev, nxt, tile(BW), tile(BW), tile(BW), prev, nxt,
                  const((8, BW)), const((4, LANES, LANES)), const((1, BW)), const((4, BW, D)),
                  const((D, 4 * D)), const((1, 4 * D)), const((D, D)), const((1, D)),
                  const((D, LANES)), const((1, LANES))],
        out_specs=[tile(D), pl.BlockSpec((TM,) + ROW_TILE, lambda i: (cur(i), 0, 0)),
                   pl.BlockSpec((TM, LANES), lambda i: (jnp.maximum(i - 1, 0), 0)),
                   pl.BlockSpec((8, LANES), lambda i: (0, 0))],
        out_shape=[jax.ShapeDtypeStruct((rows, D), F32), jax.ShapeDtypeStruct((rows,) + ROW_TILE, F32),
                   jax.ShapeDtypeStruct((rows, LANES), F32), jax.ShapeDtypeStruct((8, LANES), F32)],
        scratch_shapes=[pltpu.VMEM((8, LANES), F32), pltpu.VMEM((TM, D), BF)],
        compiler_params=_cparams(("arbitrary",), VMEM_LIMIT),
    )(xa, xb, mods, g1, a, u, u, u, sb, yf, px, px, px, wconv, wpool, pscale, wb, wg, bg, wo, g2, wr, br)


def _wait_row_copies(src_ref, dst_ref, sem, n):
    for _ in range(n):
        pltpu.make_async_copy(src_ref.at[pl.ds(0, 1)], dst_ref.at[pl.ds(0, 1)], sem).wait()


def _dispatch_body(dest_ref, h_ref, xs_ref, sem):
    def body(t, c):
        for r in range(TOP_K):
            pltpu.make_async_copy(h_ref.at[pl.ds(t, 1)], xs_ref.at[pl.ds(dest_ref[0, t * TOP_K + r], 1)],
                                  sem).start(priority=r % 2)
        return c
    lax.fori_loop(0, h_ref.shape[0], body, 0, unroll=8)
    _wait_row_copies(h_ref, xs_ref, sem, h_ref.shape[0] * TOP_K)


def _dispatch(dest3, h2, n_slots):
    rows = h2.shape[0]
    tmv = dest3.shape[2] // TOP_K
    return pl.pallas_call(
        _dispatch_body,
        grid=(rows // tmv,),
        in_specs=[pl.BlockSpec((None, 1, tmv * TOP_K), lambda i: (i, 0, 0), memory_space=pltpu.SMEM),
                  pl.BlockSpec((tmv,) + ROW_TILE, lambda i: (i, 0, 0))],
        out_specs=pl.BlockSpec(memory_space=pl.ANY),
        out_shape=jax.ShapeDtypeStruct((n_slots,) + ROW_TILE, F32),
        scratch_shapes=[pltpu.SemaphoreType.DMA(())],
        compiler_params=_cparams(("arbitrary",), VMEM_LIMIT),
    )(dest3, h2)


def _expert_body(te_ref, na_ref, xs_ref, wg_ref, bg_ref, wu_ref, bu_ref, wd_ref, bd_ref, y_ref, wgb, wub, wdb):
    j = pl.program_id(0)

    @pl.when(j < na_ref[0])
    def _():
        @pl.when((j == 0) | (te_ref[j] != te_ref[jnp.maximum(j - 1, 0)]))
        def _():
            wgb[...] = wg_ref[...].astype(BF)
            wub[...] = wu_ref[...].astype(BF)
            wdb[...] = wd_ref[...].astype(BF)

        x = xs_ref[...].reshape(TME, D).astype(BF)
        a = jnp.minimum(jnp.dot(x, wgb[...], preferred_element_type=F32) + bg_ref[...], SWIGLU_LIMIT)
        u = jnp.clip(jnp.dot(x, wub[...], preferred_element_type=F32) + bu_ref[...], -SWIGLU_LIMIT, SWIGLU_LIMIT)
        act = a * _sigmoid(SWIGLU_ALPHA * a) * (u + 1.0)
        y = jnp.dot(act.astype(BF), wdb[...], preferred_element_type=F32) + bd_ref[...]
        y_ref[...] = y.reshape(y_ref.shape)


def _experts(tile_expert, n_active, xs, layer, wg, bg, wu, bu, wd, bd):
    n_slots = xs.shape[0]
    f = wg.shape[3]
    slot = lambda j, te, na: (jnp.minimum(j, na[0] - 1), 0, 0)
    wsel = lambda j, te, na: (layer, te[j], 0, 0)
    return pl.pallas_call(
        _expert_body,
        grid_spec=pltpu.PrefetchScalarGridSpec(
            num_scalar_prefetch=2,
            grid=(n_slots // TME,),
            in_specs=[pl.BlockSpec((TME,) + ROW_TILE, slot),
                      pl.BlockSpec((None, None, D, f), wsel), pl.BlockSpec((None, None, 1, f), wsel),
                      pl.BlockSpec((None, None, D, f), wsel), pl.BlockSpec((None, None, 1, f), wsel),
                      pl.BlockSpec((None, None, f, D), wsel), pl.BlockSpec((None, None, 1, D), wsel)],
            out_specs=pl.BlockSpec((TME,) + ROW_TILE, slot),
            scratch_shapes=[pltpu.VMEM((D, f), BF), pltpu.VMEM((D, f), BF), pltpu.VMEM((f, D), BF)]),
        out_shape=jax.ShapeDtypeStruct((n_slots,) + ROW_TILE, F32),
        compiler_params=_cparams(("arbitrary",), VMEM_LIMIT),
    )(tile_expert, n_active, xs, wg, bg, wu, bu, wd, bd)


def _combine_body(dest_ref, x_ref, mod_ref, route_ref, gf_ref, y_ref, o_ref, buf, sem, *, final):
    tmv = x_ref.shape[0]

    def body(t, c):
        for r in range(TOP_K):
            pltpu.make_async_copy(y_ref.at[pl.ds(dest_ref[0, t * TOP_K + r], 1)], buf.at[r, pl.ds(t, 1)],
                                  sem).start(priority=r % 2)
        return c
    lax.fori_loop(0, tmv, body, 0, unroll=8)
    _wait_row_copies(y_ref, buf.at[0], sem, tmv * TOP_K)
    route = route_ref[...]
    acc = jnp.zeros((tmv, D), F32)
    for r in range(TOP_K):
        acc = acc + route[:, TOP_K + r:TOP_K + r + 1] * buf[r].reshape(tmv, D)
    xn = x_ref[...] + mod_ref[5:6, :] * acc
    if final:
        xn = xn * lax.rsqrt(jnp.mean(xn * xn, axis=-1, keepdims=True) + EPS) * gf_ref[...]
    o_ref[...] = xn


def _combine(dest3, x_mid, mods, route, g_final, y, dims, final):
    rows = x_mid.shape[0]
    tmv = dest3.shape[2] // TOP_K
    tps, nb = dims["s"] // tmv, dims["nb"]
    return pl.pallas_call(
        functools.partial(_combine_body, final=final),
        grid=(rows // tmv,),
        in_specs=[pl.BlockSpec((None, 1, tmv * TOP_K), lambda i: (i, 0, 0), memory_space=pltpu.SMEM),
                  pl.BlockSpec((tmv, D), lambda i: (i, 0)),
                  pl.BlockSpec((None, MOD_ROWS, D), lambda i: (jnp.minimum(i // tps, nb), 0, 0)),
                  pl.BlockSpec((tmv, LANES), lambda i: (i, 0)),
                  pl.BlockSpec((1, D), lambda i: (0, 0)),
                  pl.BlockSpec(memory_space=pl.ANY)],
        out_specs=pl.BlockSpec((tmv, D), lambda i: (i, 0)),
        out_shape=jax.ShapeDtypeStruct((rows, D), F32),
        scratch_shapes=[pltpu.VMEM((TOP_K, tmv) + ROW_TILE, F32), pltpu.SemaphoreType.DMA(())],
        compiler_params=_cparams(("arbitrary",), VMEM_LIMIT),
    )(dest3, x_mid, mods, route, g_final, y)


def _moe(x_mid, h2, route, counts, mods, g_final, layer, wg, bg, wu, bu, wd, bd, dims, final):
    rows = x_mid.shape[0]
    n_exp = wg.shape[1]
    n_slots = rows * TOP_K + n_exp * TME
    e = route[:, 0:TOP_K].astype(jnp.int32)
    rank = route[:, 2 * TOP_K:3 * TOP_K].astype(jnp.int32)
    cnt = counts[0, :n_exp].astype(jnp.int32)
    cnt_pad = ((cnt + TME - 1) // TME) * TME
    offs_end = jnp.cumsum(cnt_pad)
    offs = offs_end - cnt_pad
    onehot = e[:, :, None] == jnp.arange(n_exp, dtype=jnp.int32)[None, None, :]
    dest = jnp.sum(jnp.where(onehot, offs[None, None, :], 0), axis=-1) + rank
    tmv = max(t for t in (TM_MOVE, TM) if dims["s"] % t == 0 and dims["t_lat"] % t == 0 and rows % t == 0)
    dest3 = dest.reshape(rows // tmv, 1, tmv * TOP_K)
    tile_start = jnp.arange(n_slots // TME, dtype=jnp.int32) * TME
    tile_expert = jnp.minimum(jnp.sum(tile_start[:, None] >= offs_end[None, :], axis=1), n_exp - 1).astype(jnp.int32)
    n_active = (offs_end[-1:] // TME).astype(jnp.int32)
    xs = _dispatch(dest3, h2, n_slots)
    y = _experts(tile_expert, n_active, xs, layer, wg, bg, wu, bu, wd, bd)
    return _combine(dest3, x_mid, mods, route, g_final, y, dims, final)


def _rope_tables(s, ctx):
    rows = s // GRID_W
    row = np.repeat(np.arange(rows), GRID_W).astype(np.float32)
    col = np.tile(np.arange(GRID_W), rows).astype(np.float32)
    half = HEAD_DIM // 2
    inv = (np.float32(ROPE_BASE) ** (-np.arange(0, half, 2, dtype=np.float32) / half)).astype(np.float32)
    ar = row[:, None] * inv
    ac = col[:, None] * inv
    ang = np.concatenate([ar, ar, ac, ac], axis=-1)
    cos = np.concatenate([np.cos(ang), np.ones((ctx, HEAD_DIM), np.float32)], axis=0)
    sin = np.concatenate([np.sin(ang), np.zeros((ctx, HEAD_DIM), np.float32)], axis=0)
    return (jnp.asarray(np.tile(cos, (1, 2)), F32), jnp.asarray(np.tile(sin, (1, 2)), F32))


def _rot_columns(w):
    d, n = w.shape
    w4 = w.reshape(d, n // HEAD_DIM, 4, HEAD_DIM // 4)
    return jnp.stack([-w4[:, :, 1], w4[:, :, 0], -w4[:, :, 3], w4[:, :, 2]], axis=2).reshape(d, n)


def kernel(x, c, ctx, c_ctx, w_ada, b_ada, g_norm1, w_in, da_lambda, g_subln, w_conv, w_pool, pool_scale,
           w_branch, w_mgate, b_mgate, w_out, g_norm2, w_router, b_router, w_e_gate, b_e_gate, w_e_up,
           b_e_up, w_e_down, b_e_down, g_final):
    nb, s, _ = x.shape
    nctx = ctx.shape[1]
    depth = w_in.shape[0]
    n_exp = w_router.shape[2]
    t_lat, t_ctx = nb * s, nb * nctx
    dims = dict(nb=nb, s=s, ctx=nctx, t_lat=t_lat, t_all=t_lat + t_ctx, nlat=t_lat // TM, tps=s // TM, tpc=nctx // TM)
    assert s % TK == 0 and s // TK >= 2 and s % (FFT_N2 * 8) == 0 and nctx % TM == 0 and nb < COND_ROWS
    assert TM == VT_CHUNK and nctx % VT_CHUNK == 0 and ATT_UNROLL % 2 == 0

    cos_t, sin_t = _rope_tables(s, nctx)
    tabs = _fft_tables(s, nctx)
    cond = jnp.concatenate([c, c_ctx[None, :], jnp.zeros((COND_ROWS - 1 - nb, D), F32)], axis=0)
    mods_all = _adaln(cond, w_ada, b_ada).reshape(depth, COND_ROWS, 6, D)
    mods_all = jnp.concatenate([mods_all, jnp.zeros((depth, COND_ROWS, MOD_ROWS - 6, D), F32)], axis=2)

    xa, xb = x.reshape(t_lat, D), ctx.reshape(t_ctx, D)
    for l in range(depth):
        last = l == depth - 1
        lam_init = 0.8 - 0.6 * math.exp(-0.3 * l)
        mods = mods_all[l]
        w_l = w_in[l]
        w_ext = jnp.concatenate([w_l, _rot_columns(w_l[:, :BW]), _rot_columns(w_l[:, BW:2 * BW])], axis=1).astype(BF)
        qt, k, vt, u, sb, fx, px = _inproj(xa, xb, mods, g_norm1[l][None, :], w_ext, cos_t, sin_t, dims)

        rows = t_lat if last else t_lat + t_ctx
        a = _attention(qt, k, vt, g_subln[l], da_lambda[l], lam_init, dims, latent=True, rows=rows)
        yf = _fourier_latent(fx, tabs, dims, rows)
        if not last:
            a = _attention(qt, k, vt, g_subln[l], da_lambda[l], lam_init, dims, latent=False, fill=a)
            yf = _fourier_ctx(fx, tabs, dims, yf)

        wconv = jnp.concatenate([w_conv[l], jnp.zeros((5, BW), F32)], axis=0)
        wr = jnp.concatenate([w_router[l], jnp.zeros((D, LANES - n_exp), F32)], axis=1).astype(BF)
        br = jnp.concatenate([b_router[l], jnp.full((LANES - n_exp,), -1e30, F32)])[None, :]
        x_mid, h2, route, counts = _merge(
            xa, xb, mods, g_norm1[l][None, :], a, u, sb, yf, px, wconv, w_pool[l].astype(BF), pool_scale[l][None, :],
            w_branch[l].astype(BF), w_mgate[l].astype(BF), b_mgate[l][None, :], w_out[l].astype(BF),
            g_norm2[l][None, :], wr, br, dims, rows)
        x_new = _moe(x_mid, h2, route, counts, mods, g_final[None, :], l,
                     w_e_gate, b_e_gate[:, :, None, :], w_e_up, b_e_up[:, :, None, :],
                     w_e_down, b_e_down[:, :, None, :], dims, final=last)
        if last:
            return x_new.reshape(nb, s, D)
        xa = xb = x_new
```

```python
import functools
import math

import numpy as np
import jax
import jax.numpy as jnp
from jax import lax
from jax.experimental import pallas as pl
from jax.experimental.pallas import tpu as pltpu

F32 = jnp.float32
BF = jnp.bfloat16

D = 1024
HEADS = 4
HEAD_DIM = 64
HEAD_W = 128
BW = 512
GRID_W = 64
EPS = 1e-6
ROPE_BASE = 10000.0
POOL_WINDOWS = (2, 4, 8, 16)
TOP_K = 4
SWIGLU_LIMIT = 7.0
SWIGLU_ALPHA = 1.702
FFT_N2 = 128

TM = 256
TME = 512
TM_MOVE = 1024
TQ = 512
TK = 256
ATT_UNROLL = 2
VT_CHUNK = 256
HALO = 16
LANES = 128
COND_ROWS = 16
MOD_ROWS = 8
ROW_TILE = (D // LANES, LANES)
QSCALE = (HEAD_DIM ** -0.5) * math.log2(math.e)
EXP_HEADROOM = 64.0
NORM_MARGIN = 1.05
VMEM_LIMIT = 56 * 2 ** 20


def _cparams(sem, vmem=None):
    return pltpu.CompilerParams(dimension_semantics=sem, vmem_limit_bytes=vmem)


def _modnorm(x, g, shift, scale):
    y = x * lax.rsqrt(jnp.mean(x * x, axis=-1, keepdims=True) + EPS) * g
    return y * (1.0 + scale) + shift


def _sigmoid(x):
    return 1.0 / (1.0 + jnp.exp(-x))


def _adaln_body(c_ref, w_ref, b_ref, o_ref):
    c = c_ref[...]
    s = c * _sigmoid(c)
    o_ref[...] = jnp.dot(s.astype(BF), w_ref[...].astype(BF), preferred_element_type=F32) + b_ref[...]


def _adaln(c16, w_ada, b_ada):
    nl = w_ada.shape[0]
    tn = 1536
    return pl.pallas_call(
        _adaln_body,
        grid=(nl, 6 * D // tn),
        in_specs=[pl.BlockSpec((COND_ROWS, D), lambda l, j: (0, 0)),
                  pl.BlockSpec((None, D, tn), lambda l, j: (l, 0, j)),
                  pl.BlockSpec((None, 1, tn), lambda l, j: (l, 0, j))],
        out_specs=pl.BlockSpec((None, COND_ROWS, tn), lambda l, j: (l, 0, j)),
        out_shape=jax.ShapeDtypeStruct((nl, COND_ROWS, 6 * D), F32),
        compiler_params=_cparams(("arbitrary", "arbitrary"), VMEM_LIMIT),
    )(c16, w_ada, b_ada.reshape(nl, 1, 6 * D))


def _token_tile(xa_ref, xb_ref, n_a):
    return jnp.where(pl.program_id(0) < n_a, xa_ref[...], xb_ref[...])


def _token_specs(xa, xb):
    n_a, n_b = xa.shape[0] // TM, xb.shape[0] // TM
    return [pl.BlockSpec((TM, D), lambda i: (jnp.minimum(i, n_a - 1), 0)),
            pl.BlockSpec((TM, D), lambda i: (jnp.clip(i - n_a, 0, n_b - 1), 0))]


def _inproj_body(xa_ref, xb_ref, mod_ref, g_ref, w_ref, cos_ref, sin_ref,
                 q_ref, k_ref, v_ref, u_ref, sb_ref, fx_ref, px_ref, *, n_a):
    h = _modnorm(_token_tile(xa_ref, xb_ref, n_a), g_ref[...], mod_ref[0:1, :], mod_ref[1:2, :]).astype(BF)

    def proj(c0):
        return jnp.dot(h, w_ref[:, c0:c0 + BW], preferred_element_type=F32)

    cos = jnp.concatenate([cos_ref[...]] * 4, axis=1)
    sin = jnp.concatenate([sin_ref[...]] * 4, axis=1)
    q_ref[...] = ((proj(0) * cos + proj(8 * BW) * sin) * QSCALE).T.astype(BF)
    k_ref[...] = (proj(BW) * cos + proj(9 * BW) * sin).astype(BF)
    v_ref[...] = proj(2 * BW).T.astype(BF)
    u_ref[...] = (proj(5 * BW) * proj(3 * BW)).astype(BF)
    sb_ref[...] = proj(4 * BW).astype(BF)
    px_ref[...] = proj(7 * BW).astype(BF)
    fx_ref[...] = proj(6 * BW).astype(BF)


def _inproj(xa, xb, mods, g1, w_ext, cos_t, sin_t, dims):
    t_all, nlat, tps, tpc, nb = dims["t_all"], dims["nlat"], dims["tps"], dims["tpc"], dims["nb"]

    def tab_idx(i):
        return (jnp.where(i < nlat, i % tps, tps + (i - nlat) % tpc), 0)

    out = jax.ShapeDtypeStruct((t_all, BW), BF)
    row = pl.BlockSpec((TM, BW), lambda i: (i, 0))
    return pl.pallas_call(
        functools.partial(_inproj_body, n_a=xa.shape[0] // TM),
        grid=(t_all // TM,),
        in_specs=_token_specs(xa, xb) + [
                  pl.BlockSpec((None, MOD_ROWS, D), lambda i: (jnp.minimum(i // tps, nb), 0, 0)),
                  pl.BlockSpec((1, D), lambda i: (0, 0)),
                  pl.BlockSpec((D, 10 * BW), lambda i: (0, 0)),
                  pl.BlockSpec((TM, LANES), tab_idx),
                  pl.BlockSpec((TM, LANES), tab_idx)],
        out_specs=[pl.BlockSpec((BW, TM), lambda i: (0, i)), row,
                   pl.BlockSpec((None, BW, VT_CHUNK), lambda i: (i, 0, 0))] + [row] * 4,
        out_shape=[jax.ShapeDtypeStruct((BW, t_all), BF), out,
                   jax.ShapeDtypeStruct((t_all // VT_CHUNK, BW, VT_CHUNK), BF)] + [out] * 4,
        compiler_params=_cparams(("arbitrary",), VMEM_LIMIT),
    )(xa, xb, mods, g1, w_ext, cos_t, sin_t)


def _attn_body(*refs, n_ctx_chunks, n_lat_chunks, lam_init):
    if n_lat_chunks:
        qt_ref, kc_ref, vtc_ref, kl_ref, vtl_ref, g_ref, dl_ref, o_ref, acc_ref, s_buf, p_buf, l_ref, knorm_ref = refs
    else:
        qt_ref, kc_ref, vtc_ref, g_ref, dl_ref, _, o_ref, acc_ref = refs
    qt = qt_ref[...]
    tq = qt.shape[1]
    row = lax.broadcasted_iota(jnp.int32, qt.shape, 0)
    zero = jnp.zeros_like(qt)
    q2 = jnp.concatenate([jnp.where(row < HEAD_DIM, qt, zero), jnp.where(row >= HEAD_DIM, qt, zero)], axis=1)

    def scores(kc):
        s = jnp.dot(kc, q2, preferred_element_type=F32)
        return s, jnp.max(s, axis=0, keepdims=True)

    def softmax(s, smax, m, l):
        m_new = jnp.maximum(m, smax)
        p = jnp.exp2(s - m_new)
        alpha = jnp.exp2(m - m_new)
        return p.astype(BF), m_new, alpha * l + jnp.sum(p, axis=0, keepdims=True), alpha

    def accumulate(alpha, vt, p):
        acc_ref[...] = alpha * acc_ref[...] + jnp.dot(vt, p, preferred_element_type=F32)

    m = jnp.full((1, 2 * tq), -1e30, F32)
    l = jnp.zeros((1, 2 * tq), F32)
    acc_ref[...] = jnp.zeros_like(acc_ref)
    ctx_scores = [scores(kc_ref[c * VT_CHUNK:(c + 1) * VT_CHUNK, :]) for c in range(n_ctx_chunks)]
    if n_lat_chunks:
        per = TK // VT_CHUNK
        n = n_lat_chunks

        def keys(j):
            return kl_ref[pl.ds(pl.multiple_of(j * TK, TK), TK), :]

        s_buf[0] = jnp.dot(keys(0), q2, preferred_element_type=F32)
    for c in range(n_ctx_chunks):
        p, m, l, alpha = softmax(*ctx_scores[c], m, l)
        accumulate(alpha, vtc_ref[c], p)
    if n_lat_chunks:

        def values(j):
            parts = [vtl_ref[j * per + r] for r in range(per)]
            return parts[0] if per == 1 else jnp.concatenate(parts, axis=1)

        @pl.when(pl.program_id(2) == 0)
        def _():
            d_i = lax.broadcasted_iota(jnp.int32, (HEAD_W, LANES), 0)
            c_i = lax.broadcasted_iota(jnp.int32, (HEAD_W, LANES), 1)
            sel = jnp.where((d_i >= HEAD_DIM) == (c_i == 1), 1.0, 0.0) * jnp.where(c_i < 2, 1.0, 0.0)
            best = jnp.zeros((1, LANES), F32)
            for j in range(n):
                kf = keys(j).astype(F32)
                n2 = jnp.dot((kf * kf).astype(BF), sel.astype(BF), preferred_element_type=F32)
                best = jnp.maximum(best, jnp.max(n2, axis=0, keepdims=True))
            lane2 = lax.broadcasted_iota(jnp.int32, (1, 2 * tq), 1)
            knorm_ref[...] = jnp.sqrt(jnp.where(lane2 < tq, best[:, 0:1], best[:, 1:2]))

        qf = q2.astype(F32)
        bound = jnp.sqrt(jnp.sum(qf * qf, axis=0, keepdims=True)) * knorm_ref[...] * NORM_MARGIN
        gap = jnp.max(bound - m)
        l_ref[...] = l

        @pl.when(gap <= EXP_HEADROOM)
        def _():
            lsum = l
            s = s_buf[0]
            for j in range(n):
                s_next = jnp.dot(keys(j + 1), q2, preferred_element_type=F32) if j + 1 < n else None
                p = jnp.exp2(s - m)
                lsum = lsum + jnp.sum(p, axis=0, keepdims=True)
                acc_ref[...] += jnp.dot(values(j), p.astype(BF), preferred_element_type=F32)
                s = s_next
            l_ref[...] = lsum

        @pl.when(gap > EXP_HEADROOM)
        def _():
            _attn_exact_latent(m, l, n, keys, values, scores, softmax, accumulate, s_buf, p_buf, l_ref)

        l = l_ref[...]
    o = acc_ref[...] / l
    dl = dl_ref[...]
    lam = (jnp.exp(jnp.sum(dl[0:1] * dl[1:2], axis=1, keepdims=True))
           - jnp.exp(jnp.sum(dl[2:3] * dl[3:4], axis=1, keepdims=True)) + lam_init)
    d = o[:, :tq] - lam * o[:, tq:]
    y = d * lax.rsqrt(jnp.mean(d * d, axis=0, keepdims=True) + EPS) * g_ref[...] * (1.0 - lam_init)
    o_ref[...] = y.T.astype(BF)


def _attn_exact_latent(m, l, n, keys, values, scores, softmax, accumulate, s_buf, p_buf, l_ref):
    def half_step(j, prev, cur, carry):
        m, l, alpha_prev, smax = carry
        accumulate(alpha_prev, values(j - 1), p_buf[prev])
        if isinstance(j, int) and j + 1 >= n:
            smax_next = smax
        else:
            s_buf[prev], smax_next = scores(keys(jnp.minimum(j + 1, n - 1)))
        p, m, l, alpha = softmax(s_buf[cur], smax, m, l)
        p_buf[cur] = p
        return m, l, alpha, smax_next

    s0 = s_buf[0]
    p, m, l, alpha = softmax(s0, jnp.max(s0, axis=0, keepdims=True), m, l)
    p_buf[0] = p
    s_buf[1], smax = scores(keys(1))

    def body(i, carry):
        for r in range(ATT_UNROLL):
            carry = half_step(ATT_UNROLL * i + r + 1, r % 2, 1 - r % 2, carry)
        return carry

    n_body = (n - 1) // ATT_UNROLL
    carry = lax.fori_loop(0, n_body, body, (m, l, alpha, smax))
    for j in range(n_body * ATT_UNROLL + 1, n):
        carry = half_step(j, (j - 1) % 2, j % 2, carry)
    m, l, alpha, _ = carry
    accumulate(alpha, values(n - 1), p_buf[(n - 1) % 2])
    l_ref[...] = l


def _attention(qt, k, vt, g_sub, da_lambda, lam_init, dims, latent, rows=None, fill=None):
    nb, s, ctx, t_lat = dims["nb"], dims["s"], dims["ctx"], dims["t_lat"]
    cpb = ctx // VT_CHUNK
    tq = TQ if latent else min(TQ, ctx)
    g_sub_b = jnp.broadcast_to(g_sub[:, None], (HEAD_W, tq))
    common_tail = [pl.BlockSpec((HEAD_W, tq), lambda b, h, qi: (0, 0)),
                   pl.BlockSpec((4, HEAD_DIM), lambda b, h, qi: (0, 0))]
    kc_spec = pl.BlockSpec((ctx, HEAD_W), lambda b, h, qi: (t_lat // ctx + b, h))
    vtc_spec = pl.BlockSpec((cpb, HEAD_W, VT_CHUNK), lambda b, h, qi: (t_lat // ctx + b, h, 0))
    scratch = [pltpu.VMEM((HEAD_W, 2 * tq), F32)]
    if latent:
        nq = s // tq
        in_specs = [pl.BlockSpec((HEAD_W, tq), lambda b, h, qi: (h, b * nq + qi)), kc_spec, vtc_spec,
                    pl.BlockSpec((s, HEAD_W), lambda b, h, qi: (b, h)),
                    pl.BlockSpec((s // VT_CHUNK, HEAD_W, VT_CHUNK), lambda b, h, qi: (b, h, 0))] + common_tail
        args = (qt, k, vt, k, vt, g_sub_b, da_lambda)
        row0, n_lat_chunks, aliases = 0, s // TK, {}
        scratch += [pltpu.VMEM((2, TK, 2 * tq), F32), pltpu.VMEM((2, TK, 2 * tq), BF),
                    pltpu.VMEM((1, 2 * tq), F32), pltpu.VMEM((1, 2 * tq), F32)]
    else:
        nq = ctx // tq
        in_specs = [pl.BlockSpec((HEAD_W, tq), lambda b, h, qi: (h, t_lat // tq + b * nq + qi)),
                    kc_spec, vtc_spec] + common_tail
        in_specs.append(pl.BlockSpec(memory_space=pl.ANY))
        args = (qt, k, vt, g_sub_b, da_lambda, fill)
        row0, n_lat_chunks, rows, aliases = t_lat // tq, 0, fill.shape[0], {5: 0}
    return pl.pallas_call(
        functools.partial(_attn_body, n_ctx_chunks=cpb, n_lat_chunks=n_lat_chunks, lam_init=lam_init),
        grid=(nb, HEADS, nq),
        in_specs=in_specs,
        out_specs=pl.BlockSpec((tq, HEAD_W), lambda b, h, qi: (row0 + b * nq + qi, h)),
        out_shape=jax.ShapeDtypeStruct((rows, HEADS * HEAD_W), BF),
        input_output_aliases=aliases,
        scratch_shapes=scratch,
        compiler_params=_cparams(("arbitrary", "arbitrary", "arbitrary"), VMEM_LIMIT),
    )(*args)


def _channel_dft(x, cs):
    parts = [jnp.dot(x[:, g * LANES:(g + 1) * LANES], cs, preferred_element_type=F32).astype(BF)
             for g in range(x.shape[1] // LANES)]
    return jnp.concatenate([jnp.concatenate([pq[:, :LANES] for pq in parts], axis=1),
                            jnp.concatenate([pq[:, LANES:] for pq in parts], axis=1)], axis=0)


def _fft_a_body(fa_ref, cs_ref, x_ref, zr_ref, zi_ref):
    x = x_ref[...]
    x = _channel_dft(x.reshape(x.shape[0], x.shape[1] * x.shape[2]), cs_ref[...])
    z = jnp.dot(fa_ref[...], x, preferred_element_type=F32)
    zr_ref[...] = z[:FFT_N2].astype(BF).reshape(zr_ref.shape)
    zi_ref[...] = z[FFT_N2:].astype(BF).reshape(zi_ref.shape)


def _fft_b_body(fb_ref, tc_ref, ts_ref, zr_ref, zi_ref, o_ref):
    nk = zr_ref.shape[0]
    parts_r, parts_i = [], []
    for j in range(nk):
        zr = zr_ref[j].astype(F32)
        zi = zi_ref[j].astype(F32)
        tc = jnp.concatenate([tc_ref[j]] * 4, axis=1)
        ts = jnp.concatenate([ts_ref[j]] * 4, axis=1)
        parts_r.append((zr * tc - zi * ts).astype(BF))
        parts_i.append((zr * ts + zi * tc).astype(BF))
    rhs = jnp.concatenate([jnp.concatenate(parts_r, axis=1), jnp.concatenate(parts_i, axis=1)], axis=0)
    o_ref[...] = jnp.dot(fb_ref[...], rhs, preferred_element_type=F32).astype(BF).reshape(o_ref.shape)


def _fft_ctx_body(f_ref, cs_ref, x_ref, _, o_ref):
    x = _channel_dft(x_ref[...], cs_ref[...])
    o_ref[...] = jnp.dot(f_ref[...], x, preferred_element_type=F32).astype(BF)


def _fft_tables(s, ctx):
    n1, n2 = s // FFT_N2, FFT_N2
    norm = 1.0 / math.sqrt(s * LANES)
    a = 2 * np.pi * np.outer(np.arange(n2), np.arange(n2)) / n2
    c, sn = np.cos(a), np.sin(a)
    fa = np.block([[c, -sn], [sn, c]]) * norm
    tw = 2 * np.pi * np.outer(np.arange(n2), np.arange(n1)) / s
    tc = np.repeat(np.cos(tw)[:, :, None], LANES, axis=2)
    ts = np.repeat(np.sin(tw)[:, :, None], LANES, axis=2)
    b = 2 * np.pi * np.outer(np.arange(n1), np.arange(n1)) / n1
    fb = np.concatenate([np.cos(b), -np.sin(b)], axis=1)
    ch = 2 * np.pi * np.outer(np.arange(LANES), np.arange(LANES)) / LANES
    cs = np.concatenate([np.cos(ch), np.sin(ch)], axis=1)
    cx = 2 * np.pi * np.outer(np.arange(ctx), np.arange(ctx)) / ctx
    fctx = np.concatenate([np.cos(cx), -np.sin(cx)], axis=1) / math.sqrt(ctx * LANES)
    return dict(fa=jnp.asarray(fa, BF), tc=jnp.asarray(tc, F32), ts=jnp.asarray(ts, F32),
                fb=jnp.asarray(fb, BF), cs=jnp.asarray(cs, BF), fctx=jnp.asarray(fctx, BF))


def _fourier_latent(fx, tabs, dims, rows):
    nb, s = dims["nb"], dims["s"]
    n1 = s // FFT_N2
    wide = n1 * BW
    nj = min(16, n1)
    wc = nj * BW
    fx3 = fx.reshape(-1, n1, BW)
    zshape = jax.ShapeDtypeStruct((nb, FFT_N2, n1, BW), BF)
    zr4, zi4 = pl.pallas_call(
        _fft_a_body,
        grid=(nb, wide // wc),
        in_specs=[pl.BlockSpec((2 * FFT_N2, 2 * FFT_N2), lambda b, j: (0, 0)),
                  pl.BlockSpec((LANES, 2 * LANES), lambda b, j: (0, 0)),
                  pl.BlockSpec((FFT_N2, nj, BW), lambda b, j: (b, j, 0))],
        out_specs=[pl.BlockSpec((None, FFT_N2, nj, BW), lambda b, j: (b, 0, j, 0))] * 2,
        out_shape=[zshape, zshape],
        compiler_params=_cparams(("arbitrary", "arbitrary"), VMEM_LIMIT),
    )(tabs["fa"], tabs["cs"], fx3)
    kb = 16
    y = pl.pallas_call(
        _fft_b_body,
        grid=(FFT_N2 // kb, nb),
        in_specs=[pl.BlockSpec((n1, 2 * n1), lambda kk, b: (0, 0)),
                  pl.BlockSpec((kb, n1, LANES), lambda kk, b: (kk, 0, 0)),
                  pl.BlockSpec((kb, n1, LANES), lambda kk, b: (kk, 0, 0)),
                  pl.BlockSpec((None, kb, n1, BW), lambda kk, b: (b, kk, 0, 0)),
                  pl.BlockSpec((None, kb, n1, BW), lambda kk, b: (b, kk, 0, 0))],
        out_specs=pl.BlockSpec((n1, kb, BW), lambda kk, b: (b, kk, 0)),
        out_shape=jax.ShapeDtypeStruct((rows // FFT_N2, FFT_N2, BW), BF),
        compiler_params=_cparams(("arbitrary", "arbitrary"), VMEM_LIMIT),
    )(tabs["fb"], tabs["tc"], tabs["ts"], zr4, zi4)
    return y.reshape(rows, BW)


def _fourier_ctx(fx, tabs, dims, fill):
    nb, ctx, t_lat = dims["nb"], dims["ctx"], dims["t_lat"]
    return pl.pallas_call(
        _fft_ctx_body,
        grid=(nb,),
        in_specs=[pl.BlockSpec((ctx, 2 * ctx), lambda b: (0, 0)),
                  pl.BlockSpec((LANES, 2 * LANES), lambda b: (0, 0)),
                  pl.BlockSpec((ctx, BW), lambda b: (t_lat // ctx + b, 0)),
                  pl.BlockSpec(memory_space=pl.ANY)],
        out_specs=pl.BlockSpec((ctx, BW), lambda b: (t_lat // ctx + b, 0)),
        out_shape=jax.ShapeDtypeStruct(fill.shape, BF),
        input_output_aliases={3: 0},
        compiler_params=_cparams(("arbitrary",), VMEM_LIMIT),
    )(tabs["fctx"], tabs["cs"], fx, fill)


def _route_topk(h2b, wr_ref, br_ref):
    logits = jnp.dot(h2b, wr_ref[...], preferred_element_type=F32) + br_ref[...]
    lane = lax.broadcasted_iota(jnp.int32, (TM, LANES), 1).astype(F32)
    cur = logits
    vals, idxs = [], []
    for _ in range(TOP_K):
        mx = jnp.max(cur, axis=1, keepdims=True)
        idx = jnp.min(jnp.where(cur == mx, lane, float(LANES)), axis=1, keepdims=True)
        vals.append(mx)
        idxs.append(idx)
        cur = jnp.where(lane == idx, -jnp.inf, cur)
    ex = [jnp.exp(v - vals[0]) for v in vals]
    den = ex[0] + ex[1] + ex[2] + ex[3]
    member = jnp.zeros((TM, LANES), F32)
    for idx in idxs:
        member = member + jnp.where(lane == idx, 1.0, 0.0)
    return idxs, [e / den for e in ex], member


def _route_ranks(idxs, weights, member, carry_ref, live):
    lane = lax.broadcasted_iota(jnp.int32, (TM, LANES), 1).astype(F32)
    tri = (lax.broadcasted_iota(jnp.int32, (TM, TM), 0) > lax.broadcasted_iota(jnp.int32, (TM, TM), 1))
    before = jnp.dot(jnp.where(tri, 1.0, 0.0).astype(BF), member.astype(BF), preferred_element_type=F32)
    rank_all = carry_ref[0:1, :] + before
    route = jnp.zeros((TM, LANES), F32)
    for r in range(TOP_K):
        rank = jnp.sum(jnp.where(lane == idxs[r], rank_all, 0.0), axis=1, keepdims=True)
        route = (route + jnp.where(lane == float(r), idxs[r], 0.0)
                 + jnp.where(lane == float(TOP_K + r), weights[r], 0.0)
                 + jnp.where(lane == float(2 * TOP_K + r), rank, 0.0))
    carry_ref[...] = carry_ref[...] + live * jnp.sum(member, axis=0, keepdims=True)
    return route


def _merge_body(xa_ref, xb_ref, mod_ref, g1_ref, a_ref, u_ref, up_ref, un_ref, sb_ref, yf_ref, px_ref, pp_ref, pn_ref,
                wconv_ref, wpool_ref, pscale_ref, wb_ref, wg_ref, bg_ref, wo_ref, g2_ref, wr_ref, br_ref,
                xo_ref, h2_ref, route_ref, cnt_ref, carry_ref, h2p_ref, *, n_tiles, nlat, tps, tpc, s, ctx, n_a):
    step = pl.program_id(0)
    i = jnp.minimum(step, n_tiles - 1)
    is_lat = i < nlat
    j = jnp.where(is_lat, i % tps, (i - nlat) % tpc)
    first = j == 0
    last = j == jnp.where(is_lat, tps, tpc) - 1
    seq_len = jnp.where(is_lat, s, ctx)

    @pl.when(step == 0)
    def _():
        carry_ref[...] = jnp.zeros_like(carry_ref)
        h2p_ref[...] = jnp.zeros_like(h2p_ref)

    topk_prev = _route_topk(h2p_ref[...], wr_ref, br_ref)

    x = jnp.where(i < n_a, xa_ref[...], xb_ref[...])
    h = _modnorm(x, g1_ref[...], mod_ref[0:1, :], mod_ref[1:2, :]).astype(BF)
    rowi = lax.broadcasted_iota(jnp.int32, (TM, 1), 0)

    u = u_ref[...].astype(F32)
    u_prev = jnp.where(first, 0.0, up_ref[...].astype(F32)[HALO - 1:HALO, :])
    u_next = jnp.where(last, 0.0, un_ref[...].astype(F32)[0:1, :])
    u_dn = jnp.where(rowi == 0, u_prev, pltpu.roll(u, 1, axis=0))
    u_up = jnp.where(rowi == TM - 1, u_next, pltpu.roll(u, TM - 1, axis=0))
    wc = wconv_ref[...]
    y_conv = sb_ref[...].astype(F32) * (wc[0:1] * u_dn + wc[1:2] * u + wc[2:3] * u_up)

    px = px_ref[...].astype(F32)
    ext = jnp.concatenate([jnp.where(first, 0.0, pp_ref[...].astype(F32)), px,
                           jnp.where(last, 0.0, pn_ref[...].astype(F32))], axis=0)
    ext_len = TM + 2 * HALO
    pos = j * TM + rowi
    pooled = []
    for g, wd in enumerate(POOL_WINDOWS):
        e = ext[:, g * LANES:(g + 1) * LANES]
        win = e + pltpu.roll(e, 1, axis=0)
        half = 1
        while 2 * half < wd:
            win = pltpu.roll(win, half, axis=0) + pltpu.roll(win, ext_len - half, axis=0)
            half *= 2
        win = win[HALO:HALO + TM]
        cnt = (jnp.minimum(pos + wd // 2, seq_len) - jnp.maximum(pos - wd // 2, 0)).astype(F32)
        pooled.append((win / cnt - px[:, g * LANES:(g + 1) * LANES]).astype(BF))

    gates = [_sigmoid(jnp.dot(h, wg_ref[:, kk * D:(kk + 1) * D], preferred_element_type=F32)
                      + bg_ref[:, kk * D:(kk + 1) * D]) for kk in range(4)]

    def branch(kk, y):
        return gates[kk] * jnp.dot(y, wb_ref[kk], preferred_element_type=F32)

    acc = branch(0, a_ref[...]) + branch(2, yf_ref[...]) + branch(1, y_conv.astype(BF))
    y_pool = jnp.concatenate([jnp.dot(pooled[g], wpool_ref[g], preferred_element_type=F32)
                              for g in range(len(POOL_WINDOWS))], axis=1) * pscale_ref[...]
    acc = acc + branch(3, y_pool.astype(BF))

    xn = x + mod_ref[2:3, :] * jnp.dot(acc.astype(BF), wo_ref[...], preferred_element_type=F32)
    xo_ref[...] = xn
    h2 = _modnorm(xn, g2_ref[...], mod_ref[3:4, :], mod_ref[4:5, :])
    h2_ref[...] = h2.reshape(h2_ref.shape)
    h2p_ref[...] = h2.astype(BF)
    route_ref[...] = _route_ranks(*topk_prev, carry_ref, jnp.where(step > 0, 1.0, 0.0))
    cnt_ref[...] = carry_ref[...]


def _merge(xa, xb, mods, g1, a, u, sb, yf, px, wconv, wpool, pscale, wb, wg, bg, wo, g2, wr, br, dims, rows):
    nlat, tps, tpc, nb = dims["nlat"], dims["tps"], dims["tpc"], dims["nb"]
    n_tiles = rows // TM
    n_a, n_b = xa.shape[0] // TM, xb.shape[0] // TM
    nhalo = TM // HALO
    last_halo = u.shape[0] // HALO - 1
    cur = lambda i: jnp.minimum(i, n_tiles - 1)
    tile = lambda w: pl.BlockSpec((TM, w), lambda i: (cur(i), 0))
    prev = pl.BlockSpec((HALO, BW), lambda i: (jnp.maximum(cur(i) * nhalo - 1, 0), 0))
    nxt = pl.BlockSpec((HALO, BW), lambda i: (jnp.minimum((cur(i) + 1) * nhalo, last_halo), 0))
    const = lambda shape: pl.BlockSpec(shape, lambda i: (0,) * len(shape))
    return pl.pallas_call(
        functools.partial(_merge_body, n_tiles=n_tiles, nlat=nlat, tps=tps, tpc=tpc, s=dims["s"], ctx=dims["ctx"],
                          n_a=n_a),
        grid=(n_tiles + 1,),
        in_specs=[pl.BlockSpec((TM, D), lambda i: (jnp.minimum(cur(i), n_a - 1), 0)),
                  pl.BlockSpec((TM, D), lambda i: (jnp.clip(cur(i) - n_a, 0, n_b - 1), 0)),
                  pl.BlockSpec((None, MOD_ROWS, D), lambda i: (jnp.minimum(cur(i) // tps, nb), 0, 0)),
                  const((1, D)), tile(BW), tile(BW), prev, nxt, tile(BW), tile(BW), tile(BW), prev, nxt,
                  const((8, BW)), const((4, LANES, LANES)), const((1, BW)), const((4, BW, D)),
                  const((D, 4 * D)), const((1, 4 * D)), const((D, D)), const((1, D)),
                  const((D, LANES)), const((1, LANES))],
        out_specs=[tile(D), pl.BlockSpec((TM,) + ROW_TILE, lambda i: (cur(i), 0, 0)),
                   pl.BlockSpec((TM, LANES), lambda i: (jnp.maximum(i - 1, 0), 0)),
                   pl.BlockSpec((8, LANES), lambda i: (0, 0))],
        out_shape=[jax.ShapeDtypeStruct((rows, D), F32), jax.ShapeDtypeStruct((rows,) + ROW_TILE, F32),
                   jax.ShapeDtypeStruct((rows, LANES), F32), jax.ShapeDtypeStruct((8, LANES), F32)],
        scratch_shapes=[pltpu.VMEM((8, LANES), F32), pltpu.VMEM((TM, D), BF)],
        compiler_params=_cparams(("arbitrary",), VMEM_LIMIT),
    )(xa, xb, mods, g1, a, u, u, u, sb, yf, px, px, px, wconv, wpool, pscale, wb, wg, bg, wo, g2, wr, br)


def _wait_row_copies(src_ref, dst_ref, sem, n):
    for _ in range(n):
        pltpu.make_async_copy(src_ref.at[pl.ds(0, 1)], dst_ref.at[pl.ds(0, 1)], sem).wait()


def _dispatch_body(dest_ref, h_ref, xs_ref, sem):
    def body(t, c):
        for r in range(TOP_K):
            pltpu.make_async_copy(h_ref.at[pl.ds(t, 1)], xs_ref.at[pl.ds(dest_ref[0, t * TOP_K + r], 1)],
                                  sem).start(priority=r % 2)
        return c
    lax.fori_loop(0, h_ref.shape[0], body, 0, unroll=8)
    _wait_row_copies(h_ref, xs_ref, sem, h_ref.shape[0] * TOP_K)


def _dispatch(dest3, h2, n_slots):
    rows = h2.shape[0]
    tmv = dest3.shape[2] // TOP_K
    return pl.pallas_call(
        _dispatch_body,
        grid=(rows // tmv,),
        in_specs=[pl.BlockSpec((None, 1, tmv * TOP_K), lambda i: (i, 0, 0), memory_space=pltpu.SMEM),
                  pl.BlockSpec((tmv,) + ROW_TILE, lambda i: (i, 0, 0))],
        out_specs=pl.BlockSpec(memory_space=pl.ANY),
        out_shape=jax.ShapeDtypeStruct((n_slots,) + ROW_TILE, F32),
        scratch_shapes=[pltpu.SemaphoreType.DMA(())],
        compiler_params=_cparams(("arbitrary",), VMEM_LIMIT),
    )(dest3, h2)


def _expert_body(te_ref, na_ref, xs_ref, wg_ref, bg_ref, wu_ref, bu_ref, wd_ref, bd_ref, y_ref, wgb, wub, wdb):
    j = pl.program_id(0)

    @pl.when(j < na_ref[0])
    def _():
        @pl.when((j == 0) | (te_ref[j] != te_ref[jnp.maximum(j - 1, 0)]))
        def _():
            wgb[...] = wg_ref[...].astype(BF)
            wub[...] = wu_ref[...].astype(BF)
            wdb[...] = wd_ref[...].astype(BF)

        x = xs_ref[...].reshape(TME, D).astype(BF)
        a = jnp.minimum(jnp.dot(x, wgb[...], preferred_element_type=F32) + bg_ref[...], SWIGLU_LIMIT)
        u = jnp.clip(jnp.dot(x, wub[...], preferred_element_type=F32) + bu_ref[...], -SWIGLU_LIMIT, SWIGLU_LIMIT)
        act = a * _sigmoid(SWIGLU_ALPHA * a) * (u + 1.0)
        y = jnp.dot(act.astype(BF), wdb[...], preferred_element_type=F32) + bd_ref[...]
        y_ref[...] = y.reshape(y_ref.shape)


def _experts(tile_expert, n_active, xs, layer, wg, bg, wu, bu, wd, bd):
    n_slots = xs.shape[0]
    f = wg.shape[3]
    slot = lambda j, te, na: (jnp.minimum(j, na[0] - 1), 0, 0)
    wsel = lambda j, te, na: (layer, te[j], 0, 0)
    return pl.pallas_call(
        _expert_body,
        grid_spec=pltpu.PrefetchScalarGridSpec(
            num_scalar_prefetch=2,
            grid=(n_slots // TME,),
            in_specs=[pl.BlockSpec((TME,) + ROW_TILE, slot),
                      pl.BlockSpec((None, None, D, f), wsel), pl.BlockSpec((None, None, 1, f), wsel),
                      pl.BlockSpec((None, None, D, f), wsel), pl.BlockSpec((None, None, 1, f), wsel),
                      pl.BlockSpec((None, None, f, D), wsel), pl.BlockSpec((None, None, 1, D), wsel)],
            out_specs=pl.BlockSpec((TME,) + ROW_TILE, slot),
            scratch_shapes=[pltpu.VMEM((D, f), BF), pltpu.VMEM((D, f), BF), pltpu.VMEM((f, D), BF)]),
        out_shape=jax.ShapeDtypeStruct((n_slots,) + ROW_TILE, F32),
        compiler_params=_cparams(("arbitrary",), VMEM_LIMIT),
    )(tile_expert, n_active, xs, wg, bg, wu, bu, wd, bd)


def _combine_body(dest_ref, x_ref, mod_ref, route_ref, gf_ref, y_ref, o_ref, buf, sem, *, final):
    tmv = x_ref.shape[0]

    def body(t, c):
        for r in range(TOP_K):
            pltpu.make_async_copy(y_ref.at[pl.ds(dest_ref[0, t * TOP_K + r], 1)], buf.at[r, pl.ds(t, 1)],
                                  sem).start(priority=r % 2)
        return c
    lax.fori_loop(0, tmv, body, 0, unroll=8)
    _wait_row_copies(y_ref, buf.at[0], sem, tmv * TOP_K)
    route = route_ref[...]
    acc = jnp.zeros((tmv, D), F32)
    for r in range(TOP_K):
        acc = acc + route[:, TOP_K + r:TOP_K + r + 1] * buf[r].reshape(tmv, D)
    xn = x_ref[...] + mod_ref[5:6, :] * acc
    if final:
        xn = xn * lax.rsqrt(jnp.mean(xn * xn, axis=-1, keepdims=True) + EPS) * gf_ref[...]
    o_ref[...] = xn


def _combine(dest3, x_mid, mods, route, g_final, y, dims, final):
    rows = x_mid.shape[0]
    tmv = dest3.shape[2] // TOP_K
    tps, nb = dims["s"] // tmv, dims["nb"]
    return pl.pallas_call(
        functools.partial(_combine_body, final=final),
        grid=(rows // tmv,),
        in_specs=[pl.BlockSpec((None, 1, tmv * TOP_K), lambda i: (i, 0, 0), memory_space=pltpu.SMEM),
                  pl.BlockSpec((tmv, D), lambda i: (i, 0)),
                  pl.BlockSpec((None, MOD_ROWS, D), lambda i: (jnp.minimum(i // tps, nb), 0, 0)),
                  pl.BlockSpec((tmv, LANES), lambda i: (i, 0)),
                  pl.BlockSpec((1, D), lambda i: (0, 0)),
                  pl.BlockSpec(memory_space=pl.ANY)],
        out_specs=pl.BlockSpec((tmv, D), lambda i: (i, 0)),
        out_shape=jax.ShapeDtypeStruct((rows, D), F32),
        scratch_shapes=[pltpu.VMEM((TOP_K, tmv) + ROW_TILE, F32), pltpu.SemaphoreType.DMA(())],
        compiler_params=_cparams(("arbitrary",), VMEM_LIMIT),
    )(dest3, x_mid, mods, route, g_final, y)


def _moe(x_mid, h2, route, counts, mods, g_final, layer, wg, bg, wu, bu, wd, bd, dims, final):
    rows = x_mid.shape[0]
    n_exp = wg.shape[1]
    n_slots = rows * TOP_K + n_exp * TME
    e = route[:, 0:TOP_K].astype(jnp.int32)
    rank = route[:, 2 * TOP_K:3 * TOP_K].astype(jnp.int32)
    cnt = counts[0, :n_exp].astype(jnp.int32)
    cnt_pad = ((cnt + TME - 1) // TME) * TME
    offs_end = jnp.cumsum(cnt_pad)
    offs = offs_end - cnt_pad
    onehot = e[:, :, None] == jnp.arange(n_exp, dtype=jnp.int32)[None, None, :]
    dest = jnp.sum(jnp.where(onehot, offs[None, None, :], 0), axis=-1) + rank
    tmv = max(t for t in (TM_MOVE, TM) if dims["s"] % t == 0 and dims["t_lat"] % t == 0 and rows % t == 0)
    dest3 = dest.reshape(rows // tmv, 1, tmv * TOP_K)
    tile_start = jnp.arange(n_slots // TME, dtype=jnp.int32) * TME
    tile_expert = jnp.minimum(jnp.sum(tile_start[:, None] >= offs_end[None, :], axis=1), n_exp - 1).astype(jnp.int32)
    n_active = (offs_end[-1:] // TME).astype(jnp.int32)
    xs = _dispatch(dest3, h2, n_slots)
    y = _experts(tile_expert, n_active, xs, layer, wg, bg, wu, bu, wd, bd)
    return _combine(dest3, x_mid, mods, route, g_final, y, dims, final)


def _rope_tables(s, ctx):
    rows = s // GRID_W
    row = np.repeat(np.arange(rows), GRID_W).astype(np.float32)
    col = np.tile(np.arange(GRID_W), rows).astype(np.float32)
    half = HEAD_DIM // 2
    inv = (np.float32(ROPE_BASE) ** (-np.arange(0, half, 2, dtype=np.float32) / half)).astype(np.float32)
    ar = row[:, None] * inv
    ac = col[:, None] * inv
    ang = np.concatenate([ar, ar, ac, ac], axis=-1)
    cos = np.concatenate([np.cos(ang), np.ones((ctx, HEAD_DIM), np.float32)], axis=0)
    sin = np.concatenate([np.sin(ang), np.zeros((ctx, HEAD_DIM), np.float32)], axis=0)
    return (jnp.asarray(np.tile(cos, (1, 2)), F32), jnp.asarray(np.tile(sin, (1, 2)), F32))


def _rot_columns(w):
    d, n = w.shape
    w4 = w.reshape(d, n // HEAD_DIM, 4, HEAD_DIM // 4)
    return jnp.stack([-w4[:, :, 1], w4[:, :, 0], -w4[:, :, 3], w4[:, :, 2]], axis=2).reshape(d, n)


def kernel(x, c, ctx, c_ctx, w_ada, b_ada, g_norm1, w_in, da_lambda, g_subln, w_conv, w_pool, pool_scale,
           w_branch, w_mgate, b_mgate, w_out, g_norm2, w_router, b_router, w_e_gate, b_e_gate, w_e_up,
           b_e_up, w_e_down, b_e_down, g_final):
    nb, s, _ = x.shape
    nctx = ctx.shape[1]
    depth = w_in.shape[0]
    n_exp = w_router.shape[2]
    t_lat, t_ctx = nb * s, nb * nctx
    dims = dict(nb=nb, s=s, ctx=nctx, t_lat=t_lat, t_all=t_lat + t_ctx, nlat=t_lat // TM, tps=s // TM, tpc=nctx // TM)
    assert s % TK == 0 and s // TK >= 2 and s % (FFT_N2 * 8) == 0 and nctx % TM == 0 and nb < COND_ROWS
    assert TM == VT_CHUNK and nctx % VT_CHUNK == 0 and ATT_UNROLL % 2 == 0

    cos_t, sin_t = _rope_tables(s, nctx)
    tabs = _fft_tables(s, nctx)
    cond = jnp.concatenate([c, c_ctx[None, :], jnp.zeros((COND_ROWS - 1 - nb, D), F32)], axis=0)
    mods_all = _adaln(cond, w_ada, b_ada).reshape(depth, COND_ROWS, 6, D)
    mods_all = jnp.concatenate([mods_all, jnp.zeros((depth, COND_ROWS, MOD_ROWS - 6, D), F32)], axis=2)

    xa, xb = x.reshape(t_lat, D), ctx.reshape(t_ctx, D)
    for l in range(depth):
        last = l == depth - 1
        lam_init = 0.8 - 0.6 * math.exp(-0.3 * l)
        mods = mods_all[l]
        w_l = w_in[l]
        w_ext = jnp.concatenate([w_l, _rot_columns(w_l[:, :BW]), _rot_columns(w_l[:, BW:2 * BW])], axis=1).astype(BF)
        qt, k, vt, u, sb, fx, px = _inproj(xa, xb, mods, g_norm1[l][None, :], w_ext, cos_t, sin_t, dims)

        rows = t_lat if last else t_lat + t_ctx
        a = _attention(qt, k, vt, g_subln[l], da_lambda[l], lam_init, dims, latent=True, rows=rows)
        yf = _fourier_latent(fx, tabs, dims, rows)
        if not last:
            a = _attention(qt, k, vt, g_subln[l], da_lambda[l], lam_init, dims, latent=False, fill=a)
            yf = _fourier_ctx(fx, tabs, dims, yf)

        wconv = jnp.concatenate([w_conv[l], jnp.zeros((5, BW), F32)], axis=0)
        wr = jnp.concatenate([w_router[l], jnp.zeros((D, LANES - n_exp), F32)], axis=1).astype(BF)
        br = jnp.concatenate([b_router[l], jnp.full((LANES - n_exp,), -1e30, F32)])[None, :]
        x_mid, h2, route, counts = _merge(
            xa, xb, mods, g_norm1[l][None, :], a, u, sb, yf, px, wconv, w_pool[l].astype(BF), pool_scale[l][None, :],
            w_branch[l].astype(BF), w_mgate[l].astype(BF), b_mgate[l][None, :], w_out[l].astype(BF),
            g_norm2[l][None, :], wr, br, dims, rows)
        x_new = _moe(x_mid, h2, route, counts, mods, g_final[None, :], l,
                     w_e_gate, b_e_gate[:, :, None, :], w_e_up, b_e_up[:, :, None, :],
                     w_e_down, b_e_down[:, :, None, :], dims, final=last)
        if last:
            return x_new.reshape(nb, s, D)
        xa = xb = x_new
```

```python
import functools
import math

import numpy as np
import jax
import jax.numpy as jnp
from jax import lax
from jax.experimental import pallas as pl
from jax.experimental.pallas import tpu as pltpu

F32 = jnp.float32
BF = jnp.bfloat16

D = 1024
HEADS = 4
HEAD_DIM = 64
HEAD_W = 128
BW = 512
GRID_W = 64
EPS = 1e-6
ROPE_BASE = 10000.0
POOL_WINDOWS = (2, 4, 8, 16)
TOP_K = 4
SWIGLU_LIMIT = 7.0
SWIGLU_ALPHA = 1.702
FFT_N2 = 128

TM = 256
TME = 512
TM_MOVE = 1024
TQ = 512
TK = 256
ATT_UNROLL = 2
VT_CHUNK = 256
HALO = 16
LANES = 128
COND_ROWS = 16
MOD_ROWS = 8
ROW_TILE = (D // LANES, LANES)
QSCALE = (HEAD_DIM ** -0.5) * math.log2(math.e)
EXP_HEADROOM = 64.0
NORM_MARGIN = 1.05
VMEM_LIMIT = 56 * 2 ** 20


def _cparams(sem, vmem=None):
    return pltpu.CompilerParams(dimension_semantics=sem, vmem_limit_bytes=vmem)


def _modnorm(x, g, shift, scale):
    y = x * lax.rsqrt(jnp.mean(x * x, axis=-1, keepdims=True) + EPS) * g
    return y * (1.0 + scale) + shift


def _sigmoid(x):
    return 1.0 / (1.0 + jnp.exp(-x))


def _adaln_body(c_ref, w_ref, b_ref, o_ref):
    c = c_ref[...]
    s = c * _sigmoid(c)
    o_ref[...] = jnp.dot(s.astype(BF), w_ref[...].astype(BF), preferred_element_type=F32) + b_ref[...]


def _adaln(c16, w_ada, b_ada):
    nl = w_ada.shape[0]
    tn = 1536
    return pl.pallas_call(
        _adaln_body,
        grid=(nl, 6 * D // tn),
        in_specs=[pl.BlockSpec((COND_ROWS, D), lambda l, j: (0, 0)),
                  pl.BlockSpec((None, D, tn), lambda l, j: (l, 0, j)),
                  pl.BlockSpec((None, 1, tn), lambda l, j: (l, 0, j))],
        out_specs=pl.BlockSpec((None, COND_ROWS, tn), lambda l, j: (l, 0, j)),
        out_shape=jax.ShapeDtypeStruct((nl, COND_ROWS, 6 * D), F32),
        compiler_params=_cparams(("arbitrary", "arbitrary"), VMEM_LIMIT),
    )(c16, w_ada, b_ada.reshape(nl, 1, 6 * D))


def _token_tile(xa_ref, xb_ref, n_a):
    return jnp.where(pl.program_id(0) < n_a, xa_ref[...], xb_ref[...])


def _token_specs(xa, xb):
    n_a, n_b = xa.shape[0] // TM, xb.shape[0] // TM
    return [pl.BlockSpec((TM, D), lambda i: (jnp.minimum(i, n_a - 1), 0)),
            pl.BlockSpec((TM, D), lambda i: (jnp.clip(i - n_a, 0, n_b - 1), 0))]


def _inproj_body(xa_ref, xb_ref, mod_ref, g_ref, w_ref, cos_ref, sin_ref,
                 q_ref, k_ref, v_ref, u_ref, sb_ref, fx_ref, px_ref, *, n_a):
    h = _modnorm(_token_tile(xa_ref, xb_ref, n_a), g_ref[...], mod_ref[0:1, :], mod_ref[1:2, :]).astype(BF)

    def proj(c0):
        return jnp.dot(h, w_ref[:, c0:c0 + BW], preferred_element_type=F32)

    cos = jnp.concatenate([cos_ref[...]] * 4, axis=1)
    sin = jnp.concatenate([sin_ref[...]] * 4, axis=1)
    quarter = HEAD_DIM // 4
    even = (lax.broadcasted_iota(jnp.int32, (TM, BW), 1) // quarter) % 2 == 0

    def rope(x):
        rot = jnp.where(even, -pltpu.roll(x, BW - quarter, axis=1), pltpu.roll(x, quarter, axis=1))
        return x * cos + rot * sin

    q_ref[...] = (rope(proj(0)) * QSCALE).T.astype(BF)
    k_ref[...] = rope(proj(BW)).astype(BF)
    v_ref[...] = proj(2 * BW).T.astype(BF)
    u_ref[...] = (proj(5 * BW) * proj(3 * BW)).astype(BF)
    sb_ref[...] = proj(4 * BW).astype(BF)
    px_ref[...] = proj(7 * BW).astype(BF)
    fx_ref[...] = proj(6 * BW).astype(BF)


def _inproj(xa, xb, mods, g1, w_ext, cos_t, sin_t, dims):
    t_all, nlat, tps, tpc, nb = dims["t_all"], dims["nlat"], dims["tps"], dims["tpc"], dims["nb"]

    def tab_idx(i):
        return (jnp.where(i < nlat, i % tps, tps + (i - nlat) % tpc), 0)

    out = jax.ShapeDtypeStruct((t_all, BW), BF)
    row = pl.BlockSpec((TM, BW), lambda i: (i, 0))
    return pl.pallas_call(
        functools.partial(_inproj_body, n_a=xa.shape[0] // TM),
        grid=(t_all // TM,),
        in_specs=_token_specs(xa, xb) + [
                  pl.BlockSpec((None, MOD_ROWS, D), lambda i: (jnp.minimum(i // tps, nb), 0, 0)),
                  pl.BlockSpec((1, D), lambda i: (0, 0)),
                  pl.BlockSpec((D, 8 * BW), lambda i: (0, 0)),
                  pl.BlockSpec((TM, LANES), tab_idx),
                  pl.BlockSpec((TM, LANES), tab_idx)],
        out_specs=[pl.BlockSpec((BW, TM), lambda i: (0, i)), row,
                   pl.BlockSpec((None, BW, VT_CHUNK), lambda i: (i, 0, 0))] + [row] * 4,
        out_shape=[jax.ShapeDtypeStruct((BW, t_all), BF), out,
                   jax.ShapeDtypeStruct((t_all // VT_CHUNK, BW, VT_CHUNK), BF)] + [out] * 4,
        compiler_params=_cparams(("arbitrary",), VMEM_LIMIT),
    )(xa, xb, mods, g1, w_ext, cos_t, sin_t)


def _attn_body(*refs, n_ctx_chunks, n_lat_chunks, lam_init):
    if n_lat_chunks:
        qt_ref, kc_ref, vtc_ref, kl_ref, vtl_ref, g_ref, dl_ref, o_ref, acc_ref, s_buf, p_buf, l_ref, knorm_ref = refs
    else:
        qt_ref, kc_ref, vtc_ref, g_ref, dl_ref, _, o_ref, acc_ref = refs
    qt = qt_ref[...]
    tq = qt.shape[1]
    row = lax.broadcasted_iota(jnp.int32, qt.shape, 0)
    zero = jnp.zeros_like(qt)
    q2 = jnp.concatenate([jnp.where(row < HEAD_DIM, qt, zero), jnp.where(row >= HEAD_DIM, qt, zero)], axis=1)

    def scores(kc):
        s = jnp.dot(kc, q2, preferred_element_type=F32)
        return s, jnp.max(s, axis=0, keepdims=True)

    def softmax(s, smax, m, l):
        m_new = jnp.maximum(m, smax)
        p = jnp.exp2(s - m_new)
        alpha = jnp.exp2(m - m_new)
        return p.astype(BF), m_new, alpha * l + jnp.sum(p, axis=0, keepdims=True), alpha

    def accumulate(alpha, vt, p):
        acc_ref[...] = alpha * acc_ref[...] + jnp.dot(vt, p, preferred_element_type=F32)

    m = jnp.full((1, 2 * tq), -1e30, F32)
    l = jnp.zeros((1, 2 * tq), F32)
    acc_ref[...] = jnp.zeros_like(acc_ref)
    ctx_scores = [scores(kc_ref[c * VT_CHUNK:(c + 1) * VT_CHUNK, :]) for c in range(n_ctx_chunks)]
    if n_lat_chunks:
        per = TK // VT_CHUNK
        n = n_lat_chunks

        def keys(j):
            return kl_ref[pl.ds(pl.multiple_of(j * TK, TK), TK), :]

        s_buf[0] = jnp.dot(keys(0), q2, preferred_element_type=F32)
    for c in range(n_ctx_chunks):
        p, m, l, alpha = softmax(*ctx_scores[c], m, l)
        accumulate(alpha, vtc_ref[c], p)
    if n_lat_chunks:

        def values(j):
            parts = [vtl_ref[j * per + r] for r in range(per)]
            return parts[0] if per == 1 else jnp.concatenate(parts, axis=1)

        @pl.when(pl.program_id(2) == 0)
        def _():
            d_i = lax.broadcasted_iota(jnp.int32, (HEAD_W, LANES), 0)
            c_i = lax.broadcasted_iota(jnp.int32, (HEAD_W, LANES), 1)
            sel = jnp.where((d_i >= HEAD_DIM) == (c_i == 1), 1.0, 0.0) * jnp.where(c_i < 2, 1.0, 0.0)
            best = jnp.zeros((1, LANES), F32)
            for j in range(n):
                kf = keys(j).astype(F32)
                n2 = jnp.dot((kf * kf).astype(BF), sel.astype(BF), preferred_element_type=F32)
                best = jnp.maximum(best, jnp.max(n2, axis=0, keepdims=True))
            lane2 = lax.broadcasted_iota(jnp.int32, (1, 2 * tq), 1)
            knorm_ref[...] = jnp.sqrt(jnp.where(lane2 < tq, best[:, 0:1], best[:, 1:2]))

        qf = q2.astype(F32)
        bound = jnp.sqrt(jnp.sum(qf * qf, axis=0, keepdims=True)) * knorm_ref[...] * NORM_MARGIN
        gap = jnp.max(bound - m)
        l_ref[...] = l

        @pl.when(gap <= EXP_HEADROOM)
        def _():
            lsum = l
            s = s_buf[0]
            for j in range(n):
                s_next = jnp.dot(keys(j + 1), q2, preferred_element_type=F32) if j + 1 < n else None
                p = jnp.exp2(s - m)
                lsum = lsum + jnp.sum(p, axis=0, keepdims=True)
                acc_ref[...] += jnp.dot(values(j), p.astype(BF), preferred_element_type=F32)
                s = s_next
            l_ref[...] = lsum

        @pl.when(gap > EXP_HEADROOM)
        def _():
            _attn_exact_latent(m, l, n, keys, values, scores, softmax, accumulate, s_buf, p_buf, l_ref)

        l = l_ref[...]
    o = acc_ref[...] / l
    dl = dl_ref[...]
    lam = (jnp.exp(jnp.sum(dl[0:1] * dl[1:2], axis=1, keepdims=True))
           - jnp.exp(jnp.sum(dl[2:3] * dl[3:4], axis=1, keepdims=True)) + lam_init)
    d = o[:, :tq] - lam * o[:, tq:]
    y = d * lax.rsqrt(jnp.mean(d * d, axis=0, keepdims=True) + EPS) * g_ref[...] * (1.0 - lam_init)
    o_ref[...] = y.T.astype(BF)


def _attn_exact_latent(m, l, n, keys, values, scores, softmax, accumulate, s_buf, p_buf, l_ref):
    def half_step(j, prev, cur, carry):
        m, l, alpha_prev, smax = carry
        accumulate(alpha_prev, values(j - 1), p_buf[prev])
        if isinstance(j, int) and j + 1 >= n:
            smax_next = smax
        else:
            s_buf[prev], smax_next = scores(keys(jnp.minimum(j + 1, n - 1)))
        p, m, l, alpha = softmax(s_buf[cur], smax, m, l)
        p_buf[cur] = p
        return m, l, alpha, smax_next

    s0 = s_buf[0]
    p, m, l, alpha = softmax(s0, jnp.max(s0, axis=0, keepdims=True), m, l)
    p_buf[0] = p
    s_buf[1], smax = scores(keys(1))

    def body(i, carry):
        for r in range(ATT_UNROLL):
            carry = half_step(ATT_UNROLL * i + r + 1, r % 2, 1 - r % 2, carry)
        return carry

    n_body = (n - 1) // ATT_UNROLL
    carry = lax.fori_loop(0, n_body, body, (m, l, alpha, smax))
    for j in range(n_body * ATT_UNROLL + 1, n):
        carry = half_step(j, (j - 1) % 2, j % 2, carry)
    m, l, alpha, _ = carry
    accumulate(alpha, values(n - 1), p_buf[(n - 1) % 2])
    l_ref[...] = l


def _attention(qt, k, vt, g_sub, da_lambda, lam_init, dims, latent, rows=None, fill=None):
    nb, s, ctx, t_lat = dims["nb"], dims["s"], dims["ctx"], dims["t_lat"]
    cpb = ctx // VT_CHUNK
    tq = TQ if latent else min(TQ, ctx)
    g_sub_b = jnp.broadcast_to(g_sub[:, None], (HEAD_W, tq))
    common_tail = [pl.BlockSpec((HEAD_W, tq), lambda b, h, qi: (0, 0)),
                   pl.BlockSpec((4, HEAD_DIM), lambda b, h, qi: (0, 0))]
    kc_spec = pl.BlockSpec((ctx, HEAD_W), lambda b, h, qi: (t_lat // ctx + b, h))
    vtc_spec = pl.BlockSpec((cpb, HEAD_W, VT_CHUNK), lambda b, h, qi: (t_lat // ctx + b, h, 0))
    scratch = [pltpu.VMEM((HEAD_W, 2 * tq), F32)]
    if latent:
        nq = s // tq
        in_specs = [pl.BlockSpec((HEAD_W, tq), lambda b, h, qi: (h, b * nq + qi)), kc_spec, vtc_spec,
                    pl.BlockSpec((s, HEAD_W), lambda b, h, qi: (b, h)),
                    pl.BlockSpec((s // VT_CHUNK, HEAD_W, VT_CHUNK), lambda b, h, qi: (b, h, 0))] + common_tail
        args = (qt, k, vt, k, vt, g_sub_b, da_lambda)
        row0, n_lat_chunks, aliases = 0, s // TK, {}
        scratch += [pltpu.VMEM((2, TK, 2 * tq), F32), pltpu.VMEM((2, TK, 2 * tq), BF),
                    pltpu.VMEM((1, 2 * tq), F32), pltpu.VMEM((1, 2 * tq), F32)]
    else:
        nq = ctx // tq
        in_specs = [pl.BlockSpec((HEAD_W, tq), lambda b, h, qi: (h, t_lat // tq + b * nq + qi)),
                    kc_spec, vtc_spec] + common_tail
        in_specs.append(pl.BlockSpec(memory_space=pl.ANY))
        args = (qt, k, vt, g_sub_b, da_lambda, fill)
        row0, n_lat_chunks, rows, aliases = t_lat // tq, 0, fill.shape[0], {5: 0}
    return pl.pallas_call(
        functools.partial(_attn_body, n_ctx_chunks=cpb, n_lat_chunks=n_lat_chunks, lam_init=lam_init),
        grid=(nb, HEADS, nq),
        in_specs=in_specs,
        out_specs=pl.BlockSpec((tq, HEAD_W), lambda b, h, qi: (row0 + b * nq + qi, h)),
        out_shape=jax.ShapeDtypeStruct((rows, HEADS * HEAD_W), BF),
        input_output_aliases=aliases,
        scratch_shapes=scratch,
        compiler_params=_cparams(("arbitrary", "arbitrary", "arbitrary"), VMEM_LIMIT),
    )(*args)


def _channel_dft(x, cs):
    parts = [jnp.dot(x[:, g * LANES:(g + 1) * LANES], cs, preferred_element_type=F32).astype(BF)
             for g in range(x.shape[1] // LANES)]
    return jnp.concatenate([jnp.concatenate([pq[:, :LANES] for pq in parts], axis=1),
                            jnp.concatenate([pq[:, LANES:] for pq in parts], axis=1)], axis=0)


def _fft_a_body(fa_ref, cs_ref, x_ref, zr_ref, zi_ref):
    x = x_ref[...]
    x = _channel_dft(x.reshape(x.shape[0], x.shape[1] * x.shape[2]), cs_ref[...])
    z = jnp.dot(fa_ref[...], x, preferred_element_type=F32)
    zr_ref[...] = z[:FFT_N2].astype(BF).reshape(zr_ref.shape)
    zi_ref[...] = z[FFT_N2:].astype(BF).reshape(zi_ref.shape)


def _fft_b_body(fb_ref, tc_ref, ts_ref, zr_ref, zi_ref, o_ref):
    nk = zr_ref.shape[0]
    parts_r, parts_i = [], []
    for j in range(nk):
        zr = zr_ref[j].astype(F32)
        zi = zi_ref[j].astype(F32)
        tc = jnp.concatenate([tc_ref[j]] * 4, axis=1)
        ts = jnp.concatenate([ts_ref[j]] * 4, axis=1)
        parts_r.append((zr * tc - zi * ts).astype(BF))
        parts_i.append((zr * ts + zi * tc).astype(BF))
    rhs = jnp.concatenate([jnp.concatenate(parts_r, axis=1), jnp.concatenate(parts_i, axis=1)], axis=0)
    o_ref[...] = jnp.dot(fb_ref[...], rhs, preferred_element_type=F32).astype(BF).reshape(o_ref.shape)


def _fft_ctx_body(f_ref, cs_ref, x_ref, _, o_ref):
    x = _channel_dft(x_ref[...], cs_ref[...])
    o_ref[...] = jnp.dot(f_ref[...], x, preferred_element_type=F32).astype(BF)


def _fft_tables(s, ctx):
    n1, n2 = s // FFT_N2, FFT_N2
    norm = 1.0 / math.sqrt(s * LANES)
    a = 2 * np.pi * np.outer(np.arange(n2), np.arange(n2)) / n2
    c, sn = np.cos(a), np.sin(a)
    fa = np.block([[c, -sn], [sn, c]]) * norm
    tw = 2 * np.pi * np.outer(np.arange(n2), np.arange(n1)) / s
    tc = np.repeat(np.cos(tw)[:, :, None], LANES, axis=2)
    ts = np.repeat(np.sin(tw)[:, :, None], LANES, axis=2)
    b = 2 * np.pi * np.outer(np.arange(n1), np.arange(n1)) / n1
    fb = np.concatenate([np.cos(b), -np.sin(b)], axis=1)
    ch = 2 * np.pi * np.outer(np.arange(LANES), np.arange(LANES)) / LANES
    cs = np.concatenate([np.cos(ch), np.sin(ch)], axis=1)
    cx = 2 * np.pi * np.outer(np.arange(ctx), np.arange(ctx)) / ctx
    fctx = np.concatenate([np.cos(cx), -np.sin(cx)], axis=1) / math.sqrt(ctx * LANES)
    return dict(fa=jnp.asarray(fa, BF), tc=jnp.asarray(tc, F32), ts=jnp.asarray(ts, F32),
                fb=jnp.asarray(fb, BF), cs=jnp.asarray(cs, BF), fctx=jnp.asarray(fctx, BF))


def _fourier_latent(fx, tabs, dims, rows):
    nb, s = dims["nb"], dims["s"]
    n1 = s // FFT_N2
    wide = n1 * BW
    nj = min(16, n1)
    wc = nj * BW
    fx3 = fx.reshape(-1, n1, BW)
    zshape = jax.ShapeDtypeStruct((nb, FFT_N2, n1, BW), BF)
    zr4, zi4 = pl.pallas_call(
        _fft_a_body,
        grid=(nb, wide // wc),
        in_specs=[pl.BlockSpec((2 * FFT_N2, 2 * FFT_N2), lambda b, j: (0, 0)),
                  pl.BlockSpec((LANES, 2 * LANES), lambda b, j: (0, 0)),
                  pl.BlockSpec((FFT_N2, nj, BW), lambda b, j: (b, j, 0))],
        out_specs=[pl.BlockSpec((None, FFT_N2, nj, BW), lambda b, j: (b, 0, j, 0))] * 2,
        out_shape=[zshape, zshape],
        compiler_params=_cparams(("arbitrary", "arbitrary"), VMEM_LIMIT),
    )(tabs["fa"], tabs["cs"], fx3)
    kb = 16
    y = pl.pallas_call(
        _fft_b_body,
        grid=(FFT_N2 // kb, nb),
        in_specs=[pl.BlockSpec((n1, 2 * n1), lambda kk, b: (0, 0)),
                  pl.BlockSpec((kb, n1, LANES), lambda kk, b: (kk, 0, 0)),
                  pl.BlockSpec((kb, n1, LANES), lambda kk, b: (kk, 0, 0)),
                  pl.BlockSpec((None, kb, n1, BW), lambda kk, b: (b, kk, 0, 0)),
                  pl.BlockSpec((None, kb, n1, BW), lambda kk, b: (b, kk, 0, 0))],
        out_specs=pl.BlockSpec((n1, kb, BW), lambda kk, b: (b, kk, 0)),
        out_shape=jax.ShapeDtypeStruct((rows // FFT_N2, FFT_N2, BW), BF),
        compiler_params=_cparams(("arbitrary", "arbitrary"), VMEM_LIMIT),
    )(tabs["fb"], tabs["tc"], tabs["ts"], zr4, zi4)
    return y.reshape(rows, BW)


def _fourier_ctx(fx, tabs, dims, fill):
    nb, ctx, t_lat = dims["nb"], dims["ctx"], dims["t_lat"]
    return pl.pallas_call(
        _fft_ctx_body,
        grid=(nb,),
        in_specs=[pl.BlockSpec((ctx, 2 * ctx), lambda b: (0, 0)),
                  pl.BlockSpec((LANES, 2 * LANES), lambda b: (0, 0)),
                  pl.BlockSpec((ctx, BW), lambda b: (t_lat // ctx + b, 0)),
                  pl.BlockSpec(memory_space=pl.ANY)],
        out_specs=pl.BlockSpec((ctx, BW), lambda b: (t_lat // ctx + b, 0)),
        out_shape=jax.ShapeDtypeStruct(fill.shape, BF),
        input_output_aliases={3: 0},
        compiler_params=_cparams(("arbitrary",), VMEM_LIMIT),
    )(tabs["fctx"], tabs["cs"], fx, fill)


def _route_topk(h2b, wr_ref, br_ref):
    logits = jnp.dot(h2b, wr_ref[...], preferred_element_type=F32) + br_ref[...]
    lane = lax.broadcasted_iota(jnp.int32, (TM, LANES), 1).astype(F32)
    cur = logits
    vals, idxs = [], []
    for _ in range(TOP_K):
        mx = jnp.max(cur, axis=1, keepdims=True)
        idx = jnp.min(jnp.where(cur == mx, lane, float(LANES)), axis=1, keepdims=True)
        vals.append(mx)
        idxs.append(idx)
        cur = jnp.where(lane == idx, -jnp.inf, cur)
    ex = [jnp.exp(v - vals[0]) for v in vals]
    den = ex[0] + ex[1] + ex[2] + ex[3]
    member = jnp.zeros((TM, LANES), F32)
    for idx in idxs:
        member = member + jnp.where(lane == idx, 1.0, 0.0)
    return idxs, [e / den for e in ex], member


def _route_ranks(idxs, weights, member, carry_ref, live):
    lane = lax.broadcasted_iota(jnp.int32, (TM, LANES), 1).astype(F32)
    tri = (lax.broadcasted_iota(jnp.int32, (TM, TM), 0) > lax.broadcasted_iota(jnp.int32, (TM, TM), 1))
    before = jnp.dot(jnp.where(tri, 1.0, 0.0).astype(BF), member.astype(BF), preferred_element_type=F32)
    rank_all = carry_ref[0:1, :] + before
    route = jnp.zeros((TM, LANES), F32)
    for r in range(TOP_K):
        rank = jnp.sum(jnp.where(lane == idxs[r], rank_all, 0.0), axis=1, keepdims=True)
        route = (route + jnp.where(lane == float(r), idxs[r], 0.0)
                 + jnp.where(lane == float(TOP_K + r), weights[r], 0.0)
                 + jnp.where(lane == float(2 * TOP_K + r), rank, 0.0))
    carry_ref[...] = carry_ref[...] + live * jnp.sum(member, axis=0, keepdims=True)
    return route


def _merge_body(xa_ref, xb_ref, mod_ref, g1_ref, a_ref, u_ref, up_ref, un_ref, sb_ref, yf_ref, px_ref, pp_ref, pn_ref,
                wconv_ref, wpool_ref, pscale_ref, wb_ref, wg_ref, bg_ref, wo_ref, g2_ref, wr_ref, br_ref,
                xo_ref, h2_ref, route_ref, cnt_ref, carry_ref, h2p_ref, *, n_tiles, nlat, tps, tpc, s, ctx, n_a):
    step = pl.program_id(0)
    i = jnp.minimum(step, n_tiles - 1)
    is_lat = i < nlat
    j = jnp.where(is_lat, i % tps, (i - nlat) % tpc)
    first = j == 0
    last = j == jnp.where(is_lat, tps, tpc) - 1
    seq_len = jnp.where(is_lat, s, ctx)

    @pl.when(step == 0)
    def _():
        carry_ref[...] = jnp.zeros_like(carry_ref)
        h2p_ref[...] = jnp.zeros_like(h2p_ref)

    topk_prev = _route_topk(h2p_ref[...], wr_ref, br_ref)

    x = jnp.where(i < n_a, xa_ref[...], xb_ref[...])
    h = _modnorm(x, g1_ref[...], mod_ref[0:1, :], mod_ref[1:2, :]).astype(BF)
    rowi = lax.broadcasted_iota(jnp.int32, (TM, 1), 0)

    u = u_ref[...].astype(F32)
    u_prev = jnp.where(first, 0.0, up_ref[...].astype(F32)[HALO - 1:HALO, :])
    u_next = jnp.where(last, 0.0, un_ref[...].astype(F32)[0:1, :])
    u_dn = jnp.where(rowi == 0, u_prev, pltpu.roll(u, 1, axis=0))
    u_up = jnp.where(rowi == TM - 1, u_next, pltpu.roll(u, TM - 1, axis=0))
    wc = wconv_ref[...]
    y_conv = sb_ref[...].astype(F32) * (wc[0:1] * u_dn + wc[1:2] * u + wc[2:3] * u_up)

    px = px_ref[...].astype(F32)
    ext = jnp.concatenate([jnp.where(first, 0.0, pp_ref[...].astype(F32)), px,
                           jnp.where(last, 0.0, pn_ref[...].astype(F32))], axis=0)
    ext_len = TM + 2 * HALO
    pos = j * TM + rowi
    pooled = []
    for g, wd in enumerate(POOL_WINDOWS):
        e = ext[:, g * LANES:(g + 1) * LANES]
        win = e + pltpu.roll(e, 1, axis=0)
        half = 1
        while 2 * half < wd:
            win = pltpu.roll(win, half, axis=0) + pltpu.roll(win, ext_len - half, axis=0)
            half *= 2
        win = win[HALO:HALO + TM]
        cnt = (jnp.minimum(pos + wd // 2, seq_len) - jnp.maximum(pos - wd // 2, 0)).astype(F32)
        pooled.append((win / cnt - px[:, g * LANES:(g + 1) * LANES]).astype(BF))

    gates = [_sigmoid(jnp.dot(h, wg_ref[:, kk * D:(kk + 1) * D], preferred_element_type=F32)
                      + bg_ref[:, kk * D:(kk + 1) * D]) for kk in range(4)]

    def branch(kk, y):
        return gates[kk] * jnp.dot(y, wb_ref[kk], preferred_element_type=F32)

    acc = branch(0, a_ref[...]) + branch(2, yf_ref[...]) + branch(1, y_conv.astype(BF))
    y_pool = jnp.concatenate([jnp.dot(pooled[g], wpool_ref[g], preferred_element_type=F32)
                              for g in range(len(POOL_WINDOWS))], axis=1) * pscale_ref[...]
    acc = acc + branch(3, y_pool.astype(BF))

    xn = x + mod_ref[2:3, :] * jnp.dot(acc.astype(BF), wo_ref[...], preferred_element_type=F32)
    xo_ref[...] = xn
    h2 = _modnorm(xn, g2_ref[...], mod_ref[3:4, :], mod_ref[4:5, :])
    h2_ref[...] = h2.reshape(h2_ref.shape)
    h2p_ref[...] = h2.astype(BF)
    route_ref[...] = _route_ranks(*topk_prev, carry_ref, jnp.where(step > 0, 1.0, 0.0))
    cnt_ref[...] = carry_ref[...]


def _merge(xa, xb, mods, g1, a, u, sb, yf, px, wconv, wpool, pscale, wb, wg, bg, wo, g2, wr, br, dims, rows):
    nlat, tps, tpc, nb = dims["nlat"], dims["tps"], dims["tpc"], dims["nb"]
    n_tiles = rows // TM
    n_a, n_b = xa.shape[0] // TM, xb.shape[0] // TM
    nhalo = TM // HALO
    last_halo = u.shape[0] // HALO - 1
    cur = lambda i: jnp.minimum(i, n_tiles - 1)
    tile = lambda w: pl.BlockSpec((TM, w), lambda i: (cur(i), 0))
    prev = pl.BlockSpec((HALO, BW), lambda i: (jnp.maximum(cur(i) * nhalo - 1, 0), 0))
    nxt = pl.BlockSpec((HALO, BW), lambda i: (jnp.minimum((cur(i) + 1) * nhalo, last_halo), 0))
    const = lambda shape: pl.BlockSpec(shape, lambda i: (0,) * len(shape))
    return pl.pallas_call(
        functools.partial(_merge_body, n_tiles=n_tiles, nlat=nlat, tps=tps, tpc=tpc, s=dims["s"], ctx=dims["ctx"],
                          n_a=n_a),
        grid=(n_tiles + 1,),
        in_specs=[pl.BlockSpec((TM, D), lambda i: (jnp.minimum(cur(i), n_a - 1), 0)),
                  pl.BlockSpec((TM, D), lambda i: (jnp.clip(cur(i) - n_a, 0, n_b - 1), 0)),
                  pl.BlockSpec((None, MOD_ROWS, D), lambda i: (jnp.minimum(cur(i) // tps, nb), 0, 0)),
                  const((1, D)), tile(BW), tile(BW), prev, nxt, tile(BW), tile(BW), tile(BW), prev, nxt,
                  const((8, BW)), const((4, LANES, LANES)), const((1, BW)), const((4, BW, D)),
                  const((D, 4 * D)), const((1, 4 * D)), const((D, D)), const((1, D)),
                  const((D, LANES)), const((1, LANES))],
        out_specs=[tile(D), pl.BlockSpec((TM,) + ROW_TILE, lambda i: (cur(i), 0, 0)),
                   pl.BlockSpec((TM, LANES), lambda i: (jnp.maximum(i - 1, 0), 0)),
                   pl.BlockSpec((8, LANES), lambda i: (0, 0))],
        out_shape=[jax.ShapeDtypeStruct((rows, D), F32), jax.ShapeDtypeStruct((rows,) + ROW_TILE, F32),
                   jax.ShapeDtypeStruct((rows, LANES), F32), jax.ShapeDtypeStruct((8, LANES), F32)],
        scratch_shapes=[pltpu.VMEM((8, LANES), F32), pltpu.VMEM((TM, D), BF)],
        compiler_params=_cparams(("arbitrary",), VMEM_LIMIT),
    )(xa, xb, mods, g1, a, u, u, u, sb, yf, px, px, px, wconv, wpool, pscale, wb, wg, bg, wo, g2, wr, br)


def _wait_row_copies(src_ref, dst_ref, sem, n):
    for _ in range(n):
        pltpu.make_async_copy(src_ref.at[pl.ds(0, 1)], dst_ref.at[pl.ds(0, 1)], sem).wait()


def _dispatch_body(dest_ref, h_ref, xs_ref, sem):
    def body(t, c):
        for r in range(TOP_K):
            pltpu.make_async_copy(h_ref.at[pl.ds(t, 1)], xs_ref.at[pl.ds(dest_ref[0, t * TOP_K + r], 1)],
                                  sem).start(priority=r % 2)
        return c
    lax.fori_loop(0, h_ref.shape[0], body, 0, unroll=8)
    _wait_row_copies(h_ref, xs_ref, sem, h_ref.shape[0] * TOP_K)


def _dispatch(dest3, h2, n_slots):
    rows = h2.shape[0]
    tmv = dest3.shape[2] // TOP_K
    return pl.pallas_call(
        _dispatch_body,
        grid=(rows // tmv,),
        in_specs=[pl.BlockSpec((None, 1, tmv * TOP_K), lambda i: (i, 0, 0), memory_space=pltpu.SMEM),
                  pl.BlockSpec((tmv,) + ROW_TILE, lambda i: (i, 0, 0))],
        out_specs=pl.BlockSpec(memory_space=pl.ANY),
        out_shape=jax.ShapeDtypeStruct((n_slots,) + ROW_TILE, F32),
        scratch_shapes=[pltpu.SemaphoreType.DMA(())],
        compiler_params=_cparams(("arbitrary",), VMEM_LIMIT),
    )(dest3, h2)


def _expert_body(te_ref, na_ref, xs_ref, wg_ref, bg_ref, wu_ref, bu_ref, wd_ref, bd_ref, y_ref, wgb, wub, wdb):
    j = pl.program_id(0)

    @pl.when(j < na_ref[0])
    def _():
        @pl.when((j == 0) | (te_ref[j] != te_ref[jnp.maximum(j - 1, 0)]))
        def _():
            wgb[...] = wg_ref[...].astype(BF)
            wub[...] = wu_ref[...].astype(BF)
            wdb[...] = wd_ref[...].astype(BF)

        x = xs_ref[...].reshape(TME, D).astype(BF)
        a = jnp.minimum(jnp.dot(x, wgb[...], preferred_element_type=F32) + bg_ref[...], SWIGLU_LIMIT)
        u = jnp.clip(jnp.dot(x, wub[...], preferred_element_type=F32) + bu_ref[...], -SWIGLU_LIMIT, SWIGLU_LIMIT)
        act = a * _sigmoid(SWIGLU_ALPHA * a) * (u + 1.0)
        y = jnp.dot(act.astype(BF), wdb[...], preferred_element_type=F32) + bd_ref[...]
        y_ref[...] = y.reshape(y_ref.shape)


def _experts(tile_expert, n_active, xs, layer, wg, bg, wu, bu, wd, bd):
    n_slots = xs.shape[0]
    f = wg.shape[3]
    slot = lambda j, te, na: (jnp.minimum(j, na[0] - 1), 0, 0)
    wsel = lambda j, te, na: (layer, te[j], 0, 0)
    return pl.pallas_call(
        _expert_body,
        grid_spec=pltpu.PrefetchScalarGridSpec(
            num_scalar_prefetch=2,
            grid=(n_slots // TME,),
            in_specs=[pl.BlockSpec((TME,) + ROW_TILE, slot),
                      pl.BlockSpec((None, None, D, f), wsel), pl.BlockSpec((None, None, 1, f), wsel),
                      pl.BlockSpec((None, None, D, f), wsel), pl.BlockSpec((None, None, 1, f), wsel),
                      pl.BlockSpec((None, None, f, D), wsel), pl.BlockSpec((None, None, 1, D), wsel)],
            out_specs=pl.BlockSpec((TME,) + ROW_TILE, slot),
            scratch_shapes=[pltpu.VMEM((D, f), BF), pltpu.VMEM((D, f), BF), pltpu.VMEM((f, D), BF)]),
        out_shape=jax.ShapeDtypeStruct((n_slots,) + ROW_TILE, F32),
        compiler_params=_cparams(("arbitrary",), VMEM_LIMIT),
    )(tile_expert, n_active, xs, wg, bg, wu, bu, wd, bd)


def _combine_body(dest_ref, x_ref, mod_ref, route_ref, gf_ref, y_ref, o_ref, buf, sem, *, final):
    tmv = x_ref.shape[0]

    def body(t, c):
        for r in range(TOP_K):
            pltpu.make_async_copy(y_ref.at[pl.ds(dest_ref[0, t * TOP_K + r], 1)], buf.at[r, pl.ds(t, 1)],
                                  sem).start(priority=r % 2)
        return c
    lax.fori_loop(0, tmv, body, 0, unroll=8)
    _wait_row_copies(y_ref, buf.at[0], sem, tmv * TOP_K)
    route = route_ref[...]
    acc = jnp.zeros((tmv, D), F32)
    for r in range(TOP_K):
        acc = acc + route[:, TOP_K + r:TOP_K + r + 1] * buf[r].reshape(tmv, D)
    xn = x_ref[...] + mod_ref[5:6, :] * acc
    if final:
        xn = xn * lax.rsqrt(jnp.mean(xn * xn, axis=-1, keepdims=True) + EPS) * gf_ref[...]
    o_ref[...] = xn


def _combine(dest3, x_mid, mods, route, g_final, y, dims, final):
    rows = x_mid.shape[0]
    tmv = dest3.shape[2] // TOP_K
    tps, nb = dims["s"] // tmv, dims["nb"]
    return pl.pallas_call(
        functools.partial(_combine_body, final=final),
        grid=(rows // tmv,),
        in_specs=[pl.BlockSpec((None, 1, tmv * TOP_K), lambda i: (i, 0, 0), memory_space=pltpu.SMEM),
                  pl.BlockSpec((tmv, D), lambda i: (i, 0)),
                  pl.BlockSpec((None, MOD_ROWS, D), lambda i: (jnp.minimum(i // tps, nb), 0, 0)),
                  pl.BlockSpec((tmv, LANES), lambda i: (i, 0)),
                  pl.BlockSpec((1, D), lambda i: (0, 0)),
                  pl.BlockSpec(memory_space=pl.ANY)],
        out_specs=pl.BlockSpec((tmv, D), lambda i: (i, 0)),
        out_shape=jax.ShapeDtypeStruct((rows, D), F32),
        scratch_shapes=[pltpu.VMEM((TOP_K, tmv) + ROW_TILE, F32), pltpu.SemaphoreType.DMA(())],
        compiler_params=_cparams(("arbitrary",), VMEM_LIMIT),
    )(dest3, x_mid, mods, route, g_final, y)


def _moe(x_mid, h2, route, counts, mods, g_final, layer, wg, bg, wu, bu, wd, bd, dims, final):
    rows = x_mid.shape[0]
    n_exp = wg.shape[1]
    n_slots = rows * TOP_K + n_exp * TME
    e = route[:, 0:TOP_K].astype(jnp.int32)
    rank = route[:, 2 * TOP_K:3 * TOP_K].astype(jnp.int32)
    cnt = counts[0, :n_exp].astype(jnp.int32)
    cnt_pad = ((cnt + TME - 1) // TME) * TME
    offs_end = jnp.cumsum(cnt_pad)
    offs = offs_end - cnt_pad
    onehot = e[:, :, None] == jnp.arange(n_exp, dtype=jnp.int32)[None, None, :]
    dest = jnp.sum(jnp.where(onehot, offs[None, None, :], 0), axis=-1) + rank
    tmv = max(t for t in (TM_MOVE, TM) if dims["s"] % t == 0 and dims["t_lat"] % t == 0 and rows % t == 0)
    dest3 = dest.reshape(rows // tmv, 1, tmv * TOP_K)
    tile_start = jnp.arange(n_slots // TME, dtype=jnp.int32) * TME
    tile_expert = jnp.minimum(jnp.sum(tile_start[:, None] >= offs_end[None, :], axis=1), n_exp - 1).astype(jnp.int32)
    n_active = (offs_end[-1:] // TME).astype(jnp.int32)
    xs = _dispatch(dest3, h2, n_slots)
    y = _experts(tile_expert, n_active, xs, layer, wg, bg, wu, bu, wd, bd)
    return _combine(dest3, x_mid, mods, route, g_final, y, dims, final)


def _rope_tables(s, ctx):
    rows = s // GRID_W
    row = np.repeat(np.arange(rows), GRID_W).astype(np.float32)
    col = np.tile(np.arange(GRID_W), rows).astype(np.float32)
    half = HEAD_DIM // 2
    inv = (np.float32(ROPE_BASE) ** (-np.arange(0, half, 2, dtype=np.float32) / half)).astype(np.float32)
    ar = row[:, None] * inv
    ac = col[:, None] * inv
    ang = np.concatenate([ar, ar, ac, ac], axis=-1)
    cos = np.concatenate([np.cos(ang), np.ones((ctx, HEAD_DIM), np.float32)], axis=0)
    sin = np.concatenate([np.sin(ang), np.zeros((ctx, HEAD_DIM), np.float32)], axis=0)
    return (jnp.asarray(np.tile(cos, (1, 2)), F32), jnp.asarray(np.tile(sin, (1, 2)), F32))


def kernel(x, c, ctx, c_ctx, w_ada, b_ada, g_norm1, w_in, da_lambda, g_subln, w_conv, w_pool, pool_scale,
           w_branch, w_mgate, b_mgate, w_out, g_norm2, w_router, b_router, w_e_gate, b_e_gate, w_e_up,
           b_e_up, w_e_down, b_e_down, g_final):
    nb, s, _ = x.shape
    nctx = ctx.shape[1]
    depth = w_in.shape[0]
    n_exp = w_router.shape[2]
    t_lat, t_ctx = nb * s, nb * nctx
    dims = dict(nb=nb, s=s, ctx=nctx, t_lat=t_lat, t_all=t_lat + t_ctx, nlat=t_lat // TM, tps=s // TM, tpc=nctx // TM)
    assert s % TK == 0 and s // TK >= 2 and s % (FFT_N2 * 8) == 0 and nctx % TM == 0 and nb < COND_ROWS
    assert TM == VT_CHUNK and nctx % VT_CHUNK == 0 and ATT_UNROLL % 2 == 0

    cos_t, sin_t = _rope_tables(s, nctx)
    tabs = _fft_tables(s, nctx)
    cond = jnp.concatenate([c, c_ctx[None, :], jnp.zeros((COND_ROWS - 1 - nb, D), F32)], axis=0)
    mods_all = _adaln(cond, w_ada, b_ada).reshape(depth, COND_ROWS, 6, D)
    mods_all = jnp.concatenate([mods_all, jnp.zeros((depth, COND_ROWS, MOD_ROWS - 6, D), F32)], axis=2)

    xa, xb = x.reshape(t_lat, D), ctx.reshape(t_ctx, D)
    for l in range(depth):
        last = l == depth - 1
        lam_init = 0.8 - 0.6 * math.exp(-0.3 * l)
        mods = mods_all[l]
        w_l = w_in[l]
        qt, k, vt, u, sb, fx, px = _inproj(xa, xb, mods, g_norm1[l][None, :], w_l.astype(BF), cos_t, sin_t, dims)

        rows = t_lat if last else t_lat + t_ctx
        a = _attention(qt, k, vt, g_subln[l], da_lambda[l], lam_init, dims, latent=True, rows=rows)
        yf = _fourier_latent(fx, tabs, dims, rows)
        if not last:
            a = _attention(qt, k, vt, g_subln[l], da_lambda[l], lam_init, dims, latent=False, fill=a)
            yf = _fourier_ctx(fx, tabs, dims, yf)

        wconv = jnp.concatenate([w_conv[l], jnp.zeros((5, BW), F32)], axis=0)
        wr = jnp.concatenate([w_router[l], jnp.zeros((D, LANES - n_exp), F32)], axis=1).astype(BF)
        br = jnp.concatenate([b_router[l], jnp.full((LANES - n_exp,), -1e30, F32)])[None, :]
        x_mid, h2, route, counts = _merge(
            xa, xb, mods, g_norm1[l][None, :], a, u, sb, yf, px, wconv, w_pool[l].astype(BF), pool_scale[l][None, :],
            w_branch[l].astype(BF), w_mgate[l].astype(BF), b_mgate[l][None, :], w_out[l].astype(BF),
            g_norm2[l][None, :], wr, br, dims, rows)
        x_new = _moe(x_mid, h2, route, counts, mods, g_final[None, :], l,
                     w_e_gate, b_e_gate[:, :, None, :], w_e_up, b_e_up[:, :, None, :],
                     w_e_down, b_e_down[:, :, None, :], dims, final=last)
        if last:
            return x_new.reshape(nb, s, D)
        xa = xb = x_new
```

```python
import functools
import math

import numpy as np
import jax
import jax.numpy as jnp
from jax import lax
from jax.experimental import pallas as pl
from jax.experimental.pallas import tpu as pltpu

F32 = jnp.float32
BF = jnp.bfloat16

D = 1024
HEADS = 4
HEAD_DIM = 64
HEAD_W = 128
BW = 512
GRID_W = 64
EPS = 1e-6
ROPE_BASE = 10000.0
POOL_WINDOWS = (2, 4, 8, 16)
TOP_K = 4
SWIGLU_LIMIT = 7.0
SWIGLU_ALPHA = 1.702
FFT_N2 = 128

TM = 256
TME = 1024
TM_MOVE = 1024
TQ = 512
TK = 256
ATT_UNROLL = 2
VT_CHUNK = 256
HALO = 16
LANES = 128
COND_ROWS = 16
MOD_ROWS = 8
ROW_TILE = (D // LANES, LANES)
QSCALE = (HEAD_DIM ** -0.5) * math.log2(math.e)
EXP_HEADROOM = 64.0
NORM_MARGIN = 1.05
VMEM_LIMIT = 56 * 2 ** 20


def _cparams(sem, vmem=None):
    return pltpu.CompilerParams(dimension_semantics=sem, vmem_limit_bytes=vmem)


def _modnorm(x, g, shift, scale):
    y = x * lax.rsqrt(jnp.mean(x * x, axis=-1, keepdims=True) + EPS) * g
    return y * (1.0 + scale) + shift


def _sigmoid(x):
    return 1.0 / (1.0 + jnp.exp(-x))


def _adaln_body(c_ref, w_ref, b_ref, o_ref):
    c = c_ref[...]
    s = c * _sigmoid(c)
    o_ref[...] = jnp.dot(s.astype(BF), w_ref[...].astype(BF), preferred_element_type=F32) + b_ref[...]


def _adaln(c16, w_ada, b_ada):
    nl = w_ada.shape[0]
    tn = 1536
    return pl.pallas_call(
        _adaln_body,
        grid=(nl, 6 * D // tn),
        in_specs=[pl.BlockSpec((COND_ROWS, D), lambda l, j: (0, 0)),
                  pl.BlockSpec((None, D, tn), lambda l, j: (l, 0, j)),
                  pl.BlockSpec((None, 1, tn), lambda l, j: (l, 0, j))],
        out_specs=pl.BlockSpec((None, COND_ROWS, tn), lambda l, j: (l, 0, j)),
        out_shape=jax.ShapeDtypeStruct((nl, COND_ROWS, 6 * D), F32),
        compiler_params=_cparams(("arbitrary", "arbitrary"), VMEM_LIMIT),
    )(c16, w_ada, b_ada.reshape(nl, 1, 6 * D))


def _token_tile(xa_ref, xb_ref, n_a):
    return jnp.where(pl.program_id(0) < n_a, xa_ref[...], xb_ref[...])


def _token_specs(xa, xb):
    n_a, n_b = xa.shape[0] // TM, xb.shape[0] // TM
    return [pl.BlockSpec((TM, D), lambda i: (jnp.minimum(i, n_a - 1), 0)),
            pl.BlockSpec((TM, D), lambda i: (jnp.clip(i - n_a, 0, n_b - 1), 0))]


def _inproj_body(xa_ref, xb_ref, mod_ref, g_ref, w_ref, cos_ref, sin_ref,
                 q_ref, k_ref, v_ref, u_ref, sb_ref, fx_ref, px_ref, *, n_a):
    h = _modnorm(_token_tile(xa_ref, xb_ref, n_a), g_ref[...], mod_ref[0:1, :], mod_ref[1:2, :]).astype(BF)

    def proj(c0):
        return jnp.dot(h, w_ref[:, c0:c0 + BW], preferred_element_type=F32)

    cos = jnp.concatenate([cos_ref[...]] * 4, axis=1)
    sin = jnp.concatenate([sin_ref[...]] * 4, axis=1)
    quarter = HEAD_DIM // 4
    even = (lax.broadcasted_iota(jnp.int32, (TM, BW), 1) // quarter) % 2 == 0

    def rope(x):
        rot = jnp.where(even, -pltpu.roll(x, BW - quarter, axis=1), pltpu.roll(x, quarter, axis=1))
        return x * cos + rot * sin

    q_ref[...] = (rope(proj(0)) * QSCALE).T.astype(BF)
    k_ref[...] = rope(proj(BW)).astype(BF)
    v_ref[...] = proj(2 * BW).T.astype(BF)
    u_ref[...] = (proj(5 * BW) * proj(3 * BW)).astype(BF)
    sb_ref[...] = proj(4 * BW).astype(BF)
    px_ref[...] = proj(7 * BW).astype(BF)
    fx_ref[...] = proj(6 * BW).astype(BF)


def _inproj(xa, xb, mods, g1, w_ext, cos_t, sin_t, dims):
    t_all, nlat, tps, tpc, nb = dims["t_all"], dims["nlat"], dims["tps"], dims["tpc"], dims["nb"]

    def tab_idx(i):
        return (jnp.where(i < nlat, i % tps, tps + (i - nlat) % tpc), 0)

    out = jax.ShapeDtypeStruct((t_all, BW), BF)
    row = pl.BlockSpec((TM, BW), lambda i: (i, 0))
    return pl.pallas_call(
        functools.partial(_inproj_body, n_a=xa.shape[0] // TM),
        grid=(t_all // TM,),
        in_specs=_token_specs(xa, xb) + [
                  pl.BlockSpec((None, MOD_ROWS, D), lambda i: (jnp.minimum(i // tps, nb), 0, 0)),
                  pl.BlockSpec((1, D), lambda i: (0, 0)),
                  pl.BlockSpec((D, 8 * BW), lambda i: (0, 0)),
                  pl.BlockSpec((TM, LANES), tab_idx),
                  pl.BlockSpec((TM, LANES), tab_idx)],
        out_specs=[pl.BlockSpec((BW, TM), lambda i: (0, i)), row,
                   pl.BlockSpec((None, BW, VT_CHUNK), lambda i: (i, 0, 0))] + [row] * 4,
        out_shape=[jax.ShapeDtypeStruct((BW, t_all), BF), out,
                   jax.ShapeDtypeStruct((t_all // VT_CHUNK, BW, VT_CHUNK), BF)] + [out] * 4,
        compiler_params=_cparams(("arbitrary",), VMEM_LIMIT),
    )(xa, xb, mods, g1, w_ext, cos_t, sin_t)


def _attn_body(*refs, n_ctx_chunks, n_lat_chunks, lam_init):
    if n_lat_chunks:
        qt_ref, kc_ref, vtc_ref, kl_ref, vtl_ref, g_ref, dl_ref, o_ref, acc_ref, s_buf, p_buf, l_ref, knorm_ref = refs
    else:
        qt_ref, kc_ref, vtc_ref, g_ref, dl_ref, _, o_ref, acc_ref = refs
    qt = qt_ref[...]
    tq = qt.shape[1]
    row = lax.broadcasted_iota(jnp.int32, qt.shape, 0)
    zero = jnp.zeros_like(qt)
    q2 = jnp.concatenate([jnp.where(row < HEAD_DIM, qt, zero), jnp.where(row >= HEAD_DIM, qt, zero)], axis=1)

    def scores(kc):
        s = jnp.dot(kc, q2, preferred_element_type=F32)
        return s, jnp.max(s, axis=0, keepdims=True)

    def softmax(s, smax, m, l):
        m_new = jnp.maximum(m, smax)
        p = jnp.exp2(s - m_new)
        alpha = jnp.exp2(m - m_new)
        return p.astype(BF), m_new, alpha * l + jnp.sum(p, axis=0, keepdims=True), alpha

    def accumulate(alpha, vt, p):
        acc_ref[...] = alpha * acc_ref[...] + jnp.dot(vt, p, preferred_element_type=F32)

    m = jnp.full((1, 2 * tq), -1e30, F32)
    l = jnp.zeros((1, 2 * tq), F32)
    acc_ref[...] = jnp.zeros_like(acc_ref)
    ctx_scores = [scores(kc_ref[c * VT_CHUNK:(c + 1) * VT_CHUNK, :]) for c in range(n_ctx_chunks)]
    if n_lat_chunks:
        per = TK // VT_CHUNK
        n = n_lat_chunks

        def keys(j):
            return kl_ref[pl.ds(pl.multiple_of(j * TK, TK), TK), :]

        s_buf[0] = jnp.dot(keys(0), q2, preferred_element_type=F32)
    for c in range(n_ctx_chunks):
        p, m, l, alpha = softmax(*ctx_scores[c], m, l)
        accumulate(alpha, vtc_ref[c], p)
    if n_lat_chunks:

        def values(j):
            parts = [vtl_ref[j * per + r] for r in range(per)]
            return parts[0] if per == 1 else jnp.concatenate(parts, axis=1)

        @pl.when(pl.program_id(2) == 0)
        def _():
            d_i = lax.broadcasted_iota(jnp.int32, (HEAD_W, LANES), 0)
            c_i = lax.broadcasted_iota(jnp.int32, (HEAD_W, LANES), 1)
            sel = jnp.where((d_i >= HEAD_DIM) == (c_i == 1), 1.0, 0.0) * jnp.where(c_i < 2, 1.0, 0.0)
            best = jnp.zeros((1, LANES), F32)
            for j in range(n):
                kf = keys(j).astype(F32)
                n2 = jnp.dot((kf * kf).astype(BF), sel.astype(BF), preferred_element_type=F32)
                best = jnp.maximum(best, jnp.max(n2, axis=0, keepdims=True))
            lane2 = lax.broadcasted_iota(jnp.int32, (1, 2 * tq), 1)
            knorm_ref[...] = jnp.sqrt(jnp.where(lane2 < tq, best[:, 0:1], best[:, 1:2]))

        qf = q2.astype(F32)
        bound = jnp.sqrt(jnp.sum(qf * qf, axis=0, keepdims=True)) * knorm_ref[...] * NORM_MARGIN
        gap = jnp.max(bound - m)
        l_ref[...] = l

        @pl.when(gap <= EXP_HEADROOM)
        def _():
            lsum = l
            s = s_buf[0]
            for j in range(n):
                s_next = jnp.dot(keys(j + 1), q2, preferred_element_type=F32) if j + 1 < n else None
                p = jnp.exp2(s - m)
                lsum = lsum + jnp.sum(p, axis=0, keepdims=True)
                acc_ref[...] += jnp.dot(values(j), p.astype(BF), preferred_element_type=F32)
                s = s_next
            l_ref[...] = lsum

        @pl.when(gap > EXP_HEADROOM)
        def _():
            _attn_exact_latent(m, l, n, keys, values, scores, softmax, accumulate, s_buf, p_buf, l_ref)

        l = l_ref[...]
    o = acc_ref[...] / l
    dl = dl_ref[...]
    lam = (jnp.exp(jnp.sum(dl[0:1] * dl[1:2], axis=1, keepdims=True))
           - jnp.exp(jnp.sum(dl[2:3] * dl[3:4], axis=1, keepdims=True)) + lam_init)
    d = o[:, :tq] - lam * o[:, tq:]
    y = d * lax.rsqrt(jnp.mean(d * d, axis=0, keepdims=True) + EPS) * g_ref[...] * (1.0 - lam_init)
    o_ref[...] = y.T.astype(BF)


def _attn_exact_latent(m, l, n, keys, values, scores, softmax, accumulate, s_buf, p_buf, l_ref):
    def half_step(j, prev, cur, carry):
        m, l, alpha_prev, smax = carry
        accumulate(alpha_prev, values(j - 1), p_buf[prev])
        if isinstance(j, int) and j + 1 >= n:
            smax_next = smax
        else:
            s_buf[prev], smax_next = scores(keys(jnp.minimum(j + 1, n - 1)))
        p, m, l, alpha = softmax(s_buf[cur], smax, m, l)
        p_buf[cur] = p
        return m, l, alpha, smax_next

    s0 = s_buf[0]
    p, m, l, alpha = softmax(s0, jnp.max(s0, axis=0, keepdims=True), m, l)
    p_buf[0] = p
    s_buf[1], smax = scores(keys(1))

    def body(i, carry):
        for r in range(ATT_UNROLL):
            carry = half_step(ATT_UNROLL * i + r + 1, r % 2, 1 - r % 2, carry)
        return carry

    n_body = (n - 1) // ATT_UNROLL
    carry = lax.fori_loop(0, n_body, body, (m, l, alpha, smax))
    for j in range(n_body * ATT_UNROLL + 1, n):
        carry = half_step(j, (j - 1) % 2, j % 2, carry)
    m, l, alpha, _ = carry
    accumulate(alpha, values(n - 1), p_buf[(n - 1) % 2])
    l_ref[...] = l


def _attention(qt, k, vt, g_sub, da_lambda, lam_init, dims, latent, rows=None, fill=None):
    nb, s, ctx, t_lat = dims["nb"], dims["s"], dims["ctx"], dims["t_lat"]
    cpb = ctx // VT_CHUNK
    tq = TQ if latent else min(TQ, ctx)
    g_sub_b = jnp.broadcast_to(g_sub[:, None], (HEAD_W, tq))
    common_tail = [pl.BlockSpec((HEAD_W, tq), lambda b, h, qi: (0, 0)),
                   pl.BlockSpec((4, HEAD_DIM), lambda b, h, qi: (0, 0))]
    kc_spec = pl.BlockSpec((ctx, HEAD_W), lambda b, h, qi: (t_lat // ctx + b, h))
    vtc_spec = pl.BlockSpec((cpb, HEAD_W, VT_CHUNK), lambda b, h, qi: (t_lat // ctx + b, h, 0))
    scratch = [pltpu.VMEM((HEAD_W, 2 * tq), F32)]
    if latent:
        nq = s // tq
        in_specs = [pl.BlockSpec((HEAD_W, tq), lambda b, h, qi: (h, b * nq + qi)), kc_spec, vtc_spec,
                    pl.BlockSpec((s, HEAD_W), lambda b, h, qi: (b, h)),
                    pl.BlockSpec((s // VT_CHUNK, HEAD_W, VT_CHUNK), lambda b, h, qi: (b, h, 0))] + common_tail
        args = (qt, k, vt, k, vt, g_sub_b, da_lambda)
        row0, n_lat_chunks, aliases = 0, s // TK, {}
        scratch += [pltpu.VMEM((2, TK, 2 * tq), F32), pltpu.VMEM((2, TK, 2 * tq), BF),
                    pltpu.VMEM((1, 2 * tq), F32), pltpu.VMEM((1, 2 * tq), F32)]
    else:
        nq = ctx // tq
        in_specs = [pl.BlockSpec((HEAD_W, tq), lambda b, h, qi: (h, t_lat // tq + b * nq + qi)),
                    kc_spec, vtc_spec] + common_tail
        in_specs.append(pl.BlockSpec(memory_space=pl.ANY))
        args = (qt, k, vt, g_sub_b, da_lambda, fill)
        row0, n_lat_chunks, rows, aliases = t_lat // tq, 0, fill.shape[0], {5: 0}
    return pl.pallas_call(
        functools.partial(_attn_body, n_ctx_chunks=cpb, n_lat_chunks=n_lat_chunks, lam_init=lam_init),
        grid=(nb, HEADS, nq),
        in_specs=in_specs,
        out_specs=pl.BlockSpec((tq, HEAD_W), lambda b, h, qi: (row0 + b * nq + qi, h)),
        out_shape=jax.ShapeDtypeStruct((rows, HEADS * HEAD_W), BF),
        input_output_aliases=aliases,
        scratch_shapes=scratch,
        compiler_params=_cparams(("arbitrary", "arbitrary", "arbitrary"), VMEM_LIMIT),
    )(*args)


def _channel_dft(x, cs):
    parts = [jnp.dot(x[:, g * LANES:(g + 1) * LANES], cs, preferred_element_type=F32).astype(BF)
             for g in range(x.shape[1] // LANES)]
    return jnp.concatenate([jnp.concatenate([pq[:, :LANES] for pq in parts], axis=1),
                            jnp.concatenate([pq[:, LANES:] for pq in parts], axis=1)], axis=0)


def _fft_a_body(fa_ref, cs_ref, x_ref, zr_ref, zi_ref):
    x = x_ref[...]
    x = _channel_dft(x.reshape(x.shape[0], x.shape[1] * x.shape[2]), cs_ref[...])
    z = jnp.dot(fa_ref[...], x, preferred_element_type=F32)
    zr_ref[...] = z[:FFT_N2].astype(BF).reshape(zr_ref.shape)
    zi_ref[...] = z[FFT_N2:].astype(BF).reshape(zi_ref.shape)


def _fft_b_body(fb_ref, tc_ref, ts_ref, zr_ref, zi_ref, o_ref):
    nk = zr_ref.shape[0]
    parts_r, parts_i = [], []
    for j in range(nk):
        zr = zr_ref[j].astype(F32)
        zi = zi_ref[j].astype(F32)
        tc = jnp.concatenate([tc_ref[j]] * 4, axis=1)
        ts = jnp.concatenate([ts_ref[j]] * 4, axis=1)
        parts_r.append((zr * tc - zi * ts).astype(BF))
        parts_i.append((zr * ts + zi * tc).astype(BF))
    rhs = jnp.concatenate([jnp.concatenate(parts_r, axis=1), jnp.concatenate(parts_i, axis=1)], axis=0)
    o_ref[...] = jnp.dot(fb_ref[...], rhs, preferred_element_type=F32).astype(BF).reshape(o_ref.shape)


def _fft_ctx_body(f_ref, cs_ref, x_ref, _, o_ref):
    x = _channel_dft(x_ref[...], cs_ref[...])
    o_ref[...] = jnp.dot(f_ref[...], x, preferred_element_type=F32).astype(BF)


def _fft_tables(s, ctx):
    n1, n2 = s // FFT_N2, FFT_N2
    norm = 1.0 / math.sqrt(s * LANES)
    a = 2 * np.pi * np.outer(np.arange(n2), np.arange(n2)) / n2
    c, sn = np.cos(a), np.sin(a)
    fa = np.block([[c, -sn], [sn, c]]) * norm
    tw = 2 * np.pi * np.outer(np.arange(n2), np.arange(n1)) / s
    tc = np.repeat(np.cos(tw)[:, :, None], LANES, axis=2)
    ts = np.repeat(np.sin(tw)[:, :, None], LANES, axis=2)
    b = 2 * np.pi * np.outer(np.arange(n1), np.arange(n1)) / n1
    fb = np.concatenate([np.cos(b), -np.sin(b)], axis=1)
    ch = 2 * np.pi * np.outer(np.arange(LANES), np.arange(LANES)) / LANES
    cs = np.concatenate([np.cos(ch), np.sin(ch)], axis=1)
    cx = 2 * np.pi * np.outer(np.arange(ctx), np.arange(ctx)) / ctx
    fctx = np.concatenate([np.cos(cx), -np.sin(cx)], axis=1) / math.sqrt(ctx * LANES)
    return dict(fa=jnp.asarray(fa, BF), tc=jnp.asarray(tc, F32), ts=jnp.asarray(ts, F32),
                fb=jnp.asarray(fb, BF), cs=jnp.asarray(cs, BF), fctx=jnp.asarray(fctx, BF))


def _fourier_latent(fx, tabs, dims, rows):
    nb, s = dims["nb"], dims["s"]
    n1 = s // FFT_N2
    wide = n1 * BW
    nj = min(16, n1)
    wc = nj * BW
    fx3 = fx.reshape(-1, n1, BW)
    zshape = jax.ShapeDtypeStruct((nb, FFT_N2, n1, BW), BF)
    zr4, zi4 = pl.pallas_call(
        _fft_a_body,
        grid=(nb, wide // wc),
        in_specs=[pl.BlockSpec((2 * FFT_N2, 2 * FFT_N2), lambda b, j: (0, 0)),
                  pl.BlockSpec((LANES, 2 * LANES), lambda b, j: (0, 0)),
                  pl.BlockSpec((FFT_N2, nj, BW), lambda b, j: (b, j, 0))],
        out_specs=[pl.BlockSpec((None, FFT_N2, nj, BW), lambda b, j: (b, 0, j, 0))] * 2,
        out_shape=[zshape, zshape],
        compiler_params=_cparams(("arbitrary", "arbitrary"), VMEM_LIMIT),
    )(tabs["fa"], tabs["cs"], fx3)
    kb = 16
    y = pl.pallas_call(
        _fft_b_body,
        grid=(FFT_N2 // kb, nb),
        in_specs=[pl.BlockSpec((n1, 2 * n1), lambda kk, b: (0, 0)),
                  pl.BlockSpec((kb, n1, LANES), lambda kk, b: (kk, 0, 0)),
                  pl.BlockSpec((kb, n1, LANES), lambda kk, b: (kk, 0, 0)),
                  pl.BlockSpec((None, kb, n1, BW), lambda kk, b: (b, kk, 0, 0)),
                  pl.BlockSpec((None, kb, n1, BW), lambda kk, b: (b, kk, 0, 0))],
        out_specs=pl.BlockSpec((n1, kb, BW), lambda kk, b: (b, kk, 0)),
        out_shape=jax.ShapeDtypeStruct((rows // FFT_N2, FFT_N2, BW), BF),
        compiler_params=_cparams(("arbitrary", "arbitrary"), VMEM_LIMIT),
    )(tabs["fb"], tabs["tc"], tabs["ts"], zr4, zi4)
    return y.reshape(rows, BW)


def _fourier_ctx(fx, tabs, dims, fill):
    nb, ctx, t_lat = dims["nb"], dims["ctx"], dims["t_lat"]
    return pl.pallas_call(
        _fft_ctx_body,
        grid=(nb,),
        in_specs=[pl.BlockSpec((ctx, 2 * ctx), lambda b: (0, 0)),
                  pl.BlockSpec((LANES, 2 * LANES), lambda b: (0, 0)),
                  pl.BlockSpec((ctx, BW), lambda b: (t_lat // ctx + b, 0)),
                  pl.BlockSpec(memory_space=pl.ANY)],
        out_specs=pl.BlockSpec((ctx, BW), lambda b: (t_lat // ctx + b, 0)),
        out_shape=jax.ShapeDtypeStruct(fill.shape, BF),
        input_output_aliases={3: 0},
        compiler_params=_cparams(("arbitrary",), VMEM_LIMIT),
    )(tabs["fctx"], tabs["cs"], fx, fill)


def _route_topk(h2b, wr_ref, br_ref):
    logits = jnp.dot(h2b, wr_ref[...], preferred_element_type=F32) + br_ref[...]
    lane = lax.broadcasted_iota(jnp.int32, (TM, LANES), 1).astype(F32)
    cur = logits
    vals, idxs = [], []
    for _ in range(TOP_K):
        mx = jnp.max(cur, axis=1, keepdims=True)
        idx = jnp.min(jnp.where(cur == mx, lane, float(LANES)), axis=1, keepdims=True)
        vals.append(mx)
        idxs.append(idx)
        cur = jnp.where(lane == idx, -jnp.inf, cur)
    ex = [jnp.exp(v - vals[0]) for v in vals]
    den = ex[0] + ex[1] + ex[2] + ex[3]
    member = jnp.zeros((TM, LANES), F32)
    for idx in idxs:
        member = member + jnp.where(lane == idx, 1.0, 0.0)
    return idxs, [e / den for e in ex], member


def _route_ranks(idxs, weights, member, carry_ref, live):
    lane = lax.broadcasted_iota(jnp.int32, (TM, LANES), 1).astype(F32)
    tri = (lax.broadcasted_iota(jnp.int32, (TM, TM), 0) > lax.broadcasted_iota(jnp.int32, (TM, TM), 1))
    before = jnp.dot(jnp.where(tri, 1.0, 0.0).astype(BF), member.astype(BF), preferred_element_type=F32)
    rank_all = carry_ref[0:1, :] + before
    route = jnp.zeros((TM, LANES), F32)
    for r in range(TOP_K):
        rank = jnp.sum(jnp.where(lane == idxs[r], rank_all, 0.0), axis=1, keepdims=True)
        route = (route + jnp.where(lane == float(r), idxs[r], 0.0)
                 + jnp.where(lane == float(TOP_K + r), weights[r], 0.0)
                 + jnp.where(lane == float(2 * TOP_K + r), rank, 0.0))
    carry_ref[...] = carry_ref[...] + live * jnp.sum(member, axis=0, keepdims=True)
    return route


def _merge_body(xa_ref, xb_ref, mod_ref, g1_ref, a_ref, u_ref, up_ref, un_ref, sb_ref, yf_ref, px_ref, pp_ref, pn_ref,
                wconv_ref, wpool_ref, pscale_ref, wb_ref, wg_ref, bg_ref, wo_ref, g2_ref, wr_ref, br_ref,
                xo_ref, h2_ref, route_ref, cnt_ref, carry_ref, h2p_ref, *, n_tiles, nlat, tps, tpc, s, ctx, n_a):
    step = pl.program_id(0)
    i = jnp.minimum(step, n_tiles - 1)
    is_lat = i < nlat
    j = jnp.where(is_lat, i % tps, (i - nlat) % tpc)
    first = j == 0
    last = j == jnp.where(is_lat, tps, tpc) - 1
    seq_len = jnp.where(is_lat, s, ctx)

    @pl.when(step == 0)
    def _():
        carry_ref[...] = jnp.zeros_like(carry_ref)
        h2p_ref[...] = jnp.zeros_like(h2p_ref)

    topk_prev = _route_topk(h2p_ref[...], wr_ref, br_ref)

    x = jnp.where(i < n_a, xa_ref[...], xb_ref[...])
    h = _modnorm(x, g1_ref[...], mod_ref[0:1, :], mod_ref[1:2, :]).astype(BF)
    rowi = lax.broadcasted_iota(jnp.int32, (TM, 1), 0)

    u = u_ref[...].astype(F32)
    u_prev = jnp.where(first, 0.0, up_ref[...].astype(F32)[HALO - 1:HALO, :])
    u_next = jnp.where(last, 0.0, un_ref[...].astype(F32)[0:1, :])
    u_dn = jnp.where(rowi == 0, u_prev, pltpu.roll(u, 1, axis=0))
    u_up = jnp.where(rowi == TM - 1, u_next, pltpu.roll(u, TM - 1, axis=0))
    wc = wconv_ref[...]
    y_conv = sb_ref[...].astype(F32) * (wc[0:1] * u_dn + wc[1:2] * u + wc[2:3] * u_up)

    px = px_ref[...].astype(F32)
    ext = jnp.concatenate([jnp.where(first, 0.0, pp_ref[...].astype(F32)), px,
                           jnp.where(last, 0.0, pn_ref[...].astype(F32))], axis=0)
    ext_len = TM + 2 * HALO
    pos = j * TM + rowi
    pooled = []
    for g, wd in enumerate(POOL_WINDOWS):
        e = ext[:, g * LANES:(g + 1) * LANES]
        win = e + pltpu.roll(e, 1, axis=0)
        half = 1
        while 2 * half < wd:
            win = pltpu.roll(win, half, axis=0) + pltpu.roll(win, ext_len - half, axis=0)
            half *= 2
        win = win[HALO:HALO + TM]
        cnt = (jnp.minimum(pos + wd // 2, seq_len) - jnp.maximum(pos - wd // 2, 0)).astype(F32)
        pooled.append((win / cnt - px[:, g * LANES:(g + 1) * LANES]).astype(BF))

    gates = [_sigmoid(jnp.dot(h, wg_ref[:, kk * D:(kk + 1) * D], preferred_element_type=F32)
                      + bg_ref[:, kk * D:(kk + 1) * D]) for kk in range(4)]

    def branch(kk, y):
        return gates[kk] * jnp.dot(y, wb_ref[kk], preferred_element_type=F32)

    acc = branch(0, a_ref[...]) + branch(2, yf_ref[...]) + branch(1, y_conv.astype(BF))
    y_pool = jnp.concatenate([jnp.dot(pooled[g], wpool_ref[g], preferred_element_type=F32)
                              for g in range(len(POOL_WINDOWS))], axis=1) * pscale_ref[...]
    acc = acc + branch(3, y_pool.astype(BF))

    xn = x + mod_ref[2:3, :] * jnp.dot(acc.astype(BF), wo_ref[...], preferred_element_type=F32)
    xo_ref[...] = xn
    h2 = _modnorm(xn, g2_ref[...], mod_ref[3:4, :], mod_ref[4:5, :])
    h2_ref[...] = h2.reshape(h2_ref.shape)
    h2p_ref[...] = h2.astype(BF)
    route_ref[...] = _route_ranks(*topk_prev, carry_ref, jnp.where(step > 0, 1.0, 0.0))
    cnt_ref[...] = carry_ref[...]


def _merge(xa, xb, mods, g1, a, u, sb, yf, px, wconv, wpool, pscale, wb, wg, bg, wo, g2, wr, br, dims, rows):
    nlat, tps, tpc, nb = dims["nlat"], dims["tps"], dims["tpc"], dims["nb"]
    n_tiles = rows // TM
    n_a, n_b = xa.shape[0] // TM, xb.shape[0] // TM
    nhalo = TM // HALO
    last_halo = u.shape[0] // HALO - 1
    cur = lambda i: jnp.minimum(i, n_tiles - 1)
    tile = lambda w: pl.BlockSpec((TM, w), lambda i: (cur(i), 0))
    prev = pl.BlockSpec((HALO, BW), lambda i: (jnp.maximum(cur(i) * nhalo - 1, 0), 0))
    nxt = pl.BlockSpec((HALO, BW), lambda i: (jnp.minimum((cur(i) + 1) * nhalo, last_halo), 0))
    const = lambda shape: pl.BlockSpec(shape, lambda i: (0,) * len(shape))
    return pl.pallas_call(
        functools.partial(_merge_body, n_tiles=n_tiles, nlat=nlat, tps=tps, tpc=tpc, s=dims["s"], ctx=dims["ctx"],
                          n_a=n_a),
        grid=(n_tiles + 1,),
        in_specs=[pl.BlockSpec((TM, D), lambda i: (jnp.minimum(cur(i), n_a - 1), 0)),
                  pl.BlockSpec((TM, D), lambda i: (jnp.clip(cur(i) - n_a, 0, n_b - 1), 0)),
                  pl.BlockSpec((None, MOD_ROWS, D), lambda i: (jnp.minimum(cur(i) // tps, nb), 0, 0)),
                  const((1, D)), tile(BW), tile(BW), prev, nxt, tile(BW), tile(BW), tile(BW), prev, nxt,
                  const((8, BW)), const((4, LANES, LANES)), const((1, BW)), const((4, BW, D)),
                  const((D, 4 * D)), const((1, 4 * D)), const((D, D)), const((1, D)),
                  const((D, LANES)), const((1, LANES))],
        out_specs=[tile(D), pl.BlockSpec((TM,) + ROW_TILE, lambda i: (cur(i), 0, 0)),
                   pl.BlockSpec((TM, LANES), lambda i: (jnp.maximum(i - 1, 0), 0)),
                   pl.BlockSpec((8, LANES), lambda i: (0, 0))],
        out_shape=[jax.ShapeDtypeStruct((rows, D), F32), jax.ShapeDtypeStruct((rows,) + ROW_TILE, F32),
                   jax.ShapeDtypeStruct((rows, LANES), F32), jax.ShapeDtypeStruct((8, LANES), F32)],
        scratch_shapes=[pltpu.VMEM((8, LANES), F32), pltpu.VMEM((TM, D), BF)],
        compiler_params=_cparams(("arbitrary",), VMEM_LIMIT),
    )(xa, xb, mods, g1, a, u, u, u, sb, yf, px, px, px, wconv, wpool, pscale, wb, wg, bg, wo, g2, wr, br)


def _wait_row_copies(src_ref, dst_ref, sem, n):
    for _ in range(n):
        pltpu.make_async_copy(src_ref.at[pl.ds(0, 1)], dst_ref.at[pl.ds(0, 1)], sem).wait()


def _dispatch_body(dest_ref, h_ref, xs_ref, sem):
    def body(t, c):
        for r in range(TOP_K):
            pltpu.make_async_copy(h_ref.at[pl.ds(t, 1)], xs_ref.at[pl.ds(dest_ref[0, t * TOP_K + r], 1)],
                                  sem).start(priority=r % 2)
        return c
    lax.fori_loop(0, h_ref.shape[0], body, 0, unroll=8)
    _wait_row_copies(h_ref, xs_ref, sem, h_ref.shape[0] * TOP_K)


def _dispatch(dest3, h2, n_slots):
    rows = h2.shape[0]
    tmv = dest3.shape[2] // TOP_K
    return pl.pallas_call(
        _dispatch_body,
        grid=(rows // tmv,),
        in_specs=[pl.BlockSpec((None, 1, tmv * TOP_K), lambda i: (i, 0, 0), memory_space=pltpu.SMEM),
                  pl.BlockSpec((tmv,) + ROW_TILE, lambda i: (i, 0, 0))],
        out_specs=pl.BlockSpec(memory_space=pl.ANY),
        out_shape=jax.ShapeDtypeStruct((n_slots,) + ROW_TILE, F32),
        scratch_shapes=[pltpu.SemaphoreType.DMA(())],
        compiler_params=_cparams(("arbitrary",), VMEM_LIMIT),
    )(dest3, h2)


def _expert_body(te_ref, na_ref, xs_ref, wg_ref, bg_ref, wu_ref, bu_ref, wd_ref, bd_ref, y_ref, wgb, wub, wdb):
    j = pl.program_id(0)

    @pl.when(j < na_ref[0])
    def _():
        @pl.when((j == 0) | (te_ref[j] != te_ref[jnp.maximum(j - 1, 0)]))
        def _():
            wgb[...] = wg_ref[...].astype(BF)
            wub[...] = wu_ref[...].astype(BF)
            wdb[...] = wd_ref[...].astype(BF)

        x = xs_ref[...].reshape(TME, D).astype(BF)
        a = jnp.minimum(jnp.dot(x, wgb[...], preferred_element_type=F32) + bg_ref[...], SWIGLU_LIMIT)
        u = jnp.clip(jnp.dot(x, wub[...], preferred_element_type=F32) + bu_ref[...], -SWIGLU_LIMIT, SWIGLU_LIMIT)
        act = a * _sigmoid(SWIGLU_ALPHA * a) * (u + 1.0)
        y = jnp.dot(act.astype(BF), wdb[...], preferred_element_type=F32) + bd_ref[...]
        y_ref[...] = y.reshape(y_ref.shape)


def _experts(tile_expert, n_active, xs, layer, wg, bg, wu, bu, wd, bd):
    n_slots = xs.shape[0]
    f = wg.shape[3]
    slot = lambda j, te, na: (jnp.minimum(j, na[0] - 1), 0, 0)
    wsel = lambda j, te, na: (layer, te[j], 0, 0)
    return pl.pallas_call(
        _expert_body,
        grid_spec=pltpu.PrefetchScalarGridSpec(
            num_scalar_prefetch=2,
            grid=(n_slots // TME,),
            in_specs=[pl.BlockSpec((TME,) + ROW_TILE, slot),
                      pl.BlockSpec((None, None, D, f), wsel), pl.BlockSpec((None, None, 1, f), wsel),
                      pl.BlockSpec((None, None, D, f), wsel), pl.BlockSpec((None, None, 1, f), wsel),
                      pl.BlockSpec((None, None, f, D), wsel), pl.BlockSpec((None, None, 1, D), wsel)],
            out_specs=pl.BlockSpec((TME,) + ROW_TILE, slot),
            scratch_shapes=[pltpu.VMEM((D, f), BF), pltpu.VMEM((D, f), BF), pltpu.VMEM((f, D), BF)]),
        out_shape=jax.ShapeDtypeStruct((n_slots,) + ROW_TILE, F32),
        compiler_params=_cparams(("arbitrary",), VMEM_LIMIT),
    )(tile_expert, n_active, xs, wg, bg, wu, bu, wd, bd)


def _combine_body(dest_ref, x_ref, mod_ref, route_ref, gf_ref, y_ref, o_ref, buf, sem, *, final):
    tmv = x_ref.shape[0]

    def body(t, c):
        for r in range(TOP_K):
            pltpu.make_async_copy(y_ref.at[pl.ds(dest_ref[0, t * TOP_K + r], 1)], buf.at[r, pl.ds(t, 1)],
                                  sem).start(priority=r % 2)
        return c
    lax.fori_loop(0, tmv, body, 0, unroll=8)
    _wait_row_copies(y_ref, buf.at[0], sem, tmv * TOP_K)
    route = route_ref[...]
    acc = jnp.zeros((tmv, D), F32)
    for r in range(TOP_K):
        acc = acc + route[:, TOP_K + r:TOP_K + r + 1] * buf[r].reshape(tmv, D)
    xn = x_ref[...] + mod_ref[5:6, :] * acc
    if final:
        xn = xn * lax.rsqrt(jnp.mean(xn * xn, axis=-1, keepdims=True) + EPS) * gf_ref[...]
    o_ref[...] = xn


def _combine(dest3, x_mid, mods, route, g_final, y, dims, final):
    rows = x_mid.shape[0]
    tmv = dest3.shape[2] // TOP_K
    tps, nb = dims["s"] // tmv, dims["nb"]
    return pl.pallas_call(
        functools.partial(_combine_body, final=final),
        grid=(rows // tmv,),
        in_specs=[pl.BlockSpec((None, 1, tmv * TOP_K), lambda i: (i, 0, 0), memory_space=pltpu.SMEM),
                  pl.BlockSpec((tmv, D), lambda i: (i, 0)),
                  pl.BlockSpec((None, MOD_ROWS, D), lambda i: (jnp.minimum(i // tps, nb), 0, 0)),
                  pl.BlockSpec((tmv, LANES), lambda i: (i, 0)),
                  pl.BlockSpec((1, D), lambda i: (0, 0)),
                  pl.BlockSpec(memory_space=pl.ANY)],
        out_specs=pl.BlockSpec((tmv, D), lambda i: (i, 0)),
        out_shape=jax.ShapeDtypeStruct((rows, D), F32),
        scratch_shapes=[pltpu.VMEM((TOP_K, tmv) + ROW_TILE, F32), pltpu.SemaphoreType.DMA(())],
        compiler_params=_cparams(("arbitrary",), VMEM_LIMIT),
    )(dest3, x_mid, mods, route, g_final, y)


def _moe(x_mid, h2, route, counts, mods, g_final, layer, wg, bg, wu, bu, wd, bd, dims, final):
    rows = x_mid.shape[0]
    n_exp = wg.shape[1]
    n_slots = rows * TOP_K + n_exp * TME
    e = route[:, 0:TOP_K].astype(jnp.int32)
    rank = route[:, 2 * TOP_K:3 * TOP_K].astype(jnp.int32)
    cnt = counts[0, :n_exp].astype(jnp.int32)
    cnt_pad = ((cnt + TME - 1) // TME) * TME
    offs_end = jnp.cumsum(cnt_pad)
    offs = offs_end - cnt_pad
    onehot = e[:, :, None] == jnp.arange(n_exp, dtype=jnp.int32)[None, None, :]
    dest = jnp.sum(jnp.where(onehot, offs[None, None, :], 0), axis=-1) + rank
    tmv = max(t for t in (TM_MOVE, TM) if dims["s"] % t == 0 and dims["t_lat"] % t == 0 and rows % t == 0)
    dest3 = dest.reshape(rows // tmv, 1, tmv * TOP_K)
    tile_start = jnp.arange(n_slots // TME, dtype=jnp.int32) * TME
    tile_expert = jnp.minimum(jnp.sum(tile_start[:, None] >= offs_end[None, :], axis=1), n_exp - 1).astype(jnp.int32)
    n_active = (offs_end[-1:] // TME).astype(jnp.int32)
    xs = _dispatch(dest3, h2, n_slots)
    y = _experts(tile_expert, n_active, xs, layer, wg, bg, wu, bu, wd, bd)
    return _combine(dest3, x_mid, mods, route, g_final, y, dims, final)


def _rope_tables(s, ctx):
    rows = s // GRID_W
    row = np.repeat(np.arange(rows), GRID_W).astype(np.float32)
    col = np.tile(np.arange(GRID_W), rows).astype(np.float32)
    half = HEAD_DIM // 2
    inv = (np.float32(ROPE_BASE) ** (-np.arange(0, half, 2, dtype=np.float32) / half)).astype(np.float32)
    ar = row[:, None] * inv
    ac = col[:, None] * inv
    ang = np.concatenate([ar, ar, ac, ac], axis=-1)
    cos = np.concatenate([np.cos(ang), np.ones((ctx, HEAD_DIM), np.float32)], axis=0)
    sin = np.concatenate([np.sin(ang), np.zeros((ctx, HEAD_DIM), np.float32)], axis=0)
    return (jnp.asarray(np.tile(cos, (1, 2)), F32), jnp.asarray(np.tile(sin, (1, 2)), F32))


def kernel(x, c, ctx, c_ctx, w_ada, b_ada, g_norm1, w_in, da_lambda, g_subln, w_conv, w_pool, pool_scale,
           w_branch, w_mgate, b_mgate, w_out, g_norm2, w_router, b_router, w_e_gate, b_e_gate, w_e_up,
           b_e_up, w_e_down, b_e_down, g_final):
    nb, s, _ = x.shape
    nctx = ctx.shape[1]
    depth = w_in.shape[0]
    n_exp = w_router.shape[2]
    t_lat, t_ctx = nb * s, nb * nctx
    dims = dict(nb=nb, s=s, ctx=nctx, t_lat=t_lat, t_all=t_lat + t_ctx, nlat=t_lat // TM, tps=s // TM, tpc=nctx // TM)
    assert s % TK == 0 and s // TK >= 2 and s % (FFT_N2 * 8) == 0 and nctx % TM == 0 and nb < COND_ROWS
    assert TM == VT_CHUNK and nctx % VT_CHUNK == 0 and ATT_UNROLL % 2 == 0

    cos_t, sin_t = _rope_tables(s, nctx)
    tabs = _fft_tables(s, nctx)
    cond = jnp.concatenate([c, c_ctx[None, :], jnp.zeros((COND_ROWS - 1 - nb, D), F32)], axis=0)
    mods_all = _adaln(cond, w_ada, b_ada).reshape(depth, COND_ROWS, 6, D)
    mods_all = jnp.concatenate([mods_all, jnp.zeros((depth, COND_ROWS, MOD_ROWS - 6, D), F32)], axis=2)

    xa, xb = x.reshape(t_lat, D), ctx.reshape(t_ctx, D)
    for l in range(depth):
        last = l == depth - 1
        lam_init = 0.8 - 0.6 * math.exp(-0.3 * l)
        mods = mods_all[l]
        w_l = w_in[l]
        qt, k, vt, u, sb, fx, px = _inproj(xa, xb, mods, g_norm1[l][None, :], w_l.astype(BF), cos_t, sin_t, dims)

        rows = t_lat if last else t_lat + t_ctx
        a = _attention(qt, k, vt, g_subln[l], da_lambda[l], lam_init, dims, latent=True, rows=rows)
        yf = _fourier_latent(fx, tabs, dims, rows)
        if not last:
            a = _attention(qt, k, vt, g_subln[l], da_lambda[l], lam_init, dims, latent=False, fill=a)
            yf = _fourier_ctx(fx, tabs, dims, yf)

        wconv = jnp.concatenate([w_conv[l], jnp.zeros((5, BW), F32)], axis=0)
        wr = jnp.concatenate([w_router[l], jnp.zeros((D, LANES - n_exp), F32)], axis=1).astype(BF)
        br = jnp.concatenate([b_router[l], jnp.full((LANES - n_exp,), -1e30, F32)])[None, :]
        x_mid, h2, route, counts = _merge(
            xa, xb, mods, g_norm1[l][None, :], a, u, sb, yf, px, wconv, w_pool[l].astype(BF), pool_scale[l][None, :],
            w_branch[l].astype(BF), w_mgate[l].astype(BF), b_mgate[l][None, :], w_out[l].astype(BF),
            g_norm2[l][None, :], wr, br, dims, rows)
        x_new = _moe(x_mid, h2, route, counts, mods, g_final[None, :], l,
                     w_e_gate, b_e_gate[:, :, None, :], w_e_up, b_e_up[:, :, None, :],
                     w_e_down, b_e_down[:, :, None, :], dims, final=last)
        if last:
            return x_new.reshape(nb, s, D)
        xa = xb = x_new
```
